```python
import jax, jax.numpy as jnp
from jax import lax
import numpy as np

D_MODEL = 1024
BATCH = 1
SEQ = 16384
DEPTH = 2

GRID_W = 64
HEAD_DIM = 64
A_Q_HEADS = 8
A_KV_HEADS = 2
A_GROUP = A_Q_HEADS // A_KV_HEADS
B_HEADS = 4
B_V_DIM = 2 * HEAD_DIM
A_WIDTH = A_Q_HEADS * HEAD_DIM
B_WIDTH = B_HEADS * B_V_DIM
A_Q_COLS = A_Q_HEADS * HEAD_DIM
A_KV_COLS = A_KV_HEADS * HEAD_DIM
B_QK_COLS = 2 * B_HEADS * HEAD_DIM
B_V_COLS = B_HEADS * B_V_DIM
IN_COLS = A_Q_COLS + 2 * A_KV_COLS + 2 * B_QK_COLS + B_V_COLS
N_BRANCH = 2
D_FF = 2816
Q_BLOCK = 128
ROPE_THETA = 10000.0
ROPE_AXIS_DIM = HEAD_DIM // 2
N_ADA = 9
EPS = 1e-6

kernel_name = 'hybrid_gqa_axialrope_diffattn_macaron_adaln'


def rms_norm(x, g):
    xf = x.astype(jnp.float32)
    y = xf * lax.rsqrt(jnp.mean(xf * xf, axis=-1, keepdims=True) + EPS)
    return (y * g.astype(jnp.float32)).astype(x.dtype)


def modulate(x, shift, scale):
    return x * (1 + scale[:, None, :]) + shift[:, None, :]


def swiglu(x, w_gate, w_up, w_down):
    return (jax.nn.silu(x @ w_gate) * (x @ w_up)) @ w_down


def axial_angles(seq):
    rows = seq // GRID_W
    row = jnp.broadcast_to(jnp.arange(rows)[:, None], (rows, GRID_W)).reshape(seq) - rows // 2
    col = jnp.broadcast_to(jnp.arange(GRID_W)[None, :], (rows, GRID_W)).reshape(seq) - GRID_W // 2
    inv = 1.0 / (ROPE_THETA ** (jnp.arange(0, ROPE_AXIS_DIM, 2, dtype=jnp.float32) / ROPE_AXIS_DIM))
    ang_r = row.astype(jnp.float32)[:, None] * inv
    ang_c = col.astype(jnp.float32)[:, None] * inv
    return ang_r, ang_c


def rotate_section(x, ang):
    cos = jnp.cos(ang)[:, None, :]
    sin = jnp.sin(ang)[:, None, :]
    x1, x2 = jnp.split(x, 2, axis=-1)
    return jnp.concatenate([x1 * cos - x2 * sin, x2 * cos + x1 * sin], axis=-1)


def axial_rope(x, ang_r, ang_c):
    xf = x.astype(jnp.float32)
    xr, xc = jnp.split(xf, 2, axis=-1)
    return jnp.concatenate([rotate_section(xr, ang_r), rotate_section(xc, ang_c)], axis=-1).astype(x.dtype)


def gqa_attention(q, k, v):
    b, s = q.shape[0], q.shape[1]
    nblk = s // Q_BLOCK
    qb = q.reshape(b, nblk, Q_BLOCK, A_KV_HEADS, A_GROUP, HEAD_DIM).transpose(1, 0, 2, 3, 4, 5)
    scale = HEAD_DIM ** -0.5

    def block(qi):
        sc = jnp.einsum('bqkgd,bskd->bkgqs', qi, k, preferred_element_type=jnp.float32) * scale
        p = jax.nn.softmax(sc, axis=-1).astype(v.dtype)
        return jnp.einsum('bkgqs,bskd->bqkgd', p, v)

    o = lax.map(block, qb)
    return o.transpose(1, 0, 2, 3, 4, 5).reshape(b, s, A_WIDTH)


def diff_attention(q, k, v, lam, slopes):
    b, s = q.shape[0], q.shape[1]
    nblk = s // Q_BLOCK
    qb = q.reshape(b, nblk, Q_BLOCK, 2, B_HEADS, HEAD_DIM).transpose(1, 0, 2, 3, 4, 5)
    scale = HEAD_DIM ** -0.5
    kpos = jnp.arange(s, dtype=jnp.float32)

    def block(args):
        qi, bi = args
        qpos = (bi * Q_BLOCK + jnp.arange(Q_BLOCK)).astype(jnp.float32)
        dist = jnp.abs(qpos[:, None] - kpos[None, :])
        bias = -slopes[:, None, None] * dist[None]
        sc = jnp.einsum('bqmhd,bsmhd->bmhqs', qi, k, preferred_element_type=jnp.float32) * scale
        p = jax.nn.softmax(sc + bias[None, None], axis=-1)
        w = (p[:, 0] - lam * p[:, 1]).astype(v.dtype)
        return jnp.einsum('bhqs,bshe->bqhe', w, v)

    o = lax.map(block, (qb, jnp.arange(nblk)))
    return o.transpose(1, 0, 2, 3, 4).reshape(b, s, B_HEADS, B_V_DIM)


def token_mixing(n, ang_r, ang_c, lam_init, w_in, qk_g, lam_p, subln_g, w_ba, w_bb, w_gate, b_gate, w_o):
    b, s, _ = n.shape
    proj = n @ w_in
    c1 = A_Q_COLS
    c2 = c1 + A_KV_COLS
    c3 = c2 + A_KV_COLS
    c4 = c3 + B_QK_COLS
    c5 = c4 + B_QK_COLS
    qa, ka, va, qd, kd, vd = jnp.split(proj, [c1, c2, c3, c4, c5], axis=-1)

    qa = axial_rope(rms_norm(qa.reshape(b, s, A_Q_HEADS, HEAD_DIM), qk_g[0]), ang_r, ang_c)
    ka = axial_rope(rms_norm(ka.reshape(b, s, A_KV_HEADS, HEAD_DIM), qk_g[1]), ang_r, ang_c)
    va = va.reshape(b, s, A_KV_HEADS, HEAD_DIM)
    qa = qa.reshape(b, s, A_KV_HEADS, A_GROUP, HEAD_DIM)
    ya = gqa_attention(qa, ka, va) @ w_ba

    lp = lam_p.astype(jnp.float32)
    lam = jnp.exp(jnp.sum(lp[0] * lp[1])) - jnp.exp(jnp.sum(lp[2] * lp[3])) + lam_init
    slopes = 2.0 ** (-8.0 * jnp.arange(1, B_HEADS + 1, dtype=jnp.float32) / B_HEADS)
    qd = qd.reshape(b, s, 2, B_HEADS, HEAD_DIM)
    kd = kd.reshape(b, s, 2, B_HEADS, HEAD_DIM)
    vd = vd.reshape(b, s, B_HEADS, B_V_DIM)
    od = diff_attention(qd, kd, vd, lam, slopes)
    od = rms_norm(od, subln_g) * (1.0 - lam_init)
    yb = od.reshape(b, s, B_WIDTH) @ w_bb

    g = jax.nn.sigmoid(n @ w_gate + b_gate)
    ga, gb = jnp.split(g, N_BRANCH, axis=-1)
    return (ga * ya + gb * yb) @ w_o


def setup_inputs(seed: int = 0) -> dict:
    key = jax.random.key(seed)
    ks = jax.random.split(key, 24)
    D, L, F = D_MODEL, DEPTH, D_FF

    def nrm(k, shape, scale):
        return jax.random.normal(k, shape, jnp.float32) * scale

    return {
        'x': nrm(ks[0], (BATCH, SEQ, D), 1.0),
        'c': nrm(ks[1], (BATCH, D), 1.0),
        'ada_w': nrm(ks[2], (L, D, N_ADA * D), 0.5 * D ** -0.5),
        'ada_b': nrm(ks[3], (L, N_ADA * D), 0.01),
        'norm_g': 1.0 + nrm(ks[4], (L, 3, D), 0.02),
        'ffn_wg': nrm(ks[5], (L, 2, D, F), D ** -0.5),
        'ffn_wu': nrm(ks[6], (L, 2, D, F), D ** -0.5),
        'ffn_wd': nrm(ks[7], (L, 2, F, D), F ** -0.5),
        'w_in': nrm(ks[8], (L, D, IN_COLS), D ** -0.5),
        'qk_g': 1.0 + nrm(ks[9], (L, 2, HEAD_DIM), 0.02),
        'lam_p': nrm(ks[10], (L, 4, HEAD_DIM), 0.1),
        'subln_g': 1.0 + nrm(ks[11], (L, B_V_DIM), 0.02),
        'w_ba': nrm(ks[12], (L, A_WIDTH, D), A_WIDTH ** -0.5),
        'w_bb': nrm(ks[13], (L, B_WIDTH, D), B_WIDTH ** -0.5),
        'w_gate': nrm(ks[14], (L, D, N_BRANCH * D), D ** -0.5),
        'b_gate': nrm(ks[15], (L, N_BRANCH * D), 0.01),
        'w_o': nrm(ks[16], (L, D, D), D ** -0.5),
        'final_g': 1.0 + nrm(ks[17], (D,), 0.02),
    }


def reference(x, c, ada_w, ada_b, norm_g, ffn_wg, ffn_wu, ffn_wd, w_in, qk_g, lam_p, subln_g, w_ba, w_bb, w_gate, b_gate, w_o, final_g):
    s = x.shape[1]
    ang_r, ang_c = axial_angles(s)
    c_act = jax.nn.silu(c)
    h = x
    for l in range(DEPTH):
        mod = c_act @ ada_w[l] + ada_b[l]
        sh1, sc1, g1, sh2, sc2, g2, sh3, sc3, g3 = jnp.split(mod, N_ADA, axis=-1)
        lam_init = 0.8 - 0.6 * float(np.exp(-0.3 * l))

        n1 = modulate(rms_norm(h, norm_g[l, 0]), sh1, sc1)
        h = h + 0.5 * g1[:, None, :] * swiglu(n1, ffn_wg[l, 0], ffn_wu[l, 0], ffn_wd[l, 0])

        n2 = modulate(rms_norm(h, norm_g[l, 1]), sh2, sc2)
        y = token_mixing(n2, ang_r, ang_c, lam_init, w_in[l], qk_g[l], lam_p[l], subln_g[l],
                         w_ba[l], w_bb[l], w_gate[l], b_gate[l], w_o[l])
        h = h + g2[:, None, :] * y

        n3 = modulate(rms_norm(h, norm_g[l, 2]), sh3, sc3)
        h = h + 0.5 * g3[:, None, :] * swiglu(n3, ffn_wg[l, 1], ffn_wu[l, 1], ffn_wd[l, 1])
    return rms_norm(h, final_g)
```

```python
import functools
import math

import jax
import jax.numpy as jnp
from jax import lax
from jax.experimental import pallas as pl
from jax.experimental.pallas import tpu as pltpu

F32 = jnp.float32
BF16 = jnp.bfloat16

GRID_W = 64
HEAD_DIM = 64
A_Q_HEADS = 8
A_KV_HEADS = 2
A_GROUP = A_Q_HEADS // A_KV_HEADS
B_HEADS = 4
B_V_DIM = 2 * HEAD_DIM
A_Q_COLS = A_Q_HEADS * HEAD_DIM
A_KV_COLS = A_KV_HEADS * HEAD_DIM
B_QK_COLS = 2 * B_HEADS * HEAD_DIM
B_V_COLS = B_HEADS * B_V_DIM
N_ADA = 9
EPS = 1e-6
ROPE_THETA = 10000.0
ROPE_AXIS_DIM = HEAD_DIM // 2
ROPE_HALF = ROPE_AXIS_DIM // 2

LOG2E = math.log2(math.e)
QK_SCALE = HEAD_DIM ** -0.5
NEG_BIG = -1e30

VMEM_LIMIT_BYTES = 56 * 1024 * 1024

FFN_ROWS = 512
MIX_ROWS = 512
KEY_TILE = 512
GQA_Q_TILE = 256
DIFF_Q_TILE = 256
ADA_COLS = 1152


def _params(*sem):
    return pltpu.CompilerParams(dimension_semantics=sem, vmem_limit_bytes=VMEM_LIMIT_BYTES)


def _rms_rows(x):
    return x * lax.rsqrt(jnp.mean(x * x, axis=-1, keepdims=True) + EPS)


def _ada_kernel(c_ref, w_ref, b_ref, o_ref):
    c = c_ref[...]
    act = c / (1.0 + jnp.exp(-c))
    o_ref[...] = jnp.sum(w_ref[...] * act, axis=0, keepdims=True) + b_ref[...]


def _ada_mod(c_col, ada_w, ada_b):
    n_layers, d, n = ada_w.shape
    return pl.pallas_call(
        _ada_kernel,
        grid=(n_layers, n // ADA_COLS),
        in_specs=[
            pl.BlockSpec((d, 1), lambda l, j: (0, 0)),
            pl.BlockSpec((None, d, ADA_COLS), lambda l, j: (l, 0, j)),
            pl.BlockSpec((None, 1, ADA_COLS), lambda l, j: (l, 0, j)),
        ],
        out_specs=pl.BlockSpec((None, 1, ADA_COLS), lambda l, j: (l, 0, j)),
        out_shape=jax.ShapeDtypeStruct((n_layers, 1, n), F32),
        compiler_params=_params("parallel", "parallel"),
        name="ada_mod",
    )(c_col, ada_w, ada_b.reshape(n_layers, 1, n))


def _ffn_kernel(h_ref, ng_ref, sh_ref, sc_ref, gt_ref, wg_ref, wu_ref, wd_ref, fg_ref, o_ref,
                n_scr, acc_scr, *, final_norm):
    f = pl.program_id(1)

    @pl.when(f == 0)
    def _():
        n = _rms_rows(h_ref[...]) * ng_ref[...]
        n = n * (1.0 + sc_ref[...]) + sh_ref[...]
        n_scr[...] = n.astype(BF16)
        acc_scr[...] = jnp.zeros_like(acc_scr)

    n = n_scr[...]
    hg = jnp.dot(n, wg_ref[...], preferred_element_type=F32)
    hu = jnp.dot(n, wu_ref[...], preferred_element_type=F32)
    a = (hg / (1.0 + jnp.exp(-hg))) * hu
    acc_scr[...] += jnp.dot(a.astype(BF16), wd_ref[...], preferred_element_type=F32)

    @pl.when(f == pl.num_programs(1) - 1)
    def _():
        out = h_ref[...] + (0.5 * gt_ref[...]) * acc_scr[...]
        if final_norm:
            out = _rms_rows(out) * fg_ref[...]
        o_ref[...] = out


def _ffn(h, ng, sh, sc, gt, wg, wu, wd, fg, *, final_norm):
    s, d = h.shape
    f_dim = wg.shape[1]
    tf = f_dim // 2
    vec = pl.BlockSpec((1, d), lambda i, f: (0, 0))
    return pl.pallas_call(
        functools.partial(_ffn_kernel, final_norm=final_norm),
        grid=(s // FFN_ROWS, f_dim // tf),
        in_specs=[
            pl.BlockSpec((FFN_ROWS, d), lambda i, f: (i, 0)),
            vec, vec, vec, vec,
            pl.BlockSpec((d, tf), lambda i, f: (0, f)),
            pl.BlockSpec((d, tf), lambda i, f: (0, f)),
            pl.BlockSpec((tf, d), lambda i, f: (f, 0)),
            vec,
        ],
        out_specs=pl.BlockSpec((FFN_ROWS, d), lambda i, f: (i, 0)),
        out_shape=jax.ShapeDtypeStruct((s, d), F32),
        scratch_shapes=[pltpu.VMEM((FFN_ROWS, d), BF16), pltpu.VMEM((FFN_ROWS, d), F32)],
        compiler_params=_params("parallel", "arbitrary"),
        name="ffn_final" if final_norm else "ffn",
    )(h, ng, sh, sc, gt, wg, wu, wd, fg)


def _rope_t(x, cr, sr, cc, sc):
    h = ROPE_HALF
    x1r, x2r, x1c, x2c = x[0:h], x[h:2 * h], x[2 * h:3 * h], x[3 * h:4 * h]
    return jnp.concatenate(
        [x1r * cr - x2r * sr, x2r * cr + x1r * sr, x1c * cc - x2c * sc, x2c * cc + x1c * sc], axis=0)


def _head_norm_t(x, g_col):
    ms = jnp.mean(x * x, axis=0, keepdims=True)
    return x * lax.rsqrt(ms + EPS) * g_col


def _mix_in_kernel(h_ref, ng_ref, sh_ref, sc_ref, wt_ref, angr_ref, angc_ref, qkg_ref,
                   qta_ref, ka_ref, vta_ref, qtd_ref, kd_ref, vtd_ref):
    n = _rms_rows(h_ref[...]) * ng_ref[...]
    n = (n * (1.0 + sc_ref[...]) + sh_ref[...]).astype(BF16)
    pt = lax.dot_general(wt_ref[...], n, (((1,), (1,)), ((), ())), preferred_element_type=F32)
    t = pt.shape[1]

    cr, sr = jnp.cos(angr_ref[...]), jnp.sin(angr_ref[...])
    cc, sc = jnp.cos(angc_ref[...]), jnp.sin(angc_ref[...])
    gq = qkg_ref[:, 0:1]
    gk = qkg_ref[:, 1:2]
    zeros = jnp.zeros((HEAD_DIM, t), F32)
    qscale = QK_SCALE * LOG2E

    for hd in range(A_Q_HEADS):
        q = _rope_t(_head_norm_t(pt[hd * HEAD_DIM:(hd + 1) * HEAD_DIM], gq), cr, sr, cc, sc) * qscale
        halves = [q, zeros] if hd // A_GROUP == 0 else [zeros, q]
        qta_ref[hd] = jnp.concatenate(halves, axis=0).astype(BF16)

    c1 = A_Q_COLS
    c2 = c1 + A_KV_COLS
    c3 = c2 + A_KV_COLS
    c4 = c3 + B_QK_COLS
    c5 = c4 + B_QK_COLS
    kt = jnp.concatenate(
        [_rope_t(_head_norm_t(pt[c1 + j * HEAD_DIM:c1 + (j + 1) * HEAD_DIM], gk), cr, sr, cc, sc)
         for j in range(A_KV_HEADS)], axis=0)
    ka_ref[...] = kt.T.astype(BF16)
    for j in range(A_KV_HEADS):
        vta_ref[j] = pt[c2 + j * HEAD_DIM:c2 + (j + 1) * HEAD_DIM].astype(BF16)

    for cmb in range(2 * B_HEADS):
        q = pt[c3 + cmb * HEAD_DIM:c3 + (cmb + 1) * HEAD_DIM] * qscale
        halves = [q, zeros] if cmb % 2 == 0 else [zeros, q]
        qtd_ref[cmb] = jnp.concatenate(halves, axis=0).astype(BF16)
    kd_ref[...] = pt[c4:c5].T.astype(BF16)
    for hd in range(B_HEADS):
        vtd_ref[hd] = pt[c5 + hd * B_V_DIM:c5 + (hd + 1) * B_V_DIM].astype(BF16)


def _mix_in(h, ng, sh, sc, w_in_t, angr_t, angc_t, qkg_t):
    s, d = h.shape
    tm = MIX_ROWS
    nt = s // tm
    in_cols = w_in_t.shape[0]
    vec = pl.BlockSpec((1, d), lambda i: (0, 0))
    out_shape = (
        jax.ShapeDtypeStruct((A_Q_HEADS, 2 * HEAD_DIM, s), BF16),
        jax.ShapeDtypeStruct((s, A_KV_COLS), BF16),
        jax.ShapeDtypeStruct((A_KV_HEADS, nt, HEAD_DIM, tm), BF16),
        jax.ShapeDtypeStruct((2 * B_HEADS, 2 * HEAD_DIM, s), BF16),
        jax.ShapeDtypeStruct((s, B_QK_COLS), BF16),
        jax.ShapeDtypeStruct((B_HEADS, nt, B_V_DIM, tm), BF16),
    )
    out_specs = (
        pl.BlockSpec((A_Q_HEADS, 2 * HEAD_DIM, tm), lambda i: (0, 0, i)),
        pl.BlockSpec((tm, A_KV_COLS), lambda i: (i, 0)),
        pl.BlockSpec((A_KV_HEADS, None, HEAD_DIM, tm), lambda i: (0, i, 0, 0)),
        pl.BlockSpec((2 * B_HEADS, 2 * HEAD_DIM, tm), lambda i: (0, 0, i)),
        pl.BlockSpec((tm, B_QK_COLS), lambda i: (i, 0)),
        pl.BlockSpec((B_HEADS, None, B_V_DIM, tm), lambda i: (0, i, 0, 0)),
    )
    return pl.pallas_call(
        _mix_in_kernel,
        grid=(nt,),
        in_specs=[
            pl.BlockSpec((tm, d), lambda i: (i, 0)),
            vec, vec, vec,
            pl.BlockSpec((in_cols, d), lambda i: (0, 0)),
            pl.BlockSpec((ROPE_HALF, tm), lambda i: (0, i)),
            pl.BlockSpec((ROPE_HALF, tm), lambda i: (0, i)),
            pl.BlockSpec((HEAD_DIM, 2), lambda i: (0, 0)),
        ],
        out_specs=out_specs,
        out_shape=out_shape,
        compiler_params=_params("parallel"),
        name="mix_in",
    )(h, ng, sh, sc, w_in_t, angr_t, angc_t, qkg_t)


def _gqa_kernel(qt_ref, k_ref, vt_ref, o_ref, m_scr, l_scr, acc_scr):
    nk = k_ref.shape[0]
    m_scr[...] = jnp.full_like(m_scr, NEG_BIG)
    l_scr[...] = jnp.zeros_like(l_scr)
    acc_scr[...] = jnp.zeros_like(acc_scr)

    def body(kt, carry):
        k = k_ref[kt]
        vt = vt_ref[kt]
        for hh in range(A_GROUP):
            s = jnp.dot(k, qt_ref[hh], preferred_element_type=F32)
            m_old = m_scr[hh]
            m_new = jnp.maximum(m_old, jnp.max(s, axis=0, keepdims=True))
            alpha = jnp.exp2(m_old - m_new)
            p = jnp.exp2(s - m_new)
            l_scr[hh] = alpha * l_scr[hh] + jnp.sum(p, axis=0, keepdims=True)
            acc_scr[hh] = alpha * acc_scr[hh] + jnp.dot(vt, p.astype(BF16), preferred_element_type=F32)
            m_scr[hh] = m_new
        return carry

    lax.fori_loop(0, nk, body, 0)
    ot = jnp.concatenate([acc_scr[hh] / l_scr[hh] for hh in range(A_GROUP)], axis=0)
    o_ref[...] = ot.T.astype(BF16)


def _gqa(qta, ka3, vta):
    s = qta.shape[2]
    nk, tk, _ = ka3.shape
    tq = GQA_Q_TILE
    return pl.pallas_call(
        _gqa_kernel,
        grid=(A_KV_HEADS, s // tq),
        in_specs=[
            pl.BlockSpec((A_GROUP, 2 * HEAD_DIM, tq), lambda g, i: (g, 0, i)),
            pl.BlockSpec((nk, tk, A_KV_COLS), lambda g, i: (0, 0, 0)),
            pl.BlockSpec((None, nk, HEAD_DIM, tk), lambda g, i: (g, 0, 0, 0)),
        ],
        out_specs=pl.BlockSpec((tq, A_GROUP * HEAD_DIM), lambda g, i: (i, g)),
        out_shape=jax.ShapeDtypeStruct((s, A_Q_COLS), BF16),
        scratch_shapes=[
            pltpu.VMEM((A_GROUP, 1, tq), F32),
            pltpu.VMEM((A_GROUP, 1, tq), F32),
            pltpu.VMEM((A_GROUP, HEAD_DIM, tq), F32),
        ],
        compiler_params=_params("parallel", "parallel"),
        name="gqa_attn",
    )(qta, ka3, vta)


def _diff_kernel(nslope_ref, qt0_ref, qt1_ref, k0_ref, k1_ref, vt_ref, lamp_ref, sg_ref, o_ref,
                 m_scr, l_scr, acc_scr, *, lam_init):
    hd = pl.program_id(0)
    qi = pl.program_id(1)
    nk, tk, _ = k0_ref.shape
    tq = qt0_ref.shape[1]
    nslope = nslope_ref[hd]
    m_scr[...] = jnp.full_like(m_scr, NEG_BIG)
    l_scr[...] = jnp.zeros_like(l_scr)
    acc_scr[...] = jnp.zeros_like(acc_scr)
    rel = (lax.broadcasted_iota(jnp.int32, (tk, tq), 1)
           - lax.broadcasted_iota(jnp.int32, (tk, tq), 0)).astype(F32)
    q0 = (qi * tq).astype(F32)
    qts = (qt0_ref, qt1_ref)
    ks = (k0_ref, k1_ref)

    def body(kt, carry):
        off = q0 - (kt * tk).astype(F32)
        bias = jnp.abs(rel + off) * nslope
        vt = vt_ref[kt]
        for mp in range(2):
            s = jnp.dot(ks[mp][kt], qts[mp][...], preferred_element_type=F32) + bias
            m_old = m_scr[mp]
            m_new = jnp.maximum(m_old, jnp.max(s, axis=0, keepdims=True))
            alpha = jnp.exp2(m_old - m_new)
            p = jnp.exp2(s - m_new)
            l_scr[mp] = alpha * l_scr[mp] + jnp.sum(p, axis=0, keepdims=True)
            acc_scr[mp] = alpha * acc_scr[mp] + jnp.dot(vt, p.astype(BF16), preferred_element_type=F32)
            m_scr[mp] = m_new
        return carry

    lax.fori_loop(0, nk, body, 0)

    lp = lamp_ref[...]
    lam = (jnp.exp(jnp.sum(lp[0:1] * lp[1:2], axis=1, keepdims=True))
           - jnp.exp(jnp.sum(lp[2:3] * lp[3:4], axis=1, keepdims=True)) + lam_init)
    ot = acc_scr[0] / l_scr[0] - lam * (acc_scr[1] / l_scr[1])
    ot = ot * lax.rsqrt(jnp.mean(ot * ot, axis=0, keepdims=True) + EPS)
    ot = ot * sg_ref[...] * (1.0 - lam_init)
    o_ref[...] = ot.T.astype(BF16)


def _diff(nslope, qtd, kd3, vtd, lam_p, subln_col, *, lam_init):
    s = qtd.shape[2]
    nk, tk, _ = kd3.shape
    tq = DIFF_Q_TILE
    lanes = 2 * HEAD_DIM
    return pl.pallas_call(
        functools.partial(_diff_kernel, lam_init=lam_init),
        grid=(B_HEADS, s // tq),
        in_specs=[
            pl.BlockSpec(memory_space=pltpu.SMEM),
            pl.BlockSpec((None, lanes, tq), lambda h, i: (h, 0, i)),
            pl.BlockSpec((None, lanes, tq), lambda h, i: (B_HEADS + h, 0, i)),
            pl.BlockSpec((nk, tk, lanes), lambda h, i: (0, 0, h // 2)),
            pl.BlockSpec((nk, tk, lanes), lambda h, i: (0, 0, B_HEADS // 2 + h // 2)),
            pl.BlockSpec((None, nk, B_V_DIM, tk), lambda h, i: (h, 0, 0, 0)),
            pl.BlockSpec((4, HEAD_DIM), lambda h, i: (0, 0)),
            pl.BlockSpec((B_V_DIM, 1), lambda h, i: (0, 0)),
        ],
        out_specs=pl.BlockSpec((tq, B_V_DIM), lambda h, i: (i, h)),
        out_shape=jax.ShapeDtypeStruct((s, B_V_COLS), BF16),
        scratch_shapes=[
            pltpu.VMEM((2, 1, tq), F32),
            pltpu.VMEM((2, 1, tq), F32),
            pltpu.VMEM((2, B_V_DIM, tq), F32),
        ],
        compiler_params=_params("parallel", "parallel"),
        name="diff_attn",
    )(nslope, qtd, qtd, kd3, kd3, vtd, lam_p, subln_col)


def _mix_out_kernel(h_ref, ng_ref, sh_ref, sc_ref, gt_ref, oa_ref, od_ref, wba_ref, wbb_ref,
                    wgate_ref, bgate_ref, wo_ref, o_ref):
    h = h_ref[...]
    d = h.shape[1]
    n = _rms_rows(h) * ng_ref[...]
    n = (n * (1.0 + sc_ref[...]) + sh_ref[...]).astype(BF16)
    z = jnp.dot(n, wgate_ref[...], preferred_element_type=F32) + bgate_ref[...]
    g = 1.0 / (1.0 + jnp.exp(-z))
    ya = jnp.dot(oa_ref[...], wba_ref[...], preferred_element_type=F32)
    yb = jnp.dot(od_ref[...], wbb_ref[...], preferred_element_type=F32)
    mix = (g[:, :d] * ya + g[:, d:] * yb).astype(BF16)
    y = jnp.dot(mix, wo_ref[...], preferred_element_type=F32)
    o_ref[...] = h + gt_ref[...] * y


def _mix_out(h, ng, sh, sc, gt, oa, od, w_ba, w_bb, w_gate, b_gate, w_o):
    s, d = h.shape
    tm = MIX_ROWS
    vec = pl.BlockSpec((1, d), lambda i: (0, 0))

    def full(a):
        return pl.BlockSpec(a.shape, lambda i: (0,) * a.ndim)

    return pl.pallas_call(
        _mix_out_kernel,
        grid=(s // tm,),
        in_specs=[
            pl.BlockSpec((tm, d), lambda i: (i, 0)),
            vec, vec, vec, vec,
            pl.BlockSpec((tm, oa.shape[1]), lambda i: (i, 0)),
            pl.BlockSpec((tm, od.shape[1]), lambda i: (i, 0)),
            full(w_ba), full(w_bb), full(w_gate), full(b_gate), full(w_o),
        ],
        out_specs=pl.BlockSpec((tm, d), lambda i: (i, 0)),
        out_shape=jax.ShapeDtypeStruct((s, d), F32),
        compiler_params=_params("parallel"),
        name="mix_out",
    )(h, ng, sh, sc, gt, oa, od, w_ba, w_bb, w_gate, b_gate, w_o)


def _axial_angles_t(seq):
    rows = seq // GRID_W
    row = jnp.broadcast_to(jnp.arange(rows)[:, None], (rows, GRID_W)).reshape(seq) - rows // 2
    col = jnp.broadcast_to(jnp.arange(GRID_W)[None, :], (rows, GRID_W)).reshape(seq) - GRID_W // 2
    inv = 1.0 / (ROPE_THETA ** (jnp.arange(0, ROPE_AXIS_DIM, 2, dtype=F32) / ROPE_AXIS_DIM))
    return inv[:, None] * row.astype(F32)[None, :], inv[:, None] * col.astype(F32)[None, :]


def kernel(x, c, ada_w, ada_b, norm_g, ffn_wg, ffn_wu, ffn_wd, w_in, qk_g, lam_p, subln_g, w_ba, w_bb,
           w_gate, b_gate, w_o, final_g):
    batch, s, d = x.shape
    assert batch == 1 and s % KEY_TILE == 0 and MIX_ROWS == KEY_TILE
    depth = ada_w.shape[0]
    h = x.reshape(s, d)

    mod = _ada_mod(c.reshape(d, 1), ada_w, ada_b)
    angr_t, angc_t = _axial_angles_t(s)
    slopes = 2.0 ** (-8.0 * jnp.arange(1, B_HEADS + 1, dtype=F32) / B_HEADS)
    nslope = -slopes * LOG2E
    fg = final_g.reshape(1, d)

    for l in range(depth):
        sh1, sc1, g1, sh2, sc2, g2, sh3, sc3, g3 = [mod[l, :, j * d:(j + 1) * d] for j in range(N_ADA)]
        lam_init = 0.8 - 0.6 * math.exp(-0.3 * l)
        ng = norm_g[l].reshape(3, 1, d)
        wg, wu, wd = ffn_wg[l].astype(BF16), ffn_wu[l].astype(BF16), ffn_wd[l].astype(BF16)

        h = _ffn(h, ng[0], sh1, sc1, g1, wg[0], wu[0], wd[0], fg, final_norm=False)

        qta, ka, vta, qtd, kd, vtd = _mix_in(
            h, ng[1], sh2, sc2, w_in[l].T.astype(BF16), angr_t, angc_t, qk_g[l].T)
        nk = s // KEY_TILE
        oa = _gqa(qta, ka.reshape(nk, KEY_TILE, A_KV_COLS), vta)
        od = _diff(nslope, qtd, kd.reshape(nk, KEY_TILE, B_QK_COLS), vtd, lam_p[l],
                   subln_g[l].reshape(B_V_DIM, 1), lam_init=lam_init)
        h = _mix_out(h, ng[1], sh2, sc2, g2, oa, od, w_ba[l].astype(BF16), w_bb[l].astype(BF16),
                     w_gate[l].astype(BF16), b_gate[l].reshape(1, -1), w_o[l].astype(BF16))

        h = _ffn(h, ng[2], sh3, sc3, g3, wg[1], wu[1], wd[1], fg, final_norm=(l == depth - 1))
    return h.reshape(batch, s, d)
```

```python
import functools
import math

import jax
import jax.numpy as jnp
from jax import lax
from jax.experimental import pallas as pl
from jax.experimental.pallas import tpu as pltpu

F32 = jnp.float32
BF16 = jnp.bfloat16

GRID_W = 64
HEAD_DIM = 64
A_Q_HEADS = 8
A_KV_HEADS = 2
A_GROUP = A_Q_HEADS // A_KV_HEADS
B_HEADS = 4
B_V_DIM = 2 * HEAD_DIM
A_Q_COLS = A_Q_HEADS * HEAD_DIM
A_KV_COLS = A_KV_HEADS * HEAD_DIM
B_QK_COLS = 2 * B_HEADS * HEAD_DIM
B_V_COLS = B_HEADS * B_V_DIM
N_ADA = 9
EPS = 1e-6
ROPE_THETA = 10000.0
ROPE_AXIS_DIM = HEAD_DIM // 2
ROPE_HALF = ROPE_AXIS_DIM // 2

LOG2E = math.log2(math.e)
QK_SCALE = HEAD_DIM ** -0.5
NEG_BIG = -1e30

VMEM_LIMIT_BYTES = 56 * 1024 * 1024

FFN_ROWS = 512
MIX_ROWS = 512
KEY_TILE = 512
GQA_Q_TILE = 256
DIFF_Q_TILE = 512
ADA_COLS = 1152


def _params(*sem):
    return pltpu.CompilerParams(dimension_semantics=sem, vmem_limit_bytes=VMEM_LIMIT_BYTES)


def _rms_rows(x):
    return x * lax.rsqrt(jnp.mean(x * x, axis=-1, keepdims=True) + EPS)


def _ada_kernel(c_ref, w_ref, b_ref, o_ref):
    c = c_ref[...]
    act = c / (1.0 + jnp.exp(-c))
    o_ref[...] = jnp.sum(w_ref[...] * act, axis=0, keepdims=True) + b_ref[...]


def _ada_mod(c_col, ada_w, ada_b):
    n_layers, d, n = ada_w.shape
    return pl.pallas_call(
        _ada_kernel,
        grid=(n_layers, n // ADA_COLS),
        in_specs=[
            pl.BlockSpec((d, 1), lambda l, j: (0, 0)),
            pl.BlockSpec((None, d, ADA_COLS), lambda l, j: (l, 0, j)),
            pl.BlockSpec((None, 1, ADA_COLS), lambda l, j: (l, 0, j)),
        ],
        out_specs=pl.BlockSpec((None, 1, ADA_COLS), lambda l, j: (l, 0, j)),
        out_shape=jax.ShapeDtypeStruct((n_layers, 1, n), F32),
        compiler_params=_params("parallel", "parallel"),
        name="ada_mod",
    )(c_col, ada_w, ada_b.reshape(n_layers, 1, n))


def _ffn_kernel(h_ref, ng_ref, sh_ref, sc_ref, gt_ref, wg_ref, wu_ref, wd_ref, fg_ref, o_ref,
                n_scr, acc_scr, *, final_norm):
    f = pl.program_id(1)

    @pl.when(f == 0)
    def _():
        n = _rms_rows(h_ref[...]) * ng_ref[...]
        n = n * (1.0 + sc_ref[...]) + sh_ref[...]
        n_scr[...] = n.astype(BF16)
        acc_scr[...] = jnp.zeros_like(acc_scr)

    n = n_scr[...]
    hg = jnp.dot(n, wg_ref[...], preferred_element_type=F32)
    hu = jnp.dot(n, wu_ref[...], preferred_element_type=F32)
    a = (hg / (1.0 + jnp.exp(-hg))) * hu
    acc_scr[...] += jnp.dot(a.astype(BF16), wd_ref[...], preferred_element_type=F32)

    @pl.when(f == pl.num_programs(1) - 1)
    def _():
        out = h_ref[...] + (0.5 * gt_ref[...]) * acc_scr[...]
        if final_norm:
            out = _rms_rows(out) * fg_ref[...]
        o_ref[...] = out


def _ffn(h, ng, sh, sc, gt, wg, wu, wd, fg, *, final_norm):
    s, d = h.shape
    f_dim = wg.shape[1]
    tf = f_dim // 2
    vec = pl.BlockSpec((1, d), lambda i, f: (0, 0))
    return pl.pallas_call(
        functools.partial(_ffn_kernel, final_norm=final_norm),
        grid=(s // FFN_ROWS, f_dim // tf),
        in_specs=[
            pl.BlockSpec((FFN_ROWS, d), lambda i, f: (i, 0)),
            vec, vec, vec, vec,
            pl.BlockSpec((d, tf), lambda i, f: (0, f)),
            pl.BlockSpec((d, tf), lambda i, f: (0, f)),
            pl.BlockSpec((tf, d), lambda i, f: (f, 0)),
            vec,
        ],
        out_specs=pl.BlockSpec((FFN_ROWS, d), lambda i, f: (i, 0)),
        out_shape=jax.ShapeDtypeStruct((s, d), F32),
        scratch_shapes=[pltpu.VMEM((FFN_ROWS, d), BF16), pltpu.VMEM((FFN_ROWS, d), F32)],
        compiler_params=_params("parallel", "arbitrary"),
        name="ffn_final" if final_norm else "ffn",
    )(h, ng, sh, sc, gt, wg, wu, wd, fg)


def _rope_t(x, cr, sr, cc, sc):
    h = ROPE_HALF
    x1r, x2r, x1c, x2c = x[0:h], x[h:2 * h], x[2 * h:3 * h], x[3 * h:4 * h]
    return jnp.concatenate(
        [x1r * cr - x2r * sr, x2r * cr + x1r * sr, x1c * cc - x2c * sc, x2c * cc + x1c * sc], axis=0)


def _head_norm_t(x, g_col):
    ms = jnp.mean(x * x, axis=0, keepdims=True)
    return x * lax.rsqrt(ms + EPS) * g_col


def _mix_in_kernel(h_ref, ng_ref, sh_ref, sc_ref, wt_ref, angr_ref, angc_ref, qkg_ref,
                   qta_ref, ka_ref, vta_ref, qtd_ref, kd_ref, vtd_ref):
    n = _rms_rows(h_ref[...]) * ng_ref[...]
    n = (n * (1.0 + sc_ref[...]) + sh_ref[...]).astype(BF16)
    pt = lax.dot_general(wt_ref[...], n, (((1,), (1,)), ((), ())), preferred_element_type=F32)
    t = pt.shape[1]

    cr, sr = jnp.cos(angr_ref[...]), jnp.sin(angr_ref[...])
    cc, sc = jnp.cos(angc_ref[...]), jnp.sin(angc_ref[...])
    gq = qkg_ref[:, 0:1]
    gk = qkg_ref[:, 1:2]
    zeros = jnp.zeros((HEAD_DIM, t), F32)
    qscale = QK_SCALE * LOG2E

    for hd in range(A_Q_HEADS):
        q = _rope_t(_head_norm_t(pt[hd * HEAD_DIM:(hd + 1) * HEAD_DIM], gq), cr, sr, cc, sc) * qscale
        halves = [q, zeros] if hd // A_GROUP == 0 else [zeros, q]
        qta_ref[hd] = jnp.concatenate(halves, axis=0).astype(BF16)

    c1 = A_Q_COLS
    c2 = c1 + A_KV_COLS
    c3 = c2 + A_KV_COLS
    c4 = c3 + B_QK_COLS
    c5 = c4 + B_QK_COLS
    kt = jnp.concatenate(
        [_rope_t(_head_norm_t(pt[c1 + j * HEAD_DIM:c1 + (j + 1) * HEAD_DIM], gk), cr, sr, cc, sc)
         for j in range(A_KV_HEADS)], axis=0)
    ka_ref[...] = kt.T.astype(BF16)
    for j in range(A_KV_HEADS):
        vta_ref[j] = pt[c2 + j * HEAD_DIM:c2 + (j + 1) * HEAD_DIM].astype(BF16)

    for cmb in range(2 * B_HEADS):
        q = pt[c3 + cmb * HEAD_DIM:c3 + (cmb + 1) * HEAD_DIM] * qscale
        halves = [q, zeros] if cmb % 2 == 0 else [zeros, q]
        qtd_ref[cmb] = jnp.concatenate(halves, axis=0).astype(BF16)
    kd_ref[...] = pt[c4:c5].T.astype(BF16)
    for hd in range(B_HEADS):
        vtd_ref[hd] = pt[c5 + hd * B_V_DIM:c5 + (hd + 1) * B_V_DIM].astype(BF16)


def _mix_in(h, ng, sh, sc, w_in_t, angr_t, angc_t, qkg_t):
    s, d = h.shape
    tm = MIX_ROWS
    nt = s // tm
    in_cols = w_in_t.shape[0]
    vec = pl.BlockSpec((1, d), lambda i: (0, 0))
    out_shape = (
        jax.ShapeDtypeStruct((A_Q_HEADS, 2 * HEAD_DIM, s), BF16),
        jax.ShapeDtypeStruct((s, A_KV_COLS), BF16),
        jax.ShapeDtypeStruct((A_KV_HEADS, nt, HEAD_DIM, tm), BF16),
        jax.ShapeDtypeStruct((2 * B_HEADS, 2 * HEAD_DIM, s), BF16),
        jax.ShapeDtypeStruct((s, B_QK_COLS), BF16),
        jax.ShapeDtypeStruct((B_HEADS, nt, B_V_DIM, tm), BF16),
    )
    out_specs = (
        pl.BlockSpec((A_Q_HEADS, 2 * HEAD_DIM, tm), lambda i: (0, 0, i)),
        pl.BlockSpec((tm, A_KV_COLS), lambda i: (i, 0)),
        pl.BlockSpec((A_KV_HEADS, None, HEAD_DIM, tm), lambda i: (0, i, 0, 0)),
        pl.BlockSpec((2 * B_HEADS, 2 * HEAD_DIM, tm), lambda i: (0, 0, i)),
        pl.BlockSpec((tm, B_QK_COLS), lambda i: (i, 0)),
        pl.BlockSpec((B_HEADS, None, B_V_DIM, tm), lambda i: (0, i, 0, 0)),
    )
    return pl.pallas_call(
        _mix_in_kernel,
        grid=(nt,),
        in_specs=[
            pl.BlockSpec((tm, d), lambda i: (i, 0)),
            vec, vec, vec,
            pl.BlockSpec((in_cols, d), lambda i: (0, 0)),
            pl.BlockSpec((ROPE_HALF, tm), lambda i: (0, i)),
            pl.BlockSpec((ROPE_HALF, tm), lambda i: (0, i)),
            pl.BlockSpec((HEAD_DIM, 2), lambda i: (0, 0)),
        ],
        out_specs=out_specs,
        out_shape=out_shape,
        compiler_params=_params("parallel"),
        name="mix_in",
    )(h, ng, sh, sc, w_in_t, angr_t, angc_t, qkg_t)


def _softmax_tile(s, m_ref, l_ref, alpha_ref, p_ref):
    m_old = m_ref[...]
    m_new = jnp.maximum(m_old, jnp.max(s, axis=0, keepdims=True))
    alpha = jnp.exp2(m_old - m_new)
    p = jnp.exp2(s - m_new)
    l_ref[...] = alpha * l_ref[...] + jnp.sum(p, axis=0, keepdims=True)
    m_ref[...] = m_new
    alpha_ref[...] = alpha
    p_ref[...] = p.astype(BF16)


def _gqa_kernel(qt_ref, k_ref, vt_ref, o_ref, q_scr, s_buf, p_buf, alpha_buf, m_scr, l_scr, acc_scr):
    nk = k_ref.shape[0]
    tq = qt_ref.shape[2]
    for hh in range(A_GROUP):
        q_scr[:, hh * tq:(hh + 1) * tq] = qt_ref[hh]
    m_scr[...] = jnp.full_like(m_scr, NEG_BIG)
    l_scr[...] = jnp.zeros_like(l_scr)
    acc_scr[...] = jnp.zeros_like(acc_scr)
    p_buf[1] = jnp.zeros(p_buf.shape[1:], BF16)
    alpha_buf[1] = jnp.ones(alpha_buf.shape[1:], F32)

    def qk(kt, slot):
        s_buf[slot] = jnp.dot(k_ref[kt], q_scr[...], preferred_element_type=F32)

    def softmax(slot):
        _softmax_tile(s_buf[slot], m_scr, l_scr, alpha_buf.at[slot], p_buf.at[slot])

    def pv(kt, slot):
        acc_scr[...] = alpha_buf[slot] * acc_scr[...] + jnp.dot(
            vt_ref[kt], p_buf[slot], preferred_element_type=F32)

    qk(0, 0)

    def body(j, carry):
        a = 2 * j
        qk(a + 1, 1)
        softmax(0)
        pv(jnp.maximum(a - 1, 0), 1)
        qk(jnp.minimum(a + 2, nk - 1), 0)
        softmax(1)
        pv(a, 0)
        return carry

    lax.fori_loop(0, nk // 2, body, 0)
    pv(nk - 1, 1)
    inv_l = 1.0 / l_scr[...]
    ot = acc_scr[...] * inv_l
    ot = jnp.concatenate([ot[:, hh * tq:(hh + 1) * tq] for hh in range(A_GROUP)], axis=0)
    o_ref[...] = ot.T.astype(BF16)


def _gqa(qta, ka3, vta):
    s = qta.shape[2]
    nk, tk, _ = ka3.shape
    assert nk % 2 == 0
    tq = GQA_Q_TILE
    nq = A_GROUP * tq
    return pl.pallas_call(
        _gqa_kernel,
        grid=(A_KV_HEADS, s // tq),
        in_specs=[
            pl.BlockSpec((A_GROUP, 2 * HEAD_DIM, tq), lambda g, i: (g, 0, i)),
            pl.BlockSpec((nk, tk, A_KV_COLS), lambda g, i: (0, 0, 0)),
            pl.BlockSpec((None, nk, HEAD_DIM, tk), lambda g, i: (g, 0, 0, 0)),
        ],
        out_specs=pl.BlockSpec((tq, A_GROUP * HEAD_DIM), lambda g, i: (i, g)),
        out_shape=jax.ShapeDtypeStruct((s, A_Q_COLS), BF16),
        scratch_shapes=[
            pltpu.VMEM((2 * HEAD_DIM, nq), BF16),
            pltpu.VMEM((2, tk, nq), F32),
            pltpu.VMEM((2, tk, nq), BF16),
            pltpu.VMEM((2, 1, nq), F32),
            pltpu.VMEM((1, nq), F32),
            pltpu.VMEM((1, nq), F32),
            pltpu.VMEM((HEAD_DIM, nq), F32),
        ],
        compiler_params=_params("parallel", "parallel"),
        name="gqa_attn",
    )(qta, ka3, vta)


def _diff_kernel(nslope_ref, qt0_ref, qt1_ref, k0_ref, k1_ref, vt_ref, lamp_ref, sg_ref, o_ref,
                 rel_scr, s_buf, p_buf, alpha_buf, m_scr, l_scr, acc_scr, *, lam_init):
    hd = pl.program_id(0)
    qi = pl.program_id(1)
    nk, tk, _ = k0_ref.shape
    tq = qt0_ref.shape[1]
    nslope = nslope_ref[hd]
    m_scr[...] = jnp.full_like(m_scr, NEG_BIG)
    l_scr[...] = jnp.zeros_like(l_scr)
    acc_scr[...] = jnp.zeros_like(acc_scr)
    p_buf[1] = jnp.zeros(p_buf.shape[1:], BF16)
    alpha_buf[1] = jnp.ones(alpha_buf.shape[1:], F32)
    rel_scr[...] = (lax.broadcasted_iota(jnp.int32, (tk, tq), 1)
                    - lax.broadcasted_iota(jnp.int32, (tk, tq), 0)).astype(F32)
    q0 = qi * tq
    qts = (qt0_ref, qt1_ref)
    ks = (k0_ref, k1_ref)

    def qk(kt, slot):
        for mp in range(2):
            s_buf[slot, mp] = jnp.dot(ks[mp][kt], qts[mp][...], preferred_element_type=F32)

    def softmax(kt, slot):
        off = (q0 - kt * tk).astype(F32)
        bias = jnp.abs(rel_scr[...] + off) * nslope
        for mp in range(2):
            _softmax_tile(s_buf[slot, mp] + bias, m_scr.at[mp], l_scr.at[mp],
                          alpha_buf.at[slot, mp], p_buf.at[slot, mp])

    def pv(kt, slot):
        vt = vt_ref[kt]
        for mp in range(2):
            acc_scr[mp] = alpha_buf[slot, mp] * acc_scr[mp] + jnp.dot(
                vt, p_buf[slot, mp], preferred_element_type=F32)

    qk(0, 0)

    def body(j, carry):
        a = 2 * j
        qk(a + 1, 1)
        softmax(a, 0)
        pv(jnp.maximum(a - 1, 0), 1)
        qk(jnp.minimum(a + 2, nk - 1), 0)
        softmax(a + 1, 1)
        pv(a, 0)
        return carry

    lax.fori_loop(0, nk // 2, body, 0)
    pv(nk - 1, 1)

    lp = lamp_ref[...]
    lam = (jnp.exp(jnp.sum(lp[0:1] * lp[1:2], axis=1, keepdims=True))
           - jnp.exp(jnp.sum(lp[2:3] * lp[3:4], axis=1, keepdims=True)) + lam_init)
    ot = acc_scr[0] / l_scr[0] - lam * (acc_scr[1] / l_scr[1])
    ot = ot * lax.rsqrt(jnp.mean(ot * ot, axis=0, keepdims=True) + EPS)
    ot = ot * sg_ref[...] * (1.0 - lam_init)
    o_ref[...] = ot.T.astype(BF16)


def _diff(nslope, qtd, kd3, vtd, lam_p, subln_col, *, lam_init):
    s = qtd.shape[2]
    nk, tk, _ = kd3.shape
    tq = DIFF_Q_TILE
    lanes = 2 * HEAD_DIM
    return pl.pallas_call(
        functools.partial(_diff_kernel, lam_init=lam_init),
        grid=(B_HEADS, s // tq),
        in_specs=[
            pl.BlockSpec(memory_space=pltpu.SMEM),
            pl.BlockSpec((None, lanes, tq), lambda h, i: (h, 0, i)),
            pl.BlockSpec((None, lanes, tq), lambda h, i: (B_HEADS + h, 0, i)),
            pl.BlockSpec((nk, tk, lanes), lambda h, i: (0, 0, h // 2)),
            pl.BlockSpec((nk, tk, lanes), lambda h, i: (0, 0, B_HEADS // 2 + h // 2)),
            pl.BlockSpec((None, nk, B_V_DIM, tk), lambda h, i: (h, 0, 0, 0)),
            pl.BlockSpec((4, HEAD_DIM), lambda h, i: (0, 0)),
            pl.BlockSpec((B_V_DIM, 1), lambda h, i: (0, 0)),
        ],
        out_specs=pl.BlockSpec((tq, B_V_DIM), lambda h, i: (i, h)),
        out_shape=jax.ShapeDtypeStruct((s, B_V_COLS), BF16),
        scratch_shapes=[
            pltpu.VMEM((tk, tq), F32),
            pltpu.VMEM((2, 2, tk, tq), F32),
            pltpu.VMEM((2, 2, tk, tq), BF16),
            pltpu.VMEM((2, 2, 1, tq), F32),
            pltpu.VMEM((2, 1, tq), F32),
            pltpu.VMEM((2, 1, tq), F32),
            pltpu.VMEM((2, B_V_DIM, tq), F32),
        ],
        compiler_params=_params("parallel", "parallel"),
        name="diff_attn",
    )(nslope, qtd, qtd, kd3, kd3, vtd, lam_p, subln_col)


def _mix_out_kernel(h_ref, ng_ref, sh_ref, sc_ref, gt_ref, oa_ref, od_ref, wba_ref, wbb_ref,
                    wgate_ref, bgate_ref, wo_ref, o_ref):
    h = h_ref[...]
    d = h.shape[1]
    n = _rms_rows(h) * ng_ref[...]
    n = (n * (1.0 + sc_ref[...]) + sh_ref[...]).astype(BF16)
    z = jnp.dot(n, wgate_ref[...], preferred_element_type=F32) + bgate_ref[...]
    g = 1.0 / (1.0 + jnp.exp(-z))
    ya = jnp.dot(oa_ref[...], wba_ref[...], preferred_element_type=F32)
    yb = jnp.dot(od_ref[...], wbb_ref[...], preferred_element_type=F32)
    mix = (g[:, :d] * ya + g[:, d:] * yb).astype(BF16)
    y = jnp.dot(mix, wo_ref[...], preferred_element_type=F32)
    o_ref[...] = h + gt_ref[...] * y


def _mix_out(h, ng, sh, sc, gt, oa, od, w_ba, w_bb, w_gate, b_gate, w_o):
    s, d = h.shape
    tm = MIX_ROWS
    vec = pl.BlockSpec((1, d), lambda i: (0, 0))

    def full(a):
        return pl.BlockSpec(a.shape, lambda i: (0,) * a.ndim)

    return pl.pallas_call(
        _mix_out_kernel,
        grid=(s // tm,),
        in_specs=[
            pl.BlockSpec((tm, d), lambda i: (i, 0)),
            vec, vec, vec, vec,
            pl.BlockSpec((tm, oa.shape[1]), lambda i: (i, 0)),
            pl.BlockSpec((tm, od.shape[1]), lambda i: (i, 0)),
            full(w_ba), full(w_bb), full(w_gate), full(b_gate), full(w_o),
        ],
        out_specs=pl.BlockSpec((tm, d), lambda i: (i, 0)),
        out_shape=jax.ShapeDtypeStruct((s, d), F32),
        compiler_params=_params("parallel"),
        name="mix_out",
    )(h, ng, sh, sc, gt, oa, od, w_ba, w_bb, w_gate, b_gate, w_o)


def _axial_angles_t(seq):
    rows = seq // GRID_W
    row = jnp.broadcast_to(jnp.arange(rows)[:, None], (rows, GRID_W)).reshape(seq) - rows // 2
    col = jnp.broadcast_to(jnp.arange(GRID_W)[None, :], (rows, GRID_W)).reshape(seq) - GRID_W // 2
    inv = 1.0 / (ROPE_THETA ** (jnp.arange(0, ROPE_AXIS_DIM, 2, dtype=F32) / ROPE_AXIS_DIM))
    return inv[:, None] * row.astype(F32)[None, :], inv[:, None] * col.astype(F32)[None, :]


def kernel(x, c, ada_w, ada_b, norm_g, ffn_wg, ffn_wu, ffn_wd, w_in, qk_g, lam_p, subln_g, w_ba, w_bb,
           w_gate, b_gate, w_o, final_g):
    batch, s, d = x.shape
    assert batch == 1 and s % KEY_TILE == 0 and MIX_ROWS == KEY_TILE
    depth = ada_w.shape[0]
    h = x.reshape(s, d)

    mod = _ada_mod(c.reshape(d, 1), ada_w, ada_b)
    angr_t, angc_t = _axial_angles_t(s)
    slopes = 2.0 ** (-8.0 * jnp.arange(1, B_HEADS + 1, dtype=F32) / B_HEADS)
    nslope = -slopes * LOG2E
    fg = final_g.reshape(1, d)

    for l in range(depth):
        sh1, sc1, g1, sh2, sc2, g2, sh3, sc3, g3 = [mod[l, :, j * d:(j + 1) * d] for j in range(N_ADA)]
        lam_init = 0.8 - 0.6 * math.exp(-0.3 * l)
        ng = norm_g[l].reshape(3, 1, d)
        wg, wu, wd = ffn_wg[l].astype(BF16), ffn_wu[l].astype(BF16), ffn_wd[l].astype(BF16)

        h = _ffn(h, ng[0], sh1, sc1, g1, wg[0], wu[0], wd[0], fg, final_norm=False)

        qta, ka, vta, qtd, kd, vtd = _mix_in(
            h, ng[1], sh2, sc2, w_in[l].T.astype(BF16), angr_t, angc_t, qk_g[l].T)
        nk = s // KEY_TILE
        oa = _gqa(qta, ka.reshape(nk, KEY_TILE, A_KV_COLS), vta)
        od = _diff(nslope, qtd, kd.reshape(nk, KEY_TILE, B_QK_COLS), vtd, lam_p[l],
                   subln_g[l].reshape(B_V_DIM, 1), lam_init=lam_init)
        h = _mix_out(h, ng[1], sh2, sc2, g2, oa, od, w_ba[l].astype(BF16), w_bb[l].astype(BF16),
                     w_gate[l].astype(BF16), b_gate[l].reshape(1, -1), w_o[l].astype(BF16))

        h = _ffn(h, ng[2], sh3, sc3, g3, wg[1], wu[1], wd[1], fg, final_norm=(l == depth - 1))
    return h.reshape(batch, s, d)
```

```python
import functools
import math

import jax
import jax.numpy as jnp
from jax import lax
from jax.experimental import pallas as pl
from jax.experimental.pallas import tpu as pltpu

F32 = jnp.float32
BF16 = jnp.bfloat16

GRID_W = 64
HEAD_DIM = 64
A_Q_HEADS = 8
A_KV_HEADS = 2
A_GROUP = A_Q_HEADS // A_KV_HEADS
B_HEADS = 4
B_V_DIM = 2 * HEAD_DIM
A_Q_COLS = A_Q_HEADS * HEAD_DIM
A_KV_COLS = A_KV_HEADS * HEAD_DIM
B_QK_COLS = 2 * B_HEADS * HEAD_DIM
B_V_COLS = B_HEADS * B_V_DIM
N_ADA = 9
EPS = 1e-6
ROPE_THETA = 10000.0
ROPE_AXIS_DIM = HEAD_DIM // 2
ROPE_HALF = ROPE_AXIS_DIM // 2

LOG2E = math.log2(math.e)
QK_SCALE = HEAD_DIM ** -0.5
NEG_BIG = -1e30

BF16_SUBLANES = 16
LANES = 128
ONES_ROWS = BF16_SUBLANES
SKIP_LOG2 = 160.0

VMEM_LIMIT_BYTES = 56 * 1024 * 1024

FFN_ROWS = 512
MIX_ROWS = 512
KEY_TILE = 512
GQA_Q_TILE = 256
DIFF_Q_TILE = 512
ADA_COLS = 1152


def _params(*sem):
    return pltpu.CompilerParams(dimension_semantics=sem, vmem_limit_bytes=VMEM_LIMIT_BYTES)


def _rms_rows(x):
    return x * lax.rsqrt(jnp.mean(x * x, axis=-1, keepdims=True) + EPS)


def _ada_kernel(c_ref, w_ref, b_ref, o_ref):
    c = c_ref[...]
    act = c / (1.0 + jnp.exp(-c))
    o_ref[...] = jnp.sum(w_ref[...] * act, axis=0, keepdims=True) + b_ref[...]


def _ada_mod(c_col, ada_w, ada_b):
    n_layers, d, n = ada_w.shape
    return pl.pallas_call(
        _ada_kernel,
        grid=(n_layers, n // ADA_COLS),
        in_specs=[
            pl.BlockSpec((d, 1), lambda l, j: (0, 0)),
            pl.BlockSpec((None, d, ADA_COLS), lambda l, j: (l, 0, j)),
            pl.BlockSpec((None, 1, ADA_COLS), lambda l, j: (l, 0, j)),
        ],
        out_specs=pl.BlockSpec((None, 1, ADA_COLS), lambda l, j: (l, 0, j)),
        out_shape=jax.ShapeDtypeStruct((n_layers, 1, n), F32),
        compiler_params=_params("parallel", "parallel"),
        name="ada_mod",
    )(c_col, ada_w, ada_b.reshape(n_layers, 1, n))


def _ffn_kernel(h_ref, ng_ref, sh_ref, sc_ref, gt_ref, wg_ref, wu_ref, wd_ref, fg_ref, o_ref,
                n_scr, acc_scr, *, final_norm):
    f = pl.program_id(1)

    @pl.when(f == 0)
    def _():
        n = _rms_rows(h_ref[...]) * ng_ref[...]
        n = n * (1.0 + sc_ref[...]) + sh_ref[...]
        n_scr[...] = n.astype(BF16)
        acc_scr[...] = jnp.zeros_like(acc_scr)

    n = n_scr[...]
    hg = jnp.dot(n, wg_ref[...], preferred_element_type=F32)
    hu = jnp.dot(n, wu_ref[...], preferred_element_type=F32)
    a = (hg / (1.0 + jnp.exp(-hg))) * hu
    acc_scr[...] += jnp.dot(a.astype(BF16), wd_ref[...], preferred_element_type=F32)

    @pl.when(f == pl.num_programs(1) - 1)
    def _():
        out = h_ref[...] + (0.5 * gt_ref[...]) * acc_scr[...]
        if final_norm:
            out = _rms_rows(out) * fg_ref[...]
        o_ref[...] = out


def _ffn(h, ng, sh, sc, gt, wg, wu, wd, fg, *, final_norm):
    s, d = h.shape
    f_dim = wg.shape[1]
    tf = f_dim // 2
    vec = pl.BlockSpec((1, d), lambda i, f: (0, 0))
    return pl.pallas_call(
        functools.partial(_ffn_kernel, final_norm=final_norm),
        grid=(s // FFN_ROWS, f_dim // tf),
        in_specs=[
            pl.BlockSpec((FFN_ROWS, d), lambda i, f: (i, 0)),
            vec, vec, vec, vec,
            pl.BlockSpec((d, tf), lambda i, f: (0, f)),
            pl.BlockSpec((d, tf), lambda i, f: (0, f)),
            pl.BlockSpec((tf, d), lambda i, f: (f, 0)),
            vec,
        ],
        out_specs=pl.BlockSpec((FFN_ROWS, d), lambda i, f: (i, 0)),
        out_shape=jax.ShapeDtypeStruct((s, d), F32),
        scratch_shapes=[pltpu.VMEM((FFN_ROWS, d), BF16), pltpu.VMEM((FFN_ROWS, d), F32)],
        compiler_params=_params("parallel", "arbitrary"),
        name="ffn_final" if final_norm else "ffn",
    )(h, ng, sh, sc, gt, wg, wu, wd, fg)


def _rope_t(x, cr, sr, cc, sc):
    h = ROPE_HALF
    x1r, x2r, x1c, x2c = x[0:h], x[h:2 * h], x[2 * h:3 * h], x[3 * h:4 * h]
    return jnp.concatenate(
        [x1r * cr - x2r * sr, x2r * cr + x1r * sr, x1c * cc - x2c * sc, x2c * cc + x1c * sc], axis=0)


def _head_norm_t(x, g_col):
    ms = jnp.mean(x * x, axis=0, keepdims=True)
    return x * lax.rsqrt(ms + EPS) * g_col


def _max_sq_norm(x_bf16):
    xf = x_bf16.astype(F32)
    n2 = jnp.sum(xf * xf, axis=0, keepdims=True)
    return jnp.broadcast_to(jnp.max(n2, axis=1, keepdims=True), (1, LANES))


def _with_ones(v_t):
    return jnp.concatenate([v_t, jnp.ones((ONES_ROWS, v_t.shape[1]), F32)], axis=0).astype(BF16)


def _mix_in_kernel(h_ref, ng_ref, sh_ref, sc_ref, wt_ref, angr_ref, angc_ref, qkg_ref,
                   qta_ref, ka_ref, vta_ref, qtd_ref, kd_ref, vtd_ref, qn_ref, kn_ref):
    n = _rms_rows(h_ref[...]) * ng_ref[...]
    n = (n * (1.0 + sc_ref[...]) + sh_ref[...]).astype(BF16)
    pt = lax.dot_general(wt_ref[...], n, (((1,), (1,)), ((), ())), preferred_element_type=F32)
    t = pt.shape[1]

    cr, sr = jnp.cos(angr_ref[...]), jnp.sin(angr_ref[...])
    cc, sc = jnp.cos(angc_ref[...]), jnp.sin(angc_ref[...])
    gq = qkg_ref[:, 0:1]
    gk = qkg_ref[:, 1:2]
    zeros = jnp.zeros((HEAD_DIM, t), BF16)
    qscale = QK_SCALE * LOG2E

    for hd in range(A_Q_HEADS):
        q = _rope_t(_head_norm_t(pt[hd * HEAD_DIM:(hd + 1) * HEAD_DIM], gq), cr, sr, cc, sc) * qscale
        q = q.astype(BF16)
        qta_ref[hd] = jnp.concatenate([q, zeros] if hd // A_GROUP == 0 else [zeros, q], axis=0)

    c1 = A_Q_COLS
    c2 = c1 + A_KV_COLS
    c3 = c2 + A_KV_COLS
    c4 = c3 + B_QK_COLS
    c5 = c4 + B_QK_COLS
    kt = jnp.concatenate(
        [_rope_t(_head_norm_t(pt[c1 + j * HEAD_DIM:c1 + (j + 1) * HEAD_DIM], gk), cr, sr, cc, sc)
         for j in range(A_KV_HEADS)], axis=0)
    ka_ref[...] = kt.T.astype(BF16)
    for j in range(A_KV_HEADS):
        vta_ref[j] = _with_ones(pt[c2 + j * HEAD_DIM:c2 + (j + 1) * HEAD_DIM])

    qn, kn = [], []
    for cmb in range(2 * B_HEADS):
        q = (pt[c3 + cmb * HEAD_DIM:c3 + (cmb + 1) * HEAD_DIM] * qscale).astype(BF16)
        qtd_ref[cmb] = jnp.concatenate([q, zeros] if cmb % 2 == 0 else [zeros, q], axis=0)
        qn.append(_max_sq_norm(q))
        kn.append(_max_sq_norm(pt[c4 + cmb * HEAD_DIM:c4 + (cmb + 1) * HEAD_DIM].astype(BF16)))
    qn_ref[...] = jnp.concatenate(qn, axis=0)
    kn_ref[...] = jnp.concatenate(kn, axis=0)
    kd_ref[...] = pt[c4:c5].T.astype(BF16)
    for hd in range(B_HEADS):
        vtd_ref[hd] = _with_ones(pt[c5 + hd * B_V_DIM:c5 + (hd + 1) * B_V_DIM])


def _mix_in(h, ng, sh, sc, w_in_t, angr_t, angc_t, qkg_t):
    s, d = h.shape
    tm = MIX_ROWS
    nt = s // tm
    in_cols = w_in_t.shape[0]
    n_sets = 2 * B_HEADS
    vec = pl.BlockSpec((1, d), lambda i: (0, 0))
    out_shape = (
        jax.ShapeDtypeStruct((A_Q_HEADS, 2 * HEAD_DIM, s), BF16),
        jax.ShapeDtypeStruct((s, A_KV_COLS), BF16),
        jax.ShapeDtypeStruct((A_KV_HEADS, nt, HEAD_DIM + ONES_ROWS, tm), BF16),
        jax.ShapeDtypeStruct((n_sets, 2 * HEAD_DIM, s), BF16),
        jax.ShapeDtypeStruct((s, B_QK_COLS), BF16),
        jax.ShapeDtypeStruct((B_HEADS, nt, B_V_DIM + ONES_ROWS, tm), BF16),
        jax.ShapeDtypeStruct((nt, n_sets, LANES), F32),
        jax.ShapeDtypeStruct((nt, n_sets, LANES), F32),
    )
    out_specs = (
        pl.BlockSpec((A_Q_HEADS, 2 * HEAD_DIM, tm), lambda i: (0, 0, i)),
        pl.BlockSpec((tm, A_KV_COLS), lambda i: (i, 0)),
        pl.BlockSpec((A_KV_HEADS, None, HEAD_DIM + ONES_ROWS, tm), lambda i: (0, i, 0, 0)),
        pl.BlockSpec((n_sets, 2 * HEAD_DIM, tm), lambda i: (0, 0, i)),
        pl.BlockSpec((tm, B_QK_COLS), lambda i: (i, 0)),
        pl.BlockSpec((B_HEADS, None, B_V_DIM + ONES_ROWS, tm), lambda i: (0, i, 0, 0)),
        pl.BlockSpec((None, n_sets, LANES), lambda i: (i, 0, 0)),
        pl.BlockSpec((None, n_sets, LANES), lambda i: (i, 0, 0)),
    )
    return pl.pallas_call(
        _mix_in_kernel,
        grid=(nt,),
        in_specs=[
            pl.BlockSpec((tm, d), lambda i: (i, 0)),
            vec, vec, vec,
            pl.BlockSpec((in_cols, d), lambda i: (0, 0)),
            pl.BlockSpec((ROPE_HALF, tm), lambda i: (0, i)),
            pl.BlockSpec((ROPE_HALF, tm), lambda i: (0, i)),
            pl.BlockSpec((HEAD_DIM, 2), lambda i: (0, 0)),
        ],
        out_specs=out_specs,
        out_shape=out_shape,
        compiler_params=_params("parallel"),
        name="mix_in",
    )(h, ng, sh, sc, w_in_t, angr_t, angc_t, qkg_t)


def _softmax_tile(s, m_ref, alpha_ref, p_ref):
    tk, n = s.shape
    part = jnp.max(s.reshape(tk // BF16_SUBLANES, BF16_SUBLANES, n), axis=0)
    m_old = m_ref[...]
    m_new = jnp.maximum(m_old, jnp.max(part.astype(F32), axis=0, keepdims=True))
    alpha_ref[...] = jnp.exp2(m_old - m_new)
    m_ref[...] = m_new
    p_ref[...] = jnp.exp2(s - m_new.astype(BF16))


def _pipelined_sweep(qk, softmax, pv, lo, n_pairs, last):
    qk(lo, 0)

    def body(j, carry):
        a = lo + 2 * j
        qk(a + 1, 1)
        softmax(0)
        pv(jnp.maximum(a - 1, 0), 1)
        qk(jnp.minimum(a + 2, last), 0)
        softmax(1)
        pv(a, 0)
        return carry

    lax.fori_loop(0, n_pairs, body, 0)
    pv(last, 1)


def _gqa_kernel(qt_ref, k_ref, vt_ref, o_ref, q_scr, s_buf, p_buf, alpha_buf, m_scr, acc_scr):
    nk = k_ref.shape[0]
    tq = qt_ref.shape[2]
    for hh in range(A_GROUP):
        q_scr[:, hh * tq:(hh + 1) * tq] = qt_ref[hh]
    m_scr[...] = jnp.full_like(m_scr, NEG_BIG)
    acc_scr[...] = jnp.zeros_like(acc_scr)
    p_buf[1] = jnp.zeros(p_buf.shape[1:], BF16)
    alpha_buf[1] = jnp.ones(alpha_buf.shape[1:], F32)

    def qk(kt, slot):
        s_buf[slot] = jnp.dot(k_ref[kt], q_scr[...], preferred_element_type=F32).astype(BF16)

    def softmax(slot):
        _softmax_tile(s_buf[slot], m_scr, alpha_buf.at[slot], p_buf.at[slot])

    def pv(kt, slot):
        acc_scr[...] = alpha_buf[slot] * acc_scr[...] + jnp.dot(
            vt_ref[kt], p_buf[slot], preferred_element_type=F32)

    _pipelined_sweep(qk, softmax, pv, 0, nk // 2, nk - 1)
    acc = acc_scr[...]
    ot = acc[:HEAD_DIM] * (1.0 / acc[HEAD_DIM:HEAD_DIM + 1])
    ot = jnp.concatenate([ot[:, hh * tq:(hh + 1) * tq] for hh in range(A_GROUP)], axis=0)
    o_ref[...] = ot.T.astype(BF16)


def _gqa(qta, ka3, vta):
    s = qta.shape[2]
    nk, tk, _ = ka3.shape
    assert nk % 2 == 0
    tq = GQA_Q_TILE
    nq = A_GROUP * tq
    v_rows = vta.shape[2]
    return pl.pallas_call(
        _gqa_kernel,
        grid=(A_KV_HEADS, s // tq),
        in_specs=[
            pl.BlockSpec((A_GROUP, 2 * HEAD_DIM, tq), lambda g, i: (g, 0, i)),
            pl.BlockSpec((nk, tk, A_KV_COLS), lambda g, i: (0, 0, 0)),
            pl.BlockSpec((None, nk, v_rows, tk), lambda g, i: (g, 0, 0, 0)),
        ],
        out_specs=pl.BlockSpec((tq, A_GROUP * HEAD_DIM), lambda g, i: (i, g)),
        out_shape=jax.ShapeDtypeStruct((s, A_Q_COLS), BF16),
        scratch_shapes=[
            pltpu.VMEM((2 * HEAD_DIM, nq), BF16),
            pltpu.VMEM((2, tk, nq), BF16),
            pltpu.VMEM((2, tk, nq), BF16),
            pltpu.VMEM((2, 1, nq), F32),
            pltpu.VMEM((1, nq), F32),
            pltpu.VMEM((v_rows, nq), F32),
        ],
        compiler_params=_params("parallel", "parallel"),
        name="gqa_attn",
    )(qta, ka3, vta)


def _diff_kernel(coef_ref, qt0_ref, qt1_ref, k0_ref, k1_ref, vt_ref, qn_ref, kn_ref, lamp_ref, sg_ref,
                 o_ref, rel_scr, s_buf, p_buf, alpha_buf, m_scr, acc_scr, *, lam_init):
    hd = pl.program_id(0)
    qi = pl.program_id(1)
    nk, tk, _ = k0_ref.shape
    tq = qt0_ref.shape[1]
    nslope = coef_ref[0, hd]
    inv_tile_drop = coef_ref[1, hd]
    m_scr[...] = jnp.full_like(m_scr, NEG_BIG)
    acc_scr[...] = jnp.zeros_like(acc_scr)
    p_buf[1] = jnp.zeros(p_buf.shape[1:], BF16)
    alpha_buf[1] = jnp.ones(alpha_buf.shape[1:], F32)
    rel_scr[...] = (lax.broadcasted_iota(jnp.int32, (tk, tq), 1)
                    - lax.broadcasted_iota(jnp.int32, (tk, tq), 0)).astype(F32)

    k_max = jnp.max(kn_ref[...], axis=0)
    bound = 2.0 * jnp.sqrt(qn_ref[qi] * k_max)
    row = lax.broadcasted_iota(jnp.int32, bound.shape, 0)
    mine = jnp.logical_or(row == hd, row == hd + B_HEADS)
    reach = jnp.floor((jnp.where(mine, bound, 0.0) + SKIP_LOG2) * inv_tile_drop) + 1.0
    reach = jnp.minimum(jnp.max(reach), float(nk)).astype(jnp.int32)
    lo = jnp.maximum(qi - reach, 0)
    hi = jnp.minimum(qi + reach, nk - 1)
    odd = jnp.bitwise_and(hi - lo + 1, 1)
    grow_hi = jnp.where(hi < nk - 1, odd, 0)
    hi = hi + grow_hi
    lo = lo - (odd - grow_hi)

    q0 = qi * tq
    qts = (qt0_ref, qt1_ref)
    ks = (k0_ref, k1_ref)

    def qk(kt, slot):
        off = (q0 - kt * tk).astype(F32)
        bias = jnp.abs(rel_scr[...] + off) * nslope
        for mp in range(2):
            s = jnp.dot(ks[mp][kt], qts[mp][...], preferred_element_type=F32) + bias
            s_buf[slot, mp] = s.astype(BF16)

    def softmax(slot):
        for mp in range(2):
            _softmax_tile(s_buf[slot, mp], m_scr.at[mp], alpha_buf.at[slot, mp], p_buf.at[slot, mp])

    def pv(kt, slot):
        vt = vt_ref[kt]
        for mp in range(2):
            acc_scr[mp] = alpha_buf[slot, mp] * acc_scr[mp] + jnp.dot(
                vt, p_buf[slot, mp], preferred_element_type=F32)

    _pipelined_sweep(qk, softmax, pv, lo, lax.shift_right_logical(hi - lo + 1, 1), hi)

    lp = lamp_ref[...]
    lam = (jnp.exp(jnp.sum(lp[0:1] * lp[1:2], axis=1, keepdims=True))
           - jnp.exp(jnp.sum(lp[2:3] * lp[3:4], axis=1, keepdims=True)) + lam_init)
    a0, a1 = acc_scr[0], acc_scr[1]
    ot = (a0[:B_V_DIM] * (1.0 / a0[B_V_DIM:B_V_DIM + 1])
          - lam * (a1[:B_V_DIM] * (1.0 / a1[B_V_DIM:B_V_DIM + 1])))
    ot = ot * lax.rsqrt(jnp.mean(ot * ot, axis=0, keepdims=True) + EPS)
    ot = ot * sg_ref[...] * (1.0 - lam_init)
    o_ref[...] = ot.T.astype(BF16)


def _diff(coef, qtd, kd3, vtd, qn, kn, lam_p, subln_col, *, lam_init):
    s = qtd.shape[2]
    nk, tk, _ = kd3.shape
    tq = DIFF_Q_TILE
    assert nk % 2 == 0 and tq == tk == MIX_ROWS
    lanes = 2 * HEAD_DIM
    v_rows = vtd.shape[2]
    return pl.pallas_call(
        functools.partial(_diff_kernel, lam_init=lam_init),
        grid=(B_HEADS, s // tq),
        in_specs=[
            pl.BlockSpec(memory_space=pltpu.SMEM),
            pl.BlockSpec((None, lanes, tq), lambda h, i: (h, 0, i)),
            pl.BlockSpec((None, lanes, tq), lambda h, i: (B_HEADS + h, 0, i)),
            pl.BlockSpec((nk, tk, lanes), lambda h, i: (0, 0, h // 2)),
            pl.BlockSpec((nk, tk, lanes), lambda h, i: (0, 0, B_HEADS // 2 + h // 2)),
            pl.BlockSpec((None, nk, v_rows, tk), lambda h, i: (h, 0, 0, 0)),
            pl.BlockSpec(qn.shape, lambda h, i: (0, 0, 0)),
            pl.BlockSpec(kn.shape, lambda h, i: (0, 0, 0)),
            pl.BlockSpec((4, HEAD_DIM), lambda h, i: (0, 0)),
            pl.BlockSpec((B_V_DIM, 1), lambda h, i: (0, 0)),
        ],
        out_specs=pl.BlockSpec((tq, B_V_DIM), lambda h, i: (i, h)),
        out_shape=jax.ShapeDtypeStruct((s, B_V_COLS), BF16),
        scratch_shapes=[
            pltpu.VMEM((tk, tq), F32),
            pltpu.VMEM((2, 2, tk, tq), BF16),
            pltpu.VMEM((2, 2, tk, tq), BF16),
            pltpu.VMEM((2, 2, 1, tq), F32),
            pltpu.VMEM((2, 1, tq), F32),
            pltpu.VMEM((2, v_rows, tq), F32),
        ],
        compiler_params=_params("parallel", "parallel"),
        name="diff_attn",
    )(coef, qtd, qtd, kd3, kd3, vtd, qn, kn, lam_p, subln_col)


def _mix_out_kernel(h_ref, ng_ref, sh_ref, sc_ref, gt_ref, oa_ref, od_ref, wba_ref, wbb_ref,
                    wgate_ref, bgate_ref, wo_ref, o_ref):
    h = h_ref[...]
    d = h.shape[1]
    n = _rms_rows(h) * ng_ref[...]
    n = (n * (1.0 + sc_ref[...]) + sh_ref[...]).astype(BF16)
    z = jnp.dot(n, wgate_ref[...], preferred_element_type=F32) + bgate_ref[...]
    g = 1.0 / (1.0 + jnp.exp(-z))
    ya = jnp.dot(oa_ref[...], wba_ref[...], preferred_element_type=F32)
    yb = jnp.dot(od_ref[...], wbb_ref[...], preferred_element_type=F32)
    mix = (g[:, :d] * ya + g[:, d:] * yb).astype(BF16)
    y = jnp.dot(mix, wo_ref[...], preferred_element_type=F32)
    o_ref[...] = h + gt_ref[...] * y


def _mix_out(h, ng, sh, sc, gt, oa, od, w_ba, w_bb, w_gate, b_gate, w_o):
    s, d = h.shape
    tm = MIX_ROWS
    vec = pl.BlockSpec((1, d), lambda i: (0, 0))

    def full(a):
        return pl.BlockSpec(a.shape, lambda i: (0,) * a.ndim)

    return pl.pallas_call(
        _mix_out_kernel,
        grid=(s // tm,),
        in_specs=[
            pl.BlockSpec((tm, d), lambda i: (i, 0)),
            vec, vec, vec, vec,
            pl.BlockSpec((tm, oa.shape[1]), lambda i: (i, 0)),
            pl.BlockSpec((tm, od.shape[1]), lambda i: (i, 0)),
            full(w_ba), full(w_bb), full(w_gate), full(b_gate), full(w_o),
        ],
        out_specs=pl.BlockSpec((tm, d), lambda i: (i, 0)),
        out_shape=jax.ShapeDtypeStruct((s, d), F32),
        compiler_params=_params("parallel"),
        name="mix_out",
    )(h, ng, sh, sc, gt, oa, od, w_ba, w_bb, w_gate, b_gate, w_o)


def _axial_angles_t(seq):
    rows = seq // GRID_W
    row = jnp.broadcast_to(jnp.arange(rows)[:, None], (rows, GRID_W)).reshape(seq) - rows // 2
    col = jnp.broadcast_to(jnp.arange(GRID_W)[None, :], (rows, GRID_W)).reshape(seq) - GRID_W // 2
    inv = 1.0 / (ROPE_THETA ** (jnp.arange(0, ROPE_AXIS_DIM, 2, dtype=F32) / ROPE_AXIS_DIM))
    return inv[:, None] * row.astype(F32)[None, :], inv[:, None] * col.astype(F32)[None, :]


def kernel(x, c, ada_w, ada_b, norm_g, ffn_wg, ffn_wu, ffn_wd, w_in, qk_g, lam_p, subln_g, w_ba, w_bb,
           w_gate, b_gate, w_o, final_g):
    batch, s, d = x.shape
    assert batch == 1 and s % KEY_TILE == 0 and MIX_ROWS == KEY_TILE
    depth = ada_w.shape[0]
    h = x.reshape(s, d)

    mod = _ada_mod(c.reshape(d, 1), ada_w, ada_b)
    angr_t, angc_t = _axial_angles_t(s)
    slopes = 2.0 ** (-8.0 * jnp.arange(1, B_HEADS + 1, dtype=F32) / B_HEADS)
    coef = jnp.stack([-slopes * LOG2E, 1.0 / (slopes * LOG2E * KEY_TILE)])
    fg = final_g.reshape(1, d)

    for l in range(depth):
        sh1, sc1, g1, sh2, sc2, g2, sh3, sc3, g3 = [mod[l, :, j * d:(j + 1) * d] for j in range(N_ADA)]
        lam_init = 0.8 - 0.6 * math.exp(-0.3 * l)
        ng = norm_g[l].reshape(3, 1, d)
        wg, wu, wd = ffn_wg[l].astype(BF16), ffn_wu[l].astype(BF16), ffn_wd[l].astype(BF16)

        h = _ffn(h, ng[0], sh1, sc1, g1, wg[0], wu[0], wd[0], fg, final_norm=False)

        qta, ka, vta, qtd, kd, vtd, qn, kn = _mix_in(
            h, ng[1], sh2, sc2, w_in[l].T.astype(BF16), angr_t, angc_t, qk_g[l].T)
        nk = s // KEY_TILE
        oa = _gqa(qta, ka.reshape(nk, KEY_TILE, A_KV_COLS), vta)
        od = _diff(coef, qtd, kd.reshape(nk, KEY_TILE, B_QK_COLS), vtd, qn, kn, lam_p[l],
                   subln_g[l].reshape(B_V_DIM, 1), lam_init=lam_init)
        h = _mix_out(h, ng[1], sh2, sc2, g2, oa, od, w_ba[l].astype(BF16), w_bb[l].astype(BF16),
                     w_gate[l].astype(BF16), b_gate[l].reshape(1, -1), w_o[l].astype(BF16))

        h = _ffn(h, ng[2], sh3, sc3, g3, wg[1], wu[1], wd[1], fg, final_norm=(l == depth - 1))
    return h.reshape(batch, s, d)
```

```python
import functools
import math

import jax
import jax.numpy as jnp
from jax import lax
from jax.experimental import pallas as pl
from jax.experimental.pallas import tpu as pltpu

F32 = jnp.float32
BF16 = jnp.bfloat16

GRID_W = 64
HEAD_DIM = 64
A_Q_HEADS = 8
A_KV_HEADS = 2
A_GROUP = A_Q_HEADS // A_KV_HEADS
B_HEADS = 4
B_V_DIM = 2 * HEAD_DIM
A_Q_COLS = A_Q_HEADS * HEAD_DIM
A_KV_COLS = A_KV_HEADS * HEAD_DIM
B_QK_COLS = 2 * B_HEADS * HEAD_DIM
B_V_COLS = B_HEADS * B_V_DIM
N_ADA = 9
EPS = 1e-6
ROPE_THETA = 10000.0
ROPE_AXIS_DIM = HEAD_DIM // 2
ROPE_HALF = ROPE_AXIS_DIM // 2

LOG2E = math.log2(math.e)
QK_SCALE = HEAD_DIM ** -0.5
NEG_BIG = -1e30

BF16_SUBLANES = 16
LANES = 128
ONES_ROWS = BF16_SUBLANES
SKIP_LOG2 = 160.0
FAST_LOG2_LIMIT = 60.0
NORM_ROWS_A = 8
NORM_ROWS = NORM_ROWS_A + 2 * B_HEADS

VMEM_LIMIT_BYTES = 56 * 1024 * 1024

FFN_ROWS = 512
MIX_ROWS = 512
KEY_TILE = 512
GQA_Q_TILE = 256
DIFF_Q_TILE = 512
ADA_COLS = 1152


def _params(*sem):
    return pltpu.CompilerParams(dimension_semantics=sem, vmem_limit_bytes=VMEM_LIMIT_BYTES)


def _rms_rows(x):
    return x * lax.rsqrt(jnp.mean(x * x, axis=-1, keepdims=True) + EPS)


def _ada_kernel(c_ref, w_ref, b_ref, o_ref):
    c = c_ref[...]
    act = c / (1.0 + jnp.exp(-c))
    o_ref[...] = jnp.sum(w_ref[...] * act, axis=0, keepdims=True) + b_ref[...]


def _ada_mod(c_col, ada_w, ada_b):
    n_layers, d, n = ada_w.shape
    return pl.pallas_call(
        _ada_kernel,
        grid=(n_layers, n // ADA_COLS),
        in_specs=[
            pl.BlockSpec((d, 1), lambda l, j: (0, 0)),
            pl.BlockSpec((None, d, ADA_COLS), lambda l, j: (l, 0, j)),
            pl.BlockSpec((None, 1, ADA_COLS), lambda l, j: (l, 0, j)),
        ],
        out_specs=pl.BlockSpec((None, 1, ADA_COLS), lambda l, j: (l, 0, j)),
        out_shape=jax.ShapeDtypeStruct((n_layers, 1, n), F32),
        compiler_params=_params("parallel", "parallel"),
        name="ada_mod",
    )(c_col, ada_w, ada_b.reshape(n_layers, 1, n))


def _ffn_kernel(h_ref, ng_ref, sh_ref, sc_ref, gt_ref, wg_ref, wu_ref, wd_ref, fg_ref, o_ref,
                n_scr, acc_scr, *, final_norm):
    f = pl.program_id(1)

    @pl.when(f == 0)
    def _():
        n = _rms_rows(h_ref[...]) * ng_ref[...]
        n = n * (1.0 + sc_ref[...]) + sh_ref[...]
        n_scr[...] = n.astype(BF16)
        acc_scr[...] = jnp.zeros_like(acc_scr)

    n = n_scr[...]
    hg = jnp.dot(n, wg_ref[...], preferred_element_type=F32)
    hu = jnp.dot(n, wu_ref[...], preferred_element_type=F32)
    a = (hg / (1.0 + jnp.exp(-hg))) * hu
    acc_scr[...] += jnp.dot(a.astype(BF16), wd_ref[...], preferred_element_type=F32)

    @pl.when(f == pl.num_programs(1) - 1)
    def _():
        out = h_ref[...] + (0.5 * gt_ref[...]) * acc_scr[...]
        if final_norm:
            out = _rms_rows(out) * fg_ref[...]
        o_ref[...] = out


def _ffn(h, ng, sh, sc, gt, wg, wu, wd, fg, *, final_norm):
    s, d = h.shape
    f_dim = wg.shape[1]
    tf = f_dim // 2
    vec = pl.BlockSpec((1, d), lambda i, f: (0, 0))
    return pl.pallas_call(
        functools.partial(_ffn_kernel, final_norm=final_norm),
        grid=(s // FFN_ROWS, f_dim // tf),
        in_specs=[
            pl.BlockSpec((FFN_ROWS, d), lambda i, f: (i, 0)),
            vec, vec, vec, vec,
            pl.BlockSpec((d, tf), lambda i, f: (0, f)),
            pl.BlockSpec((d, tf), lambda i, f: (0, f)),
            pl.BlockSpec((tf, d), lambda i, f: (f, 0)),
            vec,
        ],
        out_specs=pl.BlockSpec((FFN_ROWS, d), lambda i, f: (i, 0)),
        out_shape=jax.ShapeDtypeStruct((s, d), F32),
        scratch_shapes=[pltpu.VMEM((FFN_ROWS, d), BF16), pltpu.VMEM((FFN_ROWS, d), F32)],
        compiler_params=_params("parallel", "arbitrary"),
        name="ffn_final" if final_norm else "ffn",
    )(h, ng, sh, sc, gt, wg, wu, wd, fg)


def _rope_t(x, cr, sr, cc, sc):
    h = ROPE_HALF
    x1r, x2r, x1c, x2c = x[0:h], x[h:2 * h], x[2 * h:3 * h], x[3 * h:4 * h]
    return jnp.concatenate(
        [x1r * cr - x2r * sr, x2r * cr + x1r * sr, x1c * cc - x2c * sc, x2c * cc + x1c * sc], axis=0)


def _head_norm_t(x, g_col):
    ms = jnp.mean(x * x, axis=0, keepdims=True)
    return x * lax.rsqrt(ms + EPS) * g_col


def _max_sq_norm(x_bf16):
    xf = x_bf16.astype(F32)
    n2 = jnp.sum(xf * xf, axis=0, keepdims=True)
    return jnp.broadcast_to(jnp.max(n2, axis=1, keepdims=True), (1, LANES))


def _with_ones(v_t):
    return jnp.concatenate([v_t, jnp.ones((ONES_ROWS, v_t.shape[1]), F32)], axis=0).astype(BF16)


def _mix_in_kernel(h_ref, ng_ref, sh_ref, sc_ref, wt_ref, angr_ref, angc_ref, qkg_ref,
                   qta_ref, ka_ref, vta_ref, qtd_ref, kd_ref, vtd_ref, qn_ref, kn_ref):
    n = _rms_rows(h_ref[...]) * ng_ref[...]
    n = (n * (1.0 + sc_ref[...]) + sh_ref[...]).astype(BF16)
    pt = lax.dot_general(wt_ref[...], n, (((1,), (1,)), ((), ())), preferred_element_type=F32)
    t = pt.shape[1]

    cr, sr = jnp.cos(angr_ref[...]), jnp.sin(angr_ref[...])
    cc, sc = jnp.cos(angc_ref[...]), jnp.sin(angc_ref[...])
    gq = qkg_ref[:, 0:1]
    gk = qkg_ref[:, 1:2]
    zeros = jnp.zeros((HEAD_DIM, t), BF16)
    qscale = QK_SCALE * LOG2E

    qn, kn = [], []
    for hd in range(A_Q_HEADS):
        q = _rope_t(_head_norm_t(pt[hd * HEAD_DIM:(hd + 1) * HEAD_DIM], gq), cr, sr, cc, sc) * qscale
        q = q.astype(BF16)
        qta_ref[hd] = jnp.concatenate([q, zeros] if hd // A_GROUP == 0 else [zeros, q], axis=0)
        qn.append(_max_sq_norm(q))

    c1 = A_Q_COLS
    c2 = c1 + A_KV_COLS
    c3 = c2 + A_KV_COLS
    c4 = c3 + B_QK_COLS
    c5 = c4 + B_QK_COLS
    kt = jnp.concatenate(
        [_rope_t(_head_norm_t(pt[c1 + j * HEAD_DIM:c1 + (j + 1) * HEAD_DIM], gk), cr, sr, cc, sc)
         for j in range(A_KV_HEADS)], axis=0)
    kt = kt.astype(BF16)
    ka_ref[...] = kt.T
    kn += [_max_sq_norm(kt[j * HEAD_DIM:(j + 1) * HEAD_DIM]) for j in range(A_KV_HEADS)]
    kn.append(jnp.zeros((NORM_ROWS_A - A_KV_HEADS, LANES), F32))
    for j in range(A_KV_HEADS):
        vta_ref[j] = _with_ones(pt[c2 + j * HEAD_DIM:c2 + (j + 1) * HEAD_DIM])

    for cmb in range(2 * B_HEADS):
        q = (pt[c3 + cmb * HEAD_DIM:c3 + (cmb + 1) * HEAD_DIM] * qscale).astype(BF16)
        qtd_ref[cmb] = jnp.concatenate([q, zeros] if cmb % 2 == 0 else [zeros, q], axis=0)
        qn.append(_max_sq_norm(q))
        kn.append(_max_sq_norm(pt[c4 + cmb * HEAD_DIM:c4 + (cmb + 1) * HEAD_DIM].astype(BF16)))
    qn_ref[...] = jnp.concatenate(qn, axis=0)
    kn_ref[...] = jnp.concatenate(kn, axis=0)
    kd_ref[...] = pt[c4:c5].T.astype(BF16)
    for hd in range(B_HEADS):
        vtd_ref[hd] = _with_ones(pt[c5 + hd * B_V_DIM:c5 + (hd + 1) * B_V_DIM])


def _mix_in(h, ng, sh, sc, w_in_t, angr_t, angc_t, qkg_t):
    s, d = h.shape
    tm = MIX_ROWS
    nt = s // tm
    in_cols = w_in_t.shape[0]
    n_sets = 2 * B_HEADS
    assert A_Q_HEADS == NORM_ROWS_A
    vec = pl.BlockSpec((1, d), lambda i: (0, 0))
    out_shape = (
        jax.ShapeDtypeStruct((A_Q_HEADS, 2 * HEAD_DIM, s), BF16),
        jax.ShapeDtypeStruct((s, A_KV_COLS), BF16),
        jax.ShapeDtypeStruct((A_KV_HEADS, nt, HEAD_DIM + ONES_ROWS, tm), BF16),
        jax.ShapeDtypeStruct((n_sets, 2 * HEAD_DIM, s), BF16),
        jax.ShapeDtypeStruct((s, B_QK_COLS), BF16),
        jax.ShapeDtypeStruct((B_HEADS, nt, B_V_DIM + ONES_ROWS, tm), BF16),
        jax.ShapeDtypeStruct((nt, NORM_ROWS, LANES), F32),
        jax.ShapeDtypeStruct((nt, NORM_ROWS, LANES), F32),
    )
    out_specs = (
        pl.BlockSpec((A_Q_HEADS, 2 * HEAD_DIM, tm), lambda i: (0, 0, i)),
        pl.BlockSpec((tm, A_KV_COLS), lambda i: (i, 0)),
        pl.BlockSpec((A_KV_HEADS, None, HEAD_DIM + ONES_ROWS, tm), lambda i: (0, i, 0, 0)),
        pl.BlockSpec((n_sets, 2 * HEAD_DIM, tm), lambda i: (0, 0, i)),
        pl.BlockSpec((tm, B_QK_COLS), lambda i: (i, 0)),
        pl.BlockSpec((B_HEADS, None, B_V_DIM + ONES_ROWS, tm), lambda i: (0, i, 0, 0)),
        pl.BlockSpec((None, NORM_ROWS, LANES), lambda i: (i, 0, 0)),
        pl.BlockSpec((None, NORM_ROWS, LANES), lambda i: (i, 0, 0)),
    )
    return pl.pallas_call(
        _mix_in_kernel,
        grid=(nt,),
        in_specs=[
            pl.BlockSpec((tm, d), lambda i: (i, 0)),
            vec, vec, vec,
            pl.BlockSpec((in_cols, d), lambda i: (0, 0)),
            pl.BlockSpec((ROPE_HALF, tm), lambda i: (0, i)),
            pl.BlockSpec((ROPE_HALF, tm), lambda i: (0, i)),
            pl.BlockSpec((HEAD_DIM, 2), lambda i: (0, 0)),
        ],
        out_specs=out_specs,
        out_shape=out_shape,
        compiler_params=_params("parallel"),
        name="mix_in",
    )(h, ng, sh, sc, w_in_t, angr_t, angc_t, qkg_t)


def _softmax_tile(s, m_ref, alpha_ref, p_ref):
    tk, n = s.shape
    part = jnp.max(s.reshape(tk // BF16_SUBLANES, BF16_SUBLANES, n), axis=0)
    m_old = m_ref[...]
    m_new = jnp.maximum(m_old, jnp.max(part.astype(F32), axis=0, keepdims=True))
    alpha_ref[...] = jnp.exp2(m_old - m_new)
    m_ref[...] = m_new
    p_ref[...] = jnp.exp2(s - m_new.astype(BF16))


def _pipelined_sweep(prep, qk, softmax, pv, n_chunks, lo, n_pairs, last):
    ctx = prep(lo)
    for c in range(n_chunks):
        qk(lo, 0, c, ctx)

    def body(j, carry):
        a = lo + 2 * j
        prev = jnp.maximum(a - 1, 0)
        nxt = jnp.minimum(a + 2, last)
        ctx = prep(a + 1)
        for c in range(n_chunks):
            softmax(0, c)
            pv(prev, 1, c)
            qk(a + 1, 1, c, ctx)
        ctx = prep(nxt)
        for c in range(n_chunks):
            softmax(1, c)
            pv(a, 0, c)
            qk(nxt, 0, c, ctx)
        return carry

    lax.fori_loop(0, n_pairs, body, 0)
    for c in range(n_chunks):
        pv(last, 1, c)


def _direct_sweep(prep, qk_exp, pv, n_chunks, lo, n_pairs, last):
    def body(j, carry):
        a = lo + 2 * j
        prev = jnp.maximum(a - 1, 0)
        ctx = prep(a)
        for c in range(n_chunks):
            qk_exp(a, 0, c, ctx)
            pv(prev, 1, c)
        ctx = prep(a + 1)
        for c in range(n_chunks):
            qk_exp(a + 1, 1, c, ctx)
            pv(a, 0, c)
        return carry

    lax.fori_loop(0, n_pairs, body, 0)
    for c in range(n_chunks):
        pv(last, 1, c)


def _score_bound(qn_tile, kn_ref, q_rows, k_row_of_q):
    k_max = jnp.max(kn_ref[...], axis=0)
    row = lax.broadcasted_iota(jnp.int32, k_max.shape, 0)
    return jnp.sqrt(jnp.where(q_rows(row), qn_tile, 0.0) * k_row_of_q(row, k_max))


GQA_CHUNKS = 2


def _gqa_kernel(qt_ref, k_ref, vt_ref, qn_ref, kn_ref, o_ref, q_scr, s_buf, p_buf, alpha_buf, m_scr,
                acc_scr):
    g = pl.program_id(0)
    qi = pl.program_id(1)
    nk, tk, _ = k_ref.shape
    tq = qt_ref.shape[2]
    per = A_GROUP // GQA_CHUNKS
    for hh in range(A_GROUP):
        q_scr[hh // per, :, (hh % per) * tq:(hh % per + 1) * tq] = qt_ref[hh]
    m_scr[...] = jnp.full_like(m_scr, NEG_BIG)
    acc_scr[...] = jnp.zeros_like(acc_scr)
    p_buf[1] = jnp.zeros(p_buf.shape[1:], BF16)
    alpha_buf[1] = jnp.ones(alpha_buf.shape[1:], F32)

    def qk(kt, slot, c, _):
        s_buf[slot, c] = jnp.dot(k_ref[kt], q_scr[c], preferred_element_type=F32).astype(BF16)

    def softmax(slot, c):
        _softmax_tile(s_buf[slot, c], m_scr.at[c], alpha_buf.at[slot, c], p_buf.at[slot, c])

    def pv(kt, slot, c):
        acc_scr[c] = alpha_buf[slot, c] * acc_scr[c] + jnp.dot(
            vt_ref[kt], p_buf[slot, c], preferred_element_type=F32)

    def qk_exp(kt, slot, c, _):
        s = jnp.dot(k_ref[kt], q_scr[c], preferred_element_type=F32)
        p_buf[slot, c] = jnp.exp2(s.astype(BF16))

    def pv_plain(kt, slot, c):
        acc_scr[c] += jnp.dot(vt_ref[kt], p_buf[slot, c], preferred_element_type=F32)

    def kmax_of_group(row, k_max):
        return jnp.max(jnp.where(row == g, k_max, 0.0), axis=0, keepdims=True)

    bound = _score_bound(qn_ref[lax.div(qi * tq, tk)], kn_ref,
                         lambda row: jnp.logical_and(row >= g * A_GROUP, row < (g + 1) * A_GROUP),
                         kmax_of_group)

    def direct():
        _direct_sweep(lambda kt: None, qk_exp, pv_plain, GQA_CHUNKS, 0, nk // 2, nk - 1)
        return 0

    def online():
        _pipelined_sweep(lambda kt: None, qk, softmax, pv, GQA_CHUNKS, 0, nk // 2, nk - 1)
        return 0

    lax.cond(jnp.max(bound) <= FAST_LOG2_LIMIT, direct, online)
    heads = []
    for c in range(GQA_CHUNKS):
        acc = acc_scr[c]
        ot = acc[:HEAD_DIM] * (1.0 / acc[HEAD_DIM:HEAD_DIM + 1])
        heads += [ot[:, i * tq:(i + 1) * tq] for i in range(per)]
    o_ref[...] = jnp.concatenate(heads, axis=0).T.astype(BF16)


def _gqa(qta, ka3, vta, qn, kn):
    s = qta.shape[2]
    nk, tk, _ = ka3.shape
    tq = GQA_Q_TILE
    assert nk % 2 == 0 and tk % tq == 0 and tk == MIX_ROWS
    nc = GQA_CHUNKS
    w = A_GROUP // nc * tq
    v_rows = vta.shape[2]
    return pl.pallas_call(
        _gqa_kernel,
        grid=(A_KV_HEADS, s // tq),
        in_specs=[
            pl.BlockSpec((A_GROUP, 2 * HEAD_DIM, tq), lambda g, i: (g, 0, i)),
            pl.BlockSpec((nk, tk, A_KV_COLS), lambda g, i: (0, 0, 0)),
            pl.BlockSpec((None, nk, v_rows, tk), lambda g, i: (g, 0, 0, 0)),
            pl.BlockSpec(qn.shape, lambda g, i: (0, 0, 0)),
            pl.BlockSpec(kn.shape, lambda g, i: (0, 0, 0)),
        ],
        out_specs=pl.BlockSpec((tq, A_GROUP * HEAD_DIM), lambda g, i: (i, g)),
        out_shape=jax.ShapeDtypeStruct((s, A_Q_COLS), BF16),
        scratch_shapes=[
            pltpu.VMEM((nc, 2 * HEAD_DIM, w), BF16),
            pltpu.VMEM((2, nc, tk, w), BF16),
            pltpu.VMEM((2, nc, tk, w), BF16),
            pltpu.VMEM((2, nc, 1, w), F32),
            pltpu.VMEM((nc, 1, w), F32),
            pltpu.VMEM((nc, v_rows, w), F32),
        ],
        compiler_params=_params("parallel", "parallel"),
        name="gqa_attn",
    )(qta, ka3, vta, qn, kn)


def _diff_kernel(coef_ref, qt0_ref, qt1_ref, k0_ref, k1_ref, vt_ref, qn_ref, kn_ref, lamp_ref, sg_ref,
                 o_ref, rel_scr, s_buf, p_buf, alpha_buf, m_scr, acc_scr, *, lam_init):
    hd = pl.program_id(0)
    qi = pl.program_id(1)
    nk, tk, _ = k0_ref.shape
    tq = qt0_ref.shape[1]
    nslope = coef_ref[0, hd]
    inv_tile_drop = coef_ref[1, hd]
    m_scr[...] = jnp.full_like(m_scr, NEG_BIG)
    acc_scr[...] = jnp.zeros_like(acc_scr)
    p_buf[1] = jnp.zeros(p_buf.shape[1:], BF16)
    alpha_buf[1] = jnp.ones(alpha_buf.shape[1:], F32)
    rel_scr[...] = (lax.broadcasted_iota(jnp.int32, (tk, tq), 1)
                    - lax.broadcasted_iota(jnp.int32, (tk, tq), 0)).astype(F32)

    bound = _score_bound(
        qn_ref[qi], kn_ref,
        lambda row: jnp.logical_or(row == NORM_ROWS_A + hd, row == NORM_ROWS_A + B_HEADS + hd),
        lambda row, k_max: k_max)
    reach = jnp.floor((2.0 * bound + SKIP_LOG2) * inv_tile_drop) + 1.0
    reach = jnp.minimum(jnp.max(reach), float(nk)).astype(jnp.int32)
    lo = jnp.maximum(qi - reach, 0)
    hi = jnp.minimum(qi + reach, nk - 1)
    odd = jnp.bitwise_and(hi - lo + 1, 1)
    grow_hi = jnp.where(hi < nk - 1, odd, 0)
    hi = hi + grow_hi
    lo = lo - (odd - grow_hi)

    q0 = qi * tq
    qts = (qt0_ref, qt1_ref)
    ks = (k0_ref, k1_ref)

    def alibi(kt):
        off = (q0 - kt * tk).astype(F32)
        return jnp.abs(rel_scr[...] + off) * nslope

    def qk(kt, slot, mp, bias):
        s = jnp.dot(ks[mp][kt], qts[mp][...], preferred_element_type=F32) + bias
        s_buf[slot, mp] = s.astype(BF16)

    def softmax(slot, mp):
        _softmax_tile(s_buf[slot, mp], m_scr.at[mp], alpha_buf.at[slot, mp], p_buf.at[slot, mp])

    def pv(kt, slot, mp):
        acc_scr[mp] = alpha_buf[slot, mp] * acc_scr[mp] + jnp.dot(
            vt_ref[kt], p_buf[slot, mp], preferred_element_type=F32)

    def qk_exp(kt, slot, mp, bias):
        s = jnp.dot(ks[mp][kt], qts[mp][...], preferred_element_type=F32) + bias
        p_buf[slot, mp] = jnp.exp2(s.astype(BF16))

    def pv_plain(kt, slot, mp):
        acc_scr[mp] += jnp.dot(vt_ref[kt], p_buf[slot, mp], preferred_element_type=F32)

    n_pairs = lax.shift_right_logical(hi - lo + 1, 1)

    def direct():
        _direct_sweep(alibi, qk_exp, pv_plain, 2, lo, n_pairs, hi)
        return 0

    def online():
        _pipelined_sweep(alibi, qk, softmax, pv, 2, lo, n_pairs, hi)
        return 0

    lax.cond(jnp.max(bound) <= FAST_LOG2_LIMIT, direct, online)

    lp = lamp_ref[...]
    lam = (jnp.exp(jnp.sum(lp[0:1] * lp[1:2], axis=1, keepdims=True))
           - jnp.exp(jnp.sum(lp[2:3] * lp[3:4], axis=1, keepdims=True)) + lam_init)
    a0, a1 = acc_scr[0], acc_scr[1]
    ot = (a0[:B_V_DIM] * (1.0 / a0[B_V_DIM:B_V_DIM + 1])
          - lam * (a1[:B_V_DIM] * (1.0 / a1[B_V_DIM:B_V_DIM + 1])))
    ot = ot * lax.rsqrt(jnp.mean(ot * ot, axis=0, keepdims=True) + EPS)
    ot = ot * sg_ref[...] * (1.0 - lam_init)
    o_ref[...] = ot.T.astype(BF16)


def _diff(coef, qtd, kd3, vtd, qn, kn, lam_p, subln_col, *, lam_init):
    s = qtd.shape[2]
    nk, tk, _ = kd3.shape
    tq = DIFF_Q_TILE
    assert nk % 2 == 0 and tq == tk == MIX_ROWS
    lanes = 2 * HEAD_DIM
    v_rows = vtd.shape[2]
    return pl.pallas_call(
        functools.partial(_diff_kernel, lam_init=lam_init),
        grid=(B_HEADS, s // tq),
        in_specs=[
            pl.BlockSpec(memory_space=pltpu.SMEM),
            pl.BlockSpec((None, lanes, tq), lambda h, i: (h, 0, i)),
            pl.BlockSpec((None, lanes, tq), lambda h, i: (B_HEADS + h, 0, i)),
            pl.BlockSpec((nk, tk, lanes), lambda h, i: (0, 0, h // 2)),
            pl.BlockSpec((nk, tk, lanes), lambda h, i: (0, 0, B_HEADS // 2 + h // 2)),
            pl.BlockSpec((None, nk, v_rows, tk), lambda h, i: (h, 0, 0, 0)),
            pl.BlockSpec(qn.shape, lambda h, i: (0, 0, 0)),
            pl.BlockSpec(kn.shape, lambda h, i: (0, 0, 0)),
            pl.BlockSpec((4, HEAD_DIM), lambda h, i: (0, 0)),
            pl.BlockSpec((B_V_DIM, 1), lambda h, i: (0, 0)),
        ],
        out_specs=pl.BlockSpec((tq, B_V_DIM), lambda h, i: (i, h)),
        out_shape=jax.ShapeDtypeStruct((s, B_V_COLS), BF16),
        scratch_shapes=[
            pltpu.VMEM((tk, tq), F32),
            pltpu.VMEM((2, 2, tk, tq), BF16),
            pltpu.VMEM((2, 2, tk, tq), BF16),
            pltpu.VMEM((2, 2, 1, tq), F32),
            pltpu.VMEM((2, 1, tq), F32),
            pltpu.VMEM((2, v_rows, tq), F32),
        ],
        compiler_params=_params("parallel", "parallel"),
        name="diff_attn",
    )(coef, qtd, qtd, kd3, kd3, vtd, qn, kn, lam_p, subln_col)


def _mix_out_kernel(h_ref, ng_ref, sh_ref, sc_ref, gt_ref, oa_ref, od_ref, wba_ref, wbb_ref,
                    wgate_ref, bgate_ref, wo_ref, o_ref):
    h = h_ref[...]
    d = h.shape[1]
    n = _rms_rows(h) * ng_ref[...]
    n = (n * (1.0 + sc_ref[...]) + sh_ref[...]).astype(BF16)
    z = jnp.dot(n, wgate_ref[...], preferred_element_type=F32) + bgate_ref[...]
    g = 1.0 / (1.0 + jnp.exp(-z))
    ya = jnp.dot(oa_ref[...], wba_ref[...], preferred_element_type=F32)
    yb = jnp.dot(od_ref[...], wbb_ref[...], preferred_element_type=F32)
    mix = (g[:, :d] * ya + g[:, d:] * yb).astype(BF16)
    y = jnp.dot(mix, wo_ref[...], preferred_element_type=F32)
    o_ref[...] = h + gt_ref[...] * y


def _mix_out(h, ng, sh, sc, gt, oa, od, w_ba, w_bb, w_gate, b_gate, w_o):
    s, d = h.shape
    tm = MIX_ROWS
    vec = pl.BlockSpec((1, d), lambda i: (0, 0))

    def full(a):
        return pl.BlockSpec(a.shape, lambda i: (0,) * a.ndim)

    return pl.pallas_call(
        _mix_out_kernel,
        grid=(s // tm,),
        in_specs=[
            pl.BlockSpec((tm, d), lambda i: (i, 0)),
            vec, vec, vec, vec,
            pl.BlockSpec((tm, oa.shape[1]), lambda i: (i, 0)),
            pl.BlockSpec((tm, od.shape[1]), lambda i: (i, 0)),
            full(w_ba), full(w_bb), full(w_gate), full(b_gate), full(w_o),
        ],
        out_specs=pl.BlockSpec((tm, d), lambda i: (i, 0)),
        out_shape=jax.ShapeDtypeStruct((s, d), F32),
        compiler_params=_params("parallel"),
        name="mix_out",
    )(h, ng, sh, sc, gt, oa, od, w_ba, w_bb, w_gate, b_gate, w_o)


def _axial_angles_t(seq):
    rows = seq // GRID_W
    row = jnp.broadcast_to(jnp.arange(rows)[:, None], (rows, GRID_W)).reshape(seq) - rows // 2
    col = jnp.broadcast_to(jnp.arange(GRID_W)[None, :], (rows, GRID_W)).reshape(seq) - GRID_W // 2
    inv = 1.0 / (ROPE_THETA ** (jnp.arange(0, ROPE_AXIS_DIM, 2, dtype=F32) / ROPE_AXIS_DIM))
    return inv[:, None] * row.astype(F32)[None, :], inv[:, None] * col.astype(F32)[None, :]


def kernel(x, c, ada_w, ada_b, norm_g, ffn_wg, ffn_wu, ffn_wd, w_in, qk_g, lam_p, subln_g, w_ba, w_bb,
           w_gate, b_gate, w_o, final_g):
    batch, s, d = x.shape
    assert batch == 1 and s % KEY_TILE == 0 and MIX_ROWS == KEY_TILE
    depth = ada_w.shape[0]
    h = x.reshape(s, d)

    mod = _ada_mod(c.reshape(d, 1), ada_w, ada_b)
    angr_t, angc_t = _axial_angles_t(s)
    slopes = 2.0 ** (-8.0 * jnp.arange(1, B_HEADS + 1, dtype=F32) / B_HEADS)
    coef = jnp.stack([-slopes * LOG2E, 1.0 / (slopes * LOG2E * KEY_TILE)])
    fg = final_g.reshape(1, d)

    for l in range(depth):
        sh1, sc1, g1, sh2, sc2, g2, sh3, sc3, g3 = [mod[l, :, j * d:(j + 1) * d] for j in range(N_ADA)]
        lam_init = 0.8 - 0.6 * math.exp(-0.3 * l)
        ng = norm_g[l].reshape(3, 1, d)
        wg, wu, wd = ffn_wg[l].astype(BF16), ffn_wu[l].astype(BF16), ffn_wd[l].astype(BF16)

        h = _ffn(h, ng[0], sh1, sc1, g1, wg[0], wu[0], wd[0], fg, final_norm=False)

        qta, ka, vta, qtd, kd, vtd, qn, kn = _mix_in(
            h, ng[1], sh2, sc2, w_in[l].T.astype(BF16), angr_t, angc_t, qk_g[l].T)
        nk = s // KEY_TILE
        oa = _gqa(qta, ka.reshape(nk, KEY_TILE, A_KV_COLS), vta, qn, kn)
        od = _diff(coef, qtd, kd.reshape(nk, KEY_TILE, B_QK_COLS), vtd, qn, kn, lam_p[l],
                   subln_g[l].reshape(B_V_DIM, 1), lam_init=lam_init)
        h = _mix_out(h, ng[1], sh2, sc2, g2, oa, od, w_ba[l].astype(BF16), w_bb[l].astype(BF16),
                     w_gate[l].astype(BF16), b_gate[l].reshape(1, -1), w_o[l].astype(BF16))

        h = _ffn(h, ng[2], sh3, sc3, g3, wg[1], wu[1], wd[1], fg, final_norm=(l == depth - 1))
    return h.reshape(batch, s, d)
```

```python
import functools
import math

import jax
import jax.numpy as jnp
from jax import lax
from jax.experimental import pallas as pl
from jax.experimental.pallas import tpu as pltpu

F32 = jnp.float32
BF16 = jnp.bfloat16

GRID_W = 64
HEAD_DIM = 64
A_Q_HEADS = 8
A_KV_HEADS = 2
A_GROUP = A_Q_HEADS // A_KV_HEADS
B_HEADS = 4
B_V_DIM = 2 * HEAD_DIM
A_Q_COLS = A_Q_HEADS * HEAD_DIM
A_KV_COLS = A_KV_HEADS * HEAD_DIM
B_QK_COLS = 2 * B_HEADS * HEAD_DIM
B_V_COLS = B_HEADS * B_V_DIM
N_ADA = 9
EPS = 1e-6
ROPE_THETA = 10000.0
ROPE_AXIS_DIM = HEAD_DIM // 2
ROPE_HALF = ROPE_AXIS_DIM // 2

LOG2E = math.log2(math.e)
QK_SCALE = HEAD_DIM ** -0.5
NEG_BIG = -1e30

BF16_SUBLANES = 16
LANES = 128
ONES_ROWS = BF16_SUBLANES
SKIP_LOG2 = 160.0
FAST_LOG2_LIMIT = 60.0
NORM_ROWS_A = 8
NORM_ROWS = NORM_ROWS_A + 2 * B_HEADS

VMEM_LIMIT_BYTES = 56 * 1024 * 1024

FFN_ROWS = 512
MIX_ROWS = 512
KEY_TILE = 512
GQA_Q_TILE = 512
DIFF_Q_TILE = 512
ADA_COLS = 1152


def _params(*sem):
    return pltpu.CompilerParams(dimension_semantics=sem, vmem_limit_bytes=VMEM_LIMIT_BYTES)


def _rms_rows(x):
    return x * lax.rsqrt(jnp.mean(x * x, axis=-1, keepdims=True) + EPS)


def _ada_kernel(c_ref, w_ref, b_ref, o_ref):
    c = c_ref[...]
    act = c / (1.0 + jnp.exp(-c))
    o_ref[...] = jnp.sum(w_ref[...] * act, axis=0, keepdims=True) + b_ref[...]


def _ada_mod(c_col, ada_w, ada_b):
    n_layers, d, n = ada_w.shape
    return pl.pallas_call(
        _ada_kernel,
        grid=(n_layers, n // ADA_COLS),
        in_specs=[
            pl.BlockSpec((d, 1), lambda l, j: (0, 0)),
            pl.BlockSpec((None, d, ADA_COLS), lambda l, j: (l, 0, j)),
            pl.BlockSpec((None, 1, ADA_COLS), lambda l, j: (l, 0, j)),
        ],
        out_specs=pl.BlockSpec((None, 1, ADA_COLS), lambda l, j: (l, 0, j)),
        out_shape=jax.ShapeDtypeStruct((n_layers, 1, n), F32),
        compiler_params=_params("parallel", "parallel"),
        name="ada_mod",
    )(c_col, ada_w, ada_b.reshape(n_layers, 1, n))


def _ffn_kernel(h_ref, ng_ref, sh_ref, sc_ref, gt_ref, wg_ref, wu_ref, wd_ref, fg_ref, o_ref,
                n_scr, acc_scr, *, final_norm):
    f = pl.program_id(1)

    @pl.when(f == 0)
    def _():
        n = _rms_rows(h_ref[...]) * ng_ref[...]
        n = n * (1.0 + sc_ref[...]) + sh_ref[...]
        n_scr[...] = n.astype(BF16)
        acc_scr[...] = jnp.zeros_like(acc_scr)

    n = n_scr[...]
    hg = jnp.dot(n, wg_ref[...], preferred_element_type=F32)
    hu = jnp.dot(n, wu_ref[...], preferred_element_type=F32)
    a = (hg / (1.0 + jnp.exp(-hg))) * hu
    acc_scr[...] += jnp.dot(a.astype(BF16), wd_ref[...], preferred_element_type=F32)

    @pl.when(f == pl.num_programs(1) - 1)
    def _():
        out = h_ref[...] + (0.5 * gt_ref[...]) * acc_scr[...]
        if final_norm:
            out = _rms_rows(out) * fg_ref[...]
        o_ref[...] = out


def _ffn(h, ng, sh, sc, gt, wg, wu, wd, fg, *, final_norm):
    s, d = h.shape
    f_dim = wg.shape[1]
    tf = f_dim // 2
    vec = pl.BlockSpec((1, d), lambda i, f: (0, 0))
    return pl.pallas_call(
        functools.partial(_ffn_kernel, final_norm=final_norm),
        grid=(s // FFN_ROWS, f_dim // tf),
        in_specs=[
            pl.BlockSpec((FFN_ROWS, d), lambda i, f: (i, 0)),
            vec, vec, vec, vec,
            pl.BlockSpec((d, tf), lambda i, f: (0, f)),
            pl.BlockSpec((d, tf), lambda i, f: (0, f)),
            pl.BlockSpec((tf, d), lambda i, f: (f, 0)),
            vec,
        ],
        out_specs=pl.BlockSpec((FFN_ROWS, d), lambda i, f: (i, 0)),
        out_shape=jax.ShapeDtypeStruct((s, d), F32),
        scratch_shapes=[pltpu.VMEM((FFN_ROWS, d), BF16), pltpu.VMEM((FFN_ROWS, d), F32)],
        compiler_params=_params("parallel", "arbitrary"),
        name="ffn_final" if final_norm else "ffn",
    )(h, ng, sh, sc, gt, wg, wu, wd, fg)


def _rope_t(x, cr, sr, cc, sc):
    h = ROPE_HALF
    x1r, x2r, x1c, x2c = x[0:h], x[h:2 * h], x[2 * h:3 * h], x[3 * h:4 * h]
    return jnp.concatenate(
        [x1r * cr - x2r * sr, x2r * cr + x1r * sr, x1c * cc - x2c * sc, x2c * cc + x1c * sc], axis=0)


def _head_norm_t(x, g_col):
    ms = jnp.mean(x * x, axis=0, keepdims=True)
    return x * lax.rsqrt(ms + EPS) * g_col


def _max_sq_norm(x_bf16):
    xf = x_bf16.astype(F32)
    n2 = jnp.sum(xf * xf, axis=0, keepdims=True)
    return jnp.broadcast_to(jnp.max(n2, axis=1, keepdims=True), (1, LANES))


def _with_ones(v_t):
    return jnp.concatenate([v_t, jnp.ones((ONES_ROWS, v_t.shape[1]), F32)], axis=0).astype(BF16)


def _mix_in_kernel(h_ref, ng_ref, sh_ref, sc_ref, wt_ref, angr_ref, angc_ref, qkg_ref,
                   qta_ref, ka_ref, vta_ref, qtd_ref, kd_ref, vtd_ref, qn_ref, kn_ref):
    n = _rms_rows(h_ref[...]) * ng_ref[...]
    n = (n * (1.0 + sc_ref[...]) + sh_ref[...]).astype(BF16)
    pt = lax.dot_general(wt_ref[...], n, (((1,), (1,)), ((), ())), preferred_element_type=F32)
    t = pt.shape[1]

    cr, sr = jnp.cos(angr_ref[...]), jnp.sin(angr_ref[...])
    cc, sc = jnp.cos(angc_ref[...]), jnp.sin(angc_ref[...])
    gq = qkg_ref[:, 0:1]
    gk = qkg_ref[:, 1:2]
    zeros = jnp.zeros((HEAD_DIM, t), BF16)
    qscale = QK_SCALE * LOG2E

    qn, kn = [], []
    for hd in range(A_Q_HEADS):
        q = _rope_t(_head_norm_t(pt[hd * HEAD_DIM:(hd + 1) * HEAD_DIM], gq), cr, sr, cc, sc) * qscale
        q = q.astype(BF16)
        qta_ref[hd] = jnp.concatenate([q, zeros] if hd // A_GROUP == 0 else [zeros, q], axis=0)
        qn.append(_max_sq_norm(q))

    c1 = A_Q_COLS
    c2 = c1 + A_KV_COLS
    c3 = c2 + A_KV_COLS
    c4 = c3 + B_QK_COLS
    c5 = c4 + B_QK_COLS
    kt = jnp.concatenate(
        [_rope_t(_head_norm_t(pt[c1 + j * HEAD_DIM:c1 + (j + 1) * HEAD_DIM], gk), cr, sr, cc, sc)
         for j in range(A_KV_HEADS)], axis=0)
    kt = kt.astype(BF16)
    ka_ref[...] = kt.T
    kn += [_max_sq_norm(kt[j * HEAD_DIM:(j + 1) * HEAD_DIM]) for j in range(A_KV_HEADS)]
    kn.append(jnp.zeros((NORM_ROWS_A - A_KV_HEADS, LANES), F32))
    for j in range(A_KV_HEADS):
        vta_ref[j] = _with_ones(pt[c2 + j * HEAD_DIM:c2 + (j + 1) * HEAD_DIM])

    for cmb in range(2 * B_HEADS):
        q = (pt[c3 + cmb * HEAD_DIM:c3 + (cmb + 1) * HEAD_DIM] * qscale).astype(BF16)
        qtd_ref[cmb] = jnp.concatenate([q, zeros] if cmb % 2 == 0 else [zeros, q], axis=0)
        qn.append(_max_sq_norm(q))
        kn.append(_max_sq_norm(pt[c4 + cmb * HEAD_DIM:c4 + (cmb + 1) * HEAD_DIM].astype(BF16)))
    qn_ref[...] = jnp.concatenate(qn, axis=0)
    kn_ref[...] = jnp.concatenate(kn, axis=0)
    kd_ref[...] = pt[c4:c5].T.astype(BF16)
    for hd in range(B_HEADS):
        vtd_ref[hd] = _with_ones(pt[c5 + hd * B_V_DIM:c5 + (hd + 1) * B_V_DIM])


def _mix_in(h, ng, sh, sc, w_in_t, angr_t, angc_t, qkg_t):
    s, d = h.shape
    tm = MIX_ROWS
    nt = s // tm
    in_cols = w_in_t.shape[0]
    n_sets = 2 * B_HEADS
    assert A_Q_HEADS == NORM_ROWS_A
    vec = pl.BlockSpec((1, d), lambda i: (0, 0))
    out_shape = (
        jax.ShapeDtypeStruct((A_Q_HEADS, 2 * HEAD_DIM, s), BF16),
        jax.ShapeDtypeStruct((s, A_KV_COLS), BF16),
        jax.ShapeDtypeStruct((A_KV_HEADS, nt, HEAD_DIM + ONES_ROWS, tm), BF16),
        jax.ShapeDtypeStruct((n_sets, 2 * HEAD_DIM, s), BF16),
        jax.ShapeDtypeStruct((s, B_QK_COLS), BF16),
        jax.ShapeDtypeStruct((B_HEADS, nt, B_V_DIM + ONES_ROWS, tm), BF16),
        jax.ShapeDtypeStruct((nt, NORM_ROWS, LANES), F32),
        jax.ShapeDtypeStruct((nt, NORM_ROWS, LANES), F32),
    )
    out_specs = (
        pl.BlockSpec((A_Q_HEADS, 2 * HEAD_DIM, tm), lambda i: (0, 0, i)),
        pl.BlockSpec((tm, A_KV_COLS), lambda i: (i, 0)),
        pl.BlockSpec((A_KV_HEADS, None, HEAD_DIM + ONES_ROWS, tm), lambda i: (0, i, 0, 0)),
        pl.BlockSpec((n_sets, 2 * HEAD_DIM, tm), lambda i: (0, 0, i)),
        pl.BlockSpec((tm, B_QK_COLS), lambda i: (i, 0)),
        pl.BlockSpec((B_HEADS, None, B_V_DIM + ONES_ROWS, tm), lambda i: (0, i, 0, 0)),
        pl.BlockSpec((None, NORM_ROWS, LANES), lambda i: (i, 0, 0)),
        pl.BlockSpec((None, NORM_ROWS, LANES), lambda i: (i, 0, 0)),
    )
    return pl.pallas_call(
        _mix_in_kernel,
        grid=(nt,),
        in_specs=[
            pl.BlockSpec((tm, d), lambda i: (i, 0)),
            vec, vec, vec,
            pl.BlockSpec((in_cols, d), lambda i: (0, 0)),
            pl.BlockSpec((ROPE_HALF, tm), lambda i: (0, i)),
            pl.BlockSpec((ROPE_HALF, tm), lambda i: (0, i)),
            pl.BlockSpec((HEAD_DIM, 2), lambda i: (0, 0)),
        ],
        out_specs=out_specs,
        out_shape=out_shape,
        compiler_params=_params("parallel"),
        name="mix_in",
    )(h, ng, sh, sc, w_in_t, angr_t, angc_t, qkg_t)


def _softmax_tile(s, m_ref, alpha_ref, p_ref):
    tk, n = s.shape
    part = jnp.max(s.reshape(tk // BF16_SUBLANES, BF16_SUBLANES, n), axis=0)
    m_old = m_ref[...]
    m_new = jnp.maximum(m_old, jnp.max(part.astype(F32), axis=0, keepdims=True))
    alpha_ref[...] = jnp.exp2(m_old - m_new)
    m_ref[...] = m_new
    p_ref[...] = jnp.exp2(s - m_new.astype(BF16))


def _pipelined_sweep(prep, qk, softmax, pv, n_chunks, lo, n_pairs, last):
    ctx = prep(lo)
    for c in range(n_chunks):
        qk(lo, 0, c, ctx)

    def body(j, carry):
        a = lo + 2 * j
        prev = jnp.maximum(a - 1, 0)
        nxt = jnp.minimum(a + 2, last)
        ctx = prep(a + 1)
        for c in range(n_chunks):
            softmax(0, c)
            pv(prev, 1, c)
            qk(a + 1, 1, c, ctx)
        ctx = prep(nxt)
        for c in range(n_chunks):
            softmax(1, c)
            pv(a, 0, c)
            qk(nxt, 0, c, ctx)
        return carry

    lax.fori_loop(0, n_pairs, body, 0)
    for c in range(n_chunks):
        pv(last, 1, c)


def _direct_sweep(prep, qk_exp, pv, n_chunks, lo, n_tiles, last):
    def phase(kt_new, slot_new, kt_old):
        ctx = prep(kt_new)
        for c in range(n_chunks):
            qk_exp(kt_new, slot_new, c, ctx)
            pv(kt_old, 1 - slot_new, c)

    ctx = prep(lo)
    for c in range(n_chunks):
        qk_exp(lo, 0, c, ctx)

    def body(j, carry):
        a = lo + 2 * j
        phase(a + 1, 1, a)
        phase(a + 2, 0, a + 1)
        return carry

    n_iter = (n_tiles - 1) // 2 if isinstance(n_tiles, int) else lax.shift_right_logical(n_tiles - 1, 1)
    lax.fori_loop(0, n_iter, body, 0)

    def tail_odd():
        for c in range(n_chunks):
            pv(last, 0, c)
        return 0

    def tail_even():
        phase(last, 1, last - 1)
        for c in range(n_chunks):
            pv(last, 1, c)
        return 0

    if isinstance(n_tiles, int):
        (tail_odd if n_tiles % 2 else tail_even)()
    else:
        lax.cond(jnp.bitwise_and(n_tiles, 1) == 1, tail_odd, tail_even)


def _score_bound(qn_tile, kn_ref, q_rows, k_row_of_q):
    k_max = jnp.max(kn_ref[...], axis=0)
    row = lax.broadcasted_iota(jnp.int32, k_max.shape, 0)
    return jnp.sqrt(jnp.where(q_rows(row), qn_tile, 0.0) * k_row_of_q(row, k_max))


GQA_CHUNKS = 2


def _gqa_kernel(qt_ref, k_ref, vt_ref, qn_ref, kn_ref, o_ref, q_scr, s_buf, p_buf, alpha_buf, m_scr,
                acc_scr):
    g = pl.program_id(0)
    qi = pl.program_id(1)
    nk, tk, _ = k_ref.shape
    tq = qt_ref.shape[2]
    per = A_GROUP // GQA_CHUNKS
    for hh in range(A_GROUP):
        q_scr[hh // per, :, (hh % per) * tq:(hh % per + 1) * tq] = qt_ref[hh]
    acc_scr[...] = jnp.zeros_like(acc_scr)

    def qk(kt, slot, c, _):
        s_buf[slot, c] = jnp.dot(k_ref[kt], q_scr[c], preferred_element_type=F32).astype(BF16)

    def softmax(slot, c):
        _softmax_tile(s_buf[slot, c], m_scr.at[c], alpha_buf.at[slot, c], p_buf.at[slot, c])

    def pv(kt, slot, c):
        acc_scr[c] = alpha_buf[slot, c] * acc_scr[c] + jnp.dot(
            vt_ref[kt], p_buf[slot, c], preferred_element_type=F32)

    def qk_exp(kt, slot, c, _):
        s = jnp.dot(k_ref[kt], q_scr[c], preferred_element_type=F32)
        p_buf[slot, c] = jnp.exp2(s.astype(BF16))

    def pv_plain(kt, slot, c):
        acc_scr[c] += jnp.dot(vt_ref[kt], p_buf[slot, c], preferred_element_type=F32)

    def kmax_of_group(row, k_max):
        return jnp.max(jnp.where(row == g, k_max, 0.0), axis=0, keepdims=True)

    bound = _score_bound(qn_ref[lax.div(qi * tq, tk)], kn_ref,
                         lambda row: jnp.logical_and(row >= g * A_GROUP, row < (g + 1) * A_GROUP),
                         kmax_of_group)

    def direct():
        _direct_sweep(lambda kt: None, qk_exp, pv_plain, GQA_CHUNKS, 0, nk, nk - 1)
        return 0

    def online():
        m_scr[...] = jnp.full_like(m_scr, NEG_BIG)
        p_buf[1] = jnp.zeros(p_buf.shape[1:], BF16)
        alpha_buf[1] = jnp.ones(alpha_buf.shape[1:], F32)
        _pipelined_sweep(lambda kt: None, qk, softmax, pv, GQA_CHUNKS, 0, nk // 2, nk - 1)
        return 0

    lax.cond(jnp.max(bound) <= FAST_LOG2_LIMIT, direct, online)
    heads = []
    for c in range(GQA_CHUNKS):
        acc = acc_scr[c]
        ot = acc[:HEAD_DIM] * (1.0 / acc[HEAD_DIM:HEAD_DIM + 1])
        heads += [ot[:, i * tq:(i + 1) * tq] for i in range(per)]
    o_ref[...] = jnp.concatenate(heads, axis=0).T.astype(BF16)


def _gqa(qta, ka3, vta, qn, kn):
    s = qta.shape[2]
    nk, tk, _ = ka3.shape
    tq = GQA_Q_TILE
    assert nk % 2 == 0 and tk % tq == 0 and tk == MIX_ROWS
    nc = GQA_CHUNKS
    w = A_GROUP // nc * tq
    v_rows = vta.shape[2]
    return pl.pallas_call(
        _gqa_kernel,
        grid=(A_KV_HEADS, s // tq),
        in_specs=[
            pl.BlockSpec((A_GROUP, 2 * HEAD_DIM, tq), lambda g, i: (g, 0, i)),
            pl.BlockSpec((nk, tk, A_KV_COLS), lambda g, i: (0, 0, 0)),
            pl.BlockSpec((None, nk, v_rows, tk), lambda g, i: (g, 0, 0, 0)),
            pl.BlockSpec(qn.shape, lambda g, i: (0, 0, 0)),
            pl.BlockSpec(kn.shape, lambda g, i: (0, 0, 0)),
        ],
        out_specs=pl.BlockSpec((tq, A_GROUP * HEAD_DIM), lambda g, i: (i, g)),
        out_shape=jax.ShapeDtypeStruct((s, A_Q_COLS), BF16),
        scratch_shapes=[
            pltpu.VMEM((nc, 2 * HEAD_DIM, w), BF16),
            pltpu.VMEM((2, nc, tk, w), BF16),
            pltpu.VMEM((2, nc, tk, w), BF16),
            pltpu.VMEM((2, nc, 1, w), F32),
            pltpu.VMEM((nc, 1, w), F32),
            pltpu.VMEM((nc, v_rows, w), F32),
        ],
        compiler_params=_params("parallel", "parallel"),
        name="gqa_attn",
    )(qta, ka3, vta, qn, kn)


def _diff_kernel(coef_ref, qt0_ref, qt1_ref, k0_ref, k1_ref, vt_ref, qn_ref, kn_ref, lamp_ref, sg_ref,
                 o_ref, rel_scr, s_buf, p_buf, alpha_buf, m_scr, acc_scr, *, lam_init):
    hd = pl.program_id(0)
    qi = pl.program_id(1)
    nk, tk, _ = k0_ref.shape
    tq = qt0_ref.shape[1]
    nslope = coef_ref[0, hd]
    inv_tile_drop = coef_ref[1, hd]
    acc_scr[...] = jnp.zeros_like(acc_scr)

    @pl.when(jnp.logical_and(hd == 0, qi == 0))
    def _():
        rel_scr[...] = (lax.broadcasted_iota(jnp.int32, (tk, tq), 1)
                        - lax.broadcasted_iota(jnp.int32, (tk, tq), 0)).astype(F32)

    bound = jnp.max(_score_bound(
        qn_ref[qi], kn_ref,
        lambda row: jnp.logical_or(row == NORM_ROWS_A + hd, row == NORM_ROWS_A + B_HEADS + hd),
        lambda row, k_max: k_max))
    use_direct = bound <= FAST_LOG2_LIMIT
    drop = jnp.where(use_direct, bound, 2.0 * bound) + SKIP_LOG2
    reach = jnp.minimum(jnp.floor(drop * inv_tile_drop) + 1.0, float(nk)).astype(jnp.int32)
    lo = jnp.maximum(qi - reach, 0)
    hi = jnp.minimum(qi + reach, nk - 1)

    q0 = qi * tq
    qts = (qt0_ref, qt1_ref)
    ks = (k0_ref, k1_ref)

    def alibi(kt):
        off = (q0 - kt * tk).astype(F32)
        return jnp.abs(rel_scr[...] + off) * nslope

    def qk(kt, slot, mp, bias):
        s = jnp.dot(ks[mp][kt], qts[mp][...], preferred_element_type=F32) + bias
        s_buf[slot, mp] = s.astype(BF16)

    def softmax(slot, mp):
        _softmax_tile(s_buf[slot, mp], m_scr.at[mp], alpha_buf.at[slot, mp], p_buf.at[slot, mp])

    def pv(kt, slot, mp):
        acc_scr[mp] = alpha_buf[slot, mp] * acc_scr[mp] + jnp.dot(
            vt_ref[kt], p_buf[slot, mp], preferred_element_type=F32)

    def qk_exp(kt, slot, mp, bias):
        s = jnp.dot(ks[mp][kt], qts[mp][...], preferred_element_type=F32) + bias
        p_buf[slot, mp] = jnp.exp2(s.astype(BF16))

    def pv_plain(kt, slot, mp):
        acc_scr[mp] += jnp.dot(vt_ref[kt], p_buf[slot, mp], preferred_element_type=F32)

    def direct():
        _direct_sweep(alibi, qk_exp, pv_plain, 2, lo, hi - lo + 1, hi)
        return 0

    def online():
        m_scr[...] = jnp.full_like(m_scr, NEG_BIG)
        p_buf[1] = jnp.zeros(p_buf.shape[1:], BF16)
        alpha_buf[1] = jnp.ones(alpha_buf.shape[1:], F32)
        odd = jnp.bitwise_and(hi - lo + 1, 1)
        grow_hi = jnp.where(hi < nk - 1, odd, 0)
        hi2 = hi + grow_hi
        lo2 = lo - (odd - grow_hi)
        _pipelined_sweep(alibi, qk, softmax, pv, 2, lo2, lax.shift_right_logical(hi2 - lo2 + 1, 1), hi2)
        return 0

    lax.cond(use_direct, direct, online)

    lp = lamp_ref[...]
    lam = (jnp.exp(jnp.sum(lp[0:1] * lp[1:2], axis=1, keepdims=True))
           - jnp.exp(jnp.sum(lp[2:3] * lp[3:4], axis=1, keepdims=True)) + lam_init)
    a0, a1 = acc_scr[0], acc_scr[1]
    ot = (a0[:B_V_DIM] * (1.0 / a0[B_V_DIM:B_V_DIM + 1])
          - lam * (a1[:B_V_DIM] * (1.0 / a1[B_V_DIM:B_V_DIM + 1])))
    ot = ot * lax.rsqrt(jnp.mean(ot * ot, axis=0, keepdims=True) + EPS)
    ot = ot * sg_ref[...] * (1.0 - lam_init)
    o_ref[...] = ot.T.astype(BF16)


def _diff(coef, qtd, kd3, vtd, qn, kn, lam_p, subln_col, *, lam_init):
    s = qtd.shape[2]
    nk, tk, _ = kd3.shape
    tq = DIFF_Q_TILE
    assert nk % 2 == 0 and tq == tk == MIX_ROWS
    lanes = 2 * HEAD_DIM
    v_rows = vtd.shape[2]
    return pl.pallas_call(
        functools.partial(_diff_kernel, lam_init=lam_init),
        grid=(B_HEADS, s // tq),
        in_specs=[
            pl.BlockSpec(memory_space=pltpu.SMEM),
            pl.BlockSpec((None, lanes, tq), lambda h, i: (h, 0, i)),
            pl.BlockSpec((None, lanes, tq), lambda h, i: (B_HEADS + h, 0, i)),
            pl.BlockSpec((nk, tk, lanes), lambda h, i: (0, 0, h // 2)),
            pl.BlockSpec((nk, tk, lanes), lambda h, i: (0, 0, B_HEADS // 2 + h // 2)),
            pl.BlockSpec((None, nk, v_rows, tk), lambda h, i: (h, 0, 0, 0)),
            pl.BlockSpec(qn.shape, lambda h, i: (0, 0, 0)),
            pl.BlockSpec(kn.shape, lambda h, i: (0, 0, 0)),
            pl.BlockSpec((4, HEAD_DIM), lambda h, i: (0, 0)),
            pl.BlockSpec((B_V_DIM, 1), lambda h, i: (0, 0)),
        ],
        out_specs=pl.BlockSpec((tq, B_V_DIM), lambda h, i: (i, h)),
        out_shape=jax.ShapeDtypeStruct((s, B_V_COLS), BF16),
        scratch_shapes=[
            pltpu.VMEM((tk, tq), F32),
            pltpu.VMEM((2, 2, tk, tq), BF16),
            pltpu.VMEM((2, 2, tk, tq), BF16),
            pltpu.VMEM((2, 2, 1, tq), F32),
            pltpu.VMEM((2, 1, tq), F32),
            pltpu.VMEM((2, v_rows, tq), F32),
        ],
        compiler_params=_params("arbitrary", "arbitrary"),
        name="diff_attn",
    )(coef, qtd, qtd, kd3, kd3, vtd, qn, kn, lam_p, subln_col)


def _mix_out_kernel(h_ref, ng_ref, sh_ref, sc_ref, gt_ref, oa_ref, od_ref, wba_ref, wbb_ref,
                    wgate_ref, bgate_ref, wo_ref, o_ref):
    h = h_ref[...]
    d = h.shape[1]
    n = _rms_rows(h) * ng_ref[...]
    n = (n * (1.0 + sc_ref[...]) + sh_ref[...]).astype(BF16)
    z = jnp.dot(n, wgate_ref[...], preferred_element_type=F32) + bgate_ref[...]
    g = 1.0 / (1.0 + jnp.exp(-z))
    ya = jnp.dot(oa_ref[...], wba_ref[...], preferred_element_type=F32)
    yb = jnp.dot(od_ref[...], wbb_ref[...], preferred_element_type=F32)
    mix = (g[:, :d] * ya + g[:, d:] * yb).astype(BF16)
    y = jnp.dot(mix, wo_ref[...], preferred_element_type=F32)
    o_ref[...] = h + gt_ref[...] * y


def _mix_out(h, ng, sh, sc, gt, oa, od, w_ba, w_bb, w_gate, b_gate, w_o):
    s, d = h.shape
    tm = MIX_ROWS
    vec = pl.BlockSpec((1, d), lambda i: (0, 0))

    def full(a):
        return pl.BlockSpec(a.shape, lambda i: (0,) * a.ndim)

    return pl.pallas_call(
        _mix_out_kernel,
        grid=(s // tm,),
        in_specs=[
            pl.BlockSpec((tm, d), lambda i: (i, 0)),
            vec, vec, vec, vec,
            pl.BlockSpec((tm, oa.shape[1]), lambda i: (i, 0)),
            pl.BlockSpec((tm, od.shape[1]), lambda i: (i, 0)),
            full(w_ba), full(w_bb), full(w_gate), full(b_gate), full(w_o),
        ],
        out_specs=pl.BlockSpec((tm, d), lambda i: (i, 0)),
        out_shape=jax.ShapeDtypeStruct((s, d), F32),
        compiler_params=_params("parallel"),
        name="mix_out",
    )(h, ng, sh, sc, gt, oa, od, w_ba, w_bb, w_gate, b_gate, w_o)


def _axial_angles_t(seq):
    rows = seq // GRID_W
    row = jnp.broadcast_to(jnp.arange(rows)[:, None], (rows, GRID_W)).reshape(seq) - rows // 2
    col = jnp.broadcast_to(jnp.arange(GRID_W)[None, :], (rows, GRID_W)).reshape(seq) - GRID_W // 2
    inv = 1.0 / (ROPE_THETA ** (jnp.arange(0, ROPE_AXIS_DIM, 2, dtype=F32) / ROPE_AXIS_DIM))
    return inv[:, None] * row.astype(F32)[None, :], inv[:, None] * col.astype(F32)[None, :]


def kernel(x, c, ada_w, ada_b, norm_g, ffn_wg, ffn_wu, ffn_wd, w_in, qk_g, lam_p, subln_g, w_ba, w_bb,
           w_gate, b_gate, w_o, final_g):
    batch, s, d = x.shape
    assert batch == 1 and s % KEY_TILE == 0 and MIX_ROWS == KEY_TILE
    depth = ada_w.shape[0]
    h = x.reshape(s, d)

    mod = _ada_mod(c.reshape(d, 1), ada_w, ada_b)
    angr_t, angc_t = _axial_angles_t(s)
    slopes = 2.0 ** (-8.0 * jnp.arange(1, B_HEADS + 1, dtype=F32) / B_HEADS)
    coef = jnp.stack([-slopes * LOG2E, 1.0 / (slopes * LOG2E * KEY_TILE)])
    fg = final_g.reshape(1, d)

    for l in range(depth):
        sh1, sc1, g1, sh2, sc2, g2, sh3, sc3, g3 = [mod[l, :, j * d:(j + 1) * d] for j in range(N_ADA)]
        lam_init = 0.8 - 0.6 * math.exp(-0.3 * l)
        ng = norm_g[l].reshape(3, 1, d)
        wg, wu, wd = ffn_wg[l].astype(BF16), ffn_wu[l].astype(BF16), ffn_wd[l].astype(BF16)

        h = _ffn(h, ng[0], sh1, sc1, g1, wg[0], wu[0], wd[0], fg, final_norm=False)

        qta, ka, vta, qtd, kd, vtd, qn, kn = _mix_in(
            h, ng[1], sh2, sc2, w_in[l].T.astype(BF16), angr_t, angc_t, qk_g[l].T)
        nk = s // KEY_TILE
        oa = _gqa(qta, ka.reshape(nk, KEY_TILE, A_KV_COLS), vta, qn, kn)
        od = _diff(coef, qtd, kd.reshape(nk, KEY_TILE, B_QK_COLS), vtd, qn, kn, lam_p[l],
                   subln_g[l].reshape(B_V_DIM, 1), lam_init=lam_init)
        h = _mix_out(h, ng[1], sh2, sc2, g2, oa, od, w_ba[l].astype(BF16), w_bb[l].astype(BF16),
                     w_gate[l].astype(BF16), b_gate[l].reshape(1, -1), w_o[l].astype(BF16))

        h = _ffn(h, ng[2], sh3, sc3, g3, wg[1], wu[1], wd[1], fg, final_norm=(l == depth - 1))
    return h.reshape(batch, s, d)
```

```python
import functools
import math

import jax
import jax.numpy as jnp
from jax import lax
from jax.experimental import pallas as pl
from jax.experimental.pallas import tpu as pltpu

F32 = jnp.float32
BF16 = jnp.bfloat16

GRID_W = 64
HEAD_DIM = 64
A_Q_HEADS = 8
A_KV_HEADS = 2
A_GROUP = A_Q_HEADS // A_KV_HEADS
B_HEADS = 4
B_V_DIM = 2 * HEAD_DIM
A_Q_COLS = A_Q_HEADS * HEAD_DIM
A_KV_COLS = A_KV_HEADS * HEAD_DIM
B_QK_COLS = 2 * B_HEADS * HEAD_DIM
B_V_COLS = B_HEADS * B_V_DIM
N_ADA = 9
EPS = 1e-6
ROPE_THETA = 10000.0
ROPE_AXIS_DIM = HEAD_DIM // 2
ROPE_HALF = ROPE_AXIS_DIM // 2

LOG2E = math.log2(math.e)
QK_SCALE = HEAD_DIM ** -0.5
NEG_BIG = -1e30

BF16_SUBLANES = 16
LANES = 128
ONES_ROWS = BF16_SUBLANES
SKIP_LOG2 = 160.0
FAST_LOG2_LIMIT = 60.0
NORM_ROWS_A = 8
NORM_ROWS = NORM_ROWS_A + 2 * B_HEADS

VMEM_LIMIT_BYTES = 56 * 1024 * 1024

FFN_ROWS = 512
MIX_ROWS = 512
KEY_TILE = 512
GQA_Q_TILE = 512
DIFF_Q_TILE = 512
ADA_COLS = 1152


def _params(*sem):
    return pltpu.CompilerParams(dimension_semantics=sem, vmem_limit_bytes=VMEM_LIMIT_BYTES)


def _rms_rows(x):
    return x * lax.rsqrt(jnp.mean(x * x, axis=-1, keepdims=True) + EPS)


def _ada_kernel(c_ref, w_ref, b_ref, o_ref):
    c = c_ref[...]
    act = c / (1.0 + jnp.exp(-c))
    o_ref[...] = jnp.sum(w_ref[...] * act, axis=0, keepdims=True) + b_ref[...]


def _ada_mod(c_col, ada_w, ada_b):
    n_layers, d, n = ada_w.shape
    return pl.pallas_call(
        _ada_kernel,
        grid=(n_layers, n // ADA_COLS),
        in_specs=[
            pl.BlockSpec((d, 1), lambda l, j: (0, 0)),
            pl.BlockSpec((None, d, ADA_COLS), lambda l, j: (l, 0, j)),
            pl.BlockSpec((None, 1, ADA_COLS), lambda l, j: (l, 0, j)),
        ],
        out_specs=pl.BlockSpec((None, 1, ADA_COLS), lambda l, j: (l, 0, j)),
        out_shape=jax.ShapeDtypeStruct((n_layers, 1, n), F32),
        compiler_params=_params("parallel", "parallel"),
        name="ada_mod",
    )(c_col, ada_w, ada_b.reshape(n_layers, 1, n))


def _ffn_kernel(h_ref, ng_ref, sh_ref, sc_ref, gt_ref, wg_ref, wu_ref, wd_ref, fg_ref, o_ref,
                n_scr, acc_scr, *, final_norm):
    f = pl.program_id(1)

    @pl.when(f == 0)
    def _():
        n = _rms_rows(h_ref[...]) * ng_ref[...]
        n = n * (1.0 + sc_ref[...]) + sh_ref[...]
        n_scr[...] = n.astype(BF16)
        acc_scr[...] = jnp.zeros_like(acc_scr)

    n = n_scr[...]
    hg = jnp.dot(n, wg_ref[...], preferred_element_type=F32)
    hu = jnp.dot(n, wu_ref[...], preferred_element_type=F32)
    a = (hg / (1.0 + jnp.exp(-hg))) * hu
    acc_scr[...] += jnp.dot(a.astype(BF16), wd_ref[...], preferred_element_type=F32)

    @pl.when(f == pl.num_programs(1) - 1)
    def _():
        out = h_ref[...] + (0.5 * gt_ref[...]) * acc_scr[...]
        if final_norm:
            out = _rms_rows(out) * fg_ref[...]
        o_ref[...] = out


def _ffn(h, ng, sh, sc, gt, wg, wu, wd, fg, *, final_norm):
    s, d = h.shape
    f_dim = wg.shape[1]
    tf = f_dim // 2
    vec = pl.BlockSpec((1, d), lambda i, f: (0, 0))
    return pl.pallas_call(
        functools.partial(_ffn_kernel, final_norm=final_norm),
        grid=(s // FFN_ROWS, f_dim // tf),
        in_specs=[
            pl.BlockSpec((FFN_ROWS, d), lambda i, f: (i, 0)),
            vec, vec, vec, vec,
            pl.BlockSpec((d, tf), lambda i, f: (0, f)),
            pl.BlockSpec((d, tf), lambda i, f: (0, f)),
            pl.BlockSpec((tf, d), lambda i, f: (f, 0)),
            vec,
        ],
        out_specs=pl.BlockSpec((FFN_ROWS, d), lambda i, f: (i, 0)),
        out_shape=jax.ShapeDtypeStruct((s, d), F32),
        scratch_shapes=[pltpu.VMEM((FFN_ROWS, d), BF16), pltpu.VMEM((FFN_ROWS, d), F32)],
        compiler_params=_params("parallel", "arbitrary"),
        name="ffn_final" if final_norm else "ffn",
    )(h, ng, sh, sc, gt, wg, wu, wd, fg)


def _rope_t(x, cr, sr, cc, sc):
    h = ROPE_HALF
    x1r, x2r, x1c, x2c = x[0:h], x[h:2 * h], x[2 * h:3 * h], x[3 * h:4 * h]
    return jnp.concatenate(
        [x1r * cr - x2r * sr, x2r * cr + x1r * sr, x1c * cc - x2c * sc, x2c * cc + x1c * sc], axis=0)


def _head_norm_t(x, g_col):
    ms = jnp.mean(x * x, axis=0, keepdims=True)
    return x * lax.rsqrt(ms + EPS) * g_col


def _max_sq_norm(x_bf16):
    xf = x_bf16.astype(F32)
    n2 = jnp.sum(xf * xf, axis=0, keepdims=True)
    return jnp.broadcast_to(jnp.max(n2, axis=1, keepdims=True), (1, LANES))


def _with_ones(v_t):
    return jnp.concatenate([v_t, jnp.ones((ONES_ROWS, v_t.shape[1]), F32)], axis=0).astype(BF16)


def _mix_in_kernel(h_ref, ng_ref, sh_ref, sc_ref, wt_ref, angr_ref, angc_ref, qkg_ref,
                   qta_ref, ka_ref, vta_ref, qtd_ref, kd_ref, vtd_ref, qn_ref, kn_ref):
    n = _rms_rows(h_ref[...]) * ng_ref[...]
    n = (n * (1.0 + sc_ref[...]) + sh_ref[...]).astype(BF16)
    pt = lax.dot_general(wt_ref[...], n, (((1,), (1,)), ((), ())), preferred_element_type=F32)
    t = pt.shape[1]

    cr, sr = jnp.cos(angr_ref[...]), jnp.sin(angr_ref[...])
    cc, sc = jnp.cos(angc_ref[...]), jnp.sin(angc_ref[...])
    gq = qkg_ref[:, 0:1]
    gk = qkg_ref[:, 1:2]
    zeros = jnp.zeros((HEAD_DIM, t), BF16)
    qscale = QK_SCALE * LOG2E

    qn, kn = [], []
    for hd in range(A_Q_HEADS):
        q = _rope_t(_head_norm_t(pt[hd * HEAD_DIM:(hd + 1) * HEAD_DIM], gq), cr, sr, cc, sc) * qscale
        q = q.astype(BF16)
        qta_ref[hd] = jnp.concatenate([q, zeros] if hd // A_GROUP == 0 else [zeros, q], axis=0)
        qn.append(_max_sq_norm(q))

    c1 = A_Q_COLS
    c2 = c1 + A_KV_COLS
    c3 = c2 + A_KV_COLS
    c4 = c3 + B_QK_COLS
    c5 = c4 + B_QK_COLS
    kt = jnp.concatenate(
        [_rope_t(_head_norm_t(pt[c1 + j * HEAD_DIM:c1 + (j + 1) * HEAD_DIM], gk), cr, sr, cc, sc)
         for j in range(A_KV_HEADS)], axis=0)
    kt = kt.astype(BF16)
    ka_ref[...] = kt.T
    kn += [_max_sq_norm(kt[j * HEAD_DIM:(j + 1) * HEAD_DIM]) for j in range(A_KV_HEADS)]
    kn.append(jnp.zeros((NORM_ROWS_A - A_KV_HEADS, LANES), F32))
    for j in range(A_KV_HEADS):
        vta_ref[j] = _with_ones(pt[c2 + j * HEAD_DIM:c2 + (j + 1) * HEAD_DIM])

    for cmb in range(2 * B_HEADS):
        q = (pt[c3 + cmb * HEAD_DIM:c3 + (cmb + 1) * HEAD_DIM] * qscale).astype(BF16)
        qtd_ref[cmb] = jnp.concatenate([q, zeros] if cmb % 2 == 0 else [zeros, q], axis=0)
        qn.append(_max_sq_norm(q))
        kn.append(_max_sq_norm(pt[c4 + cmb * HEAD_DIM:c4 + (cmb + 1) * HEAD_DIM].astype(BF16)))
    qn_ref[...] = jnp.concatenate(qn, axis=0)
    kn_ref[...] = jnp.concatenate(kn, axis=0)
    kd_ref[...] = pt[c4:c5].T.astype(BF16)
    for hd in range(B_HEADS):
        vtd_ref[hd] = _with_ones(pt[c5 + hd * B_V_DIM:c5 + (hd + 1) * B_V_DIM])


def _mix_in(h, ng, sh, sc, w_in_t, angr_t, angc_t, qkg_t):
    s, d = h.shape
    tm = MIX_ROWS
    nt = s // tm
    in_cols = w_in_t.shape[0]
    n_sets = 2 * B_HEADS
    assert A_Q_HEADS == NORM_ROWS_A
    vec = pl.BlockSpec((1, d), lambda i: (0, 0))
    out_shape = (
        jax.ShapeDtypeStruct((A_Q_HEADS, 2 * HEAD_DIM, s), BF16),
        jax.ShapeDtypeStruct((s, A_KV_COLS), BF16),
        jax.ShapeDtypeStruct((A_KV_HEADS, nt, HEAD_DIM + ONES_ROWS, tm), BF16),
        jax.ShapeDtypeStruct((n_sets, 2 * HEAD_DIM, s), BF16),
        jax.ShapeDtypeStruct((s, B_QK_COLS), BF16),
        jax.ShapeDtypeStruct((B_HEADS, nt, B_V_DIM + ONES_ROWS, tm), BF16),
        jax.ShapeDtypeStruct((nt, NORM_ROWS, LANES), F32),
        jax.ShapeDtypeStruct((nt, NORM_ROWS, LANES), F32),
    )
    out_specs = (
        pl.BlockSpec((A_Q_HEADS, 2 * HEAD_DIM, tm), lambda i: (0, 0, i)),
        pl.BlockSpec((tm, A_KV_COLS), lambda i: (i, 0)),
        pl.BlockSpec((A_KV_HEADS, None, HEAD_DIM + ONES_ROWS, tm), lambda i: (0, i, 0, 0)),
        pl.BlockSpec((n_sets, 2 * HEAD_DIM, tm), lambda i: (0, 0, i)),
        pl.BlockSpec((tm, B_QK_COLS), lambda i: (i, 0)),
        pl.BlockSpec((B_HEADS, None, B_V_DIM + ONES_ROWS, tm), lambda i: (0, i, 0, 0)),
        pl.BlockSpec((None, NORM_ROWS, LANES), lambda i: (i, 0, 0)),
        pl.BlockSpec((None, NORM_ROWS, LANES), lambda i: (i, 0, 0)),
    )
    return pl.pallas_call(
        _mix_in_kernel,
        grid=(nt,),
        in_specs=[
            pl.BlockSpec((tm, d), lambda i: (i, 0)),
            vec, vec, vec,
            pl.BlockSpec((in_cols, d), lambda i: (0, 0)),
            pl.BlockSpec((ROPE_HALF, tm), lambda i: (0, i)),
            pl.BlockSpec((ROPE_HALF, tm), lambda i: (0, i)),
            pl.BlockSpec((HEAD_DIM, 2), lambda i: (0, 0)),
        ],
        out_specs=out_specs,
        out_shape=out_shape,
        compiler_params=_params("parallel"),
        name="mix_in",
    )(h, ng, sh, sc, w_in_t, angr_t, angc_t, qkg_t)


def _softmax_tile(s, m_ref, alpha_ref, p_ref):
    tk, n = s.shape
    part = jnp.max(s.reshape(tk // BF16_SUBLANES, BF16_SUBLANES, n), axis=0)
    m_old = m_ref[...]
    m_new = jnp.maximum(m_old, jnp.max(part.astype(F32), axis=0, keepdims=True))
    alpha_ref[...] = jnp.exp2(m_old - m_new)
    m_ref[...] = m_new
    p_ref[...] = jnp.exp2(s - m_new.astype(BF16))


def _pipelined_sweep(prep, qk, softmax, pv, n_chunks, lo, n_pairs, last):
    ctx = prep(lo)
    for c in range(n_chunks):
        qk(lo, 0, c, ctx)

    def body(j, carry):
        a = lo + 2 * j
        prev = jnp.maximum(a - 1, 0)
        nxt = jnp.minimum(a + 2, last)
        ctx = prep(a + 1)
        for c in range(n_chunks):
            softmax(0, c)
            pv(prev, 1, c)
            qk(a + 1, 1, c, ctx)
        ctx = prep(nxt)
        for c in range(n_chunks):
            softmax(1, c)
            pv(a, 0, c)
            qk(nxt, 0, c, ctx)
        return carry

    lax.fori_loop(0, n_pairs, body, 0)
    for c in range(n_chunks):
        pv(last, 1, c)


def _direct_sweep(prep, qk_exp, pv, n_chunks, lo, n_tiles, last):
    def phase(kt_new, slot_new, kt_old):
        ctx = prep(kt_new)
        for c in range(n_chunks):
            qk_exp(kt_new, slot_new, c, ctx)
            pv(kt_old, 1 - slot_new, c)

    ctx = prep(lo)
    for c in range(n_chunks):
        qk_exp(lo, 0, c, ctx)

    def body(j, carry):
        a = lo + 2 * j
        phase(a + 1, 1, a)
        phase(a + 2, 0, a + 1)
        return carry

    n_iter = (n_tiles - 1) // 2 if isinstance(n_tiles, int) else lax.shift_right_logical(n_tiles - 1, 1)
    lax.fori_loop(0, n_iter, body, 0)

    def tail_odd():
        for c in range(n_chunks):
            pv(last, 0, c)
        return 0

    def tail_even():
        phase(last, 1, last - 1)
        for c in range(n_chunks):
            pv(last, 1, c)
        return 0

    if isinstance(n_tiles, int):
        (tail_odd if n_tiles % 2 else tail_even)()
    else:
        lax.cond(jnp.bitwise_and(n_tiles, 1) == 1, tail_odd, tail_even)


def _score_bound(qn_tile, kn_ref, q_rows, k_row_of_q):
    k_max = jnp.max(kn_ref[...], axis=0)
    row = lax.broadcasted_iota(jnp.int32, k_max.shape, 0)
    return jnp.sqrt(jnp.where(q_rows(row), qn_tile, 0.0) * k_row_of_q(row, k_max))


GQA_CHUNKS = 2


def _gqa_kernel(qt_ref, k_ref, vt_ref, qn_ref, kn_ref, o_ref, q_scr, s_buf, p_buf, alpha_buf, m_scr,
                acc_scr):
    g = pl.program_id(0)
    qi = pl.program_id(1)
    nk, tk, _ = k_ref.shape
    tq = qt_ref.shape[2]
    per = A_GROUP // GQA_CHUNKS
    for hh in range(A_GROUP):
        q_scr[hh // per, :, (hh % per) * tq:(hh % per + 1) * tq] = qt_ref[hh]
    acc_scr[...] = jnp.zeros_like(acc_scr)

    def qk(kt, slot, c, _):
        s_buf[slot, c] = jnp.dot(k_ref[kt], q_scr[c], preferred_element_type=F32).astype(BF16)

    def softmax(slot, c):
        _softmax_tile(s_buf[slot, c], m_scr.at[c], alpha_buf.at[slot, c], p_buf.at[slot, c])

    def pv(kt, slot, c):
        acc_scr[c] = alpha_buf[slot, c] * acc_scr[c] + jnp.dot(
            vt_ref[kt], p_buf[slot, c], preferred_element_type=F32)

    def qk_exp(kt, slot, c, _):
        s = jnp.dot(k_ref[kt], q_scr[c], preferred_element_type=F32)
        p_buf[slot, c] = jnp.exp2(s).astype(BF16)

    def pv_plain(kt, slot, c):
        acc_scr[c] += jnp.dot(vt_ref[kt], p_buf[slot, c], preferred_element_type=F32)

    def kmax_of_group(row, k_max):
        return jnp.max(jnp.where(row == g, k_max, 0.0), axis=0, keepdims=True)

    bound = _score_bound(qn_ref[lax.div(qi * tq, tk)], kn_ref,
                         lambda row: jnp.logical_and(row >= g * A_GROUP, row < (g + 1) * A_GROUP),
                         kmax_of_group)

    def direct():
        _direct_sweep(lambda kt: None, qk_exp, pv_plain, GQA_CHUNKS, 0, nk, nk - 1)
        return 0

    def online():
        m_scr[...] = jnp.full_like(m_scr, NEG_BIG)
        p_buf[1] = jnp.zeros(p_buf.shape[1:], BF16)
        alpha_buf[1] = jnp.ones(alpha_buf.shape[1:], F32)
        _pipelined_sweep(lambda kt: None, qk, softmax, pv, GQA_CHUNKS, 0, nk // 2, nk - 1)
        return 0

    lax.cond(jnp.max(bound) <= FAST_LOG2_LIMIT, direct, online)
    heads = []
    for c in range(GQA_CHUNKS):
        acc = acc_scr[c]
        ot = acc[:HEAD_DIM] * (1.0 / acc[HEAD_DIM:HEAD_DIM + 1])
        heads += [ot[:, i * tq:(i + 1) * tq] for i in range(per)]
    o_ref[...] = jnp.concatenate(heads, axis=0).T.astype(BF16)


def _gqa(qta, ka3, vta, qn, kn):
    s = qta.shape[2]
    nk, tk, _ = ka3.shape
    tq = GQA_Q_TILE
    assert nk % 2 == 0 and tk % tq == 0 and tk == MIX_ROWS
    nc = GQA_CHUNKS
    w = A_GROUP // nc * tq
    v_rows = vta.shape[2]
    return pl.pallas_call(
        _gqa_kernel,
        grid=(A_KV_HEADS, s // tq),
        in_specs=[
            pl.BlockSpec((A_GROUP, 2 * HEAD_DIM, tq), lambda g, i: (g, 0, i)),
            pl.BlockSpec((nk, tk, A_KV_COLS), lambda g, i: (0, 0, 0)),
            pl.BlockSpec((None, nk, v_rows, tk), lambda g, i: (g, 0, 0, 0)),
            pl.BlockSpec(qn.shape, lambda g, i: (0, 0, 0)),
            pl.BlockSpec(kn.shape, lambda g, i: (0, 0, 0)),
        ],
        out_specs=pl.BlockSpec((tq, A_GROUP * HEAD_DIM), lambda g, i: (i, g)),
        out_shape=jax.ShapeDtypeStruct((s, A_Q_COLS), BF16),
        scratch_shapes=[
            pltpu.VMEM((nc, 2 * HEAD_DIM, w), BF16),
            pltpu.VMEM((2, nc, tk, w), BF16),
            pltpu.VMEM((2, nc, tk, w), BF16),
            pltpu.VMEM((2, nc, 1, w), F32),
            pltpu.VMEM((nc, 1, w), F32),
            pltpu.VMEM((nc, v_rows, w), F32),
        ],
        compiler_params=_params("parallel", "parallel"),
        name="gqa_attn",
    )(qta, ka3, vta, qn, kn)


def _diff_kernel(coef_ref, qt0_ref, qt1_ref, k0_ref, k1_ref, vt_ref, qn_ref, kn_ref, lamp_ref, sg_ref,
                 o_ref, rel_scr, s_buf, p_buf, alpha_buf, m_scr, acc_scr, *, lam_init):
    hd = pl.program_id(0)
    qi = pl.program_id(1)
    nk, tk, _ = k0_ref.shape
    tq = qt0_ref.shape[1]
    nslope = coef_ref[0, hd]
    inv_tile_drop = coef_ref[1, hd]
    acc_scr[...] = jnp.zeros_like(acc_scr)

    @pl.when(jnp.logical_and(hd == 0, qi == 0))
    def _():
        rel_scr[...] = (lax.broadcasted_iota(jnp.int32, (tk, tq), 1)
                        - lax.broadcasted_iota(jnp.int32, (tk, tq), 0)).astype(F32)

    bound = jnp.max(_score_bound(
        qn_ref[qi], kn_ref,
        lambda row: jnp.logical_or(row == NORM_ROWS_A + hd, row == NORM_ROWS_A + B_HEADS + hd),
        lambda row, k_max: k_max))
    use_direct = bound <= FAST_LOG2_LIMIT
    drop = jnp.where(use_direct, bound, 2.0 * bound) + SKIP_LOG2
    reach = jnp.minimum(jnp.floor(drop * inv_tile_drop) + 1.0, float(nk)).astype(jnp.int32)
    lo = jnp.maximum(qi - reach, 0)
    hi = jnp.minimum(qi + reach, nk - 1)

    q0 = qi * tq
    qts = (qt0_ref, qt1_ref)
    ks = (k0_ref, k1_ref)

    def alibi(kt):
        off = (q0 - kt * tk).astype(F32)
        return jnp.abs(rel_scr[...] + off) * nslope

    def qk(kt, slot, mp, bias):
        s = jnp.dot(ks[mp][kt], qts[mp][...], preferred_element_type=F32) + bias
        s_buf[slot, mp] = s.astype(BF16)

    def softmax(slot, mp):
        _softmax_tile(s_buf[slot, mp], m_scr.at[mp], alpha_buf.at[slot, mp], p_buf.at[slot, mp])

    def pv(kt, slot, mp):
        acc_scr[mp] = alpha_buf[slot, mp] * acc_scr[mp] + jnp.dot(
            vt_ref[kt], p_buf[slot, mp], preferred_element_type=F32)

    def qk_exp(kt, slot, mp, bias):
        s = jnp.dot(ks[mp][kt], qts[mp][...], preferred_element_type=F32) + bias
        p_buf[slot, mp] = jnp.exp2(s).astype(BF16)

    def pv_plain(kt, slot, mp):
        acc_scr[mp] += jnp.dot(vt_ref[kt], p_buf[slot, mp], preferred_element_type=F32)

    def direct():
        _direct_sweep(alibi, qk_exp, pv_plain, 2, lo, hi - lo + 1, hi)
        return 0

    def online():
        m_scr[...] = jnp.full_like(m_scr, NEG_BIG)
        p_buf[1] = jnp.zeros(p_buf.shape[1:], BF16)
        alpha_buf[1] = jnp.ones(alpha_buf.shape[1:], F32)
        odd = jnp.bitwise_and(hi - lo + 1, 1)
        grow_hi = jnp.where(hi < nk - 1, odd, 0)
        hi2 = hi + grow_hi
        lo2 = lo - (odd - grow_hi)
        _pipelined_sweep(alibi, qk, softmax, pv, 2, lo2, lax.shift_right_logical(hi2 - lo2 + 1, 1), hi2)
        return 0

    lax.cond(use_direct, direct, online)

    lp = lamp_ref[...]
    lam = (jnp.exp(jnp.sum(lp[0:1] * lp[1:2], axis=1, keepdims=True))
           - jnp.exp(jnp.sum(lp[2:3] * lp[3:4], axis=1, keepdims=True)) + lam_init)
    a0, a1 = acc_scr[0], acc_scr[1]
    ot = (a0[:B_V_DIM] * (1.0 / a0[B_V_DIM:B_V_DIM + 1])
          - lam * (a1[:B_V_DIM] * (1.0 / a1[B_V_DIM:B_V_DIM + 1])))
    ot = ot * lax.rsqrt(jnp.mean(ot * ot, axis=0, keepdims=True) + EPS)
    ot = ot * sg_ref[...] * (1.0 - lam_init)
    o_ref[...] = ot.T.astype(BF16)


def _diff(coef, qtd, kd3, vtd, qn, kn, lam_p, subln_col, *, lam_init):
    s = qtd.shape[2]
    nk, tk, _ = kd3.shape
    tq = DIFF_Q_TILE
    assert nk % 2 == 0 and tq == tk == MIX_ROWS
    lanes = 2 * HEAD_DIM
    v_rows = vtd.shape[2]
    return pl.pallas_call(
        functools.partial(_diff_kernel, lam_init=lam_init),
        grid=(B_HEADS, s // tq),
        in_specs=[
            pl.BlockSpec(memory_space=pltpu.SMEM),
            pl.BlockSpec((None, lanes, tq), lambda h, i: (h, 0, i)),
            pl.BlockSpec((None, lanes, tq), lambda h, i: (B_HEADS + h, 0, i)),
            pl.BlockSpec((nk, tk, lanes), lambda h, i: (0, 0, h // 2)),
            pl.BlockSpec((nk, tk, lanes), lambda h, i: (0, 0, B_HEADS // 2 + h // 2)),
            pl.BlockSpec((None, nk, v_rows, tk), lambda h, i: (h, 0, 0, 0)),
            pl.BlockSpec(qn.shape, lambda h, i: (0, 0, 0)),
            pl.BlockSpec(kn.shape, lambda h, i: (0, 0, 0)),
            pl.BlockSpec((4, HEAD_DIM), lambda h, i: (0, 0)),
            pl.BlockSpec((B_V_DIM, 1), lambda h, i: (0, 0)),
        ],
        out_specs=pl.BlockSpec((tq, B_V_DIM), lambda h, i: (i, h)),
        out_shape=jax.ShapeDtypeStruct((s, B_V_COLS), BF16),
        scratch_shapes=[
            pltpu.VMEM((tk, tq), F32),
            pltpu.VMEM((2, 2, tk, tq), BF16),
            pltpu.VMEM((2, 2, tk, tq), BF16),
            pltpu.VMEM((2, 2, 1, tq), F32),
            pltpu.VMEM((2, 1, tq), F32),
            pltpu.VMEM((2, v_rows, tq), F32),
        ],
        compiler_params=_params("arbitrary", "arbitrary"),
        name="diff_attn",
    )(coef, qtd, qtd, kd3, kd3, vtd, qn, kn, lam_p, subln_col)


def _mix_out_kernel(h_ref, ng_ref, sh_ref, sc_ref, gt_ref, oa_ref, od_ref, wba_ref, wbb_ref,
                    wgate_ref, bgate_ref, wo_ref, o_ref):
    h = h_ref[...]
    d = h.shape[1]
    n = _rms_rows(h) * ng_ref[...]
    n = (n * (1.0 + sc_ref[...]) + sh_ref[...]).astype(BF16)
    z = jnp.dot(n, wgate_ref[...], preferred_element_type=F32) + bgate_ref[...]
    g = 1.0 / (1.0 + jnp.exp(-z))
    ya = jnp.dot(oa_ref[...], wba_ref[...], preferred_element_type=F32)
    yb = jnp.dot(od_ref[...], wbb_ref[...], preferred_element_type=F32)
    mix = (g[:, :d] * ya + g[:, d:] * yb).astype(BF16)
    y = jnp.dot(mix, wo_ref[...], preferred_element_type=F32)
    o_ref[...] = h + gt_ref[...] * y


def _mix_out(h, ng, sh, sc, gt, oa, od, w_ba, w_bb, w_gate, b_gate, w_o):
    s, d = h.shape
    tm = MIX_ROWS
    vec = pl.BlockSpec((1, d), lambda i: (0, 0))

    def full(a):
        return pl.BlockSpec(a.shape, lambda i: (0,) * a.ndim)

    return pl.pallas_call(
        _mix_out_kernel,
        grid=(s // tm,),
        in_specs=[
            pl.BlockSpec((tm, d), lambda i: (i, 0)),
            vec, vec, vec, vec,
            pl.BlockSpec((tm, oa.shape[1]), lambda i: (i, 0)),
            pl.BlockSpec((tm, od.shape[1]), lambda i: (i, 0)),
            full(w_ba), full(w_bb), full(w_gate), full(b_gate), full(w_o),
        ],
        out_specs=pl.BlockSpec((tm, d), lambda i: (i, 0)),
        out_shape=jax.ShapeDtypeStruct((s, d), F32),
        compiler_params=_params("parallel"),
        name="mix_out",
    )(h, ng, sh, sc, gt, oa, od, w_ba, w_bb, w_gate, b_gate, w_o)


def _axial_angles_t(seq):
    rows = seq // GRID_W
    row = jnp.broadcast_to(jnp.arange(rows)[:, None], (rows, GRID_W)).reshape(seq) - rows // 2
    col = jnp.broadcast_to(jnp.arange(GRID_W)[None, :], (rows, GRID_W)).reshape(seq) - GRID_W // 2
    inv = 1.0 / (ROPE_THETA ** (jnp.arange(0, ROPE_AXIS_DIM, 2, dtype=F32) / ROPE_AXIS_DIM))
    return inv[:, None] * row.astype(F32)[None, :], inv[:, None] * col.astype(F32)[None, :]


def kernel(x, c, ada_w, ada_b, norm_g, ffn_wg, ffn_wu, ffn_wd, w_in, qk_g, lam_p, subln_g, w_ba, w_bb,
           w_gate, b_gate, w_o, final_g):
    batch, s, d = x.shape
    assert batch == 1 and s % KEY_TILE == 0 and MIX_ROWS == KEY_TILE
    depth = ada_w.shape[0]
    h = x.reshape(s, d)

    mod = _ada_mod(c.reshape(d, 1), ada_w, ada_b)
    angr_t, angc_t = _axial_angles_t(s)
    slopes = 2.0 ** (-8.0 * jnp.arange(1, B_HEADS + 1, dtype=F32) / B_HEADS)
    coef = jnp.stack([-slopes * LOG2E, 1.0 / (slopes * LOG2E * KEY_TILE)])
    fg = final_g.reshape(1, d)

    for l in range(depth):
        sh1, sc1, g1, sh2, sc2, g2, sh3, sc3, g3 = [mod[l, :, j * d:(j + 1) * d] for j in range(N_ADA)]
        lam_init = 0.8 - 0.6 * math.exp(-0.3 * l)
        ng = norm_g[l].reshape(3, 1, d)
        wg, wu, wd = ffn_wg[l].astype(BF16), ffn_wu[l].astype(BF16), ffn_wd[l].astype(BF16)

        h = _ffn(h, ng[0], sh1, sc1, g1, wg[0], wu[0], wd[0], fg, final_norm=False)

        qta, ka, vta, qtd, kd, vtd, qn, kn = _mix_in(
            h, ng[1], sh2, sc2, w_in[l].T.astype(BF16), angr_t, angc_t, qk_g[l].T)
        nk = s // KEY_TILE
        oa = _gqa(qta, ka.reshape(nk, KEY_TILE, A_KV_COLS), vta, qn, kn)
        od = _diff(coef, qtd, kd.reshape(nk, KEY_TILE, B_QK_COLS), vtd, qn, kn, lam_p[l],
                   subln_g[l].reshape(B_V_DIM, 1), lam_init=lam_init)
        h = _mix_out(h, ng[1], sh2, sc2, g2, oa, od, w_ba[l].astype(BF16), w_bb[l].astype(BF16),
                     w_gate[l].astype(BF16), b_gate[l].reshape(1, -1), w_o[l].astype(BF16))

        h = _ffn(h, ng[2], sh3, sc3, g3, wg[1], wu[1], wd[1], fg, final_norm=(l == depth - 1))
    return h.reshape(batch, s, d)
```

```python
import functools
import math

import jax
import jax.numpy as jnp
from jax import lax
from jax.experimental import pallas as pl
from jax.experimental.pallas import tpu as pltpu

F32 = jnp.float32
BF16 = jnp.bfloat16
F8 = jnp.float8_e4m3fn

GRID_W = 64
HEAD_DIM = 64
A_Q_HEADS = 8
A_KV_HEADS = 2
A_GROUP = A_Q_HEADS // A_KV_HEADS
B_HEADS = 4
B_V_DIM = 2 * HEAD_DIM
A_Q_COLS = A_Q_HEADS * HEAD_DIM
A_KV_COLS = A_KV_HEADS * HEAD_DIM
B_QK_COLS = 2 * B_HEADS * HEAD_DIM
B_V_COLS = B_HEADS * B_V_DIM
N_ADA = 9
EPS = 1e-6
ROPE_THETA = 10000.0
ROPE_AXIS_DIM = HEAD_DIM // 2
ROPE_HALF = ROPE_AXIS_DIM // 2

LOG2E = math.log2(math.e)
QK_SCALE = HEAD_DIM ** -0.5
NEG_BIG = -1e30

BF16_SUBLANES = 16
LANES = 128
ONES_ROWS = BF16_SUBLANES
SKIP_LOG2 = 160.0
FAST_LOG2_LIMIT = 60.0
F8_Q_SCALE = 2.0 ** 6
F8_K_SCALE = 2.0 ** 2
F8_UNSCALE = 1.0 / (F8_Q_SCALE * F8_K_SCALE)
F8_SAFE_MAX = 256.0
F8_COLS = 4 * HEAD_DIM
NORM_ROWS_A = 8
NORM_ROWS = NORM_ROWS_A + 2 * B_HEADS

VMEM_LIMIT_BYTES = 56 * 1024 * 1024

FFN_ROWS = 512
MIX_ROWS = 512
KEY_TILE = 512
GQA_Q_TILE = 512
DIFF_Q_TILE = 512
ADA_COLS = 1152


def _params(*sem):
    return pltpu.CompilerParams(dimension_semantics=sem, vmem_limit_bytes=VMEM_LIMIT_BYTES)


def _rms_rows(x):
    return x * lax.rsqrt(jnp.mean(x * x, axis=-1, keepdims=True) + EPS)


def _ada_kernel(c_ref, w_ref, b_ref, o_ref):
    c = c_ref[...]
    act = c / (1.0 + jnp.exp(-c))
    o_ref[...] = jnp.sum(w_ref[...] * act, axis=0, keepdims=True) + b_ref[...]


def _ada_mod(c_col, ada_w, ada_b):
    n_layers, d, n = ada_w.shape
    return pl.pallas_call(
        _ada_kernel,
        grid=(n_layers, n // ADA_COLS),
        in_specs=[
            pl.BlockSpec((d, 1), lambda l, j: (0, 0)),
            pl.BlockSpec((None, d, ADA_COLS), lambda l, j: (l, 0, j)),
            pl.BlockSpec((None, 1, ADA_COLS), lambda l, j: (l, 0, j)),
        ],
        out_specs=pl.BlockSpec((None, 1, ADA_COLS), lambda l, j: (l, 0, j)),
        out_shape=jax.ShapeDtypeStruct((n_layers, 1, n), F32),
        compiler_params=_params("parallel", "parallel"),
        name="ada_mod",
    )(c_col, ada_w, ada_b.reshape(n_layers, 1, n))


def _ffn_kernel(h_ref, ng_ref, sh_ref, sc_ref, gt_ref, wg_ref, wu_ref, wd_ref, fg_ref, o_ref,
                n_scr, acc_scr, *, final_norm):
    f = pl.program_id(1)

    @pl.when(f == 0)
    def _():
        n = _rms_rows(h_ref[...]) * ng_ref[...]
        n = n * (1.0 + sc_ref[...]) + sh_ref[...]
        n_scr[...] = n.astype(BF16)
        acc_scr[...] = jnp.zeros_like(acc_scr)

    n = n_scr[...]
    hg = jnp.dot(n, wg_ref[...], preferred_element_type=F32)
    hu = jnp.dot(n, wu_ref[...], preferred_element_type=F32)
    a = (hg / (1.0 + jnp.exp(-hg))) * hu
    acc_scr[...] += jnp.dot(a.astype(BF16), wd_ref[...], preferred_element_type=F32)

    @pl.when(f == pl.num_programs(1) - 1)
    def _():
        out = h_ref[...] + (0.5 * gt_ref[...]) * acc_scr[...]
        if final_norm:
            out = _rms_rows(out) * fg_ref[...]
        o_ref[...] = out


def _ffn(h, ng, sh, sc, gt, wg, wu, wd, fg, *, final_norm):
    s, d = h.shape
    f_dim = wg.shape[1]
    tf = f_dim // 2
    vec = pl.BlockSpec((1, d), lambda i, f: (0, 0))
    return pl.pallas_call(
        functools.partial(_ffn_kernel, final_norm=final_norm),
        grid=(s // FFN_ROWS, f_dim // tf),
        in_specs=[
            pl.BlockSpec((FFN_ROWS, d), lambda i, f: (i, 0)),
            vec, vec, vec, vec,
            pl.BlockSpec((d, tf), lambda i, f: (0, f)),
            pl.BlockSpec((d, tf), lambda i, f: (0, f)),
            pl.BlockSpec((tf, d), lambda i, f: (f, 0)),
            vec,
        ],
        out_specs=pl.BlockSpec((FFN_ROWS, d), lambda i, f: (i, 0)),
        out_shape=jax.ShapeDtypeStruct((s, d), F32),
        scratch_shapes=[pltpu.VMEM((FFN_ROWS, d), BF16), pltpu.VMEM((FFN_ROWS, d), F32)],
        compiler_params=_params("parallel", "arbitrary"),
        name="ffn_final" if final_norm else "ffn",
    )(h, ng, sh, sc, gt, wg, wu, wd, fg)


def _rope_t(x, cr, sr, cc, sc):
    h = ROPE_HALF
    x1r, x2r, x1c, x2c = x[0:h], x[h:2 * h], x[2 * h:3 * h], x[3 * h:4 * h]
    return jnp.concatenate(
        [x1r * cr - x2r * sr, x2r * cr + x1r * sr, x1c * cc - x2c * sc, x2c * cc + x1c * sc], axis=0)


def _head_norm_t(x, g_col):
    ms = jnp.mean(x * x, axis=0, keepdims=True)
    return x * lax.rsqrt(ms + EPS) * g_col


def _max_sq_norm(x_bf16):
    xf = x_bf16.astype(F32)
    n2 = jnp.sum(xf * xf, axis=0, keepdims=True)
    return jnp.broadcast_to(jnp.max(n2, axis=1, keepdims=True), (1, LANES))


def _f8_split(x):
    hi = x.astype(F8).astype(F32)
    return hi, ((x - hi) * 16.0).astype(F8).astype(F32)


def _f8_query_rows(q_bf16):
    hi, lo16 = _f8_split(q_bf16.astype(F32) * F8_Q_SCALE)
    return jnp.concatenate([hi, hi * 0.0625, lo16 * 0.0625, lo16 * 0.00390625], axis=0).astype(F8)


def _f8_key_rows(k_bf16):
    hi, lo16 = _f8_split(k_bf16.astype(F32) * F8_K_SCALE)
    return jnp.concatenate([hi, lo16, hi, lo16], axis=0)


def _with_ones(v_t):
    return jnp.concatenate([v_t, jnp.ones((ONES_ROWS, v_t.shape[1]), F32)], axis=0).astype(BF16)


def _mix_in_kernel(h_ref, ng_ref, sh_ref, sc_ref, wt_ref, angr_ref, angc_ref, qkg_ref,
                   qta_ref, ka_ref, vta_ref, qtd_ref, kd_ref, vtd_ref, qn_ref, kn_ref,
                   qfa_ref, kfa_ref, qfd_ref, kfd_ref):
    n = _rms_rows(h_ref[...]) * ng_ref[...]
    n = (n * (1.0 + sc_ref[...]) + sh_ref[...]).astype(BF16)
    pt = lax.dot_general(wt_ref[...], n, (((1,), (1,)), ((), ())), preferred_element_type=F32)
    t = pt.shape[1]

    cr, sr = jnp.cos(angr_ref[...]), jnp.sin(angr_ref[...])
    cc, sc = jnp.cos(angc_ref[...]), jnp.sin(angc_ref[...])
    gq = qkg_ref[:, 0:1]
    gk = qkg_ref[:, 1:2]
    zeros = jnp.zeros((HEAD_DIM, t), BF16)
    qscale = QK_SCALE * LOG2E

    qn, kn = [], []
    for hd in range(A_Q_HEADS):
        q = _rope_t(_head_norm_t(pt[hd * HEAD_DIM:(hd + 1) * HEAD_DIM], gq), cr, sr, cc, sc) * qscale
        q = q.astype(BF16)
        qta_ref[hd] = jnp.concatenate([q, zeros] if hd // A_GROUP == 0 else [zeros, q], axis=0)
        qfa_ref[hd] = _f8_query_rows(q)
        qn.append(_max_sq_norm(q))

    c1 = A_Q_COLS
    c2 = c1 + A_KV_COLS
    c3 = c2 + A_KV_COLS
    c4 = c3 + B_QK_COLS
    c5 = c4 + B_QK_COLS
    kt = jnp.concatenate(
        [_rope_t(_head_norm_t(pt[c1 + j * HEAD_DIM:c1 + (j + 1) * HEAD_DIM], gk), cr, sr, cc, sc)
         for j in range(A_KV_HEADS)], axis=0)
    kt = kt.astype(BF16)
    ka_ref[...] = kt.T
    kn += [_max_sq_norm(kt[j * HEAD_DIM:(j + 1) * HEAD_DIM]) for j in range(A_KV_HEADS)]
    for j in range(A_KV_HEADS):
        kfa_ref[j] = _f8_key_rows(kt[j * HEAD_DIM:(j + 1) * HEAD_DIM]).T.astype(F8)
    kn.append(jnp.zeros((NORM_ROWS_A - A_KV_HEADS, LANES), F32))
    for j in range(A_KV_HEADS):
        vta_ref[j] = _with_ones(pt[c2 + j * HEAD_DIM:c2 + (j + 1) * HEAD_DIM])

    for cmb in range(2 * B_HEADS):
        q = (pt[c3 + cmb * HEAD_DIM:c3 + (cmb + 1) * HEAD_DIM] * qscale).astype(BF16)
        qtd_ref[cmb] = jnp.concatenate([q, zeros] if cmb % 2 == 0 else [zeros, q], axis=0)
        qfd_ref[cmb] = _f8_query_rows(q)
        qn.append(_max_sq_norm(q))
        k = pt[c4 + cmb * HEAD_DIM:c4 + (cmb + 1) * HEAD_DIM].astype(BF16)
        kfd_ref[cmb] = _f8_key_rows(k).T.astype(F8)
        kn.append(_max_sq_norm(k))
    qn_ref[...] = jnp.concatenate(qn, axis=0)
    kn_ref[...] = jnp.concatenate(kn, axis=0)
    kd_ref[...] = pt[c4:c5].T.astype(BF16)
    for hd in range(B_HEADS):
        vtd_ref[hd] = _with_ones(pt[c5 + hd * B_V_DIM:c5 + (hd + 1) * B_V_DIM])


def _mix_in(h, ng, sh, sc, w_in_t, angr_t, angc_t, qkg_t):
    s, d = h.shape
    tm = MIX_ROWS
    nt = s // tm
    in_cols = w_in_t.shape[0]
    n_sets = 2 * B_HEADS
    assert A_Q_HEADS == NORM_ROWS_A
    vec = pl.BlockSpec((1, d), lambda i: (0, 0))
    out_shape = (
        jax.ShapeDtypeStruct((A_Q_HEADS, 2 * HEAD_DIM, s), BF16),
        jax.ShapeDtypeStruct((s, A_KV_COLS), BF16),
        jax.ShapeDtypeStruct((A_KV_HEADS, nt, HEAD_DIM + ONES_ROWS, tm), BF16),
        jax.ShapeDtypeStruct((n_sets, 2 * HEAD_DIM, s), BF16),
        jax.ShapeDtypeStruct((s, B_QK_COLS), BF16),
        jax.ShapeDtypeStruct((B_HEADS, nt, B_V_DIM + ONES_ROWS, tm), BF16),
        jax.ShapeDtypeStruct((nt, NORM_ROWS, LANES), F32),
        jax.ShapeDtypeStruct((nt, NORM_ROWS, LANES), F32),
        jax.ShapeDtypeStruct((A_Q_HEADS, F8_COLS, s), F8),
        jax.ShapeDtypeStruct((A_KV_HEADS, s, F8_COLS), F8),
        jax.ShapeDtypeStruct((n_sets, F8_COLS, s), F8),
        jax.ShapeDtypeStruct((n_sets, s, F8_COLS), F8),
    )
    out_specs = (
        pl.BlockSpec((A_Q_HEADS, 2 * HEAD_DIM, tm), lambda i: (0, 0, i)),
        pl.BlockSpec((tm, A_KV_COLS), lambda i: (i, 0)),
        pl.BlockSpec((A_KV_HEADS, None, HEAD_DIM + ONES_ROWS, tm), lambda i: (0, i, 0, 0)),
        pl.BlockSpec((n_sets, 2 * HEAD_DIM, tm), lambda i: (0, 0, i)),
        pl.BlockSpec((tm, B_QK_COLS), lambda i: (i, 0)),
        pl.BlockSpec((B_HEADS, None, B_V_DIM + ONES_ROWS, tm), lambda i: (0, i, 0, 0)),
        pl.BlockSpec((None, NORM_ROWS, LANES), lambda i: (i, 0, 0)),
        pl.BlockSpec((None, NORM_ROWS, LANES), lambda i: (i, 0, 0)),
        pl.BlockSpec((A_Q_HEADS, F8_COLS, tm), lambda i: (0, 0, i)),
        pl.BlockSpec((A_KV_HEADS, tm, F8_COLS), lambda i: (0, i, 0)),
        pl.BlockSpec((n_sets, F8_COLS, tm), lambda i: (0, 0, i)),
        pl.BlockSpec((n_sets, tm, F8_COLS), lambda i: (0, i, 0)),
    )
    return pl.pallas_call(
        _mix_in_kernel,
        grid=(nt,),
        in_specs=[
            pl.BlockSpec((tm, d), lambda i: (i, 0)),
            vec, vec, vec,
            pl.BlockSpec((in_cols, d), lambda i: (0, 0)),
            pl.BlockSpec((ROPE_HALF, tm), lambda i: (0, i)),
            pl.BlockSpec((ROPE_HALF, tm), lambda i: (0, i)),
            pl.BlockSpec((HEAD_DIM, 2), lambda i: (0, 0)),
        ],
        out_specs=out_specs,
        out_shape=out_shape,
        compiler_params=_params("parallel"),
        name="mix_in",
    )(h, ng, sh, sc, w_in_t, angr_t, angc_t, qkg_t)


def _softmax_tile(s, m_ref, alpha_ref, p_ref):
    tk, n = s.shape
    part = jnp.max(s.reshape(tk // BF16_SUBLANES, BF16_SUBLANES, n), axis=0)
    m_old = m_ref[...]
    m_new = jnp.maximum(m_old, jnp.max(part.astype(F32), axis=0, keepdims=True))
    alpha_ref[...] = jnp.exp2(m_old - m_new)
    m_ref[...] = m_new
    p_ref[...] = jnp.exp2(s - m_new.astype(BF16))


def _pipelined_sweep(prep, qk, softmax, pv, n_chunks, lo, n_pairs, last):
    ctx = prep(lo)
    for c in range(n_chunks):
        qk(lo, 0, c, ctx)

    def body(j, carry):
        a = lo + 2 * j
        prev = jnp.maximum(a - 1, 0)
        nxt = jnp.minimum(a + 2, last)
        ctx = prep(a + 1)
        for c in range(n_chunks):
            softmax(0, c)
            pv(prev, 1, c)
            qk(a + 1, 1, c, ctx)
        ctx = prep(nxt)
        for c in range(n_chunks):
            softmax(1, c)
            pv(a, 0, c)
            qk(nxt, 0, c, ctx)
        return carry

    lax.fori_loop(0, n_pairs, body, 0)
    for c in range(n_chunks):
        pv(last, 1, c)


def _direct_sweep(prep, qk_exp, pv, n_chunks, lo, n_tiles, last):
    def phase(kt_new, slot_new, kt_old):
        ctx = prep(kt_new)
        for c in range(n_chunks):
            qk_exp(kt_new, slot_new, c, ctx)
            pv(kt_old, 1 - slot_new, c)

    ctx = prep(lo)
    for c in range(n_chunks):
        qk_exp(lo, 0, c, ctx)

    def body(j, carry):
        a = lo + 2 * j
        phase(a + 1, 1, a)
        phase(a + 2, 0, a + 1)
        return carry

    n_iter = (n_tiles - 1) // 2 if isinstance(n_tiles, int) else lax.shift_right_logical(n_tiles - 1, 1)
    lax.fori_loop(0, n_iter, body, 0)

    def tail_odd():
        for c in range(n_chunks):
            pv(last, 0, c)
        return 0

    def tail_even():
        phase(last, 1, last - 1)
        for c in range(n_chunks):
            pv(last, 1, c)
        return 0

    if isinstance(n_tiles, int):
        (tail_odd if n_tiles % 2 else tail_even)()
    else:
        lax.cond(jnp.bitwise_and(n_tiles, 1) == 1, tail_odd, tail_even)


def _score_bound(qn_tile, kn_ref, q_rows, k_row_of_q):
    k_max = jnp.max(kn_ref[...], axis=0)
    row = lax.broadcasted_iota(jnp.int32, k_max.shape, 0)
    sel = q_rows(row)
    q2 = jnp.where(sel, qn_tile, 0.0)
    k2 = jnp.where(sel, k_row_of_q(row, k_max), 0.0)
    fp8_ok = jnp.logical_and(jnp.max(q2) * F8_Q_SCALE ** 2 <= F8_SAFE_MAX ** 2,
                             jnp.max(k2) * F8_K_SCALE ** 2 <= F8_SAFE_MAX ** 2)
    return jnp.max(jnp.sqrt(q2 * k2)), fp8_ok


GQA_CHUNKS = 8


def _gqa_kernel(qt_ref, k_ref, vt_ref, qn_ref, kn_ref, qf_ref, kf_ref, o_ref, q_scr, qf_scr, s_buf, p_buf,
                alpha_buf, m_scr, acc_scr):
    g = pl.program_id(0)
    qi = pl.program_id(1)
    nk, tk, _ = k_ref.shape
    tq = qt_ref.shape[2]
    w = A_GROUP * tq // GQA_CHUNKS
    for c in range(GQA_CHUNKS):
        for col in range(c * w, (c + 1) * w, min(w, tq)):
            hh, j, n = col // tq, col % tq, min(w, tq)
            q_scr[c, :, col - c * w:col - c * w + n] = qt_ref[hh, :, j:j + n]
            qf_scr[c, :, col - c * w:col - c * w + n] = qf_ref[hh, :, j:j + n]
    acc_scr[...] = jnp.zeros_like(acc_scr)

    def qk(kt, slot, c, _):
        s_buf[slot, c] = jnp.dot(k_ref[kt], q_scr[c], preferred_element_type=F32).astype(BF16)

    def softmax(slot, c):
        _softmax_tile(s_buf[slot, c], m_scr.at[c], alpha_buf.at[slot, c], p_buf.at[slot, c])

    def pv(kt, slot, c):
        acc_scr[c] = alpha_buf[slot, c] * acc_scr[c] + jnp.dot(
            vt_ref[kt], p_buf[slot, c], preferred_element_type=F32)

    def qk_exp(kt, slot, c, _):
        s = jnp.dot(kf_ref[kt], qf_scr[c], preferred_element_type=F32)
        p_buf[slot, c] = jnp.exp2(s.astype(BF16) * F8_UNSCALE)

    def pv_plain(kt, slot, c):
        acc_scr[c] += jnp.dot(vt_ref[kt], p_buf[slot, c], preferred_element_type=F32)

    def kmax_of_group(row, k_max):
        return jnp.max(jnp.where(row == g, k_max, 0.0), axis=0, keepdims=True)

    bound, fp8_ok = _score_bound(
        qn_ref[lax.div(qi * tq, tk)], kn_ref,
        lambda row: jnp.logical_and(row >= g * A_GROUP, row < (g + 1) * A_GROUP), kmax_of_group)

    def direct():
        _direct_sweep(lambda kt: None, qk_exp, pv_plain, GQA_CHUNKS, 0, nk, nk - 1)
        return 0

    def online():
        m_scr[...] = jnp.full_like(m_scr, NEG_BIG)
        p_buf[1] = jnp.zeros(p_buf.shape[1:], BF16)
        alpha_buf[1] = jnp.ones(alpha_buf.shape[1:], F32)
        _pipelined_sweep(lambda kt: None, qk, softmax, pv, GQA_CHUNKS, 0, nk // 2, nk - 1)
        return 0

    lax.cond(jnp.logical_and(bound <= FAST_LOG2_LIMIT, fp8_ok), direct, online)
    cols = []
    for c in range(GQA_CHUNKS):
        acc = acc_scr[c]
        cols.append(acc[:HEAD_DIM] * (1.0 / acc[HEAD_DIM:HEAD_DIM + 1]))
    ot = jnp.concatenate(cols, axis=1)
    ot = jnp.concatenate([ot[:, hh * tq:(hh + 1) * tq] for hh in range(A_GROUP)], axis=0)
    o_ref[...] = ot.T.astype(BF16)


def _gqa(qta, ka3, vta, qn, kn, qfa, kfa4):
    s = qta.shape[2]
    nk, tk, _ = ka3.shape
    tq = GQA_Q_TILE
    assert nk % 2 == 0 and tk % tq == 0 and tk == MIX_ROWS
    nc = GQA_CHUNKS
    w = A_GROUP * tq // nc
    v_rows = vta.shape[2]
    return pl.pallas_call(
        _gqa_kernel,
        grid=(A_KV_HEADS, s // tq),
        in_specs=[
            pl.BlockSpec((A_GROUP, 2 * HEAD_DIM, tq), lambda g, i: (g, 0, i)),
            pl.BlockSpec((nk, tk, A_KV_COLS), lambda g, i: (0, 0, 0), pipeline_mode=pl.Buffered(1)),
            pl.BlockSpec((None, nk, v_rows, tk), lambda g, i: (g, 0, 0, 0), pipeline_mode=pl.Buffered(1)),
            pl.BlockSpec(qn.shape, lambda g, i: (0, 0, 0)),
            pl.BlockSpec(kn.shape, lambda g, i: (0, 0, 0)),
            pl.BlockSpec((A_GROUP, F8_COLS, tq), lambda g, i: (g, 0, i)),
            pl.BlockSpec((None, nk, tk, F8_COLS), lambda g, i: (g, 0, 0, 0), pipeline_mode=pl.Buffered(1)),
        ],
        out_specs=pl.BlockSpec((tq, A_GROUP * HEAD_DIM), lambda g, i: (i, g)),
        out_shape=jax.ShapeDtypeStruct((s, A_Q_COLS), BF16),
        scratch_shapes=[
            pltpu.VMEM((nc, 2 * HEAD_DIM, w), BF16),
            pltpu.VMEM((nc, F8_COLS, w), F8),
            pltpu.VMEM((2, nc, tk, w), BF16),
            pltpu.VMEM((2, nc, tk, w), BF16),
            pltpu.VMEM((2, nc, 1, w), F32),
            pltpu.VMEM((nc, 1, w), F32),
            pltpu.VMEM((nc, v_rows, w), F32),
        ],
        compiler_params=_params("parallel", "parallel"),
        name="gqa_attn",
    )(qta, ka3, vta, qn, kn, qfa, kfa4)


DIFF_CHUNKS = 2


def _diff_kernel(coef_ref, qt0_ref, qt1_ref, k0_ref, k1_ref, vt_ref, qn_ref, kn_ref, lamp_ref, sg_ref,
                 qf0_ref, qf1_ref, kf0_ref, kf1_ref,
                 o_ref, rel_scr, s_buf, p_buf, alpha_buf, m_scr, acc_scr, *, lam_init):
    hd = pl.program_id(0)
    qi = pl.program_id(1)
    nk, tk, _ = k0_ref.shape
    tq = qt0_ref.shape[1]
    nslope = coef_ref[0, hd]
    inv_tile_drop = coef_ref[1, hd]
    acc_scr[...] = jnp.zeros_like(acc_scr)

    @pl.when(jnp.logical_and(hd == 0, qi == 0))
    def _():
        rel_scr[...] = (lax.broadcasted_iota(jnp.int32, (tk, tq), 1)
                        - lax.broadcasted_iota(jnp.int32, (tk, tq), 0)).astype(F32)

    bound, fp8_ok = _score_bound(
        qn_ref[qi], kn_ref,
        lambda row: jnp.logical_or(row == NORM_ROWS_A + hd, row == NORM_ROWS_A + B_HEADS + hd),
        lambda row, k_max: k_max)
    use_direct = jnp.logical_and(bound <= FAST_LOG2_LIMIT, fp8_ok)
    drop = jnp.where(use_direct, bound, 2.0 * bound) + SKIP_LOG2
    reach = jnp.minimum(jnp.floor(drop * inv_tile_drop) + 1.0, float(nk)).astype(jnp.int32)
    lo = jnp.maximum(qi - reach, 0)
    hi = jnp.minimum(qi + reach, nk - 1)

    q0 = qi * tq
    w = tq // DIFF_CHUNKS
    n_chunks = 2 * DIFF_CHUNKS
    cols = [slice((c % DIFF_CHUNKS) * w, (c % DIFF_CHUNKS + 1) * w) for c in range(n_chunks)]
    qts = (qt0_ref, qt1_ref)
    ks = (k0_ref, k1_ref)
    qfs = (qf0_ref, qf1_ref)
    kfs = (kf0_ref, kf1_ref)

    def alibi(kt):
        off = (q0 - kt * tk).astype(F32)
        return jnp.abs(rel_scr[...] + off) * nslope

    def qk(kt, slot, c, bias):
        mp = c // DIFF_CHUNKS
        s = jnp.dot(ks[mp][kt], qts[mp][:, cols[c]], preferred_element_type=F32) + bias[:, cols[c]]
        s_buf[slot, c] = s.astype(BF16)

    def softmax(slot, c):
        _softmax_tile(s_buf[slot, c], m_scr.at[c], alpha_buf.at[slot, c], p_buf.at[slot, c])

    def pv(kt, slot, c):
        acc_scr[c] = alpha_buf[slot, c] * acc_scr[c] + jnp.dot(
            vt_ref[kt], p_buf[slot, c], preferred_element_type=F32)

    def alibi_scaled(kt):
        off = (q0 - kt * tk).astype(F32)
        return jnp.abs(rel_scr[...] + off) * (nslope * (1.0 / F8_UNSCALE))

    def qk_exp(kt, slot, c, bias_scaled):
        mp = c // DIFF_CHUNKS
        s = jnp.dot(kfs[mp][kt], qfs[mp][:, cols[c]], preferred_element_type=F32) + bias_scaled[:, cols[c]]
        p_buf[slot, c] = jnp.exp2(s.astype(BF16) * F8_UNSCALE)

    def pv_plain(kt, slot, c):
        acc_scr[c] += jnp.dot(vt_ref[kt], p_buf[slot, c], preferred_element_type=F32)

    def direct():
        _direct_sweep(alibi_scaled, qk_exp, pv_plain, n_chunks, lo, hi - lo + 1, hi)
        return 0

    def online():
        m_scr[...] = jnp.full_like(m_scr, NEG_BIG)
        p_buf[1] = jnp.zeros(p_buf.shape[1:], BF16)
        alpha_buf[1] = jnp.ones(alpha_buf.shape[1:], F32)
        odd = jnp.bitwise_and(hi - lo + 1, 1)
        grow_hi = jnp.where(hi < nk - 1, odd, 0)
        hi2 = hi + grow_hi
        lo2 = lo - (odd - grow_hi)
        _pipelined_sweep(alibi, qk, softmax, pv, n_chunks, lo2, lax.shift_right_logical(hi2 - lo2 + 1, 1), hi2)
        return 0

    lax.cond(use_direct, direct, online)

    lp = lamp_ref[...]
    lam = (jnp.exp(jnp.sum(lp[0:1] * lp[1:2], axis=1, keepdims=True))
           - jnp.exp(jnp.sum(lp[2:3] * lp[3:4], axis=1, keepdims=True)) + lam_init)
    a0, a1 = [jnp.concatenate([acc_scr[mp * DIFF_CHUNKS + j] for j in range(DIFF_CHUNKS)], axis=1)
              for mp in range(2)]
    ot = (a0[:B_V_DIM] * (1.0 / a0[B_V_DIM:B_V_DIM + 1])
          - lam * (a1[:B_V_DIM] * (1.0 / a1[B_V_DIM:B_V_DIM + 1])))
    ot = ot * lax.rsqrt(jnp.mean(ot * ot, axis=0, keepdims=True) + EPS)
    ot = ot * sg_ref[...] * (1.0 - lam_init)
    o_ref[...] = ot.T.astype(BF16)


def _diff(coef, qtd, kd3, vtd, qn, kn, qfd, kfd4, lam_p, subln_col, *, lam_init):
    s = qtd.shape[2]
    nk, tk, _ = kd3.shape
    tq = DIFF_Q_TILE
    assert nk % 2 == 0 and tq == tk == MIX_ROWS
    lanes = 2 * HEAD_DIM
    v_rows = vtd.shape[2]
    nc = 2 * DIFF_CHUNKS
    w = tq // DIFF_CHUNKS
    return pl.pallas_call(
        functools.partial(_diff_kernel, lam_init=lam_init),
        grid=(B_HEADS, s // tq),
        in_specs=[
            pl.BlockSpec(memory_space=pltpu.SMEM),
            pl.BlockSpec((None, lanes, tq), lambda h, i: (h, 0, i)),
            pl.BlockSpec((None, lanes, tq), lambda h, i: (B_HEADS + h, 0, i)),
            pl.BlockSpec((nk, tk, lanes), lambda h, i: (0, 0, h // 2), pipeline_mode=pl.Buffered(1)),
            pl.BlockSpec((nk, tk, lanes), lambda h, i: (0, 0, B_HEADS // 2 + h // 2),
                         pipeline_mode=pl.Buffered(1)),
            pl.BlockSpec((None, nk, v_rows, tk), lambda h, i: (h, 0, 0, 0), pipeline_mode=pl.Buffered(1)),
            pl.BlockSpec(qn.shape, lambda h, i: (0, 0, 0)),
            pl.BlockSpec(kn.shape, lambda h, i: (0, 0, 0)),
            pl.BlockSpec((4, HEAD_DIM), lambda h, i: (0, 0)),
            pl.BlockSpec((B_V_DIM, 1), lambda h, i: (0, 0)),
            pl.BlockSpec((None, F8_COLS, tq), lambda h, i: (h, 0, i)),
            pl.BlockSpec((None, F8_COLS, tq), lambda h, i: (B_HEADS + h, 0, i)),
            pl.BlockSpec((None, nk, tk, F8_COLS), lambda h, i: (h, 0, 0, 0), pipeline_mode=pl.Buffered(1)),
            pl.BlockSpec((None, nk, tk, F8_COLS), lambda h, i: (B_HEADS + h, 0, 0, 0),
                         pipeline_mode=pl.Buffered(1)),
        ],
        out_specs=pl.BlockSpec((tq, B_V_DIM), lambda h, i: (i, h)),
        out_shape=jax.ShapeDtypeStruct((s, B_V_COLS), BF16),
        scratch_shapes=[
            pltpu.VMEM((tk, tq), F32),
            pltpu.VMEM((2, nc, tk, w), BF16),
            pltpu.VMEM((2, nc, tk, w), BF16),
            pltpu.VMEM((2, nc, 1, w), F32),
            pltpu.VMEM((nc, 1, w), F32),
            pltpu.VMEM((nc, v_rows, w), F32),
        ],
        compiler_params=_params("arbitrary", "arbitrary"),
        name="diff_attn",
    )(coef, qtd, qtd, kd3, kd3, vtd, qn, kn, lam_p, subln_col, qfd, qfd, kfd4, kfd4)


def _mix_out_kernel(h_ref, ng_ref, sh_ref, sc_ref, gt_ref, oa_ref, od_ref, wba_ref, wbb_ref,
                    wgate_ref, bgate_ref, wo_ref, o_ref):
    h = h_ref[...]
    d = h.shape[1]
    n = _rms_rows(h) * ng_ref[...]
    n = (n * (1.0 + sc_ref[...]) + sh_ref[...]).astype(BF16)
    z = jnp.dot(n, wgate_ref[...], preferred_element_type=F32) + bgate_ref[...]
    g = 1.0 / (1.0 + jnp.exp(-z))
    ya = jnp.dot(oa_ref[...], wba_ref[...], preferred_element_type=F32)
    yb = jnp.dot(od_ref[...], wbb_ref[...], preferred_element_type=F32)
    mix = (g[:, :d] * ya + g[:, d:] * yb).astype(BF16)
    y = jnp.dot(mix, wo_ref[...], preferred_element_type=F32)
    o_ref[...] = h + gt_ref[...] * y


def _mix_out(h, ng, sh, sc, gt, oa, od, w_ba, w_bb, w_gate, b_gate, w_o):
    s, d = h.shape
    tm = MIX_ROWS
    vec = pl.BlockSpec((1, d), lambda i: (0, 0))

    def full(a):
        return pl.BlockSpec(a.shape, lambda i: (0,) * a.ndim)

    return pl.pallas_call(
        _mix_out_kernel,
        grid=(s // tm,),
        in_specs=[
            pl.BlockSpec((tm, d), lambda i: (i, 0)),
            vec, vec, vec, vec,
            pl.BlockSpec((tm, oa.shape[1]), lambda i: (i, 0)),
            pl.BlockSpec((tm, od.shape[1]), lambda i: (i, 0)),
            full(w_ba), full(w_bb), full(w_gate), full(b_gate), full(w_o),
        ],
        out_specs=pl.BlockSpec((tm, d), lambda i: (i, 0)),
        out_shape=jax.ShapeDtypeStruct((s, d), F32),
        compiler_params=_params("parallel"),
        name="mix_out",
    )(h, ng, sh, sc, gt, oa, od, w_ba, w_bb, w_gate, b_gate, w_o)


def _axial_angles_t(seq):
    rows = seq // GRID_W
    row = jnp.broadcast_to(jnp.arange(rows)[:, None], (rows, GRID_W)).reshape(seq) - rows // 2
    col = jnp.broadcast_to(jnp.arange(GRID_W)[None, :], (rows, GRID_W)).reshape(seq) - GRID_W // 2
    inv = 1.0 / (ROPE_THETA ** (jnp.arange(0, ROPE_AXIS_DIM, 2, dtype=F32) / ROPE_AXIS_DIM))
    return inv[:, None] * row.astype(F32)[None, :], inv[:, None] * col.astype(F32)[None, :]


def kernel(x, c, ada_w, ada_b, norm_g, ffn_wg, ffn_wu, ffn_wd, w_in, qk_g, lam_p, subln_g, w_ba, w_bb,
           w_gate, b_gate, w_o, final_g):
    batch, s, d = x.shape
    assert batch == 1 and s % KEY_TILE == 0 and MIX_ROWS == KEY_TILE
    depth = ada_w.shape[0]
    h = x.reshape(s, d)

    mod = _ada_mod(c.reshape(d, 1), ada_w, ada_b)
    angr_t, angc_t = _axial_angles_t(s)
    slopes = 2.0 ** (-8.0 * jnp.arange(1, B_HEADS + 1, dtype=F32) / B_HEADS)
    coef = jnp.stack([-slopes * LOG2E, 1.0 / (slopes * LOG2E * KEY_TILE)])
    fg = final_g.reshape(1, d)

    for l in range(depth):
        sh1, sc1, g1, sh2, sc2, g2, sh3, sc3, g3 = [mod[l, :, j * d:(j + 1) * d] for j in range(N_ADA)]
        lam_init = 0.8 - 0.6 * math.exp(-0.3 * l)
        ng = norm_g[l].reshape(3, 1, d)
        wg, wu, wd = ffn_wg[l].astype(BF16), ffn_wu[l].astype(BF16), ffn_wd[l].astype(BF16)

        h = _ffn(h, ng[0], sh1, sc1, g1, wg[0], wu[0], wd[0], fg, final_norm=False)

        qta, ka, vta, qtd, kd, vtd, qn, kn, qfa, kfa, qfd, kfd = _mix_in(
            h, ng[1], sh2, sc2, w_in[l].T.astype(BF16), angr_t, angc_t, qk_g[l].T)
        nk = s // KEY_TILE
        oa = _gqa(qta, ka.reshape(nk, KEY_TILE, A_KV_COLS), vta, qn, kn,
                  qfa, kfa.reshape(A_KV_HEADS, nk, KEY_TILE, F8_COLS))
        od = _diff(coef, qtd, kd.reshape(nk, KEY_TILE, B_QK_COLS), vtd, qn, kn,
                   qfd, kfd.reshape(2 * B_HEADS, nk, KEY_TILE, F8_COLS), lam_p[l],
                   subln_g[l].reshape(B_V_DIM, 1), lam_init=lam_init)
        h = _mix_out(h, ng[1], sh2, sc2, g2, oa, od, w_ba[l].astype(BF16), w_bb[l].astype(BF16),
                     w_gate[l].astype(BF16), b_gate[l].reshape(1, -1), w_o[l].astype(BF16))

        h = _ffn(h, ng[2], sh3, sc3, g3, wg[1], wu[1], wd[1], fg, final_norm=(l == depth - 1))
    return h.reshape(batch, s, d)
```

```python
import functools
import math

import jax
import jax.numpy as jnp
from jax import lax
from jax.experimental import pallas as pl
from jax.experimental.pallas import tpu as pltpu

F32 = jnp.float32
BF16 = jnp.bfloat16
F8 = jnp.float8_e4m3fn

GRID_W = 64
HEAD_DIM = 64
A_Q_HEADS = 8
A_KV_HEADS = 2
A_GROUP = A_Q_HEADS // A_KV_HEADS
B_HEADS = 4
B_V_DIM = 2 * HEAD_DIM
A_Q_COLS = A_Q_HEADS * HEAD_DIM
A_KV_COLS = A_KV_HEADS * HEAD_DIM
B_QK_COLS = 2 * B_HEADS * HEAD_DIM
B_V_COLS = B_HEADS * B_V_DIM
N_ADA = 9
EPS = 1e-6
ROPE_THETA = 10000.0
ROPE_AXIS_DIM = HEAD_DIM // 2
ROPE_HALF = ROPE_AXIS_DIM // 2

LOG2E = math.log2(math.e)
QK_SCALE = HEAD_DIM ** -0.5
NEG_BIG = -1e30

BF16_SUBLANES = 16
LANES = 128
ONES_ROWS = BF16_SUBLANES
SKIP_LOG2 = 160.0
FAST_LOG2_LIMIT = 60.0
F8_Q_SCALE = 2.0 ** 6
F8_K_SCALE = 2.0 ** 2
F8_UNSCALE = 1.0 / (F8_Q_SCALE * F8_K_SCALE)
F8_SAFE_MAX = 256.0
F8_COLS = 4 * HEAD_DIM
NORM_ROWS_A = 8
NORM_ROWS = NORM_ROWS_A + 2 * B_HEADS

VMEM_LIMIT_BYTES = 56 * 1024 * 1024

FFN_ROWS = 512
MIX_ROWS = 512
KEY_TILE = 512
GQA_Q_TILE = 512
DIFF_Q_TILE = 512
ADA_COLS = 1152


def _params(*sem):
    return pltpu.CompilerParams(dimension_semantics=sem, vmem_limit_bytes=VMEM_LIMIT_BYTES)


def _layer_vec(l, j, d):
    return pl.BlockSpec((None, None, 1, d), lambda *_: (l, j, 0, 0))


def _layer_mat(l, a, *, single_buffer=False):
    nd = a.ndim - 1
    mode = dict(pipeline_mode=pl.Buffered(1)) if single_buffer else {}
    return pl.BlockSpec((None,) + a.shape[1:], lambda *_: (l,) + (0,) * nd, **mode)


def _rms_rows(x):
    return x * lax.rsqrt(jnp.mean(x * x, axis=-1, keepdims=True) + EPS)


def _ada_kernel(c_ref, w_ref, b_ref, o_ref):
    c = c_ref[...]
    act = c / (1.0 + jnp.exp(-c))
    o_ref[...] = jnp.sum(w_ref[...] * act, axis=0, keepdims=True) + b_ref[...]


def _ada_mod(c_col, ada_w, ada_b):
    n_layers, d, n = ada_w.shape
    return pl.pallas_call(
        _ada_kernel,
        grid=(n_layers, n // ADA_COLS),
        in_specs=[
            pl.BlockSpec((d, 1), lambda l, j: (0, 0)),
            pl.BlockSpec((None, d, ADA_COLS), lambda l, j: (l, 0, j)),
            pl.BlockSpec((None, 1, ADA_COLS), lambda l, j: (l, 0, j)),
        ],
        out_specs=pl.BlockSpec((None, 1, ADA_COLS), lambda l, j: (l, 0, j)),
        out_shape=jax.ShapeDtypeStruct((n_layers, 1, n), F32),
        compiler_params=_params("parallel", "parallel"),
        name="ada_mod",
    )(c_col, ada_w, ada_b.reshape(n_layers, 1, n))


def _ffn_kernel(h_ref, ng_ref, sh_ref, sc_ref, gt_ref, wg_ref, wu_ref, wd_ref, fg_ref, o_ref, *, final_norm):
    h = h_ref[...]
    n = _rms_rows(h) * ng_ref[...]
    n = (n * (1.0 + sc_ref[...]) + sh_ref[...]).astype(BF16)
    hg = jnp.dot(n, wg_ref[...], preferred_element_type=F32)
    hu = jnp.dot(n, wu_ref[...], preferred_element_type=F32)
    a = (hg / (1.0 + jnp.exp(-hg))) * hu
    out = h + (0.5 * gt_ref[...]) * jnp.dot(a.astype(BF16), wd_ref[...], preferred_element_type=F32)
    if final_norm:
        out = _rms_rows(out) * fg_ref[...]
    o_ref[...] = out


def _ffn(h, l, which, norm_g4, mod4, wg, wu, wd, fg, *, final_norm):
    s, d = h.shape
    m0 = 6 * which

    def resident(a):
        return pl.BlockSpec((None, None) + a.shape[2:], lambda i: (l, which, 0, 0),
                            pipeline_mode=pl.Buffered(1))

    return pl.pallas_call(
        functools.partial(_ffn_kernel, final_norm=final_norm),
        grid=(s // FFN_ROWS,),
        in_specs=[
            pl.BlockSpec((FFN_ROWS, d), lambda i: (i, 0)),
            _layer_vec(l, 2 * which, d),
            _layer_vec(l, m0, d), _layer_vec(l, m0 + 1, d), _layer_vec(l, m0 + 2, d),
            resident(wg), resident(wu), resident(wd),
            pl.BlockSpec((1, d), lambda i: (0, 0)),
        ],
        out_specs=pl.BlockSpec((FFN_ROWS, d), lambda i: (i, 0)),
        out_shape=jax.ShapeDtypeStruct((s, d), F32),
        compiler_params=_params("parallel"),
        name="ffn_final" if final_norm else "ffn",
    )(h, norm_g4, mod4, mod4, mod4, wg, wu, wd, fg)


def _rope_t(x, cr, sr, cc, sc):
    h = ROPE_HALF
    x1r, x2r, x1c, x2c = x[0:h], x[h:2 * h], x[2 * h:3 * h], x[3 * h:4 * h]
    return jnp.concatenate(
        [x1r * cr - x2r * sr, x2r * cr + x1r * sr, x1c * cc - x2c * sc, x2c * cc + x1c * sc], axis=0)


def _head_norm_t(x, g_col):
    ms = jnp.mean(x * x, axis=0, keepdims=True)
    return x * lax.rsqrt(ms + EPS) * g_col


def _max_sq_norm(x_bf16):
    xf = x_bf16.astype(F32)
    n2 = jnp.sum(xf * xf, axis=0, keepdims=True)
    return jnp.broadcast_to(jnp.max(n2, axis=1, keepdims=True), (1, LANES))


def _f8_split(x):
    hi = x.astype(F8).astype(F32)
    return hi, ((x - hi) * 16.0).astype(F8).astype(F32)


def _f8_query_rows(q_bf16):
    hi, lo16 = _f8_split(q_bf16.astype(F32) * F8_Q_SCALE)
    return jnp.concatenate([hi, hi * 0.0625, lo16 * 0.0625, lo16 * 0.00390625], axis=0).astype(F8)


def _f8_key_rows(k_bf16):
    hi, lo16 = _f8_split(k_bf16.astype(F32) * F8_K_SCALE)
    return jnp.concatenate([hi, lo16, hi, lo16], axis=0)


def _with_ones(v_t):
    return jnp.concatenate([v_t, jnp.ones((ONES_ROWS, v_t.shape[1]), F32)], axis=0).astype(BF16)


def _mix_in_kernel(h_ref, ng_ref, sh_ref, sc_ref, wt_ref, angr_ref, angc_ref, qkg_ref,
                   qta_ref, ka_ref, vta_ref, qtd_ref, kd_ref, vtd_ref, qn_ref, kn_ref,
                   qfa_ref, kfa_ref, qfd_ref, kfd_ref):
    n = _rms_rows(h_ref[...]) * ng_ref[...]
    n = (n * (1.0 + sc_ref[...]) + sh_ref[...]).astype(BF16)
    pt = lax.dot_general(wt_ref[...], n, (((1,), (1,)), ((), ())), preferred_element_type=F32)
    t = pt.shape[1]

    cr, sr = jnp.cos(angr_ref[...]), jnp.sin(angr_ref[...])
    cc, sc = jnp.cos(angc_ref[...]), jnp.sin(angc_ref[...])
    gq = qkg_ref[:, 0:1]
    gk = qkg_ref[:, 1:2]
    zeros = jnp.zeros((HEAD_DIM, t), BF16)
    qscale = QK_SCALE * LOG2E

    qn, kn = [], []
    for hd in range(A_Q_HEADS):
        q = _rope_t(_head_norm_t(pt[hd * HEAD_DIM:(hd + 1) * HEAD_DIM], gq), cr, sr, cc, sc) * qscale
        q = q.astype(BF16)
        qta_ref[hd] = jnp.concatenate([q, zeros] if hd // A_GROUP == 0 else [zeros, q], axis=0)
        qfa_ref[hd] = _f8_query_rows(q)
        qn.append(_max_sq_norm(q))

    c1 = A_Q_COLS
    c2 = c1 + A_KV_COLS
    c3 = c2 + A_KV_COLS
    c4 = c3 + B_QK_COLS
    c5 = c4 + B_QK_COLS
    kt = jnp.concatenate(
        [_rope_t(_head_norm_t(pt[c1 + j * HEAD_DIM:c1 + (j + 1) * HEAD_DIM], gk), cr, sr, cc, sc)
         for j in range(A_KV_HEADS)], axis=0)
    kt = kt.astype(BF16)
    ka_ref[...] = kt.T
    kn += [_max_sq_norm(kt[j * HEAD_DIM:(j + 1) * HEAD_DIM]) for j in range(A_KV_HEADS)]
    for j in range(A_KV_HEADS):
        kfa_ref[j] = _f8_key_rows(kt[j * HEAD_DIM:(j + 1) * HEAD_DIM]).astype(BF16).T.astype(F8)
    kn.append(jnp.zeros((NORM_ROWS_A - A_KV_HEADS, LANES), F32))
    for j in range(A_KV_HEADS):
        vta_ref[j] = _with_ones(pt[c2 + j * HEAD_DIM:c2 + (j + 1) * HEAD_DIM])

    for cmb in range(2 * B_HEADS):
        q = (pt[c3 + cmb * HEAD_DIM:c3 + (cmb + 1) * HEAD_DIM] * qscale).astype(BF16)
        qtd_ref[cmb] = jnp.concatenate([q, zeros] if cmb % 2 == 0 else [zeros, q], axis=0)
        qfd_ref[cmb] = _f8_query_rows(q)
        qn.append(_max_sq_norm(q))
        k = pt[c4 + cmb * HEAD_DIM:c4 + (cmb + 1) * HEAD_DIM].astype(BF16)
        kfd_ref[cmb] = _f8_key_rows(k).astype(BF16).T.astype(F8)
        kn.append(_max_sq_norm(k))
    qn_ref[...] = jnp.concatenate(qn, axis=0)
    kn_ref[...] = jnp.concatenate(kn, axis=0)
    kd_ref[...] = pt[c4:c5].T.astype(BF16)
    for hd in range(B_HEADS):
        vtd_ref[hd] = _with_ones(pt[c5 + hd * B_V_DIM:c5 + (hd + 1) * B_V_DIM])


def _mix_in(h, l, norm_g4, mod4, w_in_t, angr_t, angc_t, qkg_t):
    s, d = h.shape
    tm = MIX_ROWS
    nt = s // tm
    n_sets = 2 * B_HEADS
    assert A_Q_HEADS == NORM_ROWS_A
    out_shape = (
        jax.ShapeDtypeStruct((A_Q_HEADS, 2 * HEAD_DIM, s), BF16),
        jax.ShapeDtypeStruct((s, A_KV_COLS), BF16),
        jax.ShapeDtypeStruct((A_KV_HEADS, nt, HEAD_DIM + ONES_ROWS, tm), BF16),
        jax.ShapeDtypeStruct((n_sets, 2 * HEAD_DIM, s), BF16),
        jax.ShapeDtypeStruct((s, B_QK_COLS), BF16),
        jax.ShapeDtypeStruct((B_HEADS, nt, B_V_DIM + ONES_ROWS, tm), BF16),
        jax.ShapeDtypeStruct((nt, NORM_ROWS, LANES), F32),
        jax.ShapeDtypeStruct((nt, NORM_ROWS, LANES), F32),
        jax.ShapeDtypeStruct((A_Q_HEADS, F8_COLS, s), F8),
        jax.ShapeDtypeStruct((A_KV_HEADS, s, F8_COLS), F8),
        jax.ShapeDtypeStruct((n_sets, F8_COLS, s), F8),
        jax.ShapeDtypeStruct((n_sets, s, F8_COLS), F8),
    )
    out_specs = (
        pl.BlockSpec((A_Q_HEADS, 2 * HEAD_DIM, tm), lambda i: (0, 0, i)),
        pl.BlockSpec((tm, A_KV_COLS), lambda i: (i, 0)),
        pl.BlockSpec((A_KV_HEADS, None, HEAD_DIM + ONES_ROWS, tm), lambda i: (0, i, 0, 0)),
        pl.BlockSpec((n_sets, 2 * HEAD_DIM, tm), lambda i: (0, 0, i)),
        pl.BlockSpec((tm, B_QK_COLS), lambda i: (i, 0)),
        pl.BlockSpec((B_HEADS, None, B_V_DIM + ONES_ROWS, tm), lambda i: (0, i, 0, 0)),
        pl.BlockSpec((None, NORM_ROWS, LANES), lambda i: (i, 0, 0)),
        pl.BlockSpec((None, NORM_ROWS, LANES), lambda i: (i, 0, 0)),
        pl.BlockSpec((A_Q_HEADS, F8_COLS, tm), lambda i: (0, 0, i)),
        pl.BlockSpec((A_KV_HEADS, tm, F8_COLS), lambda i: (0, i, 0)),
        pl.BlockSpec((n_sets, F8_COLS, tm), lambda i: (0, 0, i)),
        pl.BlockSpec((n_sets, tm, F8_COLS), lambda i: (0, i, 0)),
    )
    return pl.pallas_call(
        _mix_in_kernel,
        grid=(nt,),
        in_specs=[
            pl.BlockSpec((tm, d), lambda i: (i, 0)),
            _layer_vec(l, 1, d), _layer_vec(l, 3, d), _layer_vec(l, 4, d),
            _layer_mat(l, w_in_t, single_buffer=True),
            pl.BlockSpec((ROPE_HALF, tm), lambda i: (0, i)),
            pl.BlockSpec((ROPE_HALF, tm), lambda i: (0, i)),
            _layer_mat(l, qkg_t),
        ],
        out_specs=out_specs,
        out_shape=out_shape,
        compiler_params=_params("parallel"),
        name="mix_in",
    )(h, norm_g4, mod4, mod4, w_in_t, angr_t, angc_t, qkg_t)


def _softmax_tile(s, m_ref, alpha_ref, p_ref):
    tk, n = s.shape
    part = jnp.max(s.reshape(tk // BF16_SUBLANES, BF16_SUBLANES, n), axis=0)
    m_old = m_ref[...]
    m_new = jnp.maximum(m_old, jnp.max(part.astype(F32), axis=0, keepdims=True))
    alpha_ref[...] = jnp.exp2(m_old - m_new)
    m_ref[...] = m_new
    p_ref[...] = jnp.exp2(s - m_new.astype(BF16))


def _pipelined_sweep(prep, qk, softmax, pv, n_chunks, lo, n_pairs, last):
    ctx = prep(lo)
    for c in range(n_chunks):
        qk(lo, 0, c, ctx)

    def body(j, carry):
        a = lo + 2 * j
        prev = jnp.maximum(a - 1, 0)
        nxt = jnp.minimum(a + 2, last)
        ctx = prep(a + 1)
        for c in range(n_chunks):
            softmax(0, c)
            pv(prev, 1, c)
            qk(a + 1, 1, c, ctx)
        ctx = prep(nxt)
        for c in range(n_chunks):
            softmax(1, c)
            pv(a, 0, c)
            qk(nxt, 0, c, ctx)
        return carry

    lax.fori_loop(0, n_pairs, body, 0)
    for c in range(n_chunks):
        pv(last, 1, c)


def _direct_sweep(prep, qk_exp, pv, n_chunks, lo, n_tiles, last):
    def phase(kt_new, slot_new, kt_old):
        ctx = prep(kt_new)
        for c in range(n_chunks):
            qk_exp(kt_new, slot_new, c, ctx)
            pv(kt_old, 1 - slot_new, c)

    ctx = prep(lo)
    for c in range(n_chunks):
        qk_exp(lo, 0, c, ctx)

    def body(j, carry):
        a = lo + 2 * j
        phase(a + 1, 1, a)
        phase(a + 2, 0, a + 1)
        return carry

    n_iter = (n_tiles - 1) // 2 if isinstance(n_tiles, int) else lax.shift_right_logical(n_tiles - 1, 1)
    lax.fori_loop(0, n_iter, body, 0)

    def tail_odd():
        for c in range(n_chunks):
            pv(last, 0, c)
        return 0

    def tail_even():
        phase(last, 1, last - 1)
        for c in range(n_chunks):
            pv(last, 1, c)
        return 0

    if isinstance(n_tiles, int):
        (tail_odd if n_tiles % 2 else tail_even)()
    else:
        lax.cond(jnp.bitwise_and(n_tiles, 1) == 1, tail_odd, tail_even)


def _score_bound(qn_tile, kn_ref, q_rows, k_row_of_q):
    k_max = jnp.max(kn_ref[...], axis=0)
    row = lax.broadcasted_iota(jnp.int32, k_max.shape, 0)
    sel = q_rows(row)
    q2 = jnp.where(sel, qn_tile, 0.0)
    k2 = jnp.where(sel, k_row_of_q(row, k_max), 0.0)
    fp8_ok = jnp.logical_and(jnp.max(q2) * F8_Q_SCALE ** 2 <= F8_SAFE_MAX ** 2,
                             jnp.max(k2) * F8_K_SCALE ** 2 <= F8_SAFE_MAX ** 2)
    return jnp.max(jnp.sqrt(q2 * k2)), fp8_ok


GQA_CHUNKS = 8


def _gqa_kernel(qt_ref, k_ref, vt_ref, qn_ref, kn_ref, qf_ref, kf_ref, o_ref, q_scr, qf_scr, s_buf, p_buf,
                alpha_buf, m_scr, acc_scr):
    g = pl.program_id(0)
    qi = pl.program_id(1)
    nk, tk, _ = k_ref.shape
    tq = qt_ref.shape[2]
    w = A_GROUP * tq // GQA_CHUNKS
    for c in range(GQA_CHUNKS):
        for col in range(c * w, (c + 1) * w, min(w, tq)):
            hh, j, n = col // tq, col % tq, min(w, tq)
            q_scr[c, :, col - c * w:col - c * w + n] = qt_ref[hh, :, j:j + n]
            qf_scr[c, :, col - c * w:col - c * w + n] = qf_ref[hh, :, j:j + n]
    acc_scr[...] = jnp.zeros_like(acc_scr)

    def qk(kt, slot, c, _):
        s_buf[slot, c] = jnp.dot(k_ref[kt], q_scr[c], preferred_element_type=F32).astype(BF16)

    def softmax(slot, c):
        _softmax_tile(s_buf[slot, c], m_scr.at[c], alpha_buf.at[slot, c], p_buf.at[slot, c])

    def pv(kt, slot, c):
        acc_scr[c] = alpha_buf[slot, c] * acc_scr[c] + jnp.dot(
            vt_ref[kt], p_buf[slot, c], preferred_element_type=F32)

    def qk_exp(kt, slot, c, _):
        s = jnp.dot(kf_ref[kt], qf_scr[c], preferred_element_type=F32)
        p_buf[slot, c] = jnp.exp2(s.astype(BF16) * F8_UNSCALE)

    def pv_plain(kt, slot, c):
        acc_scr[c] += jnp.dot(vt_ref[kt], p_buf[slot, c], preferred_element_type=F32)

    def kmax_of_group(row, k_max):
        return jnp.max(jnp.where(row == g, k_max, 0.0), axis=0, keepdims=True)

    bound, fp8_ok = _score_bound(
        qn_ref[lax.div(qi * tq, tk)], kn_ref,
        lambda row: jnp.logical_and(row >= g * A_GROUP, row < (g + 1) * A_GROUP), kmax_of_group)

    def direct():
        _direct_sweep(lambda kt: None, qk_exp, pv_plain, GQA_CHUNKS, 0, nk, nk - 1)
        return 0

    def online():
        m_scr[...] = jnp.full_like(m_scr, NEG_BIG)
        p_buf[1] = jnp.zeros(p_buf.shape[1:], BF16)
        alpha_buf[1] = jnp.ones(alpha_buf.shape[1:], F32)
        _pipelined_sweep(lambda kt: None, qk, softmax, pv, GQA_CHUNKS, 0, nk // 2, nk - 1)
        return 0

    lax.cond(jnp.logical_and(bound <= FAST_LOG2_LIMIT, fp8_ok), direct, online)
    cols = []
    for c in range(GQA_CHUNKS):
        acc = acc_scr[c]
        cols.append(acc[:HEAD_DIM] * (1.0 / acc[HEAD_DIM:HEAD_DIM + 1]))
    ot = jnp.concatenate(cols, axis=1)
    ot = jnp.concatenate([ot[:, hh * tq:(hh + 1) * tq] for hh in range(A_GROUP)], axis=0)
    o_ref[...] = ot.T.astype(BF16)


def _gqa(qta, ka3, vta, qn, kn, qfa, kfa4):
    s = qta.shape[2]
    nk, tk, _ = ka3.shape
    tq = GQA_Q_TILE
    assert nk % 2 == 0 and tk % tq == 0 and tk == MIX_ROWS
    nc = GQA_CHUNKS
    w = A_GROUP * tq // nc
    v_rows = vta.shape[2]
    return pl.pallas_call(
        _gqa_kernel,
        grid=(A_KV_HEADS, s // tq),
        in_specs=[
            pl.BlockSpec((A_GROUP, 2 * HEAD_DIM, tq), lambda g, i: (g, 0, i)),
            pl.BlockSpec((nk, tk, A_KV_COLS), lambda g, i: (0, 0, 0), pipeline_mode=pl.Buffered(1)),
            pl.BlockSpec((None, nk, v_rows, tk), lambda g, i: (g, 0, 0, 0), pipeline_mode=pl.Buffered(1)),
            pl.BlockSpec(qn.shape, lambda g, i: (0, 0, 0)),
            pl.BlockSpec(kn.shape, lambda g, i: (0, 0, 0)),
            pl.BlockSpec((A_GROUP, F8_COLS, tq), lambda g, i: (g, 0, i)),
            pl.BlockSpec((None, nk, tk, F8_COLS), lambda g, i: (g, 0, 0, 0), pipeline_mode=pl.Buffered(1)),
        ],
        out_specs=pl.BlockSpec((tq, A_GROUP * HEAD_DIM), lambda g, i: (i, g)),
        out_shape=jax.ShapeDtypeStruct((s, A_Q_COLS), BF16),
        scratch_shapes=[
            pltpu.VMEM((nc, 2 * HEAD_DIM, w), BF16),
            pltpu.VMEM((nc, F8_COLS, w), F8),
            pltpu.VMEM((2, nc, tk, w), BF16),
            pltpu.VMEM((2, nc, tk, w), BF16),
            pltpu.VMEM((2, nc, 1, w), F32),
            pltpu.VMEM((nc, 1, w), F32),
            pltpu.VMEM((nc, v_rows, w), F32),
        ],
        compiler_params=_params("parallel", "parallel"),
        name="gqa_attn",
    )(qta, ka3, vta, qn, kn, qfa, kfa4)


DIFF_CHUNKS = 2


def _diff_kernel(coef_ref, qt0_ref, qt1_ref, k0_ref, k1_ref, vt_ref, qn_ref, kn_ref, lamp_ref, sg_ref,
                 qf0_ref, qf1_ref, kf0_ref, kf1_ref,
                 o_ref, rel_scr, s_buf, p_buf, alpha_buf, m_scr, acc_scr, *, lam_init):
    hd = pl.program_id(0)
    qi = pl.program_id(1)
    nk, tk, _ = k0_ref.shape
    tq = qt0_ref.shape[1]
    nslope = coef_ref[0, hd]
    inv_tile_drop = coef_ref[1, hd]
    acc_scr[...] = jnp.zeros_like(acc_scr)

    @pl.when(jnp.logical_and(hd == 0, qi == 0))
    def _():
        rel_scr[...] = (lax.broadcasted_iota(jnp.int32, (tk, tq), 1)
                        - lax.broadcasted_iota(jnp.int32, (tk, tq), 0)).astype(F32)

    bound, fp8_ok = _score_bound(
        qn_ref[qi], kn_ref,
        lambda row: jnp.logical_or(row == NORM_ROWS_A + hd, row == NORM_ROWS_A + B_HEADS + hd),
        lambda row, k_max: k_max)
    use_direct = jnp.logical_and(bound <= FAST_LOG2_LIMIT, fp8_ok)
    drop = jnp.where(use_direct, bound, 2.0 * bound) + SKIP_LOG2
    reach = jnp.minimum(jnp.floor(drop * inv_tile_drop) + 1.0, float(nk)).astype(jnp.int32)
    lo = jnp.maximum(qi - reach, 0)
    hi = jnp.minimum(qi + reach, nk - 1)

    q0 = qi * tq
    w = tq // DIFF_CHUNKS
    n_chunks = 2 * DIFF_CHUNKS
    cols = [slice((c % DIFF_CHUNKS) * w, (c % DIFF_CHUNKS + 1) * w) for c in range(n_chunks)]
    qts = (qt0_ref, qt1_ref)
    ks = (k0_ref, k1_ref)
    qfs = (qf0_ref, qf1_ref)
    kfs = (kf0_ref, kf1_ref)

    def alibi(kt):
        off = (q0 - kt * tk).astype(F32)
        return jnp.abs(rel_scr[...] + off) * nslope

    def qk(kt, slot, c, bias):
        mp = c // DIFF_CHUNKS
        s = jnp.dot(ks[mp][kt], qts[mp][:, cols[c]], preferred_element_type=F32) + bias[:, cols[c]]
        s_buf[slot, c] = s.astype(BF16)

    def softmax(slot, c):
        _softmax_tile(s_buf[slot, c], m_scr.at[c], alpha_buf.at[slot, c], p_buf.at[slot, c])

    def pv(kt, slot, c):
        acc_scr[c] = alpha_buf[slot, c] * acc_scr[c] + jnp.dot(
            vt_ref[kt], p_buf[slot, c], preferred_element_type=F32)

    def alibi_scaled(kt):
        off = (q0 - kt * tk).astype(F32)
        return jnp.abs(rel_scr[...] + off) * (nslope * (1.0 / F8_UNSCALE))

    def qk_exp(kt, slot, c, bias_scaled):
        mp = c // DIFF_CHUNKS
        s = jnp.dot(kfs[mp][kt], qfs[mp][:, cols[c]], preferred_element_type=F32) + bias_scaled[:, cols[c]]
        p_buf[slot, c] = jnp.exp2(s.astype(BF16) * F8_UNSCALE)

    def pv_plain(kt, slot, c):
        acc_scr[c] += jnp.dot(vt_ref[kt], p_buf[slot, c], preferred_element_type=F32)

    def direct():
        _direct_sweep(alibi_scaled, qk_exp, pv_plain, n_chunks, lo, hi - lo + 1, hi)
        return 0

    def online():
        m_scr[...] = jnp.full_like(m_scr, NEG_BIG)
        p_buf[1] = jnp.zeros(p_buf.shape[1:], BF16)
        alpha_buf[1] = jnp.ones(alpha_buf.shape[1:], F32)
        odd = jnp.bitwise_and(hi - lo + 1, 1)
        grow_hi = jnp.where(hi < nk - 1, odd, 0)
        hi2 = hi + grow_hi
        lo2 = lo - (odd - grow_hi)
        _pipelined_sweep(alibi, qk, softmax, pv, n_chunks, lo2, lax.shift_right_logical(hi2 - lo2 + 1, 1), hi2)
        return 0

    lax.cond(use_direct, direct, online)

    lp = lamp_ref[...]
    lam = (jnp.exp(jnp.sum(lp[0:1] * lp[1:2], axis=1, keepdims=True))
           - jnp.exp(jnp.sum(lp[2:3] * lp[3:4], axis=1, keepdims=True)) + lam_init)
    a0, a1 = [jnp.concatenate([acc_scr[mp * DIFF_CHUNKS + j] for j in range(DIFF_CHUNKS)], axis=1)
              for mp in range(2)]
    ot = (a0[:B_V_DIM] * (1.0 / a0[B_V_DIM:B_V_DIM + 1])
          - lam * (a1[:B_V_DIM] * (1.0 / a1[B_V_DIM:B_V_DIM + 1])))
    ot = ot * lax.rsqrt(jnp.mean(ot * ot, axis=0, keepdims=True) + EPS)
    ot = ot * sg_ref[...] * (1.0 - lam_init)
    o_ref[...] = ot.T.astype(BF16)


def _diff(l, coef, qtd, kd3, vtd, qn, kn, qfd, kfd4, lam_p, subln_col, *, lam_init):
    s = qtd.shape[2]
    nk, tk, _ = kd3.shape
    tq = DIFF_Q_TILE
    assert nk % 2 == 0 and tq == tk == MIX_ROWS
    lanes = 2 * HEAD_DIM
    v_rows = vtd.shape[2]
    nc = 2 * DIFF_CHUNKS
    w = tq // DIFF_CHUNKS
    return pl.pallas_call(
        functools.partial(_diff_kernel, lam_init=lam_init),
        grid=(B_HEADS, s // tq),
        in_specs=[
            pl.BlockSpec(memory_space=pltpu.SMEM),
            pl.BlockSpec((None, lanes, tq), lambda h, i: (h, 0, i)),
            pl.BlockSpec((None, lanes, tq), lambda h, i: (B_HEADS + h, 0, i)),
            pl.BlockSpec((nk, tk, lanes), lambda h, i: (0, 0, h // 2), pipeline_mode=pl.Buffered(1)),
            pl.BlockSpec((nk, tk, lanes), lambda h, i: (0, 0, B_HEADS // 2 + h // 2),
                         pipeline_mode=pl.Buffered(1)),
            pl.BlockSpec((None, nk, v_rows, tk), lambda h, i: (h, 0, 0, 0), pipeline_mode=pl.Buffered(1)),
            pl.BlockSpec(qn.shape, lambda h, i: (0, 0, 0)),
            pl.BlockSpec(kn.shape, lambda h, i: (0, 0, 0)),
            _layer_mat(l, lam_p),
            _layer_mat(l, subln_col),
            pl.BlockSpec((None, F8_COLS, tq), lambda h, i: (h, 0, i)),
            pl.BlockSpec((None, F8_COLS, tq), lambda h, i: (B_HEADS + h, 0, i)),
            pl.BlockSpec((None, nk, tk, F8_COLS), lambda h, i: (h, 0, 0, 0), pipeline_mode=pl.Buffered(1)),
            pl.BlockSpec((None, nk, tk, F8_COLS), lambda h, i: (B_HEADS + h, 0, 0, 0),
                         pipeline_mode=pl.Buffered(1)),
        ],
        out_specs=pl.BlockSpec((tq, B_V_DIM), lambda h, i: (i, h)),
        out_shape=jax.ShapeDtypeStruct((s, B_V_COLS), BF16),
        scratch_shapes=[
            pltpu.VMEM((tk, tq), F32),
            pltpu.VMEM((2, nc, tk, w), BF16),
            pltpu.VMEM((2, nc, tk, w), BF16),
            pltpu.VMEM((2, nc, 1, w), F32),
            pltpu.VMEM((nc, 1, w), F32),
            pltpu.VMEM((nc, v_rows, w), F32),
        ],
        compiler_params=_params("arbitrary", "arbitrary"),
        name="diff_attn",
    )(coef, qtd, qtd, kd3, kd3, vtd, qn, kn, lam_p, subln_col, qfd, qfd, kfd4, kfd4)


def _mix_out_kernel(h_ref, ng_ref, sh_ref, sc_ref, gt_ref, oa_ref, od_ref, wba_ref, wbb_ref,
                    wgate_ref, bgate_ref, wo_ref, o_ref):
    h = h_ref[...]
    d = h.shape[1]
    n = _rms_rows(h) * ng_ref[...]
    n = (n * (1.0 + sc_ref[...]) + sh_ref[...]).astype(BF16)
    z = jnp.dot(n, wgate_ref[...], preferred_element_type=F32) + bgate_ref[...]
    g = 1.0 / (1.0 + jnp.exp(-z))
    ya = jnp.dot(oa_ref[...], wba_ref[...], preferred_element_type=F32)
    yb = jnp.dot(od_ref[...], wbb_ref[...], preferred_element_type=F32)
    mix = (g[:, :d] * ya + g[:, d:] * yb).astype(BF16)
    y = jnp.dot(mix, wo_ref[...], preferred_element_type=F32)
    o_ref[...] = h + gt_ref[...] * y


def _mix_out(h, l, norm_g4, mod4, oa, od, w_ba, w_bb, w_gate, b_gate, w_o):
    s, d = h.shape
    tm = MIX_ROWS
    return pl.pallas_call(
        _mix_out_kernel,
        grid=(s // tm,),
        in_specs=[
            pl.BlockSpec((tm, d), lambda i: (i, 0)),
            _layer_vec(l, 1, d), _layer_vec(l, 3, d), _layer_vec(l, 4, d), _layer_vec(l, 5, d),
            pl.BlockSpec((tm, oa.shape[1]), lambda i: (i, 0)),
            pl.BlockSpec((tm, od.shape[1]), lambda i: (i, 0)),
            _layer_mat(l, w_ba, single_buffer=True), _layer_mat(l, w_bb, single_buffer=True),
            _layer_mat(l, w_gate, single_buffer=True), _layer_mat(l, b_gate),
            _layer_mat(l, w_o, single_buffer=True),
        ],
        out_specs=pl.BlockSpec((tm, d), lambda i: (i, 0)),
        out_shape=jax.ShapeDtypeStruct((s, d), F32),
        compiler_params=_params("parallel"),
        name="mix_out",
    )(h, norm_g4, mod4, mod4, mod4, oa, od, w_ba, w_bb, w_gate, b_gate, w_o)


def _axial_angles_t(seq):
    rows = seq // GRID_W
    row = jnp.broadcast_to(jnp.arange(rows)[:, None], (rows, GRID_W)).reshape(seq) - rows // 2
    col = jnp.broadcast_to(jnp.arange(GRID_W)[None, :], (rows, GRID_W)).reshape(seq) - GRID_W // 2
    inv = 1.0 / (ROPE_THETA ** (jnp.arange(0, ROPE_AXIS_DIM, 2, dtype=F32) / ROPE_AXIS_DIM))
    return inv[:, None] * row.astype(F32)[None, :], inv[:, None] * col.astype(F32)[None, :]


def kernel(x, c, ada_w, ada_b, norm_g, ffn_wg, ffn_wu, ffn_wd, w_in, qk_g, lam_p, subln_g, w_ba, w_bb,
           w_gate, b_gate, w_o, final_g):
    batch, s, d = x.shape
    assert batch == 1 and s % KEY_TILE == 0 and MIX_ROWS == KEY_TILE
    depth = ada_w.shape[0]
    h = x.reshape(s, d)

    mod4 = _ada_mod(c.reshape(d, 1), ada_w, ada_b).reshape(depth, N_ADA, 1, d)
    norm_g4 = norm_g.reshape(depth, 3, 1, d)
    angr_t, angc_t = _axial_angles_t(s)
    slopes = 2.0 ** (-8.0 * jnp.arange(1, B_HEADS + 1, dtype=F32) / B_HEADS)
    coef = jnp.stack([-slopes * LOG2E, 1.0 / (slopes * LOG2E * KEY_TILE)])
    fg = final_g.reshape(1, d)
    wg, wu, wd = ffn_wg.astype(BF16), ffn_wu.astype(BF16), ffn_wd.astype(BF16)
    w_in_t = jnp.swapaxes(w_in, 1, 2).astype(BF16)
    qkg_t = jnp.swapaxes(qk_g, 1, 2)
    w_ba16, w_bb16, w_gate16, w_o16 = (w.astype(BF16) for w in (w_ba, w_bb, w_gate, w_o))
    b_gate3 = b_gate.reshape(depth, 1, -1)
    subln_col = subln_g.reshape(depth, B_V_DIM, 1)
    nk = s // KEY_TILE

    for l in range(depth):
        lam_init = 0.8 - 0.6 * math.exp(-0.3 * l)
        h = _ffn(h, l, 0, norm_g4, mod4, wg, wu, wd, fg, final_norm=False)
        qta, ka, vta, qtd, kd, vtd, qn, kn, qfa, kfa, qfd, kfd = _mix_in(
            h, l, norm_g4, mod4, w_in_t, angr_t, angc_t, qkg_t)
        oa = _gqa(qta, ka.reshape(nk, KEY_TILE, A_KV_COLS), vta, qn, kn,
                  qfa, kfa.reshape(A_KV_HEADS, nk, KEY_TILE, F8_COLS))
        od = _diff(l, coef, qtd, kd.reshape(nk, KEY_TILE, B_QK_COLS), vtd, qn, kn,
                   qfd, kfd.reshape(2 * B_HEADS, nk, KEY_TILE, F8_COLS), lam_p, subln_col, lam_init=lam_init)
        h = _mix_out(h, l, norm_g4, mod4, oa, od, w_ba16, w_bb16, w_gate16, b_gate3, w_o16)
        h = _ffn(h, l, 1, norm_g4, mod4, wg, wu, wd, fg, final_norm=(l == depth - 1))
    return h.reshape(batch, s, d)
```

```python
import functools
import math

import jax
import jax.numpy as jnp
from jax import lax
from jax.experimental import pallas as pl
from jax.experimental.pallas import tpu as pltpu

F32 = jnp.float32
BF16 = jnp.bfloat16
F8 = jnp.float8_e4m3fn

GRID_W = 64
HEAD_DIM = 64
A_Q_HEADS = 8
A_KV_HEADS = 2
A_GROUP = A_Q_HEADS // A_KV_HEADS
B_HEADS = 4
B_V_DIM = 2 * HEAD_DIM
A_Q_COLS = A_Q_HEADS * HEAD_DIM
A_KV_COLS = A_KV_HEADS * HEAD_DIM
B_QK_COLS = 2 * B_HEADS * HEAD_DIM
B_V_COLS = B_HEADS * B_V_DIM
N_ADA = 9
EPS = 1e-6
ROPE_THETA = 10000.0
ROPE_AXIS_DIM = HEAD_DIM // 2
ROPE_HALF = ROPE_AXIS_DIM // 2

LOG2E = math.log2(math.e)
QK_SCALE = HEAD_DIM ** -0.5
NEG_BIG = -1e30

BF16_SUBLANES = 16
LANES = 128
ONES_ROWS = BF16_SUBLANES
SKIP_LOG2 = 160.0
FAST_LOG2_LIMIT = 60.0
F8_Q_SCALE = 2.0 ** 6
F8_K_SCALE = 2.0 ** 2
F8_UNSCALE = 1.0 / (F8_Q_SCALE * F8_K_SCALE)
F8_SAFE_MAX = 256.0
F8_COLS = 4 * HEAD_DIM
NORM_ROWS_A = 8
NORM_ROWS = NORM_ROWS_A + 2 * B_HEADS

VMEM_LIMIT_BYTES = 56 * 1024 * 1024

FFN_ROWS = 512
MIX_ROWS = 512
KEY_TILE = 512
GQA_Q_TILE = 512
DIFF_Q_TILE = 512
ADA_COLS = 1152


def _params(*sem):
    return pltpu.CompilerParams(dimension_semantics=sem, vmem_limit_bytes=VMEM_LIMIT_BYTES)


def _layer_vec(l, j, d):
    return pl.BlockSpec((None, None, 1, d), lambda *_: (l, j, 0, 0))


def _layer_mat(l, a, *, single_buffer=False):
    nd = a.ndim - 1
    mode = dict(pipeline_mode=pl.Buffered(1)) if single_buffer else {}
    return pl.BlockSpec((None,) + a.shape[1:], lambda *_: (l,) + (0,) * nd, **mode)


def _rms_rows(x):
    return x * lax.rsqrt(jnp.mean(x * x, axis=-1, keepdims=True) + EPS)


def _ada_kernel(c_ref, w_ref, b_ref, o_ref):
    c = c_ref[...]
    act = c / (1.0 + jnp.exp(-c))
    o_ref[...] = jnp.sum(w_ref[...] * act, axis=0, keepdims=True) + b_ref[...]


def _ada_mod(c_col, ada_w, ada_b):
    n_layers, d, n = ada_w.shape
    return pl.pallas_call(
        _ada_kernel,
        grid=(n_layers, n // ADA_COLS),
        in_specs=[
            pl.BlockSpec((d, 1), lambda l, j: (0, 0)),
            pl.BlockSpec((None, d, ADA_COLS), lambda l, j: (l, 0, j)),
            pl.BlockSpec((None, 1, ADA_COLS), lambda l, j: (l, 0, j)),
        ],
        out_specs=pl.BlockSpec((None, 1, ADA_COLS), lambda l, j: (l, 0, j)),
        out_shape=jax.ShapeDtypeStruct((n_layers, 1, n), F32),
        compiler_params=_params("parallel", "parallel"),
        name="ada_mod",
    )(c_col, ada_w, ada_b.reshape(n_layers, 1, n))


def _ffn_kernel(h_ref, ng_ref, sh_ref, sc_ref, gt_ref, wg_ref, wu_ref, wd_ref, fg_ref, o_ref, *, final_norm):
    h = h_ref[...]
    n = _rms_rows(h) * ng_ref[...]
    n = (n * (1.0 + sc_ref[...]) + sh_ref[...]).astype(BF16)
    hg = jnp.dot(n, wg_ref[...], preferred_element_type=F32)
    hu = jnp.dot(n, wu_ref[...], preferred_element_type=F32)
    a = (hg / (1.0 + jnp.exp(-hg))) * hu
    out = h + (0.5 * gt_ref[...]) * jnp.dot(a.astype(BF16), wd_ref[...], preferred_element_type=F32)
    if final_norm:
        out = _rms_rows(out) * fg_ref[...]
    o_ref[...] = out


def _ffn(h, l, which, norm_g4, mod4, wg, wu, wd, fg, *, final_norm):
    s, d = h.shape
    m0 = 6 * which

    def resident(a):
        return pl.BlockSpec((None, None) + a.shape[2:], lambda i: (l, which, 0, 0),
                            pipeline_mode=pl.Buffered(1))

    return pl.pallas_call(
        functools.partial(_ffn_kernel, final_norm=final_norm),
        grid=(s // FFN_ROWS,),
        in_specs=[
            pl.BlockSpec((FFN_ROWS, d), lambda i: (i, 0)),
            _layer_vec(l, 2 * which, d),
            _layer_vec(l, m0, d), _layer_vec(l, m0 + 1, d), _layer_vec(l, m0 + 2, d),
            resident(wg), resident(wu), resident(wd),
            pl.BlockSpec((1, d), lambda i: (0, 0)),
        ],
        out_specs=pl.BlockSpec((FFN_ROWS, d), lambda i: (i, 0)),
        out_shape=jax.ShapeDtypeStruct((s, d), F32),
        compiler_params=_params("parallel"),
        name="ffn_final" if final_norm else "ffn",
    )(h, norm_g4, mod4, mod4, mod4, wg, wu, wd, fg)


def _rope_t(x, cr, sr, cc, sc):
    h = ROPE_HALF
    x1r, x2r, x1c, x2c = x[0:h], x[h:2 * h], x[2 * h:3 * h], x[3 * h:4 * h]
    return jnp.concatenate(
        [x1r * cr - x2r * sr, x2r * cr + x1r * sr, x1c * cc - x2c * sc, x2c * cc + x1c * sc], axis=0)


def _head_norm_t(x, g_col):
    ms = jnp.mean(x * x, axis=0, keepdims=True)
    return x * lax.rsqrt(ms + EPS) * g_col


def _max_sq_norm(x_bf16):
    xf = x_bf16.astype(F32)
    n2 = jnp.sum(xf * xf, axis=0, keepdims=True)
    return jnp.broadcast_to(jnp.max(n2, axis=1, keepdims=True), (1, LANES))


def _f8_split(x):
    hi = x.astype(F8).astype(F32)
    return hi, ((x - hi) * 16.0).astype(F8).astype(F32)


def _f8_query_rows(q_bf16):
    hi, lo16 = _f8_split(q_bf16.astype(F32) * F8_Q_SCALE)
    return jnp.concatenate([hi, hi * 0.0625, lo16 * 0.0625, lo16 * 0.00390625], axis=0).astype(F8)


def _f8_key_rows(k_bf16):
    hi, lo16 = _f8_split(k_bf16.astype(F32) * F8_K_SCALE)
    return jnp.concatenate([hi, lo16, hi, lo16], axis=0)


def _with_ones(v_t):
    return jnp.concatenate([v_t, jnp.ones((ONES_ROWS, v_t.shape[1]), F32)], axis=0).astype(BF16)


def _mix_in_kernel(h_ref, ng_ref, sh_ref, sc_ref, wt_ref, angr_ref, angc_ref, qkg_ref,
                   qta_ref, ka_ref, vta_ref, qtd_ref, kd_ref, vtd_ref, qn_ref, kn_ref,
                   qfa_ref, kfa_ref, qfd_ref, kfd_ref):
    n = _rms_rows(h_ref[...]) * ng_ref[...]
    n = (n * (1.0 + sc_ref[...]) + sh_ref[...]).astype(BF16)
    pt = lax.dot_general(wt_ref[...], n, (((1,), (1,)), ((), ())), preferred_element_type=F32)
    t = pt.shape[1]

    cr, sr = jnp.cos(angr_ref[...]), jnp.sin(angr_ref[...])
    cc, sc = jnp.cos(angc_ref[...]), jnp.sin(angc_ref[...])
    gq = qkg_ref[:, 0:1]
    gk = qkg_ref[:, 1:2]
    zeros = jnp.zeros((HEAD_DIM, t), BF16)
    qscale = QK_SCALE * LOG2E

    qn, kn = [], []
    for hd in range(A_Q_HEADS):
        q = _rope_t(_head_norm_t(pt[hd * HEAD_DIM:(hd + 1) * HEAD_DIM], gq), cr, sr, cc, sc) * qscale
        q = q.astype(BF16)
        qta_ref[hd] = jnp.concatenate([q, zeros] if hd // A_GROUP == 0 else [zeros, q], axis=0)
        qfa_ref[hd] = _f8_query_rows(q)
        qn.append(_max_sq_norm(q))

    c1 = A_Q_COLS
    c2 = c1 + A_KV_COLS
    c3 = c2 + A_KV_COLS
    c4 = c3 + B_QK_COLS
    c5 = c4 + B_QK_COLS
    kt = jnp.concatenate(
        [_rope_t(_head_norm_t(pt[c1 + j * HEAD_DIM:c1 + (j + 1) * HEAD_DIM], gk), cr, sr, cc, sc)
         for j in range(A_KV_HEADS)], axis=0)
    kt = kt.astype(BF16)
    ka_ref[...] = kt.T
    kn += [_max_sq_norm(kt[j * HEAD_DIM:(j + 1) * HEAD_DIM]) for j in range(A_KV_HEADS)]
    for j in range(A_KV_HEADS):
        kfa_ref[j] = _f8_key_rows(kt[j * HEAD_DIM:(j + 1) * HEAD_DIM]).astype(BF16).T.astype(F8)
    kn.append(jnp.zeros((NORM_ROWS_A - A_KV_HEADS, LANES), F32))
    for j in range(A_KV_HEADS):
        vta_ref[j] = _with_ones(pt[c2 + j * HEAD_DIM:c2 + (j + 1) * HEAD_DIM])

    for cmb in range(2 * B_HEADS):
        q = (pt[c3 + cmb * HEAD_DIM:c3 + (cmb + 1) * HEAD_DIM] * qscale).astype(BF16)
        qtd_ref[cmb] = jnp.concatenate([q, zeros] if cmb % 2 == 0 else [zeros, q], axis=0)
        qfd_ref[cmb] = _f8_query_rows(q)
        qn.append(_max_sq_norm(q))
        k = pt[c4 + cmb * HEAD_DIM:c4 + (cmb + 1) * HEAD_DIM].astype(BF16)
        kfd_ref[cmb] = _f8_key_rows(k).astype(BF16).T.astype(F8)
        kn.append(_max_sq_norm(k))
    qn_ref[...] = jnp.concatenate(qn, axis=0)
    kn_ref[...] = jnp.concatenate(kn, axis=0)
    kd_ref[...] = pt[c4:c5].T.astype(BF16)
    for hd in range(B_HEADS):
        vtd_ref[hd] = _with_ones(pt[c5 + hd * B_V_DIM:c5 + (hd + 1) * B_V_DIM])


def _mix_in(h, l, norm_g4, mod4, w_in_t, angr_t, angc_t, qkg_t):
    s, d = h.shape
    tm = MIX_ROWS
    nt = s // tm
    n_sets = 2 * B_HEADS
    assert A_Q_HEADS == NORM_ROWS_A
    out_shape = (
        jax.ShapeDtypeStruct((A_Q_HEADS, 2 * HEAD_DIM, s), BF16),
        jax.ShapeDtypeStruct((s, A_KV_COLS), BF16),
        jax.ShapeDtypeStruct((A_KV_HEADS, nt, HEAD_DIM + ONES_ROWS, tm), BF16),
        jax.ShapeDtypeStruct((n_sets, 2 * HEAD_DIM, s), BF16),
        jax.ShapeDtypeStruct((s, B_QK_COLS), BF16),
        jax.ShapeDtypeStruct((B_HEADS, nt, B_V_DIM + ONES_ROWS, tm), BF16),
        jax.ShapeDtypeStruct((nt, NORM_ROWS, LANES), F32),
        jax.ShapeDtypeStruct((nt, NORM_ROWS, LANES), F32),
        jax.ShapeDtypeStruct((A_Q_HEADS, F8_COLS, s), F8),
        jax.ShapeDtypeStruct((A_KV_HEADS, s, F8_COLS), F8),
        jax.ShapeDtypeStruct((n_sets, F8_COLS, s), F8),
        jax.ShapeDtypeStruct((n_sets, s, F8_COLS), F8),
    )
    out_specs = (
        pl.BlockSpec((A_Q_HEADS, 2 * HEAD_DIM, tm), lambda i: (0, 0, i)),
        pl.BlockSpec((tm, A_KV_COLS), lambda i: (i, 0)),
        pl.BlockSpec((A_KV_HEADS, None, HEAD_DIM + ONES_ROWS, tm), lambda i: (0, i, 0, 0)),
        pl.BlockSpec((n_sets, 2 * HEAD_DIM, tm), lambda i: (0, 0, i)),
        pl.BlockSpec((tm, B_QK_COLS), lambda i: (i, 0)),
        pl.BlockSpec((B_HEADS, None, B_V_DIM + ONES_ROWS, tm), lambda i: (0, i, 0, 0)),
        pl.BlockSpec((None, NORM_ROWS, LANES), lambda i: (i, 0, 0)),
        pl.BlockSpec((None, NORM_ROWS, LANES), lambda i: (i, 0, 0)),
        pl.BlockSpec((A_Q_HEADS, F8_COLS, tm), lambda i: (0, 0, i)),
        pl.BlockSpec((A_KV_HEADS, tm, F8_COLS), lambda i: (0, i, 0)),
        pl.BlockSpec((n_sets, F8_COLS, tm), lambda i: (0, 0, i)),
        pl.BlockSpec((n_sets, tm, F8_COLS), lambda i: (0, i, 0)),
    )
    return pl.pallas_call(
        _mix_in_kernel,
        grid=(nt,),
        in_specs=[
            pl.BlockSpec((tm, d), lambda i: (i, 0)),
            _layer_vec(l, 1, d), _layer_vec(l, 3, d), _layer_vec(l, 4, d),
            _layer_mat(l, w_in_t, single_buffer=True),
            pl.BlockSpec((ROPE_HALF, tm), lambda i: (0, i)),
            pl.BlockSpec((ROPE_HALF, tm), lambda i: (0, i)),
            _layer_mat(l, qkg_t),
        ],
        out_specs=out_specs,
        out_shape=out_shape,
        compiler_params=_params("parallel"),
        name="mix_in",
    )(h, norm_g4, mod4, mod4, w_in_t, angr_t, angc_t, qkg_t)


def _softmax_tile(s, m_ref, alpha_ref, p_ref):
    tk, n = s.shape
    part = jnp.max(s.reshape(tk // BF16_SUBLANES, BF16_SUBLANES, n), axis=0)
    m_old = m_ref[...]
    m_new = jnp.maximum(m_old, jnp.max(part.astype(F32), axis=0, keepdims=True))
    alpha_ref[...] = jnp.exp2(m_old - m_new)
    m_ref[...] = m_new
    p_ref[...] = jnp.exp2(s - m_new.astype(BF16))


def _pipelined_sweep(prep, qk, softmax, pv, n_chunks, lo, n_pairs, last):
    ctx = prep(lo)
    for c in range(n_chunks):
        qk(lo, 0, c, ctx)

    def body(j, carry):
        a = lo + 2 * j
        prev = jnp.maximum(a - 1, 0)
        nxt = jnp.minimum(a + 2, last)
        ctx = prep(a + 1)
        for c in range(n_chunks):
            softmax(0, c)
            pv(prev, 1, c)
            qk(a + 1, 1, c, ctx)
        ctx = prep(nxt)
        for c in range(n_chunks):
            softmax(1, c)
            pv(a, 0, c)
            qk(nxt, 0, c, ctx)
        return carry

    lax.fori_loop(0, n_pairs, body, 0)
    for c in range(n_chunks):
        pv(last, 1, c)


def _direct_sweep(prep, qk_exp, pv, n_chunks, lo, n_tiles, last):
    def phase(kt_new, slot_new, kt_old):
        ctx = prep(kt_new)
        for c in range(n_chunks):
            qk_exp(kt_new, slot_new, c, ctx)
            pv(kt_old, 1 - slot_new, c)

    ctx = prep(lo)
    for c in range(n_chunks):
        qk_exp(lo, 0, c, ctx)

    def body(j, carry):
        a = lo + 2 * j
        phase(a + 1, 1, a)
        phase(a + 2, 0, a + 1)
        return carry

    n_iter = (n_tiles - 1) // 2 if isinstance(n_tiles, int) else lax.shift_right_logical(n_tiles - 1, 1)
    lax.fori_loop(0, n_iter, body, 0)

    def tail_odd():
        for c in range(n_chunks):
            pv(last, 0, c)
        return 0

    def tail_even():
        phase(last, 1, last - 1)
        for c in range(n_chunks):
            pv(last, 1, c)
        return 0

    if isinstance(n_tiles, int):
        (tail_odd if n_tiles % 2 else tail_even)()
    else:
        lax.cond(jnp.bitwise_and(n_tiles, 1) == 1, tail_odd, tail_even)


def _score_bound(qn_tile, kn_ref, q_rows, k_row_of_q):
    k_max = jnp.max(kn_ref[...], axis=0)
    row = lax.broadcasted_iota(jnp.int32, k_max.shape, 0)
    sel = q_rows(row)
    q2 = jnp.where(sel, qn_tile, 0.0)
    k2 = jnp.where(sel, k_row_of_q(row, k_max), 0.0)
    fp8_ok = jnp.logical_and(jnp.max(q2) * F8_Q_SCALE ** 2 <= F8_SAFE_MAX ** 2,
                             jnp.max(k2) * F8_K_SCALE ** 2 <= F8_SAFE_MAX ** 2)
    return jnp.max(jnp.sqrt(q2 * k2)), fp8_ok


GQA_CHUNKS = 8


def _gqa_kernel(qt_ref, k_ref, vt_ref, qn_ref, kn_ref, qf_ref, kf_ref, o_ref, q_scr, qf_scr, s_buf, p_buf,
                alpha_buf, m_scr, acc_scr):
    g = pl.program_id(0)
    qi = pl.program_id(1)
    nk, tk, _ = k_ref.shape
    tq = qt_ref.shape[2]
    w = A_GROUP * tq // GQA_CHUNKS
    for c in range(GQA_CHUNKS):
        for col in range(c * w, (c + 1) * w, min(w, tq)):
            hh, j, n = col // tq, col % tq, min(w, tq)
            q_scr[c, :, col - c * w:col - c * w + n] = qt_ref[hh, :, j:j + n]
            qf_scr[c, :, col - c * w:col - c * w + n] = qf_ref[hh, :, j:j + n]
    acc_scr[...] = jnp.zeros_like(acc_scr)

    def qk(kt, slot, c, _):
        s_buf[slot, c] = jnp.dot(k_ref[kt], q_scr[c], preferred_element_type=F32).astype(BF16)

    def softmax(slot, c):
        _softmax_tile(s_buf[slot, c], m_scr.at[c], alpha_buf.at[slot, c], p_buf.at[slot, c])

    def pv(kt, slot, c):
        acc_scr[c] = alpha_buf[slot, c] * acc_scr[c] + jnp.dot(
            vt_ref[kt], p_buf[slot, c], preferred_element_type=F32)

    def qk_exp(kt, slot, c, _):
        s = jnp.dot(kf_ref[kt], qf_scr[c], preferred_element_type=F32)
        p_buf[slot, c] = jnp.exp2(s.astype(BF16) * F8_UNSCALE)

    def pv_plain(kt, slot, c):
        acc_scr[c] += jnp.dot(vt_ref[kt], p_buf[slot, c], preferred_element_type=F32)

    def kmax_of_group(row, k_max):
        return jnp.max(jnp.where(row == g, k_max, 0.0), axis=0, keepdims=True)

    bound, fp8_ok = _score_bound(
        qn_ref[lax.div(qi * tq, tk)], kn_ref,
        lambda row: jnp.logical_and(row >= g * A_GROUP, row < (g + 1) * A_GROUP), kmax_of_group)

    def direct():
        _direct_sweep(lambda kt: None, qk_exp, pv_plain, GQA_CHUNKS, 0, nk, nk - 1)
        return 0

    def online():
        m_scr[...] = jnp.full_like(m_scr, NEG_BIG)
        p_buf[1] = jnp.zeros(p_buf.shape[1:], BF16)
        alpha_buf[1] = jnp.ones(alpha_buf.shape[1:], F32)
        _pipelined_sweep(lambda kt: None, qk, softmax, pv, GQA_CHUNKS, 0, nk // 2, nk - 1)
        return 0

    lax.cond(jnp.logical_and(bound <= FAST_LOG2_LIMIT, fp8_ok), direct, online)
    cols = []
    for c in range(GQA_CHUNKS):
        acc = acc_scr[c]
        cols.append(acc[:HEAD_DIM] * (1.0 / acc[HEAD_DIM:HEAD_DIM + 1]))
    ot = jnp.concatenate(cols, axis=1)
    ot = jnp.concatenate([ot[:, hh * tq:(hh + 1) * tq] for hh in range(A_GROUP)], axis=0)
    o_ref[...] = ot.T.astype(BF16)


def _gqa(qta, ka3, vta, qn, kn, qfa, kfa4):
    s = qta.shape[2]
    nk, tk, _ = ka3.shape
    tq = GQA_Q_TILE
    assert nk % 2 == 0 and tk % tq == 0 and tk == MIX_ROWS
    nc = GQA_CHUNKS
    w = A_GROUP * tq // nc
    v_rows = vta.shape[2]
    return pl.pallas_call(
        _gqa_kernel,
        grid=(A_KV_HEADS, s // tq),
        in_specs=[
            pl.BlockSpec((A_GROUP, 2 * HEAD_DIM, tq), lambda g, i: (g, 0, i)),
            pl.BlockSpec((nk, tk, A_KV_COLS), lambda g, i: (0, 0, 0), pipeline_mode=pl.Buffered(1)),
            pl.BlockSpec((None, nk, v_rows, tk), lambda g, i: (g, 0, 0, 0), pipeline_mode=pl.Buffered(1)),
            pl.BlockSpec(qn.shape, lambda g, i: (0, 0, 0)),
            pl.BlockSpec(kn.shape, lambda g, i: (0, 0, 0)),
            pl.BlockSpec((A_GROUP, F8_COLS, tq), lambda g, i: (g, 0, i)),
            pl.BlockSpec((None, nk, tk, F8_COLS), lambda g, i: (g, 0, 0, 0), pipeline_mode=pl.Buffered(1)),
        ],
        out_specs=pl.BlockSpec((tq, A_GROUP * HEAD_DIM), lambda g, i: (i, g)),
        out_shape=jax.ShapeDtypeStruct((s, A_Q_COLS), BF16),
        scratch_shapes=[
            pltpu.VMEM((nc, 2 * HEAD_DIM, w), BF16),
            pltpu.VMEM((nc, F8_COLS, w), F8),
            pltpu.VMEM((2, nc, tk, w), BF16),
            pltpu.VMEM((2, nc, tk, w), BF16),
            pltpu.VMEM((2, nc, 1, w), F32),
            pltpu.VMEM((nc, 1, w), F32),
            pltpu.VMEM((nc, v_rows, w), F32),
        ],
        compiler_params=_params("parallel", "parallel"),
        name="gqa_attn",
    )(qta, ka3, vta, qn, kn, qfa, kfa4)


DIFF_CHUNKS = 2


def _diff_kernel(coef_ref, qt0_ref, qt1_ref, k0_ref, k1_ref, vt_ref, qn_ref, kn_ref, lamp_ref, sg_ref,
                 qf0_ref, qf1_ref, kf0_ref, kf1_ref,
                 o_ref, rel_scr, s_buf, p_buf, alpha_buf, m_scr, acc_scr, *, lam_init):
    hd = pl.program_id(0)
    qi = pl.program_id(1)
    nk, tk, _ = k0_ref.shape
    tq = qt0_ref.shape[1]
    slope_f8 = coef_ref[0, hd]
    inv_tile_drop = coef_ref[1, hd]
    acc_scr[...] = jnp.zeros_like(acc_scr)

    @pl.when(qi == 0)
    def _():
        rel_scr[...] = (lax.broadcasted_iota(jnp.int32, (tk, tq), 1)
                        - lax.broadcasted_iota(jnp.int32, (tk, tq), 0)).astype(F32) * slope_f8

    bound, fp8_ok = _score_bound(
        qn_ref[qi], kn_ref,
        lambda row: jnp.logical_or(row == NORM_ROWS_A + hd, row == NORM_ROWS_A + B_HEADS + hd),
        lambda row, k_max: k_max)
    use_direct = jnp.logical_and(bound <= FAST_LOG2_LIMIT, fp8_ok)
    drop = jnp.where(use_direct, bound, 2.0 * bound) + SKIP_LOG2
    reach = jnp.minimum(jnp.floor(drop * inv_tile_drop) + 1.0, float(nk)).astype(jnp.int32)
    lo = jnp.maximum(qi - reach, 0)
    hi = jnp.minimum(qi + reach, nk - 1)

    q0 = qi * tq
    w = tq // DIFF_CHUNKS
    n_chunks = 2 * DIFF_CHUNKS
    cols = [slice((c % DIFF_CHUNKS) * w, (c % DIFF_CHUNKS + 1) * w) for c in range(n_chunks)]
    qts = (qt0_ref, qt1_ref)
    ks = (k0_ref, k1_ref)
    qfs = (qf0_ref, qf1_ref)
    kfs = (kf0_ref, kf1_ref)

    def penalty_scaled(kt):
        return jnp.abs(rel_scr[...] + (q0 - kt * tk).astype(F32) * slope_f8)

    def penalty(kt):
        return penalty_scaled(kt) * F8_UNSCALE

    def qk(kt, slot, c, pen):
        mp = c // DIFF_CHUNKS
        s = jnp.dot(ks[mp][kt], qts[mp][:, cols[c]], preferred_element_type=F32) - pen[:, cols[c]]
        s_buf[slot, c] = s.astype(BF16)

    def softmax(slot, c):
        _softmax_tile(s_buf[slot, c], m_scr.at[c], alpha_buf.at[slot, c], p_buf.at[slot, c])

    def pv(kt, slot, c):
        acc_scr[c] = alpha_buf[slot, c] * acc_scr[c] + jnp.dot(
            vt_ref[kt], p_buf[slot, c], preferred_element_type=F32)

    def qk_exp(kt, slot, c, pen_scaled):
        mp = c // DIFF_CHUNKS
        s = jnp.dot(kfs[mp][kt], qfs[mp][:, cols[c]], preferred_element_type=F32) - pen_scaled[:, cols[c]]
        p_buf[slot, c] = jnp.exp2(s.astype(BF16) * F8_UNSCALE)

    def pv_plain(kt, slot, c):
        acc_scr[c] += jnp.dot(vt_ref[kt], p_buf[slot, c], preferred_element_type=F32)

    def direct():
        _direct_sweep(penalty_scaled, qk_exp, pv_plain, n_chunks, lo, hi - lo + 1, hi)
        return 0

    def online():
        m_scr[...] = jnp.full_like(m_scr, NEG_BIG)
        p_buf[1] = jnp.zeros(p_buf.shape[1:], BF16)
        alpha_buf[1] = jnp.ones(alpha_buf.shape[1:], F32)
        odd = jnp.bitwise_and(hi - lo + 1, 1)
        grow_hi = jnp.where(hi < nk - 1, odd, 0)
        hi2 = hi + grow_hi
        lo2 = lo - (odd - grow_hi)
        _pipelined_sweep(penalty, qk, softmax, pv, n_chunks, lo2, lax.shift_right_logical(hi2 - lo2 + 1, 1), hi2)
        return 0

    lax.cond(use_direct, direct, online)

    lp = lamp_ref[...]
    lam = (jnp.exp(jnp.sum(lp[0:1] * lp[1:2], axis=1, keepdims=True))
           - jnp.exp(jnp.sum(lp[2:3] * lp[3:4], axis=1, keepdims=True)) + lam_init)
    a0, a1 = [jnp.concatenate([acc_scr[mp * DIFF_CHUNKS + j] for j in range(DIFF_CHUNKS)], axis=1)
              for mp in range(2)]
    ot = (a0[:B_V_DIM] * (1.0 / a0[B_V_DIM:B_V_DIM + 1])
          - lam * (a1[:B_V_DIM] * (1.0 / a1[B_V_DIM:B_V_DIM + 1])))
    ot = ot * lax.rsqrt(jnp.mean(ot * ot, axis=0, keepdims=True) + EPS)
    ot = ot * sg_ref[...] * (1.0 - lam_init)
    o_ref[...] = ot.T.astype(BF16)


def _diff(l, coef, qtd, kd3, vtd, qn, kn, qfd, kfd4, lam_p, subln_col, *, lam_init):
    s = qtd.shape[2]
    nk, tk, _ = kd3.shape
    tq = DIFF_Q_TILE
    assert nk % 2 == 0 and tq == tk == MIX_ROWS
    lanes = 2 * HEAD_DIM
    v_rows = vtd.shape[2]
    nc = 2 * DIFF_CHUNKS
    w = tq // DIFF_CHUNKS
    return pl.pallas_call(
        functools.partial(_diff_kernel, lam_init=lam_init),
        grid=(B_HEADS, s // tq),
        in_specs=[
            pl.BlockSpec(memory_space=pltpu.SMEM),
            pl.BlockSpec((None, lanes, tq), lambda h, i: (h, 0, i)),
            pl.BlockSpec((None, lanes, tq), lambda h, i: (B_HEADS + h, 0, i)),
            pl.BlockSpec((nk, tk, lanes), lambda h, i: (0, 0, h // 2), pipeline_mode=pl.Buffered(1)),
            pl.BlockSpec((nk, tk, lanes), lambda h, i: (0, 0, B_HEADS // 2 + h // 2),
                         pipeline_mode=pl.Buffered(1)),
            pl.BlockSpec((None, nk, v_rows, tk), lambda h, i: (h, 0, 0, 0), pipeline_mode=pl.Buffered(1)),
            pl.BlockSpec(qn.shape, lambda h, i: (0, 0, 0)),
            pl.BlockSpec(kn.shape, lambda h, i: (0, 0, 0)),
            _layer_mat(l, lam_p),
            _layer_mat(l, subln_col),
            pl.BlockSpec((None, F8_COLS, tq), lambda h, i: (h, 0, i)),
            pl.BlockSpec((None, F8_COLS, tq), lambda h, i: (B_HEADS + h, 0, i)),
            pl.BlockSpec((None, nk, tk, F8_COLS), lambda h, i: (h, 0, 0, 0), pipeline_mode=pl.Buffered(1)),
            pl.BlockSpec((None, nk, tk, F8_COLS), lambda h, i: (B_HEADS + h, 0, 0, 0),
                         pipeline_mode=pl.Buffered(1)),
        ],
        out_specs=pl.BlockSpec((tq, B_V_DIM), lambda h, i: (i, h)),
        out_shape=jax.ShapeDtypeStruct((s, B_V_COLS), BF16),
        scratch_shapes=[
            pltpu.VMEM((tk, tq), F32),
            pltpu.VMEM((2, nc, tk, w), BF16),
            pltpu.VMEM((2, nc, tk, w), BF16),
            pltpu.VMEM((2, nc, 1, w), F32),
            pltpu.VMEM((nc, 1, w), F32),
            pltpu.VMEM((nc, v_rows, w), F32),
        ],
        compiler_params=_params("arbitrary", "arbitrary"),
        name="diff_attn",
    )(coef, qtd, qtd, kd3, kd3, vtd, qn, kn, lam_p, subln_col, qfd, qfd, kfd4, kfd4)


def _mix_out_kernel(h_ref, ng_ref, sh_ref, sc_ref, gt_ref, oa_ref, od_ref, wba_ref, wbb_ref,
                    wgate_ref, bgate_ref, wo_ref, o_ref):
    h = h_ref[...]
    d = h.shape[1]
    n = _rms_rows(h) * ng_ref[...]
    n = (n * (1.0 + sc_ref[...]) + sh_ref[...]).astype(BF16)
    z = jnp.dot(n, wgate_ref[...], preferred_element_type=F32) + bgate_ref[...]
    g = 1.0 / (1.0 + jnp.exp(-z))
    ya = jnp.dot(oa_ref[...], wba_ref[...], preferred_element_type=F32)
    yb = jnp.dot(od_ref[...], wbb_ref[...], preferred_element_type=F32)
    mix = (g[:, :d] * ya + g[:, d:] * yb).astype(BF16)
    y = jnp.dot(mix, wo_ref[...], preferred_element_type=F32)
    o_ref[...] = h + gt_ref[...] * y


def _mix_out(h, l, norm_g4, mod4, oa, od, w_ba, w_bb, w_gate, b_gate, w_o):
    s, d = h.shape
    tm = MIX_ROWS
    return pl.pallas_call(
        _mix_out_kernel,
        grid=(s // tm,),
        in_specs=[
            pl.BlockSpec((tm, d), lambda i: (i, 0)),
            _layer_vec(l, 1, d), _layer_vec(l, 3, d), _layer_vec(l, 4, d), _layer_vec(l, 5, d),
            pl.BlockSpec((tm, oa.shape[1]), lambda i: (i, 0)),
            pl.BlockSpec((tm, od.shape[1]), lambda i: (i, 0)),
            _layer_mat(l, w_ba, single_buffer=True), _layer_mat(l, w_bb, single_buffer=True),
            _layer_mat(l, w_gate, single_buffer=True), _layer_mat(l, b_gate),
            _layer_mat(l, w_o, single_buffer=True),
        ],
        out_specs=pl.BlockSpec((tm, d), lambda i: (i, 0)),
        out_shape=jax.ShapeDtypeStruct((s, d), F32),
        compiler_params=_params("parallel"),
        name="mix_out",
    )(h, norm_g4, mod4, mod4, mod4, oa, od, w_ba, w_bb, w_gate, b_gate, w_o)


def _axial_angles_t(seq):
    rows = seq // GRID_W
    row = jnp.broadcast_to(jnp.arange(rows)[:, None], (rows, GRID_W)).reshape(seq) - rows // 2
    col = jnp.broadcast_to(jnp.arange(GRID_W)[None, :], (rows, GRID_W)).reshape(seq) - GRID_W // 2
    inv = 1.0 / (ROPE_THETA ** (jnp.arange(0, ROPE_AXIS_DIM, 2, dtype=F32) / ROPE_AXIS_DIM))
    return inv[:, None] * row.astype(F32)[None, :], inv[:, None] * col.astype(F32)[None, :]


def kernel(x, c, ada_w, ada_b, norm_g, ffn_wg, ffn_wu, ffn_wd, w_in, qk_g, lam_p, subln_g, w_ba, w_bb,
           w_gate, b_gate, w_o, final_g):
    batch, s, d = x.shape
    assert batch == 1 and s % KEY_TILE == 0 and MIX_ROWS == KEY_TILE
    depth = ada_w.shape[0]
    h = x.reshape(s, d)

    mod4 = _ada_mod(c.reshape(d, 1), ada_w, ada_b).reshape(depth, N_ADA, 1, d)
    norm_g4 = norm_g.reshape(depth, 3, 1, d)
    angr_t, angc_t = _axial_angles_t(s)
    slopes = 2.0 ** (-8.0 * jnp.arange(1, B_HEADS + 1, dtype=F32) / B_HEADS)
    coef = jnp.stack([slopes * (LOG2E / F8_UNSCALE), 1.0 / (slopes * LOG2E * KEY_TILE)])
    fg = final_g.reshape(1, d)
    wg, wu, wd = ffn_wg.astype(BF16), ffn_wu.astype(BF16), ffn_wd.astype(BF16)
    w_in_t = jnp.swapaxes(w_in, 1, 2).astype(BF16)
    qkg_t = jnp.swapaxes(qk_g, 1, 2)
    w_ba16, w_bb16, w_gate16, w_o16 = (w.astype(BF16) for w in (w_ba, w_bb, w_gate, w_o))
    b_gate3 = b_gate.reshape(depth, 1, -1)
    subln_col = subln_g.reshape(depth, B_V_DIM, 1)
    nk = s // KEY_TILE

    for l in range(depth):
        lam_init = 0.8 - 0.6 * math.exp(-0.3 * l)
        h = _ffn(h, l, 0, norm_g4, mod4, wg, wu, wd, fg, final_norm=False)
        qta, ka, vta, qtd, kd, vtd, qn, kn, qfa, kfa, qfd, kfd = _mix_in(
            h, l, norm_g4, mod4, w_in_t, angr_t, angc_t, qkg_t)
        oa = _gqa(qta, ka.reshape(nk, KEY_TILE, A_KV_COLS), vta, qn, kn,
                  qfa, kfa.reshape(A_KV_HEADS, nk, KEY_TILE, F8_COLS))
        od = _diff(l, coef, qtd, kd.reshape(nk, KEY_TILE, B_QK_COLS), vtd, qn, kn,
                   qfd, kfd.reshape(2 * B_HEADS, nk, KEY_TILE, F8_COLS), lam_p, subln_col, lam_init=lam_init)
        h = _mix_out(h, l, norm_g4, mod4, oa, od, w_ba16, w_bb16, w_gate16, b_gate3, w_o16)
        h = _ffn(h, l, 1, norm_g4, mod4, wg, wu, wd, fg, final_norm=(l == depth - 1))
    return h.reshape(batch, s, d)
```

```python
import functools
import math

import jax
import jax.numpy as jnp
from jax import lax
from jax.experimental import pallas as pl
from jax.experimental.pallas import tpu as pltpu

F32 = jnp.float32
BF16 = jnp.bfloat16
F8 = jnp.float8_e4m3fn

GRID_W = 64
HEAD_DIM = 64
A_Q_HEADS = 8
A_KV_HEADS = 2
A_GROUP = A_Q_HEADS // A_KV_HEADS
B_HEADS = 4
B_V_DIM = 2 * HEAD_DIM
A_Q_COLS = A_Q_HEADS * HEAD_DIM
A_KV_COLS = A_KV_HEADS * HEAD_DIM
B_QK_COLS = 2 * B_HEADS * HEAD_DIM
B_V_COLS = B_HEADS * B_V_DIM
N_ADA = 9
EPS = 1e-6
ROPE_THETA = 10000.0
ROPE_AXIS_DIM = HEAD_DIM // 2
ROPE_HALF = ROPE_AXIS_DIM // 2

LOG2E = math.log2(math.e)
QK_SCALE = HEAD_DIM ** -0.5
NEG_BIG = -1e30

BF16_SUBLANES = 16
LANES = 128
ONES_ROWS = BF16_SUBLANES
SKIP_LOG2 = 64.0
FAST_LOG2_LIMIT = 60.0
F8_Q_SCALE = 2.0 ** 6
F8_K_SCALE = 2.0 ** 2
F8_UNSCALE = 1.0 / (F8_Q_SCALE * F8_K_SCALE)
F8_SAFE_MAX = 256.0
F8_COLS = 4 * HEAD_DIM
NORM_ROWS_A = 8
NORM_ROWS = NORM_ROWS_A + 2 * B_HEADS

VMEM_LIMIT_BYTES = 56 * 1024 * 1024

FFN_ROWS = 512
MIX_ROWS = 512
KEY_TILE = 512
GQA_Q_TILE = 512
DIFF_Q_TILE = 512
ADA_COLS = 1152


def _params(*sem):
    return pltpu.CompilerParams(dimension_semantics=sem, vmem_limit_bytes=VMEM_LIMIT_BYTES)


def _layer_vec(l, j, d):
    return pl.BlockSpec((None, None, 1, d), lambda *_: (l, j, 0, 0))


def _layer_mat(l, a, *, single_buffer=False):
    nd = a.ndim - 1
    mode = dict(pipeline_mode=pl.Buffered(1)) if single_buffer else {}
    return pl.BlockSpec((None,) + a.shape[1:], lambda *_: (l,) + (0,) * nd, **mode)


def _rms_rows(x):
    return x * lax.rsqrt(jnp.mean(x * x, axis=-1, keepdims=True) + EPS)


def _ada_kernel(c_ref, w_ref, b_ref, o_ref):
    c = c_ref[...]
    act = c / (1.0 + jnp.exp(-c))
    o_ref[...] = jnp.sum(w_ref[...] * act, axis=0, keepdims=True) + b_ref[...]


def _ada_mod(c_col, ada_w, ada_b):
    n_layers, d, n = ada_w.shape
    return pl.pallas_call(
        _ada_kernel,
        grid=(n_layers, n // ADA_COLS),
        in_specs=[
            pl.BlockSpec((d, 1), lambda l, j: (0, 0)),
            pl.BlockSpec((None, d, ADA_COLS), lambda l, j: (l, 0, j)),
            pl.BlockSpec((None, 1, ADA_COLS), lambda l, j: (l, 0, j)),
        ],
        out_specs=pl.BlockSpec((None, 1, ADA_COLS), lambda l, j: (l, 0, j)),
        out_shape=jax.ShapeDtypeStruct((n_layers, 1, n), F32),
        compiler_params=_params("parallel", "parallel"),
        name="ada_mod",
    )(c_col, ada_w, ada_b.reshape(n_layers, 1, n))


def _ffn_kernel(h_ref, ng_ref, sh_ref, sc_ref, gt_ref, wg_ref, wu_ref, wd_ref, fg_ref, o_ref, *, final_norm):
    h = h_ref[...]
    n = _rms_rows(h) * ng_ref[...]
    n = (n * (1.0 + sc_ref[...]) + sh_ref[...]).astype(BF16)
    hg = jnp.dot(n, wg_ref[...], preferred_element_type=F32)
    hu = jnp.dot(n, wu_ref[...], preferred_element_type=F32)
    a = (hg / (1.0 + jnp.exp(-hg))) * hu
    out = h + (0.5 * gt_ref[...]) * jnp.dot(a.astype(BF16), wd_ref[...], preferred_element_type=F32)
    if final_norm:
        out = _rms_rows(out) * fg_ref[...]
    o_ref[...] = out


def _ffn(h, l, which, norm_g4, mod4, wg, wu, wd, fg, *, final_norm):
    s, d = h.shape
    m0 = 6 * which

    def resident(a):
        return pl.BlockSpec((None, None) + a.shape[2:], lambda i: (l, which, 0, 0),
                            pipeline_mode=pl.Buffered(1))

    return pl.pallas_call(
        functools.partial(_ffn_kernel, final_norm=final_norm),
        grid=(s // FFN_ROWS,),
        in_specs=[
            pl.BlockSpec((FFN_ROWS, d), lambda i: (i, 0)),
            _layer_vec(l, 2 * which, d),
            _layer_vec(l, m0, d), _layer_vec(l, m0 + 1, d), _layer_vec(l, m0 + 2, d),
            resident(wg), resident(wu), resident(wd),
            pl.BlockSpec((1, d), lambda i: (0, 0)),
        ],
        out_specs=pl.BlockSpec((FFN_ROWS, d), lambda i: (i, 0)),
        out_shape=jax.ShapeDtypeStruct((s, d), F32),
        compiler_params=_params("parallel"),
        name="ffn_final" if final_norm else "ffn",
    )(h, norm_g4, mod4, mod4, mod4, wg, wu, wd, fg)


def _rope_t(x, cr, sr, cc, sc):
    h = ROPE_HALF
    x1r, x2r, x1c, x2c = x[0:h], x[h:2 * h], x[2 * h:3 * h], x[3 * h:4 * h]
    return jnp.concatenate(
        [x1r * cr - x2r * sr, x2r * cr + x1r * sr, x1c * cc - x2c * sc, x2c * cc + x1c * sc], axis=0)


def _head_norm_t(x, g_col):
    ms = jnp.mean(x * x, axis=0, keepdims=True)
    return x * lax.rsqrt(ms + EPS) * g_col


def _max_sq_norm(x_bf16):
    xf = x_bf16.astype(F32)
    n2 = jnp.sum(xf * xf, axis=0, keepdims=True)
    return jnp.broadcast_to(jnp.max(n2, axis=1, keepdims=True), (1, LANES))


def _f8_split(x):
    hi = x.astype(F8).astype(F32)
    return hi, ((x - hi) * 16.0).astype(F8).astype(F32)


def _f8_query_rows(q_bf16):
    hi, lo16 = _f8_split(q_bf16.astype(F32) * F8_Q_SCALE)
    return jnp.concatenate([hi, hi * 0.0625, lo16 * 0.0625, lo16 * 0.00390625], axis=0).astype(F8)


def _f8_key_rows(k_bf16):
    hi, lo16 = _f8_split(k_bf16.astype(F32) * F8_K_SCALE)
    return jnp.concatenate([hi, lo16, hi, lo16], axis=0)


def _with_ones(v_t):
    return jnp.concatenate([v_t, jnp.ones((ONES_ROWS, v_t.shape[1]), F32)], axis=0).astype(BF16)


def _mix_in_kernel(h_ref, ng_ref, sh_ref, sc_ref, wt_ref, angr_ref, angc_ref, qkg_ref,
                   qta_ref, ka_ref, vta_ref, qtd_ref, kd_ref, vtd_ref, qn_ref, kn_ref,
                   qfa_ref, kfa_ref, qfd_ref, kfd_ref):
    n = _rms_rows(h_ref[...]) * ng_ref[...]
    n = (n * (1.0 + sc_ref[...]) + sh_ref[...]).astype(BF16)
    pt = lax.dot_general(wt_ref[...], n, (((1,), (1,)), ((), ())), preferred_element_type=F32)
    t = pt.shape[1]

    cr, sr = jnp.cos(angr_ref[...]), jnp.sin(angr_ref[...])
    cc, sc = jnp.cos(angc_ref[...]), jnp.sin(angc_ref[...])
    gq = qkg_ref[:, 0:1]
    gk = qkg_ref[:, 1:2]
    zeros = jnp.zeros((HEAD_DIM, t), BF16)
    qscale = QK_SCALE * LOG2E

    qn, kn = [], []
    for hd in range(A_Q_HEADS):
        q = _rope_t(_head_norm_t(pt[hd * HEAD_DIM:(hd + 1) * HEAD_DIM], gq), cr, sr, cc, sc) * qscale
        q = q.astype(BF16)
        qta_ref[hd] = jnp.concatenate([q, zeros] if hd // A_GROUP == 0 else [zeros, q], axis=0)
        qfa_ref[hd] = _f8_query_rows(q)
        qn.append(_max_sq_norm(q))

    c1 = A_Q_COLS
    c2 = c1 + A_KV_COLS
    c3 = c2 + A_KV_COLS
    c4 = c3 + B_QK_COLS
    c5 = c4 + B_QK_COLS
    kt = jnp.concatenate(
        [_rope_t(_head_norm_t(pt[c1 + j * HEAD_DIM:c1 + (j + 1) * HEAD_DIM], gk), cr, sr, cc, sc)
         for j in range(A_KV_HEADS)], axis=0)
    kt = kt.astype(BF16)
    ka_ref[...] = kt.T
    kn += [_max_sq_norm(kt[j * HEAD_DIM:(j + 1) * HEAD_DIM]) for j in range(A_KV_HEADS)]
    for j in range(A_KV_HEADS):
        kfa_ref[j] = _f8_key_rows(kt[j * HEAD_DIM:(j + 1) * HEAD_DIM]).astype(BF16).T.astype(F8)
    kn.append(jnp.zeros((NORM_ROWS_A - A_KV_HEADS, LANES), F32))
    for j in range(A_KV_HEADS):
        vta_ref[j] = _with_ones(pt[c2 + j * HEAD_DIM:c2 + (j + 1) * HEAD_DIM])

    for cmb in range(2 * B_HEADS):
        q = (pt[c3 + cmb * HEAD_DIM:c3 + (cmb + 1) * HEAD_DIM] * qscale).astype(BF16)
        qtd_ref[cmb] = jnp.concatenate([q, zeros] if cmb % 2 == 0 else [zeros, q], axis=0)
        qfd_ref[cmb] = _f8_query_rows(q)
        qn.append(_max_sq_norm(q))
        k = pt[c4 + cmb * HEAD_DIM:c4 + (cmb + 1) * HEAD_DIM].astype(BF16)
        kfd_ref[cmb] = _f8_key_rows(k).astype(BF16).T.astype(F8)
        kn.append(_max_sq_norm(k))
    qn_ref[...] = jnp.concatenate(qn, axis=0)
    kn_ref[...] = jnp.concatenate(kn, axis=0)
    kd_ref[...] = pt[c4:c5].T.astype(BF16)
    for hd in range(B_HEADS):
        vtd_ref[hd] = _with_ones(pt[c5 + hd * B_V_DIM:c5 + (hd + 1) * B_V_DIM])


def _mix_in(h, l, norm_g4, mod4, w_in_t, angr_t, angc_t, qkg_t):
    s, d = h.shape
    tm = MIX_ROWS
    nt = s // tm
    n_sets = 2 * B_HEADS
    assert A_Q_HEADS == NORM_ROWS_A
    out_shape = (
        jax.ShapeDtypeStruct((A_Q_HEADS, 2 * HEAD_DIM, s), BF16),
        jax.ShapeDtypeStruct((s, A_KV_COLS), BF16),
        jax.ShapeDtypeStruct((A_KV_HEADS, nt, HEAD_DIM + ONES_ROWS, tm), BF16),
        jax.ShapeDtypeStruct((n_sets, 2 * HEAD_DIM, s), BF16),
        jax.ShapeDtypeStruct((s, B_QK_COLS), BF16),
        jax.ShapeDtypeStruct((B_HEADS, nt, B_V_DIM + ONES_ROWS, tm), BF16),
        jax.ShapeDtypeStruct((nt, NORM_ROWS, LANES), F32),
        jax.ShapeDtypeStruct((nt, NORM_ROWS, LANES), F32),
        jax.ShapeDtypeStruct((A_Q_HEADS, F8_COLS, s), F8),
        jax.ShapeDtypeStruct((A_KV_HEADS, s, F8_COLS), F8),
        jax.ShapeDtypeStruct((n_sets, F8_COLS, s), F8),
        jax.ShapeDtypeStruct((n_sets, s, F8_COLS), F8),
    )
    out_specs = (
        pl.BlockSpec((A_Q_HEADS, 2 * HEAD_DIM, tm), lambda i: (0, 0, i)),
        pl.BlockSpec((tm, A_KV_COLS), lambda i: (i, 0)),
        pl.BlockSpec((A_KV_HEADS, None, HEAD_DIM + ONES_ROWS, tm), lambda i: (0, i, 0, 0)),
        pl.BlockSpec((n_sets, 2 * HEAD_DIM, tm), lambda i: (0, 0, i)),
        pl.BlockSpec((tm, B_QK_COLS), lambda i: (i, 0)),
        pl.BlockSpec((B_HEADS, None, B_V_DIM + ONES_ROWS, tm), lambda i: (0, i, 0, 0)),
        pl.BlockSpec((None, NORM_ROWS, LANES), lambda i: (i, 0, 0)),
        pl.BlockSpec((None, NORM_ROWS, LANES), lambda i: (i, 0, 0)),
        pl.BlockSpec((A_Q_HEADS, F8_COLS, tm), lambda i: (0, 0, i)),
        pl.BlockSpec((A_KV_HEADS, tm, F8_COLS), lambda i: (0, i, 0)),
        pl.BlockSpec((n_sets, F8_COLS, tm), lambda i: (0, 0, i)),
        pl.BlockSpec((n_sets, tm, F8_COLS), lambda i: (0, i, 0)),
    )
    return pl.pallas_call(
        _mix_in_kernel,
        grid=(nt,),
        in_specs=[
            pl.BlockSpec((tm, d), lambda i: (i, 0)),
            _layer_vec(l, 1, d), _layer_vec(l, 3, d), _layer_vec(l, 4, d),
            _layer_mat(l, w_in_t, single_buffer=True),
            pl.BlockSpec((ROPE_HALF, tm), lambda i: (0, i)),
            pl.BlockSpec((ROPE_HALF, tm), lambda i: (0, i)),
            _layer_mat(l, qkg_t),
        ],
        out_specs=out_specs,
        out_shape=out_shape,
        compiler_params=_params("parallel"),
        name="mix_in",
    )(h, norm_g4, mod4, mod4, w_in_t, angr_t, angc_t, qkg_t)


def _softmax_tile(s, m_ref, alpha_ref, p_ref):
    tk, n = s.shape
    part = jnp.max(s.reshape(tk // BF16_SUBLANES, BF16_SUBLANES, n), axis=0)
    m_old = m_ref[...]
    m_new = jnp.maximum(m_old, jnp.max(part.astype(F32), axis=0, keepdims=True))
    alpha_ref[...] = jnp.exp2(m_old - m_new)
    m_ref[...] = m_new
    p_ref[...] = jnp.exp2(s - m_new.astype(BF16))


def _pipelined_sweep(prep, qk, softmax, pv, n_chunks, lo, n_pairs, last):
    ctx = prep(lo)
    for c in range(n_chunks):
        qk(lo, 0, c, ctx)

    def body(j, carry):
        a = lo + 2 * j
        prev = jnp.maximum(a - 1, 0)
        nxt = jnp.minimum(a + 2, last)
        ctx = prep(a + 1)
        for c in range(n_chunks):
            softmax(0, c)
            pv(prev, 1, c)
            qk(a + 1, 1, c, ctx)
        ctx = prep(nxt)
        for c in range(n_chunks):
            softmax(1, c)
            pv(a, 0, c)
            qk(nxt, 0, c, ctx)
        return carry

    lax.fori_loop(0, n_pairs, body, 0)
    for c in range(n_chunks):
        pv(last, 1, c)


def _direct_sweep(prep, qk_exp, pv, n_chunks, lo, n_tiles, last):
    def phase(kt_new, slot_new, kt_old):
        ctx = prep(kt_new)
        for c in range(n_chunks):
            qk_exp(kt_new, slot_new, c, ctx)
            pv(kt_old, 1 - slot_new, c)

    ctx = prep(lo)
    for c in range(n_chunks):
        qk_exp(lo, 0, c, ctx)

    def body(j, carry):
        a = lo + 2 * j
        phase(a + 1, 1, a)
        phase(a + 2, 0, a + 1)
        return carry

    n_iter = (n_tiles - 1) // 2 if isinstance(n_tiles, int) else lax.shift_right_logical(n_tiles - 1, 1)
    lax.fori_loop(0, n_iter, body, 0)

    def tail_odd():
        for c in range(n_chunks):
            pv(last, 0, c)
        return 0

    def tail_even():
        phase(last, 1, last - 1)
        for c in range(n_chunks):
            pv(last, 1, c)
        return 0

    if isinstance(n_tiles, int):
        (tail_odd if n_tiles % 2 else tail_even)()
    else:
        lax.cond(jnp.bitwise_and(n_tiles, 1) == 1, tail_odd, tail_even)


def _score_bound(qn_tile, kn_ref, q_rows, k_row_of_q):
    k_max = jnp.max(kn_ref[...], axis=0)
    row = lax.broadcasted_iota(jnp.int32, k_max.shape, 0)
    sel = q_rows(row)
    q2 = jnp.where(sel, qn_tile, 0.0)
    k2 = jnp.where(sel, k_row_of_q(row, k_max), 0.0)
    fp8_ok = jnp.logical_and(jnp.max(q2) * F8_Q_SCALE ** 2 <= F8_SAFE_MAX ** 2,
                             jnp.max(k2) * F8_K_SCALE ** 2 <= F8_SAFE_MAX ** 2)
    return jnp.max(jnp.sqrt(q2 * k2)), fp8_ok


GQA_CHUNKS = 8


def _gqa_kernel(qt_ref, k_ref, vt_ref, qn_ref, kn_ref, qf_ref, kf_ref, o_ref, q_scr, qf_scr, s_buf, p_buf,
                alpha_buf, m_scr, acc_scr):
    g = pl.program_id(0)
    qi = pl.program_id(1)
    nk, tk, _ = k_ref.shape
    tq = qt_ref.shape[2]
    w = A_GROUP * tq // GQA_CHUNKS
    for c in range(GQA_CHUNKS):
        for col in range(c * w, (c + 1) * w, min(w, tq)):
            hh, j, n = col // tq, col % tq, min(w, tq)
            q_scr[c, :, col - c * w:col - c * w + n] = qt_ref[hh, :, j:j + n]
            qf_scr[c, :, col - c * w:col - c * w + n] = qf_ref[hh, :, j:j + n]
    acc_scr[...] = jnp.zeros_like(acc_scr)

    def qk(kt, slot, c, _):
        s_buf[slot, c] = jnp.dot(k_ref[kt], q_scr[c], preferred_element_type=F32).astype(BF16)

    def softmax(slot, c):
        _softmax_tile(s_buf[slot, c], m_scr.at[c], alpha_buf.at[slot, c], p_buf.at[slot, c])

    def pv(kt, slot, c):
        acc_scr[c] = alpha_buf[slot, c] * acc_scr[c] + jnp.dot(
            vt_ref[kt], p_buf[slot, c], preferred_element_type=F32)

    def qk_exp(kt, slot, c, _):
        s = jnp.dot(kf_ref[kt], qf_scr[c], preferred_element_type=F32)
        p_buf[slot, c] = jnp.exp2(s.astype(BF16) * F8_UNSCALE)

    def pv_plain(kt, slot, c):
        acc_scr[c] += jnp.dot(vt_ref[kt], p_buf[slot, c], preferred_element_type=F32)

    def kmax_of_group(row, k_max):
        return jnp.max(jnp.where(row == g, k_max, 0.0), axis=0, keepdims=True)

    bound, fp8_ok = _score_bound(
        qn_ref[lax.div(qi * tq, tk)], kn_ref,
        lambda row: jnp.logical_and(row >= g * A_GROUP, row < (g + 1) * A_GROUP), kmax_of_group)

    def direct():
        _direct_sweep(lambda kt: None, qk_exp, pv_plain, GQA_CHUNKS, 0, nk, nk - 1)
        return 0

    def online():
        m_scr[...] = jnp.full_like(m_scr, NEG_BIG)
        p_buf[1] = jnp.zeros(p_buf.shape[1:], BF16)
        alpha_buf[1] = jnp.ones(alpha_buf.shape[1:], F32)
        _pipelined_sweep(lambda kt: None, qk, softmax, pv, GQA_CHUNKS, 0, nk // 2, nk - 1)
        return 0

    lax.cond(jnp.logical_and(bound <= FAST_LOG2_LIMIT, fp8_ok), direct, online)
    cols = []
    for c in range(GQA_CHUNKS):
        acc = acc_scr[c]
        cols.append(acc[:HEAD_DIM] * (1.0 / acc[HEAD_DIM:HEAD_DIM + 1]))
    ot = jnp.concatenate(cols, axis=1)
    ot = jnp.concatenate([ot[:, hh * tq:(hh + 1) * tq] for hh in range(A_GROUP)], axis=0)
    o_ref[...] = ot.T.astype(BF16)


def _gqa(qta, ka3, vta, qn, kn, qfa, kfa4):
    s = qta.shape[2]
    nk, tk, _ = ka3.shape
    tq = GQA_Q_TILE
    assert nk % 2 == 0 and tk % tq == 0 and tk == MIX_ROWS
    nc = GQA_CHUNKS
    w = A_GROUP * tq // nc
    v_rows = vta.shape[2]
    return pl.pallas_call(
        _gqa_kernel,
        grid=(A_KV_HEADS, s // tq),
        in_specs=[
            pl.BlockSpec((A_GROUP, 2 * HEAD_DIM, tq), lambda g, i: (g, 0, i)),
            pl.BlockSpec((nk, tk, A_KV_COLS), lambda g, i: (0, 0, 0), pipeline_mode=pl.Buffered(1)),
            pl.BlockSpec((None, nk, v_rows, tk), lambda g, i: (g, 0, 0, 0), pipeline_mode=pl.Buffered(1)),
            pl.BlockSpec(qn.shape, lambda g, i: (0, 0, 0)),
            pl.BlockSpec(kn.shape, lambda g, i: (0, 0, 0)),
            pl.BlockSpec((A_GROUP, F8_COLS, tq), lambda g, i: (g, 0, i)),
            pl.BlockSpec((None, nk, tk, F8_COLS), lambda g, i: (g, 0, 0, 0), pipeline_mode=pl.Buffered(1)),
        ],
        out_specs=pl.BlockSpec((tq, A_GROUP * HEAD_DIM), lambda g, i: (i, g)),
        out_shape=jax.ShapeDtypeStruct((s, A_Q_COLS), BF16),
        scratch_shapes=[
            pltpu.VMEM((nc, 2 * HEAD_DIM, w), BF16),
            pltpu.VMEM((nc, F8_COLS, w), F8),
            pltpu.VMEM((2, nc, tk, w), BF16),
            pltpu.VMEM((2, nc, tk, w), BF16),
            pltpu.VMEM((2, nc, 1, w), F32),
            pltpu.VMEM((nc, 1, w), F32),
            pltpu.VMEM((nc, v_rows, w), F32),
        ],
        compiler_params=_params("parallel", "parallel"),
        name="gqa_attn",
    )(qta, ka3, vta, qn, kn, qfa, kfa4)


DIFF_CHUNKS = 2


def _diff_kernel(coef_ref, qt0_ref, qt1_ref, k0_ref, k1_ref, vt_ref, qn_ref, kn_ref, lamp_ref, sg_ref,
                 qf0_ref, qf1_ref, kf0_ref, kf1_ref,
                 o_ref, rel_scr, s_buf, p_buf, alpha_buf, m_scr, acc_scr, *, lam_init):
    hd = pl.program_id(0)
    qi = pl.program_id(1)
    nk, tk, _ = k0_ref.shape
    tq = qt0_ref.shape[1]
    slope_f8 = coef_ref[0, hd]
    inv_tile_drop = coef_ref[1, hd]
    acc_scr[...] = jnp.zeros_like(acc_scr)

    @pl.when(qi == 0)
    def _():
        rel_scr[...] = (lax.broadcasted_iota(jnp.int32, (tk, tq), 1)
                        - lax.broadcasted_iota(jnp.int32, (tk, tq), 0)).astype(F32) * slope_f8

    bound, fp8_ok = _score_bound(
        qn_ref[qi], kn_ref,
        lambda row: jnp.logical_or(row == NORM_ROWS_A + hd, row == NORM_ROWS_A + B_HEADS + hd),
        lambda row, k_max: k_max)
    use_direct = jnp.logical_and(bound <= FAST_LOG2_LIMIT, fp8_ok)
    drop = 2.0 * bound + SKIP_LOG2
    reach = jnp.minimum(jnp.floor(drop * inv_tile_drop) + 1.0, float(nk)).astype(jnp.int32)
    lo = jnp.maximum(qi - reach, 0)
    hi = jnp.minimum(qi + reach, nk - 1)

    q0 = qi * tq
    w = tq // DIFF_CHUNKS
    n_chunks = 2 * DIFF_CHUNKS
    cols = [slice((c % DIFF_CHUNKS) * w, (c % DIFF_CHUNKS + 1) * w) for c in range(n_chunks)]
    qts = (qt0_ref, qt1_ref)
    ks = (k0_ref, k1_ref)
    qfs = (qf0_ref, qf1_ref)
    kfs = (kf0_ref, kf1_ref)

    def penalty_scaled(kt):
        return jnp.abs(rel_scr[...] + (q0 - kt * tk).astype(F32) * slope_f8)

    def penalty(kt):
        return penalty_scaled(kt) * F8_UNSCALE

    def qk(kt, slot, c, pen):
        mp = c // DIFF_CHUNKS
        s = jnp.dot(ks[mp][kt], qts[mp][:, cols[c]], preferred_element_type=F32) - pen[:, cols[c]]
        s_buf[slot, c] = s.astype(BF16)

    def softmax(slot, c):
        _softmax_tile(s_buf[slot, c], m_scr.at[c], alpha_buf.at[slot, c], p_buf.at[slot, c])

    def pv(kt, slot, c):
        acc_scr[c] = alpha_buf[slot, c] * acc_scr[c] + jnp.dot(
            vt_ref[kt], p_buf[slot, c], preferred_element_type=F32)

    def qk_exp(kt, slot, c, pen_scaled):
        mp = c // DIFF_CHUNKS
        s = jnp.dot(kfs[mp][kt], qfs[mp][:, cols[c]], preferred_element_type=F32) - pen_scaled[:, cols[c]]
        p_buf[slot, c] = jnp.exp2(s.astype(BF16) * F8_UNSCALE)

    def pv_plain(kt, slot, c):
        acc_scr[c] += jnp.dot(vt_ref[kt], p_buf[slot, c], preferred_element_type=F32)

    def direct():
        _direct_sweep(penalty_scaled, qk_exp, pv_plain, n_chunks, lo, hi - lo + 1, hi)
        return 0

    def online():
        m_scr[...] = jnp.full_like(m_scr, NEG_BIG)
        p_buf[1] = jnp.zeros(p_buf.shape[1:], BF16)
        alpha_buf[1] = jnp.ones(alpha_buf.shape[1:], F32)
        odd = jnp.bitwise_and(hi - lo + 1, 1)
        grow_hi = jnp.where(hi < nk - 1, odd, 0)
        hi2 = hi + grow_hi
        lo2 = lo - (odd - grow_hi)
        _pipelined_sweep(penalty, qk, softmax, pv, n_chunks, lo2, lax.shift_right_logical(hi2 - lo2 + 1, 1), hi2)
        return 0

    lax.cond(use_direct, direct, online)

    lp = lamp_ref[...]
    lam = (jnp.exp(jnp.sum(lp[0:1] * lp[1:2], axis=1, keepdims=True))
           - jnp.exp(jnp.sum(lp[2:3] * lp[3:4], axis=1, keepdims=True)) + lam_init)
    a0, a1 = [jnp.concatenate([acc_scr[mp * DIFF_CHUNKS + j] for j in range(DIFF_CHUNKS)], axis=1)
              for mp in range(2)]
    ot = (a0[:B_V_DIM] * (1.0 / a0[B_V_DIM:B_V_DIM + 1])
          - lam * (a1[:B_V_DIM] * (1.0 / a1[B_V_DIM:B_V_DIM + 1])))
    ot = ot * lax.rsqrt(jnp.mean(ot * ot, axis=0, keepdims=True) + EPS)
    ot = ot * sg_ref[...] * (1.0 - lam_init)
    o_ref[...] = ot.T.astype(BF16)


def _diff(l, coef, qtd, kd3, vtd, qn, kn, qfd, kfd4, lam_p, subln_col, *, lam_init):
    s = qtd.shape[2]
    nk, tk, _ = kd3.shape
    tq = DIFF_Q_TILE
    assert nk % 2 == 0 and tq == tk == MIX_ROWS
    lanes = 2 * HEAD_DIM
    v_rows = vtd.shape[2]
    nc = 2 * DIFF_CHUNKS
    w = tq // DIFF_CHUNKS
    return pl.pallas_call(
        functools.partial(_diff_kernel, lam_init=lam_init),
        grid=(B_HEADS, s // tq),
        in_specs=[
            pl.BlockSpec(memory_space=pltpu.SMEM),
            pl.BlockSpec((None, lanes, tq), lambda h, i: (h, 0, i)),
            pl.BlockSpec((None, lanes, tq), lambda h, i: (B_HEADS + h, 0, i)),
            pl.BlockSpec((nk, tk, lanes), lambda h, i: (0, 0, h // 2), pipeline_mode=pl.Buffered(1)),
            pl.BlockSpec((nk, tk, lanes), lambda h, i: (0, 0, B_HEADS // 2 + h // 2),
                         pipeline_mode=pl.Buffered(1)),
            pl.BlockSpec((None, nk, v_rows, tk), lambda h, i: (h, 0, 0, 0), pipeline_mode=pl.Buffered(1)),
            pl.BlockSpec(qn.shape, lambda h, i: (0, 0, 0)),
            pl.BlockSpec(kn.shape, lambda h, i: (0, 0, 0)),
            _layer_mat(l, lam_p),
            _layer_mat(l, subln_col),
            pl.BlockSpec((None, F8_COLS, tq), lambda h, i: (h, 0, i)),
            pl.BlockSpec((None, F8_COLS, tq), lambda h, i: (B_HEADS + h, 0, i)),
            pl.BlockSpec((None, nk, tk, F8_COLS), lambda h, i: (h, 0, 0, 0), pipeline_mode=pl.Buffered(1)),
            pl.BlockSpec((None, nk, tk, F8_COLS), lambda h, i: (B_HEADS + h, 0, 0, 0),
                         pipeline_mode=pl.Buffered(1)),
        ],
        out_specs=pl.BlockSpec((tq, B_V_DIM), lambda h, i: (i, h)),
        out_shape=jax.ShapeDtypeStruct((s, B_V_COLS), BF16),
        scratch_shapes=[
            pltpu.VMEM((tk, tq), F32),
            pltpu.VMEM((2, nc, tk, w), BF16),
            pltpu.VMEM((2, nc, tk, w), BF16),
            pltpu.VMEM((2, nc, 1, w), F32),
            pltpu.VMEM((nc, 1, w), F32),
            pltpu.VMEM((nc, v_rows, w), F32),
        ],
        compiler_params=_params("arbitrary", "arbitrary"),
        name="diff_attn",
    )(coef, qtd, qtd, kd3, kd3, vtd, qn, kn, lam_p, subln_col, qfd, qfd, kfd4, kfd4)


def _mix_out_kernel(h_ref, ng_ref, sh_ref, sc_ref, gt_ref, oa_ref, od_ref, wba_ref, wbb_ref,
                    wgate_ref, bgate_ref, wo_ref, o_ref):
    h = h_ref[...]
    d = h.shape[1]
    n = _rms_rows(h) * ng_ref[...]
    n = (n * (1.0 + sc_ref[...]) + sh_ref[...]).astype(BF16)
    z = jnp.dot(n, wgate_ref[...], preferred_element_type=F32) + bgate_ref[...]
    g = 1.0 / (1.0 + jnp.exp(-z))
    ya = jnp.dot(oa_ref[...], wba_ref[...], preferred_element_type=F32)
    yb = jnp.dot(od_ref[...], wbb_ref[...], preferred_element_type=F32)
    mix = (g[:, :d] * ya + g[:, d:] * yb).astype(BF16)
    y = jnp.dot(mix, wo_ref[...], preferred_element_type=F32)
    o_ref[...] = h + gt_ref[...] * y


def _mix_out(h, l, norm_g4, mod4, oa, od, w_ba, w_bb, w_gate, b_gate, w_o):
    s, d = h.shape
    tm = MIX_ROWS
    return pl.pallas_call(
        _mix_out_kernel,
        grid=(s // tm,),
        in_specs=[
            pl.BlockSpec((tm, d), lambda i: (i, 0)),
            _layer_vec(l, 1, d), _layer_vec(l, 3, d), _layer_vec(l, 4, d), _layer_vec(l, 5, d),
            pl.BlockSpec((tm, oa.shape[1]), lambda i: (i, 0)),
            pl.BlockSpec((tm, od.shape[1]), lambda i: (i, 0)),
            _layer_mat(l, w_ba, single_buffer=True), _layer_mat(l, w_bb, single_buffer=True),
            _layer_mat(l, w_gate, single_buffer=True), _layer_mat(l, b_gate),
            _layer_mat(l, w_o, single_buffer=True),
        ],
        out_specs=pl.BlockSpec((tm, d), lambda i: (i, 0)),
        out_shape=jax.ShapeDtypeStruct((s, d), F32),
        compiler_params=_params("parallel"),
        name="mix_out",
    )(h, norm_g4, mod4, mod4, mod4, oa, od, w_ba, w_bb, w_gate, b_gate, w_o)


def _axial_angles_t(seq):
    rows = seq // GRID_W
    row = jnp.broadcast_to(jnp.arange(rows)[:, None], (rows, GRID_W)).reshape(seq) - rows // 2
    col = jnp.broadcast_to(jnp.arange(GRID_W)[None, :], (rows, GRID_W)).reshape(seq) - GRID_W // 2
    inv = 1.0 / (ROPE_THETA ** (jnp.arange(0, ROPE_AXIS_DIM, 2, dtype=F32) / ROPE_AXIS_DIM))
    return inv[:, None] * row.astype(F32)[None, :], inv[:, None] * col.astype(F32)[None, :]


def kernel(x, c, ada_w, ada_b, norm_g, ffn_wg, ffn_wu, ffn_wd, w_in, qk_g, lam_p, subln_g, w_ba, w_bb,
           w_gate, b_gate, w_o, final_g):
    batch, s, d = x.shape
    assert batch == 1 and s % KEY_TILE == 0 and MIX_ROWS == KEY_TILE
    depth = ada_w.shape[0]
    h = x.reshape(s, d)

    mod4 = _ada_mod(c.reshape(d, 1), ada_w, ada_b).reshape(depth, N_ADA, 1, d)
    norm_g4 = norm_g.reshape(depth, 3, 1, d)
    angr_t, angc_t = _axial_angles_t(s)
    slopes = 2.0 ** (-8.0 * jnp.arange(1, B_HEADS + 1, dtype=F32) / B_HEADS)
    coef = jnp.stack([slopes * (LOG2E / F8_UNSCALE), 1.0 / (slopes * LOG2E * KEY_TILE)])
    fg = final_g.reshape(1, d)
    wg, wu, wd = ffn_wg.astype(BF16), ffn_wu.astype(BF16), ffn_wd.astype(BF16)
    w_in_t = jnp.swapaxes(w_in, 1, 2).astype(BF16)
    qkg_t = jnp.swapaxes(qk_g, 1, 2)
    w_ba16, w_bb16, w_gate16, w_o16 = (w.astype(BF16) for w in (w_ba, w_bb, w_gate, w_o))
    b_gate3 = b_gate.reshape(depth, 1, -1)
    subln_col = subln_g.reshape(depth, B_V_DIM, 1)
    nk = s // KEY_TILE

    for l in range(depth):
        lam_init = 0.8 - 0.6 * math.exp(-0.3 * l)
        h = _ffn(h, l, 0, norm_g4, mod4, wg, wu, wd, fg, final_norm=False)
        qta, ka, vta, qtd, kd, vtd, qn, kn, qfa, kfa, qfd, kfd = _mix_in(
            h, l, norm_g4, mod4, w_in_t, angr_t, angc_t, qkg_t)
        oa = _gqa(qta, ka.reshape(nk, KEY_TILE, A_KV_COLS), vta, qn, kn,
                  qfa, kfa.reshape(A_KV_HEADS, nk, KEY_TILE, F8_COLS))
        od = _diff(l, coef, qtd, kd.reshape(nk, KEY_TILE, B_QK_COLS), vtd, qn, kn,
                   qfd, kfd.reshape(2 * B_HEADS, nk, KEY_TILE, F8_COLS), lam_p, subln_col, lam_init=lam_init)
        h = _mix_out(h, l, norm_g4, mod4, oa, od, w_ba16, w_bb16, w_gate16, b_gate3, w_o16)
        h = _ffn(h, l, 1, norm_g4, mod4, wg, wu, wd, fg, final_norm=(l == depth - 1))
    return h.reshape(batch, s, d)
```

```python
import functools
import math

import jax
import jax.numpy as jnp
from jax import lax
from jax.experimental import pallas as pl
from jax.experimental.pallas import tpu as pltpu

F32 = jnp.float32
BF16 = jnp.bfloat16
F8 = jnp.float8_e4m3fn

GRID_W = 64
HEAD_DIM = 64
A_Q_HEADS = 8
A_KV_HEADS = 2
A_GROUP = A_Q_HEADS // A_KV_HEADS
B_HEADS = 4
B_V_DIM = 2 * HEAD_DIM
A_Q_COLS = A_Q_HEADS * HEAD_DIM
A_KV_COLS = A_KV_HEADS * HEAD_DIM
B_QK_COLS = 2 * B_HEADS * HEAD_DIM
B_V_COLS = B_HEADS * B_V_DIM
N_ADA = 9
EPS = 1e-6
ROPE_THETA = 10000.0
ROPE_AXIS_DIM = HEAD_DIM // 2
ROPE_HALF = ROPE_AXIS_DIM // 2

LOG2E = math.log2(math.e)
QK_SCALE = HEAD_DIM ** -0.5
NEG_BIG = -1e30

BF16_SUBLANES = 16
LANES = 128
ONES_ROWS = BF16_SUBLANES
SKIP_LOG2 = 64.0
FAST_LOG2_LIMIT = 60.0
F8_Q_SCALE = 2.0 ** 6
F8_K_SCALE = 2.0 ** 2
F8_UNSCALE = 1.0 / (F8_Q_SCALE * F8_K_SCALE)
F8_SAFE_MAX = 256.0
F8_COLS = 4 * HEAD_DIM
NORM_ROWS_A = 8
NORM_ROWS = NORM_ROWS_A + 2 * B_HEADS

VMEM_LIMIT_BYTES = 56 * 1024 * 1024

FFN_ROWS = 512
MIX_ROWS = 512
KEY_TILE = 512
GQA_Q_TILE = 512
DIFF_Q_TILE = 512
ADA_COLS = 1152


def _params(*sem):
    return pltpu.CompilerParams(dimension_semantics=sem, vmem_limit_bytes=VMEM_LIMIT_BYTES)


def _layer_vec(l, j, d):
    return pl.BlockSpec((None, None, 1, d), lambda *_: (l, j, 0, 0))


def _layer_mat(l, a, *, single_buffer=False):
    nd = a.ndim - 1
    mode = dict(pipeline_mode=pl.Buffered(1)) if single_buffer else {}
    return pl.BlockSpec((None,) + a.shape[1:], lambda *_: (l,) + (0,) * nd, **mode)


def _rms_rows(x):
    return x * lax.rsqrt(jnp.mean(x * x, axis=-1, keepdims=True) + EPS)


def _ada_kernel(c_ref, w_ref, b_ref, o_ref):
    c = c_ref[...]
    act = c / (1.0 + jnp.exp(-c))
    o_ref[...] = jnp.sum(w_ref[...] * act, axis=0, keepdims=True) + b_ref[...]


def _ada_mod(c_col, ada_w, ada_b):
    n_layers, d, n = ada_w.shape
    return pl.pallas_call(
        _ada_kernel,
        grid=(n_layers, n // ADA_COLS),
        in_specs=[
            pl.BlockSpec((d, 1), lambda l, j: (0, 0)),
            pl.BlockSpec((None, d, ADA_COLS), lambda l, j: (l, 0, j)),
            pl.BlockSpec((None, 1, ADA_COLS), lambda l, j: (l, 0, j)),
        ],
        out_specs=pl.BlockSpec((None, 1, ADA_COLS), lambda l, j: (l, 0, j)),
        out_shape=jax.ShapeDtypeStruct((n_layers, 1, n), F32),
        compiler_params=_params("parallel", "parallel"),
        name="ada_mod",
    )(c_col, ada_w, ada_b.reshape(n_layers, 1, n))


def _ffn_kernel(h_ref, ng_ref, sh_ref, sc_ref, gt_ref, wg_ref, wu_ref, wd_ref, fg_ref, o_ref, *, final_norm):
    h = h_ref[...]
    n = _rms_rows(h) * ng_ref[...]
    n = (n * (1.0 + sc_ref[...]) + sh_ref[...]).astype(BF16)
    hg = jnp.dot(n, wg_ref[...], preferred_element_type=F32)
    hu = jnp.dot(n, wu_ref[...], preferred_element_type=F32)
    a = (hg / (1.0 + jnp.exp(-hg))) * hu
    out = h + (0.5 * gt_ref[...]) * jnp.dot(a.astype(BF16), wd_ref[...], preferred_element_type=F32)
    if final_norm:
        out = _rms_rows(out) * fg_ref[...]
    o_ref[...] = out


def _ffn(h, l, which, norm_g4, mod4, wg, wu, wd, fg, *, final_norm):
    s, d = h.shape
    m0 = 6 * which

    def resident(a):
        return pl.BlockSpec((None, None) + a.shape[2:], lambda i: (l, which, 0, 0),
                            pipeline_mode=pl.Buffered(1))

    return pl.pallas_call(
        functools.partial(_ffn_kernel, final_norm=final_norm),
        grid=(s // FFN_ROWS,),
        in_specs=[
            pl.BlockSpec((FFN_ROWS, d), lambda i: (i, 0)),
            _layer_vec(l, 2 * which, d),
            _layer_vec(l, m0, d), _layer_vec(l, m0 + 1, d), _layer_vec(l, m0 + 2, d),
            resident(wg), resident(wu), resident(wd),
            pl.BlockSpec((1, d), lambda i: (0, 0)),
        ],
        out_specs=pl.BlockSpec((FFN_ROWS, d), lambda i: (i, 0)),
        out_shape=jax.ShapeDtypeStruct((s, d), F32),
        compiler_params=_params("parallel"),
        name="ffn_final" if final_norm else "ffn",
    )(h, norm_g4, mod4, mod4, mod4, wg, wu, wd, fg)


def _rope_t(x, cr, sr, cc, sc):
    h = ROPE_HALF
    x1r, x2r, x1c, x2c = x[0:h], x[h:2 * h], x[2 * h:3 * h], x[3 * h:4 * h]
    return jnp.concatenate(
        [x1r * cr - x2r * sr, x2r * cr + x1r * sr, x1c * cc - x2c * sc, x2c * cc + x1c * sc], axis=0)


def _head_norm_t(x, g_col):
    ms = jnp.mean(x * x, axis=0, keepdims=True)
    return x * lax.rsqrt(ms + EPS) * g_col


def _max_sq_norm(x_bf16):
    xf = x_bf16.astype(F32)
    n2 = jnp.sum(xf * xf, axis=0, keepdims=True)
    return jnp.broadcast_to(jnp.max(n2, axis=1, keepdims=True), (1, LANES))


def _f8_split(x):
    hi = x.astype(F8).astype(F32)
    return hi, ((x - hi) * 16.0).astype(F8).astype(F32)


def _f8_query_rows(q_bf16):
    hi, lo16 = _f8_split(q_bf16.astype(F32) * F8_Q_SCALE)
    return jnp.concatenate([hi, hi * 0.0625, lo16 * 0.0625, lo16 * 0.00390625], axis=0).astype(F8)


def _f8_key_rows(k_bf16):
    hi, lo16 = _f8_split(k_bf16.astype(F32) * F8_K_SCALE)
    return jnp.concatenate([hi, lo16, hi, lo16], axis=0)


def _with_ones(v_t):
    return jnp.concatenate([v_t, jnp.ones((ONES_ROWS, v_t.shape[1]), F32)], axis=0).astype(BF16)


def _mix_in_kernel(h_ref, ng_ref, sh_ref, sc_ref, wt_ref, angr_ref, angc_ref, qkg_ref,
                   qta_ref, ka_ref, vta_ref, qtd_ref, kd_ref, vtd_ref, qn_ref, kn_ref,
                   qfa_ref, kfa_ref, qfd_ref, kfd_ref):
    n = _rms_rows(h_ref[...]) * ng_ref[...]
    n = (n * (1.0 + sc_ref[...]) + sh_ref[...]).astype(BF16)
    pt = lax.dot_general(wt_ref[...], n, (((1,), (1,)), ((), ())), preferred_element_type=F32)
    t = pt.shape[1]

    cr, sr = jnp.cos(angr_ref[...]), jnp.sin(angr_ref[...])
    cc, sc = jnp.cos(angc_ref[...]), jnp.sin(angc_ref[...])
    gq = qkg_ref[:, 0:1]
    gk = qkg_ref[:, 1:2]
    zeros = jnp.zeros((HEAD_DIM, t), BF16)
    qscale = QK_SCALE * LOG2E

    qn, kn = [], []
    for hd in range(A_Q_HEADS):
        q = _rope_t(_head_norm_t(pt[hd * HEAD_DIM:(hd + 1) * HEAD_DIM], gq), cr, sr, cc, sc) * qscale
        q = q.astype(BF16)
        qta_ref[hd] = jnp.concatenate([q, zeros] if hd // A_GROUP == 0 else [zeros, q], axis=0)
        qfa_ref[hd] = _f8_query_rows(q)
        qn.append(_max_sq_norm(q))

    c1 = A_Q_COLS
    c2 = c1 + A_KV_COLS
    c3 = c2 + A_KV_COLS
    c4 = c3 + B_QK_COLS
    c5 = c4 + B_QK_COLS
    kt = jnp.concatenate(
        [_rope_t(_head_norm_t(pt[c1 + j * HEAD_DIM:c1 + (j + 1) * HEAD_DIM], gk), cr, sr, cc, sc)
         for j in range(A_KV_HEADS)], axis=0)
    kt = kt.astype(BF16)
    ka_ref[...] = kt.T
    kn += [_max_sq_norm(kt[j * HEAD_DIM:(j + 1) * HEAD_DIM]) for j in range(A_KV_HEADS)]
    for j in range(A_KV_HEADS):
        kfa_ref[j] = _f8_key_rows(kt[j * HEAD_DIM:(j + 1) * HEAD_DIM]).astype(BF16).T.astype(F8)
    kn.append(jnp.zeros((NORM_ROWS_A - A_KV_HEADS, LANES), F32))
    for j in range(A_KV_HEADS):
        vta_ref[j] = _with_ones(pt[c2 + j * HEAD_DIM:c2 + (j + 1) * HEAD_DIM])

    for cmb in range(2 * B_HEADS):
        q = (pt[c3 + cmb * HEAD_DIM:c3 + (cmb + 1) * HEAD_DIM] * qscale).astype(BF16)
        qtd_ref[cmb] = jnp.concatenate([q, zeros] if cmb % 2 == 0 else [zeros, q], axis=0)
        qfd_ref[cmb] = _f8_query_rows(q)
        qn.append(_max_sq_norm(q))
        k = pt[c4 + cmb * HEAD_DIM:c4 + (cmb + 1) * HEAD_DIM].astype(BF16)
        kfd_ref[cmb] = _f8_key_rows(k).astype(BF16).T.astype(F8)
        kn.append(_max_sq_norm(k))
    qn_ref[...] = jnp.concatenate(qn, axis=0)
    kn_ref[...] = jnp.concatenate(kn, axis=0)
    kd_ref[...] = pt[c4:c5].T.astype(BF16)
    for hd in range(B_HEADS):
        vtd_ref[hd] = _with_ones(pt[c5 + hd * B_V_DIM:c5 + (hd + 1) * B_V_DIM])


def _mix_in(h, l, norm_g4, mod4, w_in_t, angr_t, angc_t, qkg_t):
    s, d = h.shape
    tm = MIX_ROWS
    nt = s // tm
    n_sets = 2 * B_HEADS
    assert A_Q_HEADS == NORM_ROWS_A
    out_shape = (
        jax.ShapeDtypeStruct((A_Q_HEADS, 2 * HEAD_DIM, s), BF16),
        jax.ShapeDtypeStruct((s, A_KV_COLS), BF16),
        jax.ShapeDtypeStruct((A_KV_HEADS, nt, HEAD_DIM + ONES_ROWS, tm), BF16),
        jax.ShapeDtypeStruct((n_sets, 2 * HEAD_DIM, s), BF16),
        jax.ShapeDtypeStruct((s, B_QK_COLS), BF16),
        jax.ShapeDtypeStruct((B_HEADS, nt, B_V_DIM + ONES_ROWS, tm), BF16),
        jax.ShapeDtypeStruct((nt, NORM_ROWS, LANES), F32),
        jax.ShapeDtypeStruct((nt, NORM_ROWS, LANES), F32),
        jax.ShapeDtypeStruct((A_Q_HEADS, F8_COLS, s), F8),
        jax.ShapeDtypeStruct((A_KV_HEADS, s, F8_COLS), F8),
        jax.ShapeDtypeStruct((n_sets, F8_COLS, s), F8),
        jax.ShapeDtypeStruct((n_sets, s, F8_COLS), F8),
    )
    out_specs = (
        pl.BlockSpec((A_Q_HEADS, 2 * HEAD_DIM, tm), lambda i: (0, 0, i)),
        pl.BlockSpec((tm, A_KV_COLS), lambda i: (i, 0)),
        pl.BlockSpec((A_KV_HEADS, None, HEAD_DIM + ONES_ROWS, tm), lambda i: (0, i, 0, 0)),
        pl.BlockSpec((n_sets, 2 * HEAD_DIM, tm), lambda i: (0, 0, i)),
        pl.BlockSpec((tm, B_QK_COLS), lambda i: (i, 0)),
        pl.BlockSpec((B_HEADS, None, B_V_DIM + ONES_ROWS, tm), lambda i: (0, i, 0, 0)),
        pl.BlockSpec((None, NORM_ROWS, LANES), lambda i: (i, 0, 0)),
        pl.BlockSpec((None, NORM_ROWS, LANES), lambda i: (i, 0, 0)),
        pl.BlockSpec((A_Q_HEADS, F8_COLS, tm), lambda i: (0, 0, i)),
        pl.BlockSpec((A_KV_HEADS, tm, F8_COLS), lambda i: (0, i, 0)),
        pl.BlockSpec((n_sets, F8_COLS, tm), lambda i: (0, 0, i)),
        pl.BlockSpec((n_sets, tm, F8_COLS), lambda i: (0, i, 0)),
    )
    return pl.pallas_call(
        _mix_in_kernel,
        grid=(nt,),
        in_specs=[
            pl.BlockSpec((tm, d), lambda i: (i, 0)),
            _layer_vec(l, 1, d), _layer_vec(l, 3, d), _layer_vec(l, 4, d),
            _layer_mat(l, w_in_t, single_buffer=True),
            pl.BlockSpec((ROPE_HALF, tm), lambda i: (0, i)),
            pl.BlockSpec((ROPE_HALF, tm), lambda i: (0, i)),
            _layer_mat(l, qkg_t),
        ],
        out_specs=out_specs,
        out_shape=out_shape,
        compiler_params=_params("parallel"),
        name="mix_in",
    )(h, norm_g4, mod4, mod4, w_in_t, angr_t, angc_t, qkg_t)


def _softmax_tile(s, m_ref, alpha_ref, p_ref):
    tk, n = s.shape
    part = jnp.max(s.reshape(tk // BF16_SUBLANES, BF16_SUBLANES, n), axis=0)
    m_old = m_ref[...]
    m_new = jnp.maximum(m_old, jnp.max(part.astype(F32), axis=0, keepdims=True))
    alpha_ref[...] = jnp.exp2(m_old - m_new)
    m_ref[...] = m_new
    p_ref[...] = jnp.exp2(s - m_new.astype(BF16))


def _pipelined_sweep(prep, qk, softmax, pv, n_chunks, lo, n_pairs, last):
    ctx = prep(lo)
    for c in range(n_chunks):
        qk(lo, 0, c, ctx)

    def body(j, carry):
        a = lo + 2 * j
        prev = jnp.maximum(a - 1, 0)
        nxt = jnp.minimum(a + 2, last)
        ctx = prep(a + 1)
        for c in range(n_chunks):
            softmax(0, c)
            pv(prev, 1, c)
            qk(a + 1, 1, c, ctx)
        ctx = prep(nxt)
        for c in range(n_chunks):
            softmax(1, c)
            pv(a, 0, c)
            qk(nxt, 0, c, ctx)
        return carry

    lax.fori_loop(0, n_pairs, body, 0)
    for c in range(n_chunks):
        pv(last, 1, c)


def _direct_sweep(prep, qk_exp, pv, n_chunks, lo, n_tiles, last, unroll):
    def phase(kt_new, slot_new, kt_old):
        ctx = prep(kt_new)
        for c in range(n_chunks):
            qk_exp(kt_new, slot_new, c, ctx)
            pv(kt_old, 1 - slot_new, c)

    ctx = prep(lo)
    for c in range(n_chunks):
        qk_exp(lo, 0, c, ctx)

    def pairs(first, n_pairs):
        def body(j, carry):
            a = first + 2 * n_pairs * j
            for i in range(n_pairs):
                phase(a + 2 * i + 1, 1, a + 2 * i)
                phase(a + 2 * i + 2, 0, a + 2 * i + 1)
            return carry
        return body

    if isinstance(n_tiles, int):
        n_pair = (n_tiles - 1) // 2
        n_big = n_pair // unroll
        n_small = n_pair - n_big * unroll
    else:
        n_pair = lax.shift_right_logical(n_tiles - 1, 1)
        n_big = lax.div(n_pair, unroll)
        n_small = n_pair - n_big * unroll
    lax.fori_loop(0, n_big, pairs(lo, unroll), 0)
    lax.fori_loop(0, n_small, pairs(lo + 2 * unroll * n_big, 1), 0)

    def tail_odd():
        for c in range(n_chunks):
            pv(last, 0, c)
        return 0

    def tail_even():
        phase(last, 1, last - 1)
        for c in range(n_chunks):
            pv(last, 1, c)
        return 0

    if isinstance(n_tiles, int):
        (tail_odd if n_tiles % 2 else tail_even)()
    else:
        lax.cond(jnp.bitwise_and(n_tiles, 1) == 1, tail_odd, tail_even)


def _score_bound(qn_tile, kn_ref, q_rows, k_row_of_q):
    k_max = jnp.max(kn_ref[...], axis=0)
    row = lax.broadcasted_iota(jnp.int32, k_max.shape, 0)
    sel = q_rows(row)
    q2 = jnp.where(sel, qn_tile, 0.0)
    k2 = jnp.where(sel, k_row_of_q(row, k_max), 0.0)
    fp8_ok = jnp.logical_and(jnp.max(q2) * F8_Q_SCALE ** 2 <= F8_SAFE_MAX ** 2,
                             jnp.max(k2) * F8_K_SCALE ** 2 <= F8_SAFE_MAX ** 2)
    return jnp.max(jnp.sqrt(q2 * k2)), fp8_ok


GQA_CHUNKS = 8
GQA_UNROLL = 5


def _gqa_kernel(qt_ref, k_ref, vt_ref, qn_ref, kn_ref, qf_ref, kf_ref, o_ref, q_scr, qf_scr, s_buf, p_buf,
                alpha_buf, m_scr, acc_scr):
    g = pl.program_id(0)
    qi = pl.program_id(1)
    nk, tk, _ = k_ref.shape
    tq = qt_ref.shape[2]
    w = A_GROUP * tq // GQA_CHUNKS
    for c in range(GQA_CHUNKS):
        for col in range(c * w, (c + 1) * w, min(w, tq)):
            hh, j, n = col // tq, col % tq, min(w, tq)
            q_scr[c, :, col - c * w:col - c * w + n] = qt_ref[hh, :, j:j + n]
            qf_scr[c, :, col - c * w:col - c * w + n] = qf_ref[hh, :, j:j + n]
    acc_scr[...] = jnp.zeros_like(acc_scr)

    def qk(kt, slot, c, _):
        s_buf[slot, c] = jnp.dot(k_ref[kt], q_scr[c], preferred_element_type=F32).astype(BF16)

    def softmax(slot, c):
        _softmax_tile(s_buf[slot, c], m_scr.at[c], alpha_buf.at[slot, c], p_buf.at[slot, c])

    def pv(kt, slot, c):
        acc_scr[c] = alpha_buf[slot, c] * acc_scr[c] + jnp.dot(
            vt_ref[kt], p_buf[slot, c], preferred_element_type=F32)

    def qk_exp(kt, slot, c, _):
        s = jnp.dot(kf_ref[kt], qf_scr[c], preferred_element_type=F32)
        p_buf[slot, c] = jnp.exp2(s.astype(BF16) * F8_UNSCALE)

    def pv_plain(kt, slot, c):
        acc_scr[c] += jnp.dot(vt_ref[kt], p_buf[slot, c], preferred_element_type=F32)

    def kmax_of_group(row, k_max):
        return jnp.max(jnp.where(row == g, k_max, 0.0), axis=0, keepdims=True)

    bound, fp8_ok = _score_bound(
        qn_ref[lax.div(qi * tq, tk)], kn_ref,
        lambda row: jnp.logical_and(row >= g * A_GROUP, row < (g + 1) * A_GROUP), kmax_of_group)

    def direct():
        _direct_sweep(lambda kt: None, qk_exp, pv_plain, GQA_CHUNKS, 0, nk, nk - 1, GQA_UNROLL)
        return 0

    def online():
        m_scr[...] = jnp.full_like(m_scr, NEG_BIG)
        p_buf[1] = jnp.zeros(p_buf.shape[1:], BF16)
        alpha_buf[1] = jnp.ones(alpha_buf.shape[1:], F32)
        _pipelined_sweep(lambda kt: None, qk, softmax, pv, GQA_CHUNKS, 0, nk // 2, nk - 1)
        return 0

    lax.cond(jnp.logical_and(bound <= FAST_LOG2_LIMIT, fp8_ok), direct, online)
    cols = []
    for c in range(GQA_CHUNKS):
        acc = acc_scr[c]
        cols.append(acc[:HEAD_DIM] * (1.0 / acc[HEAD_DIM:HEAD_DIM + 1]))
    ot = jnp.concatenate(cols, axis=1)
    ot = jnp.concatenate([ot[:, hh * tq:(hh + 1) * tq] for hh in range(A_GROUP)], axis=0)
    o_ref[...] = ot.T.astype(BF16)


def _gqa(qta, ka3, vta, qn, kn, qfa, kfa4):
    s = qta.shape[2]
    nk, tk, _ = ka3.shape
    tq = GQA_Q_TILE
    assert nk % 2 == 0 and tk % tq == 0 and tk == MIX_ROWS
    nc = GQA_CHUNKS
    w = A_GROUP * tq // nc
    v_rows = vta.shape[2]
    return pl.pallas_call(
        _gqa_kernel,
        grid=(A_KV_HEADS, s // tq),
        in_specs=[
            pl.BlockSpec((A_GROUP, 2 * HEAD_DIM, tq), lambda g, i: (g, 0, i)),
            pl.BlockSpec((nk, tk, A_KV_COLS), lambda g, i: (0, 0, 0), pipeline_mode=pl.Buffered(1)),
            pl.BlockSpec((None, nk, v_rows, tk), lambda g, i: (g, 0, 0, 0), pipeline_mode=pl.Buffered(1)),
            pl.BlockSpec(qn.shape, lambda g, i: (0, 0, 0)),
            pl.BlockSpec(kn.shape, lambda g, i: (0, 0, 0)),
            pl.BlockSpec((A_GROUP, F8_COLS, tq), lambda g, i: (g, 0, i)),
            pl.BlockSpec((None, nk, tk, F8_COLS), lambda g, i: (g, 0, 0, 0), pipeline_mode=pl.Buffered(1)),
        ],
        out_specs=pl.BlockSpec((tq, A_GROUP * HEAD_DIM), lambda g, i: (i, g)),
        out_shape=jax.ShapeDtypeStruct((s, A_Q_COLS), BF16),
        scratch_shapes=[
            pltpu.VMEM((nc, 2 * HEAD_DIM, w), BF16),
            pltpu.VMEM((nc, F8_COLS, w), F8),
            pltpu.VMEM((2, nc, tk, w), BF16),
            pltpu.VMEM((2, nc, tk, w), BF16),
            pltpu.VMEM((2, nc, 1, w), F32),
            pltpu.VMEM((nc, 1, w), F32),
            pltpu.VMEM((nc, v_rows, w), F32),
        ],
        compiler_params=_params("parallel", "parallel"),
        name="gqa_attn",
    )(qta, ka3, vta, qn, kn, qfa, kfa4)


DIFF_CHUNKS = 2
DIFF_UNROLL = 4


def _diff_kernel(coef_ref, qt0_ref, qt1_ref, k0_ref, k1_ref, vt_ref, qn_ref, kn_ref, lamp_ref, sg_ref,
                 qf0_ref, qf1_ref, kf0_ref, kf1_ref,
                 o_ref, rel_scr, s_buf, p_buf, alpha_buf, m_scr, acc_scr, *, lam_init):
    hd = pl.program_id(0)
    qi = pl.program_id(1)
    nk, tk, _ = k0_ref.shape
    tq = qt0_ref.shape[1]
    slope_f8 = coef_ref[0, hd]
    inv_tile_drop = coef_ref[1, hd]
    acc_scr[...] = jnp.zeros_like(acc_scr)

    @pl.when(qi == 0)
    def _():
        rel_scr[...] = (lax.broadcasted_iota(jnp.int32, (tk, tq), 1)
                        - lax.broadcasted_iota(jnp.int32, (tk, tq), 0)).astype(F32) * slope_f8

    bound, fp8_ok = _score_bound(
        qn_ref[qi], kn_ref,
        lambda row: jnp.logical_or(row == NORM_ROWS_A + hd, row == NORM_ROWS_A + B_HEADS + hd),
        lambda row, k_max: k_max)
    use_direct = jnp.logical_and(bound <= FAST_LOG2_LIMIT, fp8_ok)
    drop = 2.0 * bound + SKIP_LOG2
    reach = jnp.minimum(jnp.floor(drop * inv_tile_drop) + 1.0, float(nk)).astype(jnp.int32)
    lo = jnp.maximum(qi - reach, 0)
    hi = jnp.minimum(qi + reach, nk - 1)

    q0 = qi * tq
    w = tq // DIFF_CHUNKS
    n_chunks = 2 * DIFF_CHUNKS
    cols = [slice((c % DIFF_CHUNKS) * w, (c % DIFF_CHUNKS + 1) * w) for c in range(n_chunks)]
    qts = (qt0_ref, qt1_ref)
    ks = (k0_ref, k1_ref)
    qfs = (qf0_ref, qf1_ref)
    kfs = (kf0_ref, kf1_ref)

    def penalty_scaled(kt):
        return jnp.abs(rel_scr[...] + (q0 - kt * tk).astype(F32) * slope_f8)

    def penalty(kt):
        return penalty_scaled(kt) * F8_UNSCALE

    def qk(kt, slot, c, pen):
        mp = c // DIFF_CHUNKS
        s = jnp.dot(ks[mp][kt], qts[mp][:, cols[c]], preferred_element_type=F32) - pen[:, cols[c]]
        s_buf[slot, c] = s.astype(BF16)

    def softmax(slot, c):
        _softmax_tile(s_buf[slot, c], m_scr.at[c], alpha_buf.at[slot, c], p_buf.at[slot, c])

    def pv(kt, slot, c):
        acc_scr[c] = alpha_buf[slot, c] * acc_scr[c] + jnp.dot(
            vt_ref[kt], p_buf[slot, c], preferred_element_type=F32)

    def qk_exp(kt, slot, c, pen_scaled):
        mp = c // DIFF_CHUNKS
        s = jnp.dot(kfs[mp][kt], qfs[mp][:, cols[c]], preferred_element_type=F32) - pen_scaled[:, cols[c]]
        p_buf[slot, c] = jnp.exp2(s.astype(BF16) * F8_UNSCALE)

    def pv_plain(kt, slot, c):
        acc_scr[c] += jnp.dot(vt_ref[kt], p_buf[slot, c], preferred_element_type=F32)

    def direct():
        _direct_sweep(penalty_scaled, qk_exp, pv_plain, n_chunks, lo, hi - lo + 1, hi, DIFF_UNROLL)
        return 0

    def online():
        m_scr[...] = jnp.full_like(m_scr, NEG_BIG)
        p_buf[1] = jnp.zeros(p_buf.shape[1:], BF16)
        alpha_buf[1] = jnp.ones(alpha_buf.shape[1:], F32)
        odd = jnp.bitwise_and(hi - lo + 1, 1)
        grow_hi = jnp.where(hi < nk - 1, odd, 0)
        hi2 = hi + grow_hi
        lo2 = lo - (odd - grow_hi)
        _pipelined_sweep(penalty, qk, softmax, pv, n_chunks, lo2, lax.shift_right_logical(hi2 - lo2 + 1, 1), hi2)
        return 0

    lax.cond(use_direct, direct, online)

    lp = lamp_ref[...]
    lam = (jnp.exp(jnp.sum(lp[0:1] * lp[1:2], axis=1, keepdims=True))
           - jnp.exp(jnp.sum(lp[2:3] * lp[3:4], axis=1, keepdims=True)) + lam_init)
    a0, a1 = [jnp.concatenate([acc_scr[mp * DIFF_CHUNKS + j] for j in range(DIFF_CHUNKS)], axis=1)
              for mp in range(2)]
    ot = (a0[:B_V_DIM] * (1.0 / a0[B_V_DIM:B_V_DIM + 1])
          - lam * (a1[:B_V_DIM] * (1.0 / a1[B_V_DIM:B_V_DIM + 1])))
    ot = ot * lax.rsqrt(jnp.mean(ot * ot, axis=0, keepdims=True) + EPS)
    ot = ot * sg_ref[...] * (1.0 - lam_init)
    o_ref[...] = ot.T.astype(BF16)


def _diff(l, coef, qtd, kd3, vtd, qn, kn, qfd, kfd4, lam_p, subln_col, *, lam_init):
    s = qtd.shape[2]
    nk, tk, _ = kd3.shape
    tq = DIFF_Q_TILE
    assert nk % 2 == 0 and tq == tk == MIX_ROWS
    lanes = 2 * HEAD_DIM
    v_rows = vtd.shape[2]
    nc = 2 * DIFF_CHUNKS
    w = tq // DIFF_CHUNKS
    return pl.pallas_call(
        functools.partial(_diff_kernel, lam_init=lam_init),
        grid=(B_HEADS, s // tq),
        in_specs=[
            pl.BlockSpec(memory_space=pltpu.SMEM),
            pl.BlockSpec((None, lanes, tq), lambda h, i: (h, 0, i)),
            pl.BlockSpec((None, lanes, tq), lambda h, i: (B_HEADS + h, 0, i)),
            pl.BlockSpec((nk, tk, lanes), lambda h, i: (0, 0, h // 2), pipeline_mode=pl.Buffered(1)),
            pl.BlockSpec((nk, tk, lanes), lambda h, i: (0, 0, B_HEADS // 2 + h // 2),
                         pipeline_mode=pl.Buffered(1)),
            pl.BlockSpec((None, nk, v_rows, tk), lambda h, i: (h, 0, 0, 0), pipeline_mode=pl.Buffered(1)),
            pl.BlockSpec(qn.shape, lambda h, i: (0, 0, 0)),
            pl.BlockSpec(kn.shape, lambda h, i: (0, 0, 0)),
            _layer_mat(l, lam_p),
            _layer_mat(l, subln_col),
            pl.BlockSpec((None, F8_COLS, tq), lambda h, i: (h, 0, i)),
            pl.BlockSpec((None, F8_COLS, tq), lambda h, i: (B_HEADS + h, 0, i)),
            pl.BlockSpec((None, nk, tk, F8_COLS), lambda h, i: (h, 0, 0, 0), pipeline_mode=pl.Buffered(1)),
            pl.BlockSpec((None, nk, tk, F8_COLS), lambda h, i: (B_HEADS + h, 0, 0, 0),
                         pipeline_mode=pl.Buffered(1)),
        ],
        out_specs=pl.BlockSpec((tq, B_V_DIM), lambda h, i: (i, h)),
        out_shape=jax.ShapeDtypeStruct((s, B_V_COLS), BF16),
        scratch_shapes=[
            pltpu.VMEM((tk, tq), F32),
            pltpu.VMEM((2, nc, tk, w), BF16),
            pltpu.VMEM((2, nc, tk, w), BF16),
            pltpu.VMEM((2, nc, 1, w), F32),
            pltpu.VMEM((nc, 1, w), F32),
            pltpu.VMEM((nc, v_rows, w), F32),
        ],
        compiler_params=_params("arbitrary", "arbitrary"),
        name="diff_attn",
    )(coef, qtd, qtd, kd3, kd3, vtd, qn, kn, lam_p, subln_col, qfd, qfd, kfd4, kfd4)


def _mix_out_kernel(h_ref, ng_ref, sh_ref, sc_ref, gt_ref, oa_ref, od_ref, wba_ref, wbb_ref,
                    wgate_ref, bgate_ref, wo_ref, o_ref):
    h = h_ref[...]
    d = h.shape[1]
    n = _rms_rows(h) * ng_ref[...]
    n = (n * (1.0 + sc_ref[...]) + sh_ref[...]).astype(BF16)
    z = jnp.dot(n, wgate_ref[...], preferred_element_type=F32) + bgate_ref[...]
    g = 1.0 / (1.0 + jnp.exp(-z))
    ya = jnp.dot(oa_ref[...], wba_ref[...], preferred_element_type=F32)
    yb = jnp.dot(od_ref[...], wbb_ref[...], preferred_element_type=F32)
    mix = (g[:, :d] * ya + g[:, d:] * yb).astype(BF16)
    y = jnp.dot(mix, wo_ref[...], preferred_element_type=F32)
    o_ref[...] = h + gt_ref[...] * y


def _mix_out(h, l, norm_g4, mod4, oa, od, w_ba, w_bb, w_gate, b_gate, w_o):
    s, d = h.shape
    tm = MIX_ROWS
    return pl.pallas_call(
        _mix_out_kernel,
        grid=(s // tm,),
        in_specs=[
            pl.BlockSpec((tm, d), lambda i: (i, 0)),
            _layer_vec(l, 1, d), _layer_vec(l, 3, d), _layer_vec(l, 4, d), _layer_vec(l, 5, d),
            pl.BlockSpec((tm, oa.shape[1]), lambda i: (i, 0)),
            pl.BlockSpec((tm, od.shape[1]), lambda i: (i, 0)),
            _layer_mat(l, w_ba, single_buffer=True), _layer_mat(l, w_bb, single_buffer=True),
            _layer_mat(l, w_gate, single_buffer=True), _layer_mat(l, b_gate),
            _layer_mat(l, w_o, single_buffer=True),
        ],
        out_specs=pl.BlockSpec((tm, d), lambda i: (i, 0)),
        out_shape=jax.ShapeDtypeStruct((s, d), F32),
        compiler_params=_params("parallel"),
        name="mix_out",
    )(h, norm_g4, mod4, mod4, mod4, oa, od, w_ba, w_bb, w_gate, b_gate, w_o)


def _axial_angles_t(seq):
    rows = seq // GRID_W
    row = jnp.broadcast_to(jnp.arange(rows)[:, None], (rows, GRID_W)).reshape(seq) - rows // 2
    col = jnp.broadcast_to(jnp.arange(GRID_W)[None, :], (rows, GRID_W)).reshape(seq) - GRID_W // 2
    inv = 1.0 / (ROPE_THETA ** (jnp.arange(0, ROPE_AXIS_DIM, 2, dtype=F32) / ROPE_AXIS_DIM))
    return inv[:, None] * row.astype(F32)[None, :], inv[:, None] * col.astype(F32)[None, :]


def kernel(x, c, ada_w, ada_b, norm_g, ffn_wg, ffn_wu, ffn_wd, w_in, qk_g, lam_p, subln_g, w_ba, w_bb,
           w_gate, b_gate, w_o, final_g):
    batch, s, d = x.shape
    assert batch == 1 and s % KEY_TILE == 0 and MIX_ROWS == KEY_TILE
    depth = ada_w.shape[0]
    h = x.reshape(s, d)

    mod4 = _ada_mod(c.reshape(d, 1), ada_w, ada_b).reshape(depth, N_ADA, 1, d)
    norm_g4 = norm_g.reshape(depth, 3, 1, d)
    angr_t, angc_t = _axial_angles_t(s)
    slopes = 2.0 ** (-8.0 * jnp.arange(1, B_HEADS + 1, dtype=F32) / B_HEADS)
    coef = jnp.stack([slopes * (LOG2E / F8_UNSCALE), 1.0 / (slopes * LOG2E * KEY_TILE)])
    fg = final_g.reshape(1, d)
    wg, wu, wd = ffn_wg.astype(BF16), ffn_wu.astype(BF16), ffn_wd.astype(BF16)
    w_in_t = jnp.swapaxes(w_in, 1, 2).astype(BF16)
    qkg_t = jnp.swapaxes(qk_g, 1, 2)
    w_ba16, w_bb16, w_gate16, w_o16 = (w.astype(BF16) for w in (w_ba, w_bb, w_gate, w_o))
    b_gate3 = b_gate.reshape(depth, 1, -1)
    subln_col = subln_g.reshape(depth, B_V_DIM, 1)
    nk = s // KEY_TILE

    for l in range(depth):
        lam_init = 0.8 - 0.6 * math.exp(-0.3 * l)
        h = _ffn(h, l, 0, norm_g4, mod4, wg, wu, wd, fg, final_norm=False)
        qta, ka, vta, qtd, kd, vtd, qn, kn, qfa, kfa, qfd, kfd = _mix_in(
            h, l, norm_g4, mod4, w_in_t, angr_t, angc_t, qkg_t)
        oa = _gqa(qta, ka.reshape(nk, KEY_TILE, A_KV_COLS), vta, qn, kn,
                  qfa, kfa.reshape(A_KV_HEADS, nk, KEY_TILE, F8_COLS))
        od = _diff(l, coef, qtd, kd.reshape(nk, KEY_TILE, B_QK_COLS), vtd, qn, kn,
                   qfd, kfd.reshape(2 * B_HEADS, nk, KEY_TILE, F8_COLS), lam_p, subln_col, lam_init=lam_init)
        h = _mix_out(h, l, norm_g4, mod4, oa, od, w_ba16, w_bb16, w_gate16, b_gate3, w_o16)
        h = _ffn(h, l, 1, norm_g4, mod4, wg, wu, wd, fg, final_norm=(l == depth - 1))
    return h.reshape(batch, s, d)
```

```python
import functools
import math

import jax
import jax.numpy as jnp
from jax import lax
from jax.experimental import pallas as pl
from jax.experimental.pallas import tpu as pltpu

F32 = jnp.float32
BF16 = jnp.bfloat16
F8 = jnp.float8_e4m3fn

GRID_W = 64
HEAD_DIM = 64
A_Q_HEADS = 8
A_KV_HEADS = 2
A_GROUP = A_Q_HEADS // A_KV_HEADS
B_HEADS = 4
B_V_DIM = 2 * HEAD_DIM
A_Q_COLS = A_Q_HEADS * HEAD_DIM
A_KV_COLS = A_KV_HEADS * HEAD_DIM
B_QK_COLS = 2 * B_HEADS * HEAD_DIM
B_V_COLS = B_HEADS * B_V_DIM
N_ADA = 9
EPS = 1e-6
ROPE_THETA = 10000.0
ROPE_AXIS_DIM = HEAD_DIM // 2
ROPE_HALF = ROPE_AXIS_DIM // 2

LOG2E = math.log2(math.e)
QK_SCALE = HEAD_DIM ** -0.5
NEG_BIG = -1e30

BF16_SUBLANES = 16
LANES = 128
ONES_ROWS = BF16_SUBLANES
SKIP_LOG2 = 64.0
FAST_LOG2_LIMIT = 60.0
F8_Q_SCALE = 2.0 ** 6
F8_K_SCALE = 2.0 ** 2
F8_UNSCALE = 1.0 / (F8_Q_SCALE * F8_K_SCALE)
F8_SAFE_MAX = 256.0
F8_COLS = 4 * HEAD_DIM
NORM_ROWS_A = 8
NORM_ROWS = NORM_ROWS_A + 2 * B_HEADS

VMEM_LIMIT_BYTES = 56 * 1024 * 1024

FFN_ROWS = 512
MIX_ROWS = 512
KEY_TILE = 512
GQA_Q_TILE = 512
DIFF_Q_TILE = 512
ADA_COLS = 1152


def _params(*sem):
    return pltpu.CompilerParams(dimension_semantics=sem, vmem_limit_bytes=VMEM_LIMIT_BYTES)


def _layer_vec(l, j, d):
    return pl.BlockSpec((None, None, 1, d), lambda *_: (l, j, 0, 0))


def _layer_mat(l, a, *, single_buffer=False):
    nd = a.ndim - 1
    mode = dict(pipeline_mode=pl.Buffered(1)) if single_buffer else {}
    return pl.BlockSpec((None,) + a.shape[1:], lambda *_: (l,) + (0,) * nd, **mode)


def _rms_rows(x):
    return x * lax.rsqrt(jnp.mean(x * x, axis=-1, keepdims=True) + EPS)


def _ada_kernel(c_ref, w_ref, b_ref, o_ref):
    c = c_ref[...]
    act = c / (1.0 + jnp.exp(-c))
    o_ref[...] = jnp.sum(w_ref[...] * act, axis=0, keepdims=True) + b_ref[...]


def _ada_mod(c_col, ada_w, ada_b):
    n_layers, d, n = ada_w.shape
    return pl.pallas_call(
        _ada_kernel,
        grid=(n_layers, n // ADA_COLS),
        in_specs=[
            pl.BlockSpec((d, 1), lambda l, j: (0, 0)),
            pl.BlockSpec((None, d, ADA_COLS), lambda l, j: (l, 0, j)),
            pl.BlockSpec((None, 1, ADA_COLS), lambda l, j: (l, 0, j)),
        ],
        out_specs=pl.BlockSpec((None, 1, ADA_COLS), lambda l, j: (l, 0, j)),
        out_shape=jax.ShapeDtypeStruct((n_layers, 1, n), F32),
        compiler_params=_params("parallel", "parallel"),
        name="ada_mod",
    )(c_col, ada_w, ada_b.reshape(n_layers, 1, n))


def _ffn_kernel(h_ref, ng_ref, sh_ref, sc_ref, gt_ref, wg_ref, wu_ref, wd_ref, fg_ref, o_ref, *, final_norm):
    h = h_ref[...]
    n = _rms_rows(h) * ng_ref[...]
    n = (n * (1.0 + sc_ref[...]) + sh_ref[...]).astype(BF16)
    hg = jnp.dot(n, wg_ref[...], preferred_element_type=F32)
    hu = jnp.dot(n, wu_ref[...], preferred_element_type=F32)
    a = (hg / (1.0 + jnp.exp(-hg))) * hu
    out = h + (0.5 * gt_ref[...]) * jnp.dot(a.astype(BF16), wd_ref[...], preferred_element_type=F32)
    if final_norm:
        out = _rms_rows(out) * fg_ref[...]
    o_ref[...] = out


def _ffn(h, l, which, norm_g4, mod4, wg, wu, wd, fg, *, final_norm):
    s, d = h.shape
    m0 = 6 * which

    def resident(a):
        return pl.BlockSpec((None, None) + a.shape[2:], lambda i: (l, which, 0, 0),
                            pipeline_mode=pl.Buffered(1))

    return pl.pallas_call(
        functools.partial(_ffn_kernel, final_norm=final_norm),
        grid=(s // FFN_ROWS,),
        in_specs=[
            pl.BlockSpec((FFN_ROWS, d), lambda i: (i, 0)),
            _layer_vec(l, 2 * which, d),
            _layer_vec(l, m0, d), _layer_vec(l, m0 + 1, d), _layer_vec(l, m0 + 2, d),
            resident(wg), resident(wu), resident(wd),
            pl.BlockSpec((1, d), lambda i: (0, 0)),
        ],
        out_specs=pl.BlockSpec((FFN_ROWS, d), lambda i: (i, 0)),
        out_shape=jax.ShapeDtypeStruct((s, d), F32),
        compiler_params=_params("parallel"),
        name="ffn_final" if final_norm else "ffn",
    )(h, norm_g4, mod4, mod4, mod4, wg, wu, wd, fg)


def _rope_t(x, cr, sr, cc, sc):
    h = ROPE_HALF
    x1r, x2r, x1c, x2c = x[0:h], x[h:2 * h], x[2 * h:3 * h], x[3 * h:4 * h]
    return jnp.concatenate(
        [x1r * cr - x2r * sr, x2r * cr + x1r * sr, x1c * cc - x2c * sc, x2c * cc + x1c * sc], axis=0)


def _head_norm_t(x, g_col):
    ms = jnp.mean(x * x, axis=0, keepdims=True)
    return x * lax.rsqrt(ms + EPS) * g_col


def _max_sq_norm(x_bf16):
    xf = x_bf16.astype(F32)
    n2 = jnp.sum(xf * xf, axis=0, keepdims=True)
    return jnp.broadcast_to(jnp.max(n2, axis=1, keepdims=True), (1, LANES))


def _f8_split(x):
    hi = x.astype(F8).astype(F32)
    return hi, ((x - hi) * 16.0).astype(F8).astype(F32)


def _f8_query_rows(q_bf16):
    hi, lo16 = _f8_split(q_bf16.astype(F32) * F8_Q_SCALE)
    return jnp.concatenate([hi, hi * 0.0625, lo16 * 0.0625, lo16 * 0.00390625], axis=0).astype(F8)


def _f8_key_rows(k_bf16):
    hi, lo16 = _f8_split(k_bf16.astype(F32) * F8_K_SCALE)
    return jnp.concatenate([hi, lo16, hi, lo16], axis=0)


def _with_ones(v_t):
    return jnp.concatenate([v_t, jnp.ones((ONES_ROWS, v_t.shape[1]), F32)], axis=0).astype(BF16)


def _mix_in_kernel(h_ref, ng_ref, sh_ref, sc_ref, wt_ref, angr_ref, angc_ref, qkg_ref,
                   qta_ref, ka_ref, vta_ref, qtd_ref, kd_ref, vtd_ref, qn_ref, kn_ref,
                   qfa_ref, kfa_ref, qfd_ref, kfd_ref):
    n = _rms_rows(h_ref[...]) * ng_ref[...]
    n = (n * (1.0 + sc_ref[...]) + sh_ref[...]).astype(BF16)
    pt = lax.dot_general(wt_ref[...], n, (((1,), (1,)), ((), ())), preferred_element_type=F32)
    t = pt.shape[1]

    cr, sr = jnp.cos(angr_ref[...]), jnp.sin(angr_ref[...])
    cc, sc = jnp.cos(angc_ref[...]), jnp.sin(angc_ref[...])
    gq = qkg_ref[:, 0:1]
    gk = qkg_ref[:, 1:2]
    zeros = jnp.zeros((HEAD_DIM, t), BF16)
    qscale = QK_SCALE * LOG2E

    qn, kn = [], []
    for hd in range(A_Q_HEADS):
        q = _rope_t(_head_norm_t(pt[hd * HEAD_DIM:(hd + 1) * HEAD_DIM], gq), cr, sr, cc, sc) * qscale
        q = q.astype(BF16)
        qta_ref[hd] = jnp.concatenate([q, zeros] if hd // A_GROUP == 0 else [zeros, q], axis=0)
        qfa_ref[hd] = _f8_query_rows(q)
        qn.append(_max_sq_norm(q))

    c1 = A_Q_COLS
    c2 = c1 + A_KV_COLS
    c3 = c2 + A_KV_COLS
    c4 = c3 + B_QK_COLS
    c5 = c4 + B_QK_COLS
    kt = jnp.concatenate(
        [_rope_t(_head_norm_t(pt[c1 + j * HEAD_DIM:c1 + (j + 1) * HEAD_DIM], gk), cr, sr, cc, sc)
         for j in range(A_KV_HEADS)], axis=0)
    kt = kt.astype(BF16)
    ka_ref[...] = kt.T
    kn += [_max_sq_norm(kt[j * HEAD_DIM:(j + 1) * HEAD_DIM]) for j in range(A_KV_HEADS)]
    for j in range(A_KV_HEADS):
        kfa_ref[j] = _f8_key_rows(kt[j * HEAD_DIM:(j + 1) * HEAD_DIM]).astype(BF16).T.astype(F8)
    kn.append(jnp.zeros((NORM_ROWS_A - A_KV_HEADS, LANES), F32))
    for j in range(A_KV_HEADS):
        vta_ref[j] = _with_ones(pt[c2 + j * HEAD_DIM:c2 + (j + 1) * HEAD_DIM])

    for cmb in range(2 * B_HEADS):
        q = (pt[c3 + cmb * HEAD_DIM:c3 + (cmb + 1) * HEAD_DIM] * qscale).astype(BF16)
        qtd_ref[cmb] = jnp.concatenate([q, zeros] if cmb % 2 == 0 else [zeros, q], axis=0)
        qfd_ref[cmb] = _f8_query_rows(q)
        qn.append(_max_sq_norm(q))
        k = pt[c4 + cmb * HEAD_DIM:c4 + (cmb + 1) * HEAD_DIM].astype(BF16)
        kfd_ref[cmb] = _f8_key_rows(k).astype(BF16).T.astype(F8)
        kn.append(_max_sq_norm(k))
    qn_ref[...] = jnp.concatenate(qn, axis=0)
    kn_ref[...] = jnp.concatenate(kn, axis=0)
    kd_ref[...] = pt[c4:c5].T.astype(BF16)
    for hd in range(B_HEADS):
        vtd_ref[hd] = _with_ones(pt[c5 + hd * B_V_DIM:c5 + (hd + 1) * B_V_DIM])


def _mix_in(h, l, norm_g4, mod4, w_in_t, angr_t, angc_t, qkg_t):
    s, d = h.shape
    tm = MIX_ROWS
    nt = s // tm
    n_sets = 2 * B_HEADS
    assert A_Q_HEADS == NORM_ROWS_A
    out_shape = (
        jax.ShapeDtypeStruct((A_Q_HEADS, 2 * HEAD_DIM, s), BF16),
        jax.ShapeDtypeStruct((s, A_KV_COLS), BF16),
        jax.ShapeDtypeStruct((A_KV_HEADS, nt, HEAD_DIM + ONES_ROWS, tm), BF16),
        jax.ShapeDtypeStruct((n_sets, 2 * HEAD_DIM, s), BF16),
        jax.ShapeDtypeStruct((s, B_QK_COLS), BF16),
        jax.ShapeDtypeStruct((B_HEADS, nt, B_V_DIM + ONES_ROWS, tm), BF16),
        jax.ShapeDtypeStruct((nt, NORM_ROWS, LANES), F32),
        jax.ShapeDtypeStruct((nt, NORM_ROWS, LANES), F32),
        jax.ShapeDtypeStruct((A_Q_HEADS, F8_COLS, s), F8),
        jax.ShapeDtypeStruct((A_KV_HEADS, s, F8_COLS), F8),
        jax.ShapeDtypeStruct((n_sets, F8_COLS, s), F8),
        jax.ShapeDtypeStruct((n_sets, s, F8_COLS), F8),
    )
    out_specs = (
        pl.BlockSpec((A_Q_HEADS, 2 * HEAD_DIM, tm), lambda i: (0, 0, i)),
        pl.BlockSpec((tm, A_KV_COLS), lambda i: (i, 0)),
        pl.BlockSpec((A_KV_HEADS, None, HEAD_DIM + ONES_ROWS, tm), lambda i: (0, i, 0, 0)),
        pl.BlockSpec((n_sets, 2 * HEAD_DIM, tm), lambda i: (0, 0, i)),
        pl.BlockSpec((tm, B_QK_COLS), lambda i: (i, 0)),
        pl.BlockSpec((B_HEADS, None, B_V_DIM + ONES_ROWS, tm), lambda i: (0, i, 0, 0)),
        pl.BlockSpec((None, NORM_ROWS, LANES), lambda i: (i, 0, 0)),
        pl.BlockSpec((None, NORM_ROWS, LANES), lambda i: (i, 0, 0)),
        pl.BlockSpec((A_Q_HEADS, F8_COLS, tm), lambda i: (0, 0, i)),
        pl.BlockSpec((A_KV_HEADS, tm, F8_COLS), lambda i: (0, i, 0)),
        pl.BlockSpec((n_sets, F8_COLS, tm), lambda i: (0, 0, i)),
        pl.BlockSpec((n_sets, tm, F8_COLS), lambda i: (0, i, 0)),
    )
    return pl.pallas_call(
        _mix_in_kernel,
        grid=(nt,),
        in_specs=[
            pl.BlockSpec((tm, d), lambda i: (i, 0)),
            _layer_vec(l, 1, d), _layer_vec(l, 3, d), _layer_vec(l, 4, d),
            _layer_mat(l, w_in_t, single_buffer=True),
            pl.BlockSpec((ROPE_HALF, tm), lambda i: (0, i)),
            pl.BlockSpec((ROPE_HALF, tm), lambda i: (0, i)),
            _layer_mat(l, qkg_t),
        ],
        out_specs=out_specs,
        out_shape=out_shape,
        compiler_params=_params("parallel"),
        name="mix_in",
    )(h, norm_g4, mod4, mod4, w_in_t, angr_t, angc_t, qkg_t)


def _softmax_tile(s, m_ref, alpha_ref, p_ref):
    tk, n = s.shape
    part = jnp.max(s.reshape(tk // BF16_SUBLANES, BF16_SUBLANES, n), axis=0)
    m_old = m_ref[...]
    m_new = jnp.maximum(m_old, jnp.max(part.astype(F32), axis=0, keepdims=True))
    alpha_ref[...] = jnp.exp2(m_old - m_new)
    m_ref[...] = m_new
    p_ref[...] = jnp.exp2(s - m_new.astype(BF16))


def _pipelined_sweep(prep, qk, softmax, pv, n_chunks, lo, n_pairs, last):
    ctx = prep(lo)
    for c in range(n_chunks):
        qk(lo, 0, c, ctx)

    def body(j, carry):
        a = lo + 2 * j
        prev = jnp.maximum(a - 1, 0)
        nxt = jnp.minimum(a + 2, last)
        ctx = prep(a + 1)
        for c in range(n_chunks):
            softmax(0, c)
            pv(prev, 1, c)
            qk(a + 1, 1, c, ctx)
        ctx = prep(nxt)
        for c in range(n_chunks):
            softmax(1, c)
            pv(a, 0, c)
            qk(nxt, 0, c, ctx)
        return carry

    lax.fori_loop(0, n_pairs, body, 0)
    for c in range(n_chunks):
        pv(last, 1, c)


def _direct_sweep(prep, qk_exp, pv, n_chunks, lo, n_tiles, last, unroll):
    def phase(kt_new, slot_new, kt_old):
        ctx = prep(kt_new)
        for c in range(n_chunks):
            qk_exp(kt_new, slot_new, c, ctx)
            pv(kt_old, 1 - slot_new, c)

    ctx = prep(lo)
    for c in range(n_chunks):
        qk_exp(lo, 0, c, ctx)

    def pairs(first, n_pairs):
        def body(j, carry):
            a = first + 2 * n_pairs * j
            for i in range(n_pairs):
                phase(a + 2 * i + 1, 1, a + 2 * i)
                phase(a + 2 * i + 2, 0, a + 2 * i + 1)
            return carry
        return body

    if isinstance(n_tiles, int):
        n_pair = (n_tiles - 1) // 2
        n_big = n_pair // unroll
        n_small = n_pair - n_big * unroll
    else:
        n_pair = lax.shift_right_logical(n_tiles - 1, 1)
        n_big = lax.div(n_pair, unroll)
        n_small = n_pair - n_big * unroll
    lax.fori_loop(0, n_big, pairs(lo, unroll), 0)
    lax.fori_loop(0, n_small, pairs(lo + 2 * unroll * n_big, 1), 0)

    def tail_odd():
        for c in range(n_chunks):
            pv(last, 0, c)
        return 0

    def tail_even():
        phase(last, 1, last - 1)
        for c in range(n_chunks):
            pv(last, 1, c)
        return 0

    if isinstance(n_tiles, int):
        (tail_odd if n_tiles % 2 else tail_even)()
    else:
        lax.cond(jnp.bitwise_and(n_tiles, 1) == 1, tail_odd, tail_even)


def _score_bound(qn_tile, kn_ref, q_rows, k_row_of_q):
    k_max = jnp.max(kn_ref[...], axis=0)
    row = lax.broadcasted_iota(jnp.int32, k_max.shape, 0)
    sel = q_rows(row)
    q2 = jnp.where(sel, qn_tile, 0.0)
    k2 = jnp.where(sel, k_row_of_q(row, k_max), 0.0)
    fp8_ok = jnp.logical_and(jnp.max(q2) * F8_Q_SCALE ** 2 <= F8_SAFE_MAX ** 2,
                             jnp.max(k2) * F8_K_SCALE ** 2 <= F8_SAFE_MAX ** 2)
    return jnp.max(jnp.sqrt(q2 * k2)), fp8_ok


GQA_CHUNKS = 8
GQA_UNROLL = 15


def _gqa_kernel(qt_ref, k_ref, vt_ref, qn_ref, kn_ref, qf_ref, kf_ref, o_ref, q_scr, qf_scr, s_buf, p_buf,
                alpha_buf, m_scr, acc_scr):
    g = pl.program_id(0)
    qi = pl.program_id(1)
    nk, tk, _ = k_ref.shape
    tq = qt_ref.shape[2]
    w = A_GROUP * tq // GQA_CHUNKS
    for c in range(GQA_CHUNKS):
        for col in range(c * w, (c + 1) * w, min(w, tq)):
            hh, j, n = col // tq, col % tq, min(w, tq)
            q_scr[c, :, col - c * w:col - c * w + n] = qt_ref[hh, :, j:j + n]
            qf_scr[c, :, col - c * w:col - c * w + n] = qf_ref[hh, :, j:j + n]
    acc_scr[...] = jnp.zeros_like(acc_scr)

    def qk(kt, slot, c, _):
        s_buf[slot, c] = jnp.dot(k_ref[kt], q_scr[c], preferred_element_type=F32).astype(BF16)

    def softmax(slot, c):
        _softmax_tile(s_buf[slot, c], m_scr.at[c], alpha_buf.at[slot, c], p_buf.at[slot, c])

    def pv(kt, slot, c):
        acc_scr[c] = alpha_buf[slot, c] * acc_scr[c] + jnp.dot(
            vt_ref[kt], p_buf[slot, c], preferred_element_type=F32)

    def qk_exp(kt, slot, c, _):
        s = jnp.dot(kf_ref[kt], qf_scr[c], preferred_element_type=F32)
        p_buf[slot, c] = jnp.exp2(s.astype(BF16) * F8_UNSCALE)

    def pv_plain(kt, slot, c):
        acc_scr[c] += jnp.dot(vt_ref[kt], p_buf[slot, c], preferred_element_type=F32)

    def kmax_of_group(row, k_max):
        return jnp.max(jnp.where(row == g, k_max, 0.0), axis=0, keepdims=True)

    bound, fp8_ok = _score_bound(
        qn_ref[lax.div(qi * tq, tk)], kn_ref,
        lambda row: jnp.logical_and(row >= g * A_GROUP, row < (g + 1) * A_GROUP), kmax_of_group)

    def direct():
        _direct_sweep(lambda kt: None, qk_exp, pv_plain, GQA_CHUNKS, 0, nk, nk - 1, GQA_UNROLL)
        return 0

    def online():
        m_scr[...] = jnp.full_like(m_scr, NEG_BIG)
        p_buf[1] = jnp.zeros(p_buf.shape[1:], BF16)
        alpha_buf[1] = jnp.ones(alpha_buf.shape[1:], F32)
        _pipelined_sweep(lambda kt: None, qk, softmax, pv, GQA_CHUNKS, 0, nk // 2, nk - 1)
        return 0

    lax.cond(jnp.logical_and(bound <= FAST_LOG2_LIMIT, fp8_ok), direct, online)
    cols = []
    for c in range(GQA_CHUNKS):
        acc = acc_scr[c]
        cols.append(acc[:HEAD_DIM] * (1.0 / acc[HEAD_DIM:HEAD_DIM + 1]))
    ot = jnp.concatenate(cols, axis=1)
    ot = jnp.concatenate([ot[:, hh * tq:(hh + 1) * tq] for hh in range(A_GROUP)], axis=0)
    o_ref[...] = ot.T.astype(BF16)


def _gqa(qta, ka3, vta, qn, kn, qfa, kfa4):
    s = qta.shape[2]
    nk, tk, _ = ka3.shape
    tq = GQA_Q_TILE
    assert nk % 2 == 0 and tk % tq == 0 and tk == MIX_ROWS
    nc = GQA_CHUNKS
    w = A_GROUP * tq // nc
    v_rows = vta.shape[2]
    return pl.pallas_call(
        _gqa_kernel,
        grid=(A_KV_HEADS, s // tq),
        in_specs=[
            pl.BlockSpec((A_GROUP, 2 * HEAD_DIM, tq), lambda g, i: (g, 0, i)),
            pl.BlockSpec((nk, tk, A_KV_COLS), lambda g, i: (0, 0, 0), pipeline_mode=pl.Buffered(1)),
            pl.BlockSpec((None, nk, v_rows, tk), lambda g, i: (g, 0, 0, 0), pipeline_mode=pl.Buffered(1)),
            pl.BlockSpec(qn.shape, lambda g, i: (0, 0, 0)),
            pl.BlockSpec(kn.shape, lambda g, i: (0, 0, 0)),
            pl.BlockSpec((A_GROUP, F8_COLS, tq), lambda g, i: (g, 0, i)),
            pl.BlockSpec((None, nk, tk, F8_COLS), lambda g, i: (g, 0, 0, 0), pipeline_mode=pl.Buffered(1)),
        ],
        out_specs=pl.BlockSpec((tq, A_GROUP * HEAD_DIM), lambda g, i: (i, g)),
        out_shape=jax.ShapeDtypeStruct((s, A_Q_COLS), BF16),
        scratch_shapes=[
            pltpu.VMEM((nc, 2 * HEAD_DIM, w), BF16),
            pltpu.VMEM((nc, F8_COLS, w), F8),
            pltpu.VMEM((2, nc, tk, w), BF16),
            pltpu.VMEM((2, nc, tk, w), BF16),
            pltpu.VMEM((2, nc, 1, w), F32),
            pltpu.VMEM((nc, 1, w), F32),
            pltpu.VMEM((nc, v_rows, w), F32),
        ],
        compiler_params=_params("parallel", "parallel"),
        name="gqa_attn",
    )(qta, ka3, vta, qn, kn, qfa, kfa4)


DIFF_CHUNKS = 2
DIFF_UNROLL = 5


def _diff_kernel(coef_ref, qt0_ref, qt1_ref, k0_ref, k1_ref, vt_ref, qn_ref, kn_ref, lamp_ref, sg_ref,
                 qf0_ref, qf1_ref, kf0_ref, kf1_ref,
                 o_ref, rel_scr, s_buf, p_buf, alpha_buf, m_scr, acc_scr, *, lam_init):
    hd = pl.program_id(0)
    qi = pl.program_id(1)
    nk, tk, _ = k0_ref.shape
    tq = qt0_ref.shape[1]
    slope_f8 = coef_ref[0, hd]
    inv_tile_drop = coef_ref[1, hd]
    acc_scr[...] = jnp.zeros_like(acc_scr)

    @pl.when(qi == 0)
    def _():
        rel_scr[...] = (lax.broadcasted_iota(jnp.int32, (tk, tq), 1)
                        - lax.broadcasted_iota(jnp.int32, (tk, tq), 0)).astype(F32) * slope_f8

    bound, fp8_ok = _score_bound(
        qn_ref[qi], kn_ref,
        lambda row: jnp.logical_or(row == NORM_ROWS_A + hd, row == NORM_ROWS_A + B_HEADS + hd),
        lambda row, k_max: k_max)
    use_direct = jnp.logical_and(bound <= FAST_LOG2_LIMIT, fp8_ok)
    drop = 2.0 * bound + SKIP_LOG2
    reach = jnp.minimum(jnp.floor(drop * inv_tile_drop) + 1.0, float(nk)).astype(jnp.int32)
    lo = jnp.maximum(qi - reach, 0)
    hi = jnp.minimum(qi + reach, nk - 1)

    q0 = qi * tq
    w = tq // DIFF_CHUNKS
    n_chunks = 2 * DIFF_CHUNKS
    cols = [slice((c % DIFF_CHUNKS) * w, (c % DIFF_CHUNKS + 1) * w) for c in range(n_chunks)]
    qts = (qt0_ref, qt1_ref)
    ks = (k0_ref, k1_ref)
    qfs = (qf0_ref, qf1_ref)
    kfs = (kf0_ref, kf1_ref)

    def penalty_scaled(kt):
        return jnp.abs(rel_scr[...] + (q0 - kt * tk).astype(F32) * slope_f8)

    def penalty(kt):
        return penalty_scaled(kt) * F8_UNSCALE

    def qk(kt, slot, c, pen):
        mp = c // DIFF_CHUNKS
        s = jnp.dot(ks[mp][kt], qts[mp][:, cols[c]], preferred_element_type=F32) - pen[:, cols[c]]
        s_buf[slot, c] = s.astype(BF16)

    def softmax(slot, c):
        _softmax_tile(s_buf[slot, c], m_scr.at[c], alpha_buf.at[slot, c], p_buf.at[slot, c])

    def pv(kt, slot, c):
        acc_scr[c] = alpha_buf[slot, c] * acc_scr[c] + jnp.dot(
            vt_ref[kt], p_buf[slot, c], preferred_element_type=F32)

    def qk_exp(kt, slot, c, pen_scaled):
        mp = c // DIFF_CHUNKS
        s = jnp.dot(kfs[mp][kt], qfs[mp][:, cols[c]], preferred_element_type=F32) - pen_scaled[:, cols[c]]
        p_buf[slot, c] = jnp.exp2(s.astype(BF16) * F8_UNSCALE)

    def pv_plain(kt, slot, c):
        acc_scr[c] += jnp.dot(vt_ref[kt], p_buf[slot, c], preferred_element_type=F32)

    def direct():
        _direct_sweep(penalty_scaled, qk_exp, pv_plain, n_chunks, lo, hi - lo + 1, hi, DIFF_UNROLL)
        return 0

    def online():
        m_scr[...] = jnp.full_like(m_scr, NEG_BIG)
        p_buf[1] = jnp.zeros(p_buf.shape[1:], BF16)
        alpha_buf[1] = jnp.ones(alpha_buf.shape[1:], F32)
        odd = jnp.bitwise_and(hi - lo + 1, 1)
        grow_hi = jnp.where(hi < nk - 1, odd, 0)
        hi2 = hi + grow_hi
        lo2 = lo - (odd - grow_hi)
        _pipelined_sweep(penalty, qk, softmax, pv, n_chunks, lo2, lax.shift_right_logical(hi2 - lo2 + 1, 1), hi2)
        return 0

    lax.cond(use_direct, direct, online)

    lp = lamp_ref[...]
    lam = (jnp.exp(jnp.sum(lp[0:1] * lp[1:2], axis=1, keepdims=True))
           - jnp.exp(jnp.sum(lp[2:3] * lp[3:4], axis=1, keepdims=True)) + lam_init)
    a0, a1 = [jnp.concatenate([acc_scr[mp * DIFF_CHUNKS + j] for j in range(DIFF_CHUNKS)], axis=1)
              for mp in range(2)]
    ot = (a0[:B_V_DIM] * (1.0 / a0[B_V_DIM:B_V_DIM + 1])
          - lam * (a1[:B_V_DIM] * (1.0 / a1[B_V_DIM:B_V_DIM + 1])))
    ot = ot * lax.rsqrt(jnp.mean(ot * ot, axis=0, keepdims=True) + EPS)
    ot = ot * sg_ref[...] * (1.0 - lam_init)
    o_ref[...] = ot.T.astype(BF16)


def _diff(l, coef, qtd, kd3, vtd, qn, kn, qfd, kfd4, lam_p, subln_col, *, lam_init):
    s = qtd.shape[2]
    nk, tk, _ = kd3.shape
    tq = DIFF_Q_TILE
    assert nk % 2 == 0 and tq == tk == MIX_ROWS
    lanes = 2 * HEAD_DIM
    v_rows = vtd.shape[2]
    nc = 2 * DIFF_CHUNKS
    w = tq // DIFF_CHUNKS
    return pl.pallas_call(
        functools.partial(_diff_kernel, lam_init=lam_init),
        grid=(B_HEADS, s // tq),
        in_specs=[
            pl.BlockSpec(memory_space=pltpu.SMEM),
            pl.BlockSpec((None, lanes, tq), lambda h, i: (h, 0, i)),
            pl.BlockSpec((None, lanes, tq), lambda h, i: (B_HEADS + h, 0, i)),
            pl.BlockSpec((nk, tk, lanes), lambda h, i: (0, 0, h // 2), pipeline_mode=pl.Buffered(1)),
            pl.BlockSpec((nk, tk, lanes), lambda h, i: (0, 0, B_HEADS // 2 + h // 2),
                         pipeline_mode=pl.Buffered(1)),
            pl.BlockSpec((None, nk, v_rows, tk), lambda h, i: (h, 0, 0, 0), pipeline_mode=pl.Buffered(1)),
            pl.BlockSpec(qn.shape, lambda h, i: (0, 0, 0)),
            pl.BlockSpec(kn.shape, lambda h, i: (0, 0, 0)),
            _layer_mat(l, lam_p),
            _layer_mat(l, subln_col),
            pl.BlockSpec((None, F8_COLS, tq), lambda h, i: (h, 0, i)),
            pl.BlockSpec((None, F8_COLS, tq), lambda h, i: (B_HEADS + h, 0, i)),
            pl.BlockSpec((None, nk, tk, F8_COLS), lambda h, i: (h, 0, 0, 0), pipeline_mode=pl.Buffered(1)),
            pl.BlockSpec((None, nk, tk, F8_COLS), lambda h, i: (B_HEADS + h, 0, 0, 0),
                         pipeline_mode=pl.Buffered(1)),
        ],
        out_specs=pl.BlockSpec((tq, B_V_DIM), lambda h, i: (i, h)),
        out_shape=jax.ShapeDtypeStruct((s, B_V_COLS), BF16),
        scratch_shapes=[
            pltpu.VMEM((tk, tq), F32),
            pltpu.VMEM((2, nc, tk, w), BF16),
            pltpu.VMEM((2, nc, tk, w), BF16),
            pltpu.VMEM((2, nc, 1, w), F32),
            pltpu.VMEM((nc, 1, w), F32),
            pltpu.VMEM((nc, v_rows, w), F32),
        ],
        compiler_params=_params("arbitrary", "arbitrary"),
        name="diff_attn",
    )(coef, qtd, qtd, kd3, kd3, vtd, qn, kn, lam_p, subln_col, qfd, qfd, kfd4, kfd4)


def _mix_out_kernel(h_ref, ng_ref, sh_ref, sc_ref, gt_ref, oa_ref, od_ref, wba_ref, wbb_ref,
                    wgate_ref, bgate_ref, wo_ref, o_ref):
    h = h_ref[...]
    d = h.shape[1]
    n = _rms_rows(h) * ng_ref[...]
    n = (n * (1.0 + sc_ref[...]) + sh_ref[...]).astype(BF16)
    z = jnp.dot(n, wgate_ref[...], preferred_element_type=F32) + bgate_ref[...]
    g = 1.0 / (1.0 + jnp.exp(-z))
    ya = jnp.dot(oa_ref[...], wba_ref[...], preferred_element_type=F32)
    yb = jnp.dot(od_ref[...], wbb_ref[...], preferred_element_type=F32)
    mix = (g[:, :d] * ya + g[:, d:] * yb).astype(BF16)
    y = jnp.dot(mix, wo_ref[...], preferred_element_type=F32)
    o_ref[...] = h + gt_ref[...] * y


def _mix_out(h, l, norm_g4, mod4, oa, od, w_ba, w_bb, w_gate, b_gate, w_o):
    s, d = h.shape
    tm = MIX_ROWS
    return pl.pallas_call(
        _mix_out_kernel,
        grid=(s // tm,),
        in_specs=[
            pl.BlockSpec((tm, d), lambda i: (i, 0)),
            _layer_vec(l, 1, d), _layer_vec(l, 3, d), _layer_vec(l, 4, d), _layer_vec(l, 5, d),
            pl.BlockSpec((tm, oa.shape[1]), lambda i: (i, 0)),
            pl.BlockSpec((tm, od.shape[1]), lambda i: (i, 0)),
            _layer_mat(l, w_ba, single_buffer=True), _layer_mat(l, w_bb, single_buffer=True),
            _layer_mat(l, w_gate, single_buffer=True), _layer_mat(l, b_gate),
            _layer_mat(l, w_o, single_buffer=True),
        ],
        out_specs=pl.BlockSpec((tm, d), lambda i: (i, 0)),
        out_shape=jax.ShapeDtypeStruct((s, d), F32),
        compiler_params=_params("parallel"),
        name="mix_out",
    )(h, norm_g4, mod4, mod4, mod4, oa, od, w_ba, w_bb, w_gate, b_gate, w_o)


def _axial_angles_t(seq):
    rows = seq // GRID_W
    row = jnp.broadcast_to(jnp.arange(rows)[:, None], (rows, GRID_W)).reshape(seq) - rows // 2
    col = jnp.broadcast_to(jnp.arange(GRID_W)[None, :], (rows, GRID_W)).reshape(seq) - GRID_W // 2
    inv = 1.0 / (ROPE_THETA ** (jnp.arange(0, ROPE_AXIS_DIM, 2, dtype=F32) / ROPE_AXIS_DIM))
    return inv[:, None] * row.astype(F32)[None, :], inv[:, None] * col.astype(F32)[None, :]


def kernel(x, c, ada_w, ada_b, norm_g, ffn_wg, ffn_wu, ffn_wd, w_in, qk_g, lam_p, subln_g, w_ba, w_bb,
           w_gate, b_gate, w_o, final_g):
    batch, s, d = x.shape
    assert batch == 1 and s % KEY_TILE == 0 and MIX_ROWS == KEY_TILE
    depth = ada_w.shape[0]
    h = x.reshape(s, d)

    mod4 = _ada_mod(c.reshape(d, 1), ada_w, ada_b).reshape(depth, N_ADA, 1, d)
    norm_g4 = norm_g.reshape(depth, 3, 1, d)
    angr_t, angc_t = _axial_angles_t(s)
    slopes = 2.0 ** (-8.0 * jnp.arange(1, B_HEADS + 1, dtype=F32) / B_HEADS)
    coef = jnp.stack([slopes * (LOG2E / F8_UNSCALE), 1.0 / (slopes * LOG2E * KEY_TILE)])
    fg = final_g.reshape(1, d)
    wg, wu, wd = ffn_wg.astype(BF16), ffn_wu.astype(BF16), ffn_wd.astype(BF16)
    w_in_t = jnp.swapaxes(w_in, 1, 2).astype(BF16)
    qkg_t = jnp.swapaxes(qk_g, 1, 2)
    w_ba16, w_bb16, w_gate16, w_o16 = (w.astype(BF16) for w in (w_ba, w_bb, w_gate, w_o))
    b_gate3 = b_gate.reshape(depth, 1, -1)
    subln_col = subln_g.reshape(depth, B_V_DIM, 1)
    nk = s // KEY_TILE

    for l in range(depth):
        lam_init = 0.8 - 0.6 * math.exp(-0.3 * l)
        h = _ffn(h, l, 0, norm_g4, mod4, wg, wu, wd, fg, final_norm=False)
        qta, ka, vta, qtd, kd, vtd, qn, kn, qfa, kfa, qfd, kfd = _mix_in(
            h, l, norm_g4, mod4, w_in_t, angr_t, angc_t, qkg_t)
        oa = _gqa(qta, ka.reshape(nk, KEY_TILE, A_KV_COLS), vta, qn, kn,
                  qfa, kfa.reshape(A_KV_HEADS, nk, KEY_TILE, F8_COLS))
        od = _diff(l, coef, qtd, kd.reshape(nk, KEY_TILE, B_QK_COLS), vtd, qn, kn,
                   qfd, kfd.reshape(2 * B_HEADS, nk, KEY_TILE, F8_COLS), lam_p, subln_col, lam_init=lam_init)
        h = _mix_out(h, l, norm_g4, mod4, oa, od, w_ba16, w_bb16, w_gate16, b_gate3, w_o16)
        h = _ffn(h, l, 1, norm_g4, mod4, wg, wu, wd, fg, final_norm=(l == depth - 1))
    return h.reshape(batch, s, d)
```

```python
import functools
import math

import jax
import jax.numpy as jnp
from jax import lax
from jax.experimental import pallas as pl
from jax.experimental.pallas import tpu as pltpu

F32 = jnp.float32
BF16 = jnp.bfloat16
F8 = jnp.float8_e4m3fn

GRID_W = 64
HEAD_DIM = 64
A_Q_HEADS = 8
A_KV_HEADS = 2
A_GROUP = A_Q_HEADS // A_KV_HEADS
B_HEADS = 4
B_V_DIM = 2 * HEAD_DIM
A_Q_COLS = A_Q_HEADS * HEAD_DIM
A_KV_COLS = A_KV_HEADS * HEAD_DIM
B_QK_COLS = 2 * B_HEADS * HEAD_DIM
B_V_COLS = B_HEADS * B_V_DIM
N_ADA = 9
EPS = 1e-6
ROPE_THETA = 10000.0
ROPE_AXIS_DIM = HEAD_DIM // 2
ROPE_HALF = ROPE_AXIS_DIM // 2

LOG2E = math.log2(math.e)
QK_SCALE = HEAD_DIM ** -0.5
NEG_BIG = -1e30

BF16_SUBLANES = 16
LANES = 128
ONES_ROWS = BF16_SUBLANES
SKIP_LOG2 = 64.0
FAST_LOG2_LIMIT = 60.0
F8_Q_SCALE = 2.0 ** 6
F8_K_SCALE = 2.0 ** 2
F8_UNSCALE = 1.0 / (F8_Q_SCALE * F8_K_SCALE)
F8_SAFE_MAX = 256.0
F8_COLS = 4 * HEAD_DIM
NORM_ROWS_A = 8
NORM_ROWS = NORM_ROWS_A + 2 * B_HEADS

VMEM_LIMIT_BYTES = 56 * 1024 * 1024

FFN_ROWS = 512
MIX_ROWS = 512
KEY_TILE = 512
GQA_Q_TILE = 512
DIFF_Q_TILE = 512
ADA_COLS = 1152


def _params(*sem):
    return pltpu.CompilerParams(dimension_semantics=sem, vmem_limit_bytes=VMEM_LIMIT_BYTES)


def _layer_vec(l, j, d):
    return pl.BlockSpec((None, None, 1, d), lambda *_: (l, j, 0, 0))


def _layer_mat(l, a, *, single_buffer=False):
    nd = a.ndim - 1
    mode = dict(pipeline_mode=pl.Buffered(1)) if single_buffer else {}
    return pl.BlockSpec((None,) + a.shape[1:], lambda *_: (l,) + (0,) * nd, **mode)


def _rms_rows(x):
    return x * lax.rsqrt(jnp.mean(x * x, axis=-1, keepdims=True) + EPS)


def _ada_kernel(c_ref, w_ref, b_ref, o_ref):
    c = c_ref[...]
    act = c / (1.0 + jnp.exp(-c))
    o_ref[...] = jnp.sum(w_ref[...] * act, axis=0, keepdims=True) + b_ref[...]


def _ada_mod(c_col, ada_w, ada_b):
    n_layers, d, n = ada_w.shape
    return pl.pallas_call(
        _ada_kernel,
        grid=(n_layers, n // ADA_COLS),
        in_specs=[
            pl.BlockSpec((d, 1), lambda l, j: (0, 0)),
            pl.BlockSpec((None, d, ADA_COLS), lambda l, j: (l, 0, j)),
            pl.BlockSpec((None, 1, ADA_COLS), lambda l, j: (l, 0, j)),
        ],
        out_specs=pl.BlockSpec((None, 1, ADA_COLS), lambda l, j: (l, 0, j)),
        out_shape=jax.ShapeDtypeStruct((n_layers, 1, n), F32),
        compiler_params=_params("parallel", "parallel"),
        name="ada_mod",
    )(c_col, ada_w, ada_b.reshape(n_layers, 1, n))


def _ffn_kernel(h_ref, ng_ref, sh_ref, sc_ref, gt_ref, wg_ref, wu_ref, wd_ref, fg_ref, o_ref, *, final_norm):
    h = h_ref[...]
    n = _rms_rows(h) * ng_ref[...]
    n = (n * (1.0 + sc_ref[...]) + sh_ref[...]).astype(BF16)
    hg = jnp.dot(n, wg_ref[...], preferred_element_type=F32)
    hu = jnp.dot(n, wu_ref[...], preferred_element_type=F32)
    a = (hg / (1.0 + jnp.exp(-hg))) * hu
    out = h + (0.5 * gt_ref[...]) * jnp.dot(a.astype(BF16), wd_ref[...], preferred_element_type=F32)
    if final_norm:
        out = _rms_rows(out) * fg_ref[...]
    o_ref[...] = out


def _ffn(h, l, which, norm_g4, mod4, wg, wu, wd, fg, *, final_norm):
    s, d = h.shape
    m0 = 6 * which

    def resident(a):
        return pl.BlockSpec((None, None) + a.shape[2:], lambda i: (l, which, 0, 0),
                            pipeline_mode=pl.Buffered(1))

    return pl.pallas_call(
        functools.partial(_ffn_kernel, final_norm=final_norm),
        grid=(s // FFN_ROWS,),
        in_specs=[
            pl.BlockSpec((FFN_ROWS, d), lambda i: (i, 0)),
            _layer_vec(l, 2 * which, d),
            _layer_vec(l, m0, d), _layer_vec(l, m0 + 1, d), _layer_vec(l, m0 + 2, d),
            resident(wg), resident(wu), resident(wd),
            pl.BlockSpec((1, d), lambda i: (0, 0)),
        ],
        out_specs=pl.BlockSpec((FFN_ROWS, d), lambda i: (i, 0)),
        out_shape=jax.ShapeDtypeStruct((s, d), F32),
        compiler_params=_params("parallel"),
        name="ffn_final" if final_norm else "ffn",
    )(h, norm_g4, mod4, mod4, mod4, wg, wu, wd, fg)


def _rope_t(x, cr, sr, cc, sc):
    h = ROPE_HALF
    x1r, x2r, x1c, x2c = x[0:h], x[h:2 * h], x[2 * h:3 * h], x[3 * h:4 * h]
    return jnp.concatenate(
        [x1r * cr - x2r * sr, x2r * cr + x1r * sr, x1c * cc - x2c * sc, x2c * cc + x1c * sc], axis=0)


def _head_norm_t(x, g_col):
    ms = jnp.mean(x * x, axis=0, keepdims=True)
    return x * lax.rsqrt(ms + EPS) * g_col


def _max_sq_norm(x_bf16):
    xf = x_bf16.astype(F32)
    n2 = jnp.sum(xf * xf, axis=0, keepdims=True)
    return jnp.broadcast_to(jnp.max(n2, axis=1, keepdims=True), (1, LANES))


def _f8_split(x):
    hi = x.astype(F8).astype(F32)
    return hi, ((x - hi) * 16.0).astype(F8).astype(F32)


def _f8_query_rows(q_bf16):
    hi, lo16 = _f8_split(q_bf16.astype(F32) * F8_Q_SCALE)
    return jnp.concatenate([hi, hi * 0.0625, lo16 * 0.0625, lo16 * 0.00390625], axis=0).astype(F8)


def _f8_key_rows(k_bf16):
    hi, lo16 = _f8_split(k_bf16.astype(F32) * F8_K_SCALE)
    return jnp.concatenate([hi, lo16, hi, lo16], axis=0)


def _with_ones(v_t):
    return jnp.concatenate([v_t, jnp.ones((ONES_ROWS, v_t.shape[1]), F32)], axis=0).astype(BF16)


def _mix_in_kernel(h_ref, ng_ref, sh_ref, sc_ref, wt_ref, angr_ref, angc_ref, qkg_ref,
                   qta_ref, ka_ref, vta_ref, qtd_ref, kd_ref, vtd_ref, qn_ref, kn_ref,
                   qfa_ref, kfa_ref, qfd_ref, kfd_ref):
    n = _rms_rows(h_ref[...]) * ng_ref[...]
    n = (n * (1.0 + sc_ref[...]) + sh_ref[...]).astype(BF16)
    pt = lax.dot_general(wt_ref[...], n, (((1,), (1,)), ((), ())), preferred_element_type=F32)
    t = pt.shape[1]

    cr, sr = jnp.cos(angr_ref[...]), jnp.sin(angr_ref[...])
    cc, sc = jnp.cos(angc_ref[...]), jnp.sin(angc_ref[...])
    gq = qkg_ref[:, 0:1]
    gk = qkg_ref[:, 1:2]
    zeros = jnp.zeros((HEAD_DIM, t), BF16)
    qscale = QK_SCALE * LOG2E

    qn, kn = [], []
    for hd in range(A_Q_HEADS):
        q = _rope_t(_head_norm_t(pt[hd * HEAD_DIM:(hd + 1) * HEAD_DIM], gq), cr, sr, cc, sc) * qscale
        q = q.astype(BF16)
        qta_ref[hd] = jnp.concatenate([q, zeros] if hd // A_GROUP == 0 else [zeros, q], axis=0)
        qfa_ref[hd] = _f8_query_rows(q)
        qn.append(_max_sq_norm(q))

    c1 = A_Q_COLS
    c2 = c1 + A_KV_COLS
    c3 = c2 + A_KV_COLS
    c4 = c3 + B_QK_COLS
    c5 = c4 + B_QK_COLS
    kt = jnp.concatenate(
        [_rope_t(_head_norm_t(pt[c1 + j * HEAD_DIM:c1 + (j + 1) * HEAD_DIM], gk), cr, sr, cc, sc)
         for j in range(A_KV_HEADS)], axis=0)
    kt = kt.astype(BF16)
    ka_ref[...] = kt.T
    kn += [_max_sq_norm(kt[j * HEAD_DIM:(j + 1) * HEAD_DIM]) for j in range(A_KV_HEADS)]
    for j in range(A_KV_HEADS):
        kfa_ref[j] = _f8_key_rows(kt[j * HEAD_DIM:(j + 1) * HEAD_DIM]).astype(BF16).T.astype(F8)
    kn.append(jnp.zeros((NORM_ROWS_A - A_KV_HEADS, LANES), F32))
    for j in range(A_KV_HEADS):
        vta_ref[j] = _with_ones(pt[c2 + j * HEAD_DIM:c2 + (j + 1) * HEAD_DIM])

    for cmb in range(2 * B_HEADS):
        q = (pt[c3 + cmb * HEAD_DIM:c3 + (cmb + 1) * HEAD_DIM] * qscale).astype(BF16)
        qtd_ref[cmb] = jnp.concatenate([q, zeros] if cmb % 2 == 0 else [zeros, q], axis=0)
        qfd_ref[cmb] = _f8_query_rows(q)
        qn.append(_max_sq_norm(q))
        k = pt[c4 + cmb * HEAD_DIM:c4 + (cmb + 1) * HEAD_DIM].astype(BF16)
        kfd_ref[cmb] = _f8_key_rows(k).astype(BF16).T.astype(F8)
        kn.append(_max_sq_norm(k))
    qn_ref[...] = jnp.concatenate(qn, axis=0)
    kn_ref[...] = jnp.concatenate(kn, axis=0)
    kd_ref[...] = pt[c4:c5].T.astype(BF16)
    for hd in range(B_HEADS):
        vtd_ref[hd] = _with_ones(pt[c5 + hd * B_V_DIM:c5 + (hd + 1) * B_V_DIM])


def _mix_in(h, l, norm_g4, mod4, w_in_t, angr_t, angc_t, qkg_t):
    s, d = h.shape
    tm = MIX_ROWS
    nt = s // tm
    n_sets = 2 * B_HEADS
    assert A_Q_HEADS == NORM_ROWS_A
    out_shape = (
        jax.ShapeDtypeStruct((A_Q_HEADS, 2 * HEAD_DIM, s), BF16),
        jax.ShapeDtypeStruct((s, A_KV_COLS), BF16),
        jax.ShapeDtypeStruct((A_KV_HEADS, nt, HEAD_DIM + ONES_ROWS, tm), BF16),
        jax.ShapeDtypeStruct((n_sets, 2 * HEAD_DIM, s), BF16),
        jax.ShapeDtypeStruct((s, B_QK_COLS), BF16),
        jax.ShapeDtypeStruct((B_HEADS, nt, B_V_DIM + ONES_ROWS, tm), BF16),
        jax.ShapeDtypeStruct((nt, NORM_ROWS, LANES), F32),
        jax.ShapeDtypeStruct((nt, NORM_ROWS, LANES), F32),
        jax.ShapeDtypeStruct((A_Q_HEADS, F8_COLS, s), F8),
        jax.ShapeDtypeStruct((A_KV_HEADS, s, F8_COLS), F8),
        jax.ShapeDtypeStruct((n_sets, F8_COLS, s), F8),
        jax.ShapeDtypeStruct((n_sets, s, F8_COLS), F8),
    )
    out_specs = (
        pl.BlockSpec((A_Q_HEADS, 2 * HEAD_DIM, tm), lambda i: (0, 0, i)),
        pl.BlockSpec((tm, A_KV_COLS), lambda i: (i, 0)),
        pl.BlockSpec((A_KV_HEADS, None, HEAD_DIM + ONES_ROWS, tm), lambda i: (0, i, 0, 0)),
        pl.BlockSpec((n_sets, 2 * HEAD_DIM, tm), lambda i: (0, 0, i)),
        pl.BlockSpec((tm, B_QK_COLS), lambda i: (i, 0)),
        pl.BlockSpec((B_HEADS, None, B_V_DIM + ONES_ROWS, tm), lambda i: (0, i, 0, 0)),
        pl.BlockSpec((None, NORM_ROWS, LANES), lambda i: (i, 0, 0)),
        pl.BlockSpec((None, NORM_ROWS, LANES), lambda i: (i, 0, 0)),
        pl.BlockSpec((A_Q_HEADS, F8_COLS, tm), lambda i: (0, 0, i)),
        pl.BlockSpec((A_KV_HEADS, tm, F8_COLS), lambda i: (0, i, 0)),
        pl.BlockSpec((n_sets, F8_COLS, tm), lambda i: (0, 0, i)),
        pl.BlockSpec((n_sets, tm, F8_COLS), lambda i: (0, i, 0)),
    )
    return pl.pallas_call(
        _mix_in_kernel,
        grid=(nt,),
        in_specs=[
            pl.BlockSpec((tm, d), lambda i: (i, 0)),
            _layer_vec(l, 1, d), _layer_vec(l, 3, d), _layer_vec(l, 4, d),
            _layer_mat(l, w_in_t, single_buffer=True),
            pl.BlockSpec((ROPE_HALF, tm), lambda i: (0, i)),
            pl.BlockSpec((ROPE_HALF, tm), lambda i: (0, i)),
            _layer_mat(l, qkg_t),
        ],
        out_specs=out_specs,
        out_shape=out_shape,
        compiler_params=_params("parallel"),
        name="mix_in",
    )(h, norm_g4, mod4, mod4, w_in_t, angr_t, angc_t, qkg_t)


def _softmax_tile(s, m_ref, alpha_ref, p_ref):
    tk, n = s.shape
    part = jnp.max(s.reshape(tk // BF16_SUBLANES, BF16_SUBLANES, n), axis=0)
    m_old = m_ref[...]
    m_new = jnp.maximum(m_old, jnp.max(part.astype(F32), axis=0, keepdims=True))
    alpha_ref[...] = jnp.exp2(m_old - m_new)
    m_ref[...] = m_new
    p_ref[...] = jnp.exp2(s - m_new.astype(BF16))


def _pipelined_sweep(prep, qk, softmax, pv, n_chunks, lo, n_pairs, last):
    ctx = prep(lo)
    for c in range(n_chunks):
        qk(lo, 0, c, ctx)

    def body(j, carry):
        a = lo + 2 * j
        prev = jnp.maximum(a - 1, 0)
        nxt = jnp.minimum(a + 2, last)
        ctx = prep(a + 1)
        for c in range(n_chunks):
            softmax(0, c)
            pv(prev, 1, c)
            qk(a + 1, 1, c, ctx)
        ctx = prep(nxt)
        for c in range(n_chunks):
            softmax(1, c)
            pv(a, 0, c)
            qk(nxt, 0, c, ctx)
        return carry

    lax.fori_loop(0, n_pairs, body, 0)
    for c in range(n_chunks):
        pv(last, 1, c)


def _direct_sweep(prep, qk_exp, pv, n_chunks, lo, n_tiles, last, unroll):
    def phase(kt_new, slot_new, kt_old):
        ctx = prep(kt_new)
        for c in range(n_chunks):
            qk_exp(kt_new, slot_new, c, ctx)
            pv(kt_old, 1 - slot_new, c)

    ctx = prep(lo)
    for c in range(n_chunks):
        qk_exp(lo, 0, c, ctx)

    def pairs(first, n_pairs):
        def body(j, carry):
            a = first + 2 * n_pairs * j
            for i in range(n_pairs):
                phase(a + 2 * i + 1, 1, a + 2 * i)
                phase(a + 2 * i + 2, 0, a + 2 * i + 1)
            return carry
        return body

    assert unroll[-1] == 1
    static = isinstance(n_tiles, int)
    left = (n_tiles - 1) // 2 if static else lax.shift_right_logical(n_tiles - 1, 1)
    first = lo
    for size in unroll:
        trips = left // size if static else lax.div(left, size)
        lax.fori_loop(0, trips, pairs(first, size), 0)
        first = first + 2 * size * trips
        left = left - size * trips

    def tail_odd():
        for c in range(n_chunks):
            pv(last, 0, c)
        return 0

    def tail_even():
        phase(last, 1, last - 1)
        for c in range(n_chunks):
            pv(last, 1, c)
        return 0

    if isinstance(n_tiles, int):
        (tail_odd if n_tiles % 2 else tail_even)()
    else:
        lax.cond(jnp.bitwise_and(n_tiles, 1) == 1, tail_odd, tail_even)


def _score_bound(qn_tile, kn_ref, q_rows, k_row_of_q):
    k_max = jnp.max(kn_ref[...], axis=0)
    row = lax.broadcasted_iota(jnp.int32, k_max.shape, 0)
    sel = q_rows(row)
    q2 = jnp.where(sel, qn_tile, 0.0)
    k2 = jnp.where(sel, k_row_of_q(row, k_max), 0.0)
    fp8_ok = jnp.logical_and(jnp.max(q2) * F8_Q_SCALE ** 2 <= F8_SAFE_MAX ** 2,
                             jnp.max(k2) * F8_K_SCALE ** 2 <= F8_SAFE_MAX ** 2)
    return jnp.max(jnp.sqrt(q2 * k2)), fp8_ok


GQA_CHUNKS = 8
GQA_UNROLL = (15, 1)


def _gqa_kernel(qt_ref, k_ref, vt_ref, qn_ref, kn_ref, qf_ref, kf_ref, o_ref, q_scr, qf_scr, s_buf, p_buf,
                alpha_buf, m_scr, acc_scr):
    g = pl.program_id(0)
    qi = pl.program_id(1)
    nk, tk, _ = k_ref.shape
    tq = qt_ref.shape[2]
    w = A_GROUP * tq // GQA_CHUNKS
    for c in range(GQA_CHUNKS):
        for col in range(c * w, (c + 1) * w, min(w, tq)):
            hh, j, n = col // tq, col % tq, min(w, tq)
            q_scr[c, :, col - c * w:col - c * w + n] = qt_ref[hh, :, j:j + n]
            qf_scr[c, :, col - c * w:col - c * w + n] = qf_ref[hh, :, j:j + n]
    acc_scr[...] = jnp.zeros_like(acc_scr)

    def qk(kt, slot, c, _):
        s_buf[slot, c] = jnp.dot(k_ref[kt], q_scr[c], preferred_element_type=F32).astype(BF16)

    def softmax(slot, c):
        _softmax_tile(s_buf[slot, c], m_scr.at[c], alpha_buf.at[slot, c], p_buf.at[slot, c])

    def pv(kt, slot, c):
        acc_scr[c] = alpha_buf[slot, c] * acc_scr[c] + jnp.dot(
            vt_ref[kt], p_buf[slot, c], preferred_element_type=F32)

    def qk_exp(kt, slot, c, _):
        s = jnp.dot(kf_ref[kt], qf_scr[c], preferred_element_type=F32)
        p_buf[slot, c] = jnp.exp2(s.astype(BF16) * F8_UNSCALE)

    def pv_plain(kt, slot, c):
        acc_scr[c] += jnp.dot(vt_ref[kt], p_buf[slot, c], preferred_element_type=F32)

    def kmax_of_group(row, k_max):
        return jnp.max(jnp.where(row == g, k_max, 0.0), axis=0, keepdims=True)

    bound, fp8_ok = _score_bound(
        qn_ref[lax.div(qi * tq, tk)], kn_ref,
        lambda row: jnp.logical_and(row >= g * A_GROUP, row < (g + 1) * A_GROUP), kmax_of_group)

    def direct():
        _direct_sweep(lambda kt: None, qk_exp, pv_plain, GQA_CHUNKS, 0, nk, nk - 1, GQA_UNROLL)
        return 0

    def online():
        m_scr[...] = jnp.full_like(m_scr, NEG_BIG)
        p_buf[1] = jnp.zeros(p_buf.shape[1:], BF16)
        alpha_buf[1] = jnp.ones(alpha_buf.shape[1:], F32)
        _pipelined_sweep(lambda kt: None, qk, softmax, pv, GQA_CHUNKS, 0, nk // 2, nk - 1)
        return 0

    lax.cond(jnp.logical_and(bound <= FAST_LOG2_LIMIT, fp8_ok), direct, online)
    cols = []
    for c in range(GQA_CHUNKS):
        acc = acc_scr[c]
        cols.append(acc[:HEAD_DIM] * (1.0 / acc[HEAD_DIM:HEAD_DIM + 1]))
    ot = jnp.concatenate(cols, axis=1)
    ot = jnp.concatenate([ot[:, hh * tq:(hh + 1) * tq] for hh in range(A_GROUP)], axis=0)
    o_ref[...] = ot.T.astype(BF16)


def _gqa(qta, ka3, vta, qn, kn, qfa, kfa4):
    s = qta.shape[2]
    nk, tk, _ = ka3.shape
    tq = GQA_Q_TILE
    assert nk % 2 == 0 and tk % tq == 0 and tk == MIX_ROWS
    nc = GQA_CHUNKS
    w = A_GROUP * tq // nc
    v_rows = vta.shape[2]
    return pl.pallas_call(
        _gqa_kernel,
        grid=(A_KV_HEADS, s // tq),
        in_specs=[
            pl.BlockSpec((A_GROUP, 2 * HEAD_DIM, tq), lambda g, i: (g, 0, i)),
            pl.BlockSpec((nk, tk, A_KV_COLS), lambda g, i: (0, 0, 0), pipeline_mode=pl.Buffered(1)),
            pl.BlockSpec((None, nk, v_rows, tk), lambda g, i: (g, 0, 0, 0), pipeline_mode=pl.Buffered(1)),
            pl.BlockSpec(qn.shape, lambda g, i: (0, 0, 0)),
            pl.BlockSpec(kn.shape, lambda g, i: (0, 0, 0)),
            pl.BlockSpec((A_GROUP, F8_COLS, tq), lambda g, i: (g, 0, i)),
            pl.BlockSpec((None, nk, tk, F8_COLS), lambda g, i: (g, 0, 0, 0), pipeline_mode=pl.Buffered(1)),
        ],
        out_specs=pl.BlockSpec((tq, A_GROUP * HEAD_DIM), lambda g, i: (i, g)),
        out_shape=jax.ShapeDtypeStruct((s, A_Q_COLS), BF16),
        scratch_shapes=[
            pltpu.VMEM((nc, 2 * HEAD_DIM, w), BF16),
            pltpu.VMEM((nc, F8_COLS, w), F8),
            pltpu.VMEM((2, nc, tk, w), BF16),
            pltpu.VMEM((2, nc, tk, w), BF16),
            pltpu.VMEM((2, nc, 1, w), F32),
            pltpu.VMEM((nc, 1, w), F32),
            pltpu.VMEM((nc, v_rows, w), F32),
        ],
        compiler_params=_params("parallel", "parallel"),
        name="gqa_attn",
    )(qta, ka3, vta, qn, kn, qfa, kfa4)


DIFF_CHUNKS = 2
DIFF_UNROLL = (5, 3, 1)


def _diff_kernel(coef_ref, qt0_ref, qt1_ref, k0_ref, k1_ref, vt_ref, qn_ref, kn_ref, lamp_ref, sg_ref,
                 qf0_ref, qf1_ref, kf0_ref, kf1_ref,
                 o_ref, rel_scr, s_buf, p_buf, alpha_buf, m_scr, acc_scr, *, lam_init):
    hd = pl.program_id(0)
    qi = pl.program_id(1)
    nk, tk, _ = k0_ref.shape
    tq = qt0_ref.shape[1]
    slope_f8 = coef_ref[0, hd]
    inv_tile_drop = coef_ref[1, hd]
    acc_scr[...] = jnp.zeros_like(acc_scr)

    @pl.when(qi == 0)
    def _():
        rel_scr[...] = (lax.broadcasted_iota(jnp.int32, (tk, tq), 1)
                        - lax.broadcasted_iota(jnp.int32, (tk, tq), 0)).astype(F32) * slope_f8

    bound, fp8_ok = _score_bound(
        qn_ref[qi], kn_ref,
        lambda row: jnp.logical_or(row == NORM_ROWS_A + hd, row == NORM_ROWS_A + B_HEADS + hd),
        lambda row, k_max: k_max)
    use_direct = jnp.logical_and(bound <= FAST_LOG2_LIMIT, fp8_ok)
    drop = 2.0 * bound + SKIP_LOG2
    reach = jnp.minimum(jnp.floor(drop * inv_tile_drop) + 1.0, float(nk)).astype(jnp.int32)
    lo = jnp.maximum(qi - reach, 0)
    hi = jnp.minimum(qi + reach, nk - 1)

    q0 = qi * tq
    w = tq // DIFF_CHUNKS
    n_chunks = 2 * DIFF_CHUNKS
    cols = [slice((c % DIFF_CHUNKS) * w, (c % DIFF_CHUNKS + 1) * w) for c in range(n_chunks)]
    qts = (qt0_ref, qt1_ref)
    ks = (k0_ref, k1_ref)
    qfs = (qf0_ref, qf1_ref)
    kfs = (kf0_ref, kf1_ref)

    def penalty_scaled(kt):
        return jnp.abs(rel_scr[...] + (q0 - kt * tk).astype(F32) * slope_f8)

    def penalty(kt):
        return penalty_scaled(kt) * F8_UNSCALE

    def qk(kt, slot, c, pen):
        mp = c // DIFF_CHUNKS
        s = jnp.dot(ks[mp][kt], qts[mp][:, cols[c]], preferred_element_type=F32) - pen[:, cols[c]]
        s_buf[slot, c] = s.astype(BF16)

    def softmax(slot, c):
        _softmax_tile(s_buf[slot, c], m_scr.at[c], alpha_buf.at[slot, c], p_buf.at[slot, c])

    def pv(kt, slot, c):
        acc_scr[c] = alpha_buf[slot, c] * acc_scr[c] + jnp.dot(
            vt_ref[kt], p_buf[slot, c], preferred_element_type=F32)

    def qk_exp(kt, slot, c, pen_scaled):
        mp = c // DIFF_CHUNKS
        s = jnp.dot(kfs[mp][kt], qfs[mp][:, cols[c]], preferred_element_type=F32) - pen_scaled[:, cols[c]]
        p_buf[slot, c] = jnp.exp2(s.astype(BF16) * F8_UNSCALE)

    def pv_plain(kt, slot, c):
        acc_scr[c] += jnp.dot(vt_ref[kt], p_buf[slot, c], preferred_element_type=F32)

    def direct_all():
        _direct_sweep(penalty_scaled, qk_exp, pv_plain, n_chunks, 0, nk, nk - 1, ((nk - 1) // 2, 1))
        return 0

    def direct_some():
        _direct_sweep(penalty_scaled, qk_exp, pv_plain, n_chunks, lo, hi - lo + 1, hi, DIFF_UNROLL)
        return 0

    def direct():
        return lax.cond(hi - lo + 1 == nk, direct_all, direct_some)

    def online():
        m_scr[...] = jnp.full_like(m_scr, NEG_BIG)
        p_buf[1] = jnp.zeros(p_buf.shape[1:], BF16)
        alpha_buf[1] = jnp.ones(alpha_buf.shape[1:], F32)
        odd = jnp.bitwise_and(hi - lo + 1, 1)
        grow_hi = jnp.where(hi < nk - 1, odd, 0)
        hi2 = hi + grow_hi
        lo2 = lo - (odd - grow_hi)
        _pipelined_sweep(penalty, qk, softmax, pv, n_chunks, lo2, lax.shift_right_logical(hi2 - lo2 + 1, 1), hi2)
        return 0

    lax.cond(use_direct, direct, online)

    lp = lamp_ref[...]
    lam = (jnp.exp(jnp.sum(lp[0:1] * lp[1:2], axis=1, keepdims=True))
           - jnp.exp(jnp.sum(lp[2:3] * lp[3:4], axis=1, keepdims=True)) + lam_init)
    a0, a1 = [jnp.concatenate([acc_scr[mp * DIFF_CHUNKS + j] for j in range(DIFF_CHUNKS)], axis=1)
              for mp in range(2)]
    ot = (a0[:B_V_DIM] * (1.0 / a0[B_V_DIM:B_V_DIM + 1])
          - lam * (a1[:B_V_DIM] * (1.0 / a1[B_V_DIM:B_V_DIM + 1])))
    ot = ot * lax.rsqrt(jnp.mean(ot * ot, axis=0, keepdims=True) + EPS)
    ot = ot * sg_ref[...] * (1.0 - lam_init)
    o_ref[...] = ot.T.astype(BF16)


def _diff(l, coef, qtd, kd3, vtd, qn, kn, qfd, kfd4, lam_p, subln_col, *, lam_init):
    s = qtd.shape[2]
    nk, tk, _ = kd3.shape
    tq = DIFF_Q_TILE
    assert nk % 2 == 0 and tq == tk == MIX_ROWS
    lanes = 2 * HEAD_DIM
    v_rows = vtd.shape[2]
    nc = 2 * DIFF_CHUNKS
    w = tq // DIFF_CHUNKS
    return pl.pallas_call(
        functools.partial(_diff_kernel, lam_init=lam_init),
        grid=(B_HEADS, s // tq),
        in_specs=[
            pl.BlockSpec(memory_space=pltpu.SMEM),
            pl.BlockSpec((None, lanes, tq), lambda h, i: (h, 0, i)),
            pl.BlockSpec((None, lanes, tq), lambda h, i: (B_HEADS + h, 0, i)),
            pl.BlockSpec((nk, tk, lanes), lambda h, i: (0, 0, h // 2), pipeline_mode=pl.Buffered(1)),
            pl.BlockSpec((nk, tk, lanes), lambda h, i: (0, 0, B_HEADS // 2 + h // 2),
                         pipeline_mode=pl.Buffered(1)),
            pl.BlockSpec((None, nk, v_rows, tk), lambda h, i: (h, 0, 0, 0), pipeline_mode=pl.Buffered(1)),
            pl.BlockSpec(qn.shape, lambda h, i: (0, 0, 0)),
            pl.BlockSpec(kn.shape, lambda h, i: (0, 0, 0)),
            _layer_mat(l, lam_p),
            _layer_mat(l, subln_col),
            pl.BlockSpec((None, F8_COLS, tq), lambda h, i: (h, 0, i)),
            pl.BlockSpec((None, F8_COLS, tq), lambda h, i: (B_HEADS + h, 0, i)),
            pl.BlockSpec((None, nk, tk, F8_COLS), lambda h, i: (h, 0, 0, 0), pipeline_mode=pl.Buffered(1)),
            pl.BlockSpec((None, nk, tk, F8_COLS), lambda h, i: (B_HEADS + h, 0, 0, 0),
                         pipeline_mode=pl.Buffered(1)),
        ],
        out_specs=pl.BlockSpec((tq, B_V_DIM), lambda h, i: (i, h)),
        out_shape=jax.ShapeDtypeStruct((s, B_V_COLS), BF16),
        scratch_shapes=[
            pltpu.VMEM((tk, tq), F32),
            pltpu.VMEM((2, nc, tk, w), BF16),
            pltpu.VMEM((2, nc, tk, w), BF16),
            pltpu.VMEM((2, nc, 1, w), F32),
            pltpu.VMEM((nc, 1, w), F32),
            pltpu.VMEM((nc, v_rows, w), F32),
        ],
        compiler_params=_params("arbitrary", "arbitrary"),
        name="diff_attn",
    )(coef, qtd, qtd, kd3, kd3, vtd, qn, kn, lam_p, subln_col, qfd, qfd, kfd4, kfd4)


def _mix_out_kernel(h_ref, ng_ref, sh_ref, sc_ref, gt_ref, oa_ref, od_ref, wba_ref, wbb_ref,
                    wgate_ref, bgate_ref, wo_ref, o_ref):
    h = h_ref[...]
    d = h.shape[1]
    n = _rms_rows(h) * ng_ref[...]
    n = (n * (1.0 + sc_ref[...]) + sh_ref[...]).astype(BF16)
    z = jnp.dot(n, wgate_ref[...], preferred_element_type=F32) + bgate_ref[...]
    g = 1.0 / (1.0 + jnp.exp(-z))
    ya = jnp.dot(oa_ref[...], wba_ref[...], preferred_element_type=F32)
    yb = jnp.dot(od_ref[...], wbb_ref[...], preferred_element_type=F32)
    mix = (g[:, :d] * ya + g[:, d:] * yb).astype(BF16)
    y = jnp.dot(mix, wo_ref[...], preferred_element_type=F32)
    o_ref[...] = h + gt_ref[...] * y


def _mix_out(h, l, norm_g4, mod4, oa, od, w_ba, w_bb, w_gate, b_gate, w_o):
    s, d = h.shape
    tm = MIX_ROWS
    return pl.pallas_call(
        _mix_out_kernel,
        grid=(s // tm,),
        in_specs=[
            pl.BlockSpec((tm, d), lambda i: (i, 0)),
            _layer_vec(l, 1, d), _layer_vec(l, 3, d), _layer_vec(l, 4, d), _layer_vec(l, 5, d),
            pl.BlockSpec((tm, oa.shape[1]), lambda i: (i, 0)),
            pl.BlockSpec((tm, od.shape[1]), lambda i: (i, 0)),
            _layer_mat(l, w_ba, single_buffer=True), _layer_mat(l, w_bb, single_buffer=True),
            _layer_mat(l, w_gate, single_buffer=True), _layer_mat(l, b_gate),
            _layer_mat(l, w_o, single_buffer=True),
        ],
        out_specs=pl.BlockSpec((tm, d), lambda i: (i, 0)),
        out_shape=jax.ShapeDtypeStruct((s, d), F32),
        compiler_params=_params("parallel"),
        name="mix_out",
    )(h, norm_g4, mod4, mod4, mod4, oa, od, w_ba, w_bb, w_gate, b_gate, w_o)


def _axial_angles_t(seq):
    rows = seq // GRID_W
    row = jnp.broadcast_to(jnp.arange(rows)[:, None], (rows, GRID_W)).reshape(seq) - rows // 2
    col = jnp.broadcast_to(jnp.arange(GRID_W)[None, :], (rows, GRID_W)).reshape(seq) - GRID_W // 2
    inv = 1.0 / (ROPE_THETA ** (jnp.arange(0, ROPE_AXIS_DIM, 2, dtype=F32) / ROPE_AXIS_DIM))
    return inv[:, None] * row.astype(F32)[None, :], inv[:, None] * col.astype(F32)[None, :]


def kernel(x, c, ada_w, ada_b, norm_g, ffn_wg, ffn_wu, ffn_wd, w_in, qk_g, lam_p, subln_g, w_ba, w_bb,
           w_gate, b_gate, w_o, final_g):
    batch, s, d = x.shape
    assert batch == 1 and s % KEY_TILE == 0 and MIX_ROWS == KEY_TILE
    depth = ada_w.shape[0]
    h = x.reshape(s, d)

    mod4 = _ada_mod(c.reshape(d, 1), ada_w, ada_b).reshape(depth, N_ADA, 1, d)
    norm_g4 = norm_g.reshape(depth, 3, 1, d)
    angr_t, angc_t = _axial_angles_t(s)
    slopes = 2.0 ** (-8.0 * jnp.arange(1, B_HEADS + 1, dtype=F32) / B_HEADS)
    coef = jnp.stack([slopes * (LOG2E / F8_UNSCALE), 1.0 / (slopes * LOG2E * KEY_TILE)])
    fg = final_g.reshape(1, d)
    wg, wu, wd = ffn_wg.astype(BF16), ffn_wu.astype(BF16), ffn_wd.astype(BF16)
    w_in_t = jnp.swapaxes(w_in, 1, 2).astype(BF16)
    qkg_t = jnp.swapaxes(qk_g, 1, 2)
    w_ba16, w_bb16, w_gate16, w_o16 = (w.astype(BF16) for w in (w_ba, w_bb, w_gate, w_o))
    b_gate3 = b_gate.reshape(depth, 1, -1)
    subln_col = subln_g.reshape(depth, B_V_DIM, 1)
    nk = s // KEY_TILE

    for l in range(depth):
        lam_init = 0.8 - 0.6 * math.exp(-0.3 * l)
        h = _ffn(h, l, 0, norm_g4, mod4, wg, wu, wd, fg, final_norm=False)
        qta, ka, vta, qtd, kd, vtd, qn, kn, qfa, kfa, qfd, kfd = _mix_in(
            h, l, norm_g4, mod4, w_in_t, angr_t, angc_t, qkg_t)
        oa = _gqa(qta, ka.reshape(nk, KEY_TILE, A_KV_COLS), vta, qn, kn,
                  qfa, kfa.reshape(A_KV_HEADS, nk, KEY_TILE, F8_COLS))
        od = _diff(l, coef, qtd, kd.reshape(nk, KEY_TILE, B_QK_COLS), vtd, qn, kn,
                   qfd, kfd.reshape(2 * B_HEADS, nk, KEY_TILE, F8_COLS), lam_p, subln_col, lam_init=lam_init)
        h = _mix_out(h, l, norm_g4, mod4, oa, od, w_ba16, w_bb16, w_gate16, b_gate3, w_o16)
        h = _ffn(h, l, 1, norm_g4, mod4, wg, wu, wd, fg, final_norm=(l == depth - 1))
    return h.reshape(batch, s, d)
```

```python
import functools
import math

import jax
import jax.numpy as jnp
from jax import lax
from jax.experimental import pallas as pl
from jax.experimental.pallas import tpu as pltpu

F32 = jnp.float32
BF16 = jnp.bfloat16
F8 = jnp.float8_e4m3fn

GRID_W = 64
HEAD_DIM = 64
A_Q_HEADS = 8
A_KV_HEADS = 2
A_GROUP = A_Q_HEADS // A_KV_HEADS
B_HEADS = 4
B_V_DIM = 2 * HEAD_DIM
A_Q_COLS = A_Q_HEADS * HEAD_DIM
A_KV_COLS = A_KV_HEADS * HEAD_DIM
B_QK_COLS = 2 * B_HEADS * HEAD_DIM
B_V_COLS = B_HEADS * B_V_DIM
N_ADA = 9
EPS = 1e-6
ROPE_THETA = 10000.0
ROPE_AXIS_DIM = HEAD_DIM // 2
ROPE_HALF = ROPE_AXIS_DIM // 2

LOG2E = math.log2(math.e)
QK_SCALE = HEAD_DIM ** -0.5
NEG_BIG = -1e30

BF16_SUBLANES = 16
LANES = 128
ONES_ROWS = BF16_SUBLANES
SKIP_LOG2 = 64.0
FAST_LOG2_LIMIT = 60.0
F8_Q_SCALE = 2.0 ** 6
F8_K_SCALE = 2.0 ** 2
F8_UNSCALE = 1.0 / (F8_Q_SCALE * F8_K_SCALE)
F8_SAFE_MAX = 256.0
F8_COLS = 4 * HEAD_DIM
NORM_ROWS_A = 8
NORM_ROWS = NORM_ROWS_A + 2 * B_HEADS

VMEM_LIMIT_BYTES = 56 * 1024 * 1024

FFN_ROWS = 512
MIX_ROWS = 512
KEY_TILE = 512
GQA_Q_TILE = 512
DIFF_Q_TILE = 512
ADA_COLS = 1152


def _params(*sem):
    return pltpu.CompilerParams(dimension_semantics=sem, vmem_limit_bytes=VMEM_LIMIT_BYTES)


def _layer_vec(l, j, d):
    return pl.BlockSpec((None, None, 1, d), lambda *_: (l, j, 0, 0))


def _layer_mat(l, a, *, single_buffer=False):
    nd = a.ndim - 1
    mode = dict(pipeline_mode=pl.Buffered(1)) if single_buffer else {}
    return pl.BlockSpec((None,) + a.shape[1:], lambda *_: (l,) + (0,) * nd, **mode)


def _rms_rows(x):
    return x * lax.rsqrt(jnp.mean(x * x, axis=-1, keepdims=True) + EPS)


def _ada_kernel(c_ref, w_ref, b_ref, o_ref):
    c = c_ref[...]
    act = c / (1.0 + jnp.exp(-c))
    o_ref[...] = jnp.sum(w_ref[...] * act, axis=0, keepdims=True) + b_ref[...]


def _ada_mod(c_col, ada_w, ada_b):
    n_layers, d, n = ada_w.shape
    return pl.pallas_call(
        _ada_kernel,
        grid=(n_layers, n // ADA_COLS),
        in_specs=[
            pl.BlockSpec((d, 1), lambda l, j: (0, 0)),
            pl.BlockSpec((None, d, ADA_COLS), lambda l, j: (l, 0, j)),
            pl.BlockSpec((None, 1, ADA_COLS), lambda l, j: (l, 0, j)),
        ],
        out_specs=pl.BlockSpec((None, 1, ADA_COLS), lambda l, j: (l, 0, j)),
        out_shape=jax.ShapeDtypeStruct((n_layers, 1, n), F32),
        compiler_params=_params("parallel", "parallel"),
        name="ada_mod",
    )(c_col, ada_w, ada_b.reshape(n_layers, 1, n))


def _ffn_kernel(h_ref, ng_ref, sh_ref, sc_ref, gt_ref, wg_ref, wu_ref, wd_ref, fg_ref, o_ref, *, final_norm):
    h = h_ref[...]
    n = _rms_rows(h) * ng_ref[...]
    n = (n * (1.0 + sc_ref[...]) + sh_ref[...]).astype(BF16)
    hg = jnp.dot(n, wg_ref[...], preferred_element_type=F32)
    hu = jnp.dot(n, wu_ref[...], preferred_element_type=F32)
    a = (hg / (1.0 + jnp.exp(-hg))) * hu
    out = h + (0.5 * gt_ref[...]) * jnp.dot(a.astype(BF16), wd_ref[...], preferred_element_type=F32)
    if final_norm:
        out = _rms_rows(out) * fg_ref[...]
    o_ref[...] = out


def _ffn(h, l, which, norm_g4, mod4, wg, wu, wd, fg, *, final_norm):
    s, d = h.shape
    m0 = 6 * which

    def resident(a):
        return pl.BlockSpec((None, None) + a.shape[2:], lambda i: (l, which, 0, 0),
                            pipeline_mode=pl.Buffered(1))

    return pl.pallas_call(
        functools.partial(_ffn_kernel, final_norm=final_norm),
        grid=(s // FFN_ROWS,),
        in_specs=[
            pl.BlockSpec((FFN_ROWS, d), lambda i: (i, 0)),
            _layer_vec(l, 2 * which, d),
            _layer_vec(l, m0, d), _layer_vec(l, m0 + 1, d), _layer_vec(l, m0 + 2, d),
            resident(wg), resident(wu), resident(wd),
            pl.BlockSpec((1, d), lambda i: (0, 0)),
        ],
        out_specs=pl.BlockSpec((FFN_ROWS, d), lambda i: (i, 0)),
        out_shape=jax.ShapeDtypeStruct((s, d), F32),
        compiler_params=_params("parallel"),
        name="ffn_final" if final_norm else "ffn",
    )(h, norm_g4, mod4, mod4, mod4, wg, wu, wd, fg)


def _rope_t(x, cr, sr, cc, sc):
    h = ROPE_HALF
    x1r, x2r, x1c, x2c = x[0:h], x[h:2 * h], x[2 * h:3 * h], x[3 * h:4 * h]
    return jnp.concatenate(
        [x1r * cr - x2r * sr, x2r * cr + x1r * sr, x1c * cc - x2c * sc, x2c * cc + x1c * sc], axis=0)


def _head_norm_t(x, g_col):
    ms = jnp.mean(x * x, axis=0, keepdims=True)
    return x * lax.rsqrt(ms + EPS) * g_col


def _max_sq_norm(x_bf16):
    xf = x_bf16.astype(F32)
    n2 = jnp.sum(xf * xf, axis=0, keepdims=True)
    return jnp.broadcast_to(jnp.max(n2, axis=1, keepdims=True), (1, LANES))


def _f8_split(x):
    hi = x.astype(F8).astype(F32)
    return hi, ((x - hi) * 16.0).astype(F8).astype(F32)


def _f8_query_rows(q_bf16, *, weighted):
    hi, lo16 = _f8_split(q_bf16.astype(F32) * F8_Q_SCALE)
    rows = [hi, hi * 0.0625, lo16 * 0.0625, lo16 * 0.00390625] if weighted else [hi, hi, lo16, lo16]
    return jnp.concatenate(rows, axis=0).astype(F8)


def _f8_key_rows(k_bf16, *, weighted):
    hi, lo16 = _f8_split(k_bf16.astype(F32) * F8_K_SCALE)
    rows = [hi, lo16 * 0.0625, hi * 0.0625, lo16 * 0.00390625] if weighted else [hi, lo16, hi, lo16]
    return jnp.concatenate(rows, axis=0)


def _with_ones(v_t):
    return jnp.concatenate([v_t, jnp.ones((ONES_ROWS, v_t.shape[1]), F32)], axis=0).astype(BF16)


def _mix_in_kernel(h_ref, ng_ref, sh_ref, sc_ref, wt_ref, angr_ref, angc_ref, qkg_ref,
                   qta_ref, ka_ref, vta_ref, qtd_ref, kd_ref, vtd_ref, qn_ref, kn_ref,
                   qfa_ref, kfa_ref, qfd_ref, kfd_ref):
    n = _rms_rows(h_ref[...]) * ng_ref[...]
    n = (n * (1.0 + sc_ref[...]) + sh_ref[...]).astype(BF16)
    pt = lax.dot_general(wt_ref[...], n, (((1,), (1,)), ((), ())), preferred_element_type=F32)
    t = pt.shape[1]

    cr, sr = jnp.cos(angr_ref[...]), jnp.sin(angr_ref[...])
    cc, sc = jnp.cos(angc_ref[...]), jnp.sin(angc_ref[...])
    gq = qkg_ref[:, 0:1]
    gk = qkg_ref[:, 1:2]
    zeros = jnp.zeros((HEAD_DIM, t), BF16)
    qscale = QK_SCALE * LOG2E

    qn, kn = [], []
    for hd in range(A_Q_HEADS):
        q = _rope_t(_head_norm_t(pt[hd * HEAD_DIM:(hd + 1) * HEAD_DIM], gq), cr, sr, cc, sc) * qscale
        q = q.astype(BF16)
        qta_ref[hd] = jnp.concatenate([q, zeros] if hd // A_GROUP == 0 else [zeros, q], axis=0)
        qfa_ref[hd] = _f8_query_rows(q, weighted=False)
        qn.append(_max_sq_norm(q))

    c1 = A_Q_COLS
    c2 = c1 + A_KV_COLS
    c3 = c2 + A_KV_COLS
    c4 = c3 + B_QK_COLS
    c5 = c4 + B_QK_COLS
    kt = jnp.concatenate(
        [_rope_t(_head_norm_t(pt[c1 + j * HEAD_DIM:c1 + (j + 1) * HEAD_DIM], gk), cr, sr, cc, sc)
         for j in range(A_KV_HEADS)], axis=0)
    kt = kt.astype(BF16)
    ka_ref[...] = kt.T
    kn += [_max_sq_norm(kt[j * HEAD_DIM:(j + 1) * HEAD_DIM]) for j in range(A_KV_HEADS)]
    for j in range(A_KV_HEADS):
        kfa_ref[j] = _f8_key_rows(kt[j * HEAD_DIM:(j + 1) * HEAD_DIM], weighted=True).astype(BF16).T.astype(F8)
    kn.append(jnp.zeros((NORM_ROWS_A - A_KV_HEADS, LANES), F32))
    for j in range(A_KV_HEADS):
        vta_ref[j] = _with_ones(pt[c2 + j * HEAD_DIM:c2 + (j + 1) * HEAD_DIM])

    for cmb in range(2 * B_HEADS):
        q = (pt[c3 + cmb * HEAD_DIM:c3 + (cmb + 1) * HEAD_DIM] * qscale).astype(BF16)
        qtd_ref[cmb] = jnp.concatenate([q, zeros] if cmb % 2 == 0 else [zeros, q], axis=0)
        qfd_ref[cmb] = _f8_query_rows(q, weighted=True)
        qn.append(_max_sq_norm(q))
        k = pt[c4 + cmb * HEAD_DIM:c4 + (cmb + 1) * HEAD_DIM].astype(BF16)
        kfd_ref[cmb] = _f8_key_rows(k, weighted=False).astype(BF16).T.astype(F8)
        kn.append(_max_sq_norm(k))
    qn_ref[...] = jnp.concatenate(qn, axis=0)
    kn_ref[...] = jnp.concatenate(kn, axis=0)
    kd_ref[...] = pt[c4:c5].T.astype(BF16)
    for hd in range(B_HEADS):
        vtd_ref[hd] = _with_ones(pt[c5 + hd * B_V_DIM:c5 + (hd + 1) * B_V_DIM])


def _mix_in(h, l, norm_g4, mod4, w_in_t, angr_t, angc_t, qkg_t):
    s, d = h.shape
    tm = MIX_ROWS
    nt = s // tm
    n_sets = 2 * B_HEADS
    assert A_Q_HEADS == NORM_ROWS_A
    out_shape = (
        jax.ShapeDtypeStruct((A_Q_HEADS, 2 * HEAD_DIM, s), BF16),
        jax.ShapeDtypeStruct((s, A_KV_COLS), BF16),
        jax.ShapeDtypeStruct((A_KV_HEADS, nt, HEAD_DIM + ONES_ROWS, tm), BF16),
        jax.ShapeDtypeStruct((n_sets, 2 * HEAD_DIM, s), BF16),
        jax.ShapeDtypeStruct((s, B_QK_COLS), BF16),
        jax.ShapeDtypeStruct((B_HEADS, nt, B_V_DIM + ONES_ROWS, tm), BF16),
        jax.ShapeDtypeStruct((nt, NORM_ROWS, LANES), F32),
        jax.ShapeDtypeStruct((nt, NORM_ROWS, LANES), F32),
        jax.ShapeDtypeStruct((A_Q_HEADS, F8_COLS, s), F8),
        jax.ShapeDtypeStruct((A_KV_HEADS, s, F8_COLS), F8),
        jax.ShapeDtypeStruct((n_sets, F8_COLS, s), F8),
        jax.ShapeDtypeStruct((n_sets, s, F8_COLS), F8),
    )
    out_specs = (
        pl.BlockSpec((A_Q_HEADS, 2 * HEAD_DIM, tm), lambda i: (0, 0, i)),
        pl.BlockSpec((tm, A_KV_COLS), lambda i: (i, 0)),
        pl.BlockSpec((A_KV_HEADS, None, HEAD_DIM + ONES_ROWS, tm), lambda i: (0, i, 0, 0)),
        pl.BlockSpec((n_sets, 2 * HEAD_DIM, tm), lambda i: (0, 0, i)),
        pl.BlockSpec((tm, B_QK_COLS), lambda i: (i, 0)),
        pl.BlockSpec((B_HEADS, None, B_V_DIM + ONES_ROWS, tm), lambda i: (0, i, 0, 0)),
        pl.BlockSpec((None, NORM_ROWS, LANES), lambda i: (i, 0, 0)),
        pl.BlockSpec((None, NORM_ROWS, LANES), lambda i: (i, 0, 0)),
        pl.BlockSpec((A_Q_HEADS, F8_COLS, tm), lambda i: (0, 0, i)),
        pl.BlockSpec((A_KV_HEADS, tm, F8_COLS), lambda i: (0, i, 0)),
        pl.BlockSpec((n_sets, F8_COLS, tm), lambda i: (0, 0, i)),
        pl.BlockSpec((n_sets, tm, F8_COLS), lambda i: (0, i, 0)),
    )
    return pl.pallas_call(
        _mix_in_kernel,
        grid=(nt,),
        in_specs=[
            pl.BlockSpec((tm, d), lambda i: (i, 0)),
            _layer_vec(l, 1, d), _layer_vec(l, 3, d), _layer_vec(l, 4, d),
            _layer_mat(l, w_in_t, single_buffer=True),
            pl.BlockSpec((ROPE_HALF, tm), lambda i: (0, i)),
            pl.BlockSpec((ROPE_HALF, tm), lambda i: (0, i)),
            _layer_mat(l, qkg_t),
        ],
        out_specs=out_specs,
        out_shape=out_shape,
        compiler_params=_params("parallel"),
        name="mix_in",
    )(h, norm_g4, mod4, mod4, w_in_t, angr_t, angc_t, qkg_t)


def _softmax_tile(s, m_ref, alpha_ref, p_ref):
    tk, n = s.shape
    part = jnp.max(s.reshape(tk // BF16_SUBLANES, BF16_SUBLANES, n), axis=0)
    m_old = m_ref[...]
    m_new = jnp.maximum(m_old, jnp.max(part.astype(F32), axis=0, keepdims=True))
    alpha_ref[...] = jnp.exp2(m_old - m_new)
    m_ref[...] = m_new
    p_ref[...] = jnp.exp2(s - m_new.astype(BF16))


def _pipelined_sweep(prep, qk, softmax, pv, n_chunks, lo, n_pairs, last):
    ctx = prep(lo)
    for c in range(n_chunks):
        qk(lo, 0, c, ctx)

    def body(j, carry):
        a = lo + 2 * j
        prev = jnp.maximum(a - 1, 0)
        nxt = jnp.minimum(a + 2, last)
        ctx = prep(a + 1)
        for c in range(n_chunks):
            softmax(0, c)
            pv(prev, 1, c)
            qk(a + 1, 1, c, ctx)
        ctx = prep(nxt)
        for c in range(n_chunks):
            softmax(1, c)
            pv(a, 0, c)
            qk(nxt, 0, c, ctx)
        return carry

    lax.fori_loop(0, n_pairs, body, 0)
    for c in range(n_chunks):
        pv(last, 1, c)


def _direct_sweep(prep, qk_exp, pv, n_chunks, lo, n_tiles, last, unroll):
    def phase(kt_new, slot_new, kt_old):
        ctx = prep(kt_new)
        for c in range(n_chunks):
            qk_exp(kt_new, slot_new, c, ctx)
            pv(kt_old, 1 - slot_new, c)

    ctx = prep(lo)
    for c in range(n_chunks):
        qk_exp(lo, 0, c, ctx)

    def pairs(first, n_pairs):
        def body(j, carry):
            a = first + 2 * n_pairs * j
            for i in range(n_pairs):
                phase(a + 2 * i + 1, 1, a + 2 * i)
                phase(a + 2 * i + 2, 0, a + 2 * i + 1)
            return carry
        return body

    assert unroll[-1] == 1
    static = isinstance(n_tiles, int)
    left = (n_tiles - 1) // 2 if static else lax.shift_right_logical(n_tiles - 1, 1)
    first = lo
    for size in unroll:
        trips = left // size if static else lax.div(left, size)
        lax.fori_loop(0, trips, pairs(first, size), 0)
        first = first + 2 * size * trips
        left = left - size * trips

    def tail_odd():
        for c in range(n_chunks):
            pv(last, 0, c)
        return 0

    def tail_even():
        phase(last, 1, last - 1)
        for c in range(n_chunks):
            pv(last, 1, c)
        return 0

    if isinstance(n_tiles, int):
        (tail_odd if n_tiles % 2 else tail_even)()
    else:
        lax.cond(jnp.bitwise_and(n_tiles, 1) == 1, tail_odd, tail_even)


def _score_bound(qn_tile, kn_ref, q_rows, k_row_of_q):
    k_max = jnp.max(kn_ref[...], axis=0)
    row = lax.broadcasted_iota(jnp.int32, k_max.shape, 0)
    sel = q_rows(row)
    q2 = jnp.where(sel, qn_tile, 0.0)
    k2 = jnp.where(sel, k_row_of_q(row, k_max), 0.0)
    fp8_ok = jnp.logical_and(jnp.max(q2) * F8_Q_SCALE ** 2 <= F8_SAFE_MAX ** 2,
                             jnp.max(k2) * F8_K_SCALE ** 2 <= F8_SAFE_MAX ** 2)
    return jnp.max(jnp.sqrt(q2 * k2)), fp8_ok


GQA_CHUNKS = 8
GQA_UNROLL = (15, 1)


def _gqa_kernel(qt_ref, k_ref, vt_ref, qn_ref, kn_ref, qf_ref, kf_ref, o_ref, q_scr, qf_scr, s_buf, p_buf,
                alpha_buf, m_scr, acc_scr):
    g = pl.program_id(0)
    qi = pl.program_id(1)
    nk, tk, _ = k_ref.shape
    tq = qt_ref.shape[2]
    w = A_GROUP * tq // GQA_CHUNKS
    for c in range(GQA_CHUNKS):
        for col in range(c * w, (c + 1) * w, min(w, tq)):
            hh, j, n = col // tq, col % tq, min(w, tq)
            q_scr[c, :, col - c * w:col - c * w + n] = qt_ref[hh, :, j:j + n]
            qf_scr[c, :, col - c * w:col - c * w + n] = qf_ref[hh, :, j:j + n]
    acc_scr[...] = jnp.zeros_like(acc_scr)

    def qk(kt, slot, c, _):
        s_buf[slot, c] = jnp.dot(k_ref[kt], q_scr[c], preferred_element_type=F32).astype(BF16)

    def softmax(slot, c):
        _softmax_tile(s_buf[slot, c], m_scr.at[c], alpha_buf.at[slot, c], p_buf.at[slot, c])

    def pv(kt, slot, c):
        acc_scr[c] = alpha_buf[slot, c] * acc_scr[c] + jnp.dot(
            vt_ref[kt], p_buf[slot, c], preferred_element_type=F32)

    def qk_exp(kt, slot, c, _):
        s = jnp.dot(kf_ref[kt], qf_scr[c], preferred_element_type=F32)
        p_buf[slot, c] = jnp.exp2(s.astype(BF16) * F8_UNSCALE)

    def pv_plain(kt, slot, c):
        acc_scr[c] += jnp.dot(vt_ref[kt], p_buf[slot, c], preferred_element_type=F32)

    def kmax_of_group(row, k_max):
        return jnp.max(jnp.where(row == g, k_max, 0.0), axis=0, keepdims=True)

    bound, fp8_ok = _score_bound(
        qn_ref[lax.div(qi * tq, tk)], kn_ref,
        lambda row: jnp.logical_and(row >= g * A_GROUP, row < (g + 1) * A_GROUP), kmax_of_group)

    def direct():
        _direct_sweep(lambda kt: None, qk_exp, pv_plain, GQA_CHUNKS, 0, nk, nk - 1, GQA_UNROLL)
        return 0

    def online():
        m_scr[...] = jnp.full_like(m_scr, NEG_BIG)
        p_buf[1] = jnp.zeros(p_buf.shape[1:], BF16)
        alpha_buf[1] = jnp.ones(alpha_buf.shape[1:], F32)
        _pipelined_sweep(lambda kt: None, qk, softmax, pv, GQA_CHUNKS, 0, nk // 2, nk - 1)
        return 0

    lax.cond(jnp.logical_and(bound <= FAST_LOG2_LIMIT, fp8_ok), direct, online)
    cols = []
    for c in range(GQA_CHUNKS):
        acc = acc_scr[c]
        cols.append(acc[:HEAD_DIM] * (1.0 / acc[HEAD_DIM:HEAD_DIM + 1]))
    ot = jnp.concatenate(cols, axis=1)
    ot = jnp.concatenate([ot[:, hh * tq:(hh + 1) * tq] for hh in range(A_GROUP)], axis=0)
    o_ref[...] = ot.T.astype(BF16)


def _gqa(qta, ka3, vta, qn, kn, qfa, kfa4):
    s = qta.shape[2]
    nk, tk, _ = ka3.shape
    tq = GQA_Q_TILE
    assert nk % 2 == 0 and tk % tq == 0 and tk == MIX_ROWS
    nc = GQA_CHUNKS
    w = A_GROUP * tq // nc
    v_rows = vta.shape[2]
    return pl.pallas_call(
        _gqa_kernel,
        grid=(A_KV_HEADS, s // tq),
        in_specs=[
            pl.BlockSpec((A_GROUP, 2 * HEAD_DIM, tq), lambda g, i: (g, 0, i)),
            pl.BlockSpec((nk, tk, A_KV_COLS), lambda g, i: (0, 0, 0), pipeline_mode=pl.Buffered(1)),
            pl.BlockSpec((None, nk, v_rows, tk), lambda g, i: (g, 0, 0, 0), pipeline_mode=pl.Buffered(1)),
            pl.BlockSpec(qn.shape, lambda g, i: (0, 0, 0)),
            pl.BlockSpec(kn.shape, lambda g, i: (0, 0, 0)),
            pl.BlockSpec((A_GROUP, F8_COLS, tq), lambda g, i: (g, 0, i)),
            pl.BlockSpec((None, nk, tk, F8_COLS), lambda g, i: (g, 0, 0, 0), pipeline_mode=pl.Buffered(1)),
        ],
        out_specs=pl.BlockSpec((tq, A_GROUP * HEAD_DIM), lambda g, i: (i, g)),
        out_shape=jax.ShapeDtypeStruct((s, A_Q_COLS), BF16),
        scratch_shapes=[
            pltpu.VMEM((nc, 2 * HEAD_DIM, w), BF16),
            pltpu.VMEM((nc, F8_COLS, w), F8),
            pltpu.VMEM((2, nc, tk, w), BF16),
            pltpu.VMEM((2, nc, tk, w), BF16),
            pltpu.VMEM((2, nc, 1, w), F32),
            pltpu.VMEM((nc, 1, w), F32),
            pltpu.VMEM((nc, v_rows, w), F32),
        ],
        compiler_params=_params("parallel", "parallel"),
        name="gqa_attn",
    )(qta, ka3, vta, qn, kn, qfa, kfa4)


DIFF_CHUNKS = 2
DIFF_UNROLL = (5, 3, 1)


def _diff_kernel(coef_ref, qt0_ref, qt1_ref, k0_ref, k1_ref, vt_ref, qn_ref, kn_ref, lamp_ref, sg_ref,
                 qf0_ref, qf1_ref, kf0_ref, kf1_ref,
                 o_ref, rel_scr, s_buf, p_buf, alpha_buf, m_scr, acc_scr, *, lam_init):
    hd = pl.program_id(0)
    qi = pl.program_id(1)
    nk, tk, _ = k0_ref.shape
    tq = qt0_ref.shape[1]
    slope_f8 = coef_ref[0, hd]
    inv_tile_drop = coef_ref[1, hd]
    acc_scr[...] = jnp.zeros_like(acc_scr)

    @pl.when(qi == 0)
    def _():
        rel_scr[...] = (lax.broadcasted_iota(jnp.int32, (tk, tq), 1)
                        - lax.broadcasted_iota(jnp.int32, (tk, tq), 0)).astype(F32) * slope_f8

    bound, fp8_ok = _score_bound(
        qn_ref[qi], kn_ref,
        lambda row: jnp.logical_or(row == NORM_ROWS_A + hd, row == NORM_ROWS_A + B_HEADS + hd),
        lambda row, k_max: k_max)
    use_direct = jnp.logical_and(bound <= FAST_LOG2_LIMIT, fp8_ok)
    drop = 2.0 * bound + SKIP_LOG2
    reach = jnp.minimum(jnp.floor(drop * inv_tile_drop) + 1.0, float(nk)).astype(jnp.int32)
    lo = jnp.maximum(qi - reach, 0)
    hi = jnp.minimum(qi + reach, nk - 1)

    q0 = qi * tq
    w = tq // DIFF_CHUNKS
    n_chunks = 2 * DIFF_CHUNKS
    cols = [slice((c % DIFF_CHUNKS) * w, (c % DIFF_CHUNKS + 1) * w) for c in range(n_chunks)]
    qts = (qt0_ref, qt1_ref)
    ks = (k0_ref, k1_ref)
    qfs = (qf0_ref, qf1_ref)
    kfs = (kf0_ref, kf1_ref)

    def penalty_scaled(kt):
        return jnp.abs(rel_scr[...] + (q0 - kt * tk).astype(F32) * slope_f8)

    def penalty(kt):
        return penalty_scaled(kt) * F8_UNSCALE

    def qk(kt, slot, c, pen):
        mp = c // DIFF_CHUNKS
        s = jnp.dot(ks[mp][kt], qts[mp][:, cols[c]], preferred_element_type=F32) - pen[:, cols[c]]
        s_buf[slot, c] = s.astype(BF16)

    def softmax(slot, c):
        _softmax_tile(s_buf[slot, c], m_scr.at[c], alpha_buf.at[slot, c], p_buf.at[slot, c])

    def pv(kt, slot, c):
        acc_scr[c] = alpha_buf[slot, c] * acc_scr[c] + jnp.dot(
            vt_ref[kt], p_buf[slot, c], preferred_element_type=F32)

    def qk_exp(kt, slot, c, pen_scaled):
        mp = c // DIFF_CHUNKS
        s = jnp.dot(kfs[mp][kt], qfs[mp][:, cols[c]], preferred_element_type=F32) - pen_scaled[:, cols[c]]
        p_buf[slot, c] = jnp.exp2(s.astype(BF16) * F8_UNSCALE)

    def pv_plain(kt, slot, c):
        acc_scr[c] += jnp.dot(vt_ref[kt], p_buf[slot, c], preferred_element_type=F32)

    def direct_all():
        _direct_sweep(penalty_scaled, qk_exp, pv_plain, n_chunks, 0, nk, nk - 1, ((nk - 1) // 2, 1))
        return 0

    def direct_some():
        _direct_sweep(penalty_scaled, qk_exp, pv_plain, n_chunks, lo, hi - lo + 1, hi, DIFF_UNROLL)
        return 0

    def direct():
        return lax.cond(hi - lo + 1 == nk, direct_all, direct_some)

    def online():
        m_scr[...] = jnp.full_like(m_scr, NEG_BIG)
        p_buf[1] = jnp.zeros(p_buf.shape[1:], BF16)
        alpha_buf[1] = jnp.ones(alpha_buf.shape[1:], F32)
        odd = jnp.bitwise_and(hi - lo + 1, 1)
        grow_hi = jnp.where(hi < nk - 1, odd, 0)
        hi2 = hi + grow_hi
        lo2 = lo - (odd - grow_hi)
        _pipelined_sweep(penalty, qk, softmax, pv, n_chunks, lo2, lax.shift_right_logical(hi2 - lo2 + 1, 1), hi2)
        return 0

    lax.cond(use_direct, direct, online)

    lp = lamp_ref[...]
    lam = (jnp.exp(jnp.sum(lp[0:1] * lp[1:2], axis=1, keepdims=True))
           - jnp.exp(jnp.sum(lp[2:3] * lp[3:4], axis=1, keepdims=True)) + lam_init)
    a0, a1 = [jnp.concatenate([acc_scr[mp * DIFF_CHUNKS + j] for j in range(DIFF_CHUNKS)], axis=1)
              for mp in range(2)]
    ot = (a0[:B_V_DIM] * (1.0 / a0[B_V_DIM:B_V_DIM + 1])
          - lam * (a1[:B_V_DIM] * (1.0 / a1[B_V_DIM:B_V_DIM + 1])))
    ot = ot * lax.rsqrt(jnp.mean(ot * ot, axis=0, keepdims=True) + EPS)
    ot = ot * sg_ref[...] * (1.0 - lam_init)
    o_ref[...] = ot.T.astype(BF16)


def _diff(l, coef, qtd, kd3, vtd, qn, kn, qfd, kfd4, lam_p, subln_col, *, lam_init):
    s = qtd.shape[2]
    nk, tk, _ = kd3.shape
    tq = DIFF_Q_TILE
    assert nk % 2 == 0 and tq == tk == MIX_ROWS
    lanes = 2 * HEAD_DIM
    v_rows = vtd.shape[2]
    nc = 2 * DIFF_CHUNKS
    w = tq // DIFF_CHUNKS
    return pl.pallas_call(
        functools.partial(_diff_kernel, lam_init=lam_init),
        grid=(B_HEADS, s // tq),
        in_specs=[
            pl.BlockSpec(memory_space=pltpu.SMEM),
            pl.BlockSpec((None, lanes, tq), lambda h, i: (h, 0, i)),
            pl.BlockSpec((None, lanes, tq), lambda h, i: (B_HEADS + h, 0, i)),
            pl.BlockSpec((nk, tk, lanes), lambda h, i: (0, 0, h // 2), pipeline_mode=pl.Buffered(1)),
            pl.BlockSpec((nk, tk, lanes), lambda h, i: (0, 0, B_HEADS // 2 + h // 2),
                         pipeline_mode=pl.Buffered(1)),
            pl.BlockSpec((None, nk, v_rows, tk), lambda h, i: (h, 0, 0, 0), pipeline_mode=pl.Buffered(1)),
            pl.BlockSpec(qn.shape, lambda h, i: (0, 0, 0)),
            pl.BlockSpec(kn.shape, lambda h, i: (0, 0, 0)),
            _layer_mat(l, lam_p),
            _layer_mat(l, subln_col),
            pl.BlockSpec((None, F8_COLS, tq), lambda h, i: (h, 0, i)),
            pl.BlockSpec((None, F8_COLS, tq), lambda h, i: (B_HEADS + h, 0, i)),
            pl.BlockSpec((None, nk, tk, F8_COLS), lambda h, i: (h, 0, 0, 0), pipeline_mode=pl.Buffered(1)),
            pl.BlockSpec((None, nk, tk, F8_COLS), lambda h, i: (B_HEADS + h, 0, 0, 0),
                         pipeline_mode=pl.Buffered(1)),
        ],
        out_specs=pl.BlockSpec((tq, B_V_DIM), lambda h, i: (i, h)),
        out_shape=jax.ShapeDtypeStruct((s, B_V_COLS), BF16),
        scratch_shapes=[
            pltpu.VMEM((tk, tq), F32),
            pltpu.VMEM((2, nc, tk, w), BF16),
            pltpu.VMEM((2, nc, tk, w), BF16),
            pltpu.VMEM((2, nc, 1, w), F32),
            pltpu.VMEM((nc, 1, w), F32),
            pltpu.VMEM((nc, v_rows, w), F32),
        ],
        compiler_params=_params("arbitrary", "arbitrary"),
        name="diff_attn",
    )(coef, qtd, qtd, kd3, kd3, vtd, qn, kn, lam_p, subln_col, qfd, qfd, kfd4, kfd4)


def _mix_out_kernel(h_ref, ng_ref, sh_ref, sc_ref, gt_ref, oa_ref, od_ref, wba_ref, wbb_ref,
                    wgate_ref, bgate_ref, wo_ref, o_ref):
    h = h_ref[...]
    d = h.shape[1]
    n = _rms_rows(h) * ng_ref[...]
    n = (n * (1.0 + sc_ref[...]) + sh_ref[...]).astype(BF16)
    z = jnp.dot(n, wgate_ref[...], preferred_element_type=F32) + bgate_ref[...]
    g = 1.0 / (1.0 + jnp.exp(-z))
    ya = jnp.dot(oa_ref[...], wba_ref[...], preferred_element_type=F32)
    yb = jnp.dot(od_ref[...], wbb_ref[...], preferred_element_type=F32)
    mix = (g[:, :d] * ya + g[:, d:] * yb).astype(BF16)
    y = jnp.dot(mix, wo_ref[...], preferred_element_type=F32)
    o_ref[...] = h + gt_ref[...] * y


def _mix_out(h, l, norm_g4, mod4, oa, od, w_ba, w_bb, w_gate, b_gate, w_o):
    s, d = h.shape
    tm = MIX_ROWS
    return pl.pallas_call(
        _mix_out_kernel,
        grid=(s // tm,),
        in_specs=[
            pl.BlockSpec((tm, d), lambda i: (i, 0)),
            _layer_vec(l, 1, d), _layer_vec(l, 3, d), _layer_vec(l, 4, d), _layer_vec(l, 5, d),
            pl.BlockSpec((tm, oa.shape[1]), lambda i: (i, 0)),
            pl.BlockSpec((tm, od.shape[1]), lambda i: (i, 0)),
            _layer_mat(l, w_ba, single_buffer=True), _layer_mat(l, w_bb, single_buffer=True),
            _layer_mat(l, w_gate, single_buffer=True), _layer_mat(l, b_gate),
            _layer_mat(l, w_o, single_buffer=True),
        ],
        out_specs=pl.BlockSpec((tm, d), lambda i: (i, 0)),
        out_shape=jax.ShapeDtypeStruct((s, d), F32),
        compiler_params=_params("parallel"),
        name="mix_out",
    )(h, norm_g4, mod4, mod4, mod4, oa, od, w_ba, w_bb, w_gate, b_gate, w_o)


def _axial_angles_t(seq):
    rows = seq // GRID_W
    row = jnp.broadcast_to(jnp.arange(rows)[:, None], (rows, GRID_W)).reshape(seq) - rows // 2
    col = jnp.broadcast_to(jnp.arange(GRID_W)[None, :], (rows, GRID_W)).reshape(seq) - GRID_W // 2
    inv = 1.0 / (ROPE_THETA ** (jnp.arange(0, ROPE_AXIS_DIM, 2, dtype=F32) / ROPE_AXIS_DIM))
    return inv[:, None] * row.astype(F32)[None, :], inv[:, None] * col.astype(F32)[None, :]


def kernel(x, c, ada_w, ada_b, norm_g, ffn_wg, ffn_wu, ffn_wd, w_in, qk_g, lam_p, subln_g, w_ba, w_bb,
           w_gate, b_gate, w_o, final_g):
    batch, s, d = x.shape
    assert batch == 1 and s % KEY_TILE == 0 and MIX_ROWS == KEY_TILE
    depth = ada_w.shape[0]
    h = x.reshape(s, d)

    mod4 = _ada_mod(c.reshape(d, 1), ada_w, ada_b).reshape(depth, N_ADA, 1, d)
    norm_g4 = norm_g.reshape(depth, 3, 1, d)
    angr_t, angc_t = _axial_angles_t(s)
    slopes = 2.0 ** (-8.0 * jnp.arange(1, B_HEADS + 1, dtype=F32) / B_HEADS)
    coef = jnp.stack([slopes * (LOG2E / F8_UNSCALE), 1.0 / (slopes * LOG2E * KEY_TILE)])
    fg = final_g.reshape(1, d)
    wg, wu, wd = ffn_wg.astype(BF16), ffn_wu.astype(BF16), ffn_wd.astype(BF16)
    w_in_t = jnp.swapaxes(w_in, 1, 2).astype(BF16)
    qkg_t = jnp.swapaxes(qk_g, 1, 2)
    w_ba16, w_bb16, w_gate16, w_o16 = (w.astype(BF16) for w in (w_ba, w_bb, w_gate, w_o))
    b_gate3 = b_gate.reshape(depth, 1, -1)
    subln_col = subln_g.reshape(depth, B_V_DIM, 1)
    nk = s // KEY_TILE

    for l in range(depth):
        lam_init = 0.8 - 0.6 * math.exp(-0.3 * l)
        h = _ffn(h, l, 0, norm_g4, mod4, wg, wu, wd, fg, final_norm=False)
        qta, ka, vta, qtd, kd, vtd, qn, kn, qfa, kfa, qfd, kfd = _mix_in(
            h, l, norm_g4, mod4, w_in_t, angr_t, angc_t, qkg_t)
        oa = _gqa(qta, ka.reshape(nk, KEY_TILE, A_KV_COLS), vta, qn, kn,
                  qfa, kfa.reshape(A_KV_HEADS, nk, KEY_TILE, F8_COLS))
        od = _diff(l, coef, qtd, kd.reshape(nk, KEY_TILE, B_QK_COLS), vtd, qn, kn,
                   qfd, kfd.reshape(2 * B_HEADS, nk, KEY_TILE, F8_COLS), lam_p, subln_col, lam_init=lam_init)
        h = _mix_out(h, l, norm_g4, mod4, oa, od, w_ba16, w_bb16, w_gate16, b_gate3, w_o16)
        h = _ffn(h, l, 1, norm_g4, mod4, wg, wu, wd, fg, final_norm=(l == depth - 1))
    return h.reshape(batch, s, d)
```

```python
import functools
import math

import jax
import jax.numpy as jnp
from jax import lax
from jax.experimental import pallas as pl
from jax.experimental.pallas import tpu as pltpu

F32 = jnp.float32
BF16 = jnp.bfloat16
F8 = jnp.float8_e4m3fn

GRID_W = 64
HEAD_DIM = 64
A_Q_HEADS = 8
A_KV_HEADS = 2
A_GROUP = A_Q_HEADS // A_KV_HEADS
B_HEADS = 4
B_V_DIM = 2 * HEAD_DIM
A_Q_COLS = A_Q_HEADS * HEAD_DIM
A_KV_COLS = A_KV_HEADS * HEAD_DIM
B_QK_COLS = 2 * B_HEADS * HEAD_DIM
B_V_COLS = B_HEADS * B_V_DIM
N_ADA = 9
EPS = 1e-6
ROPE_THETA = 10000.0
ROPE_AXIS_DIM = HEAD_DIM // 2
ROPE_HALF = ROPE_AXIS_DIM // 2

LOG2E = math.log2(math.e)
QK_SCALE = HEAD_DIM ** -0.5
NEG_BIG = -1e30

BF16_SUBLANES = 16
LANES = 128
ONES_ROWS = BF16_SUBLANES
SKIP_LOG2 = 64.0
FAST_LOG2_LIMIT = 60.0
F8_Q_SCALE = 2.0 ** 6
F8_K_SCALE = 2.0 ** 2
F8_UNSCALE = 1.0 / (F8_Q_SCALE * F8_K_SCALE)
F8_SAFE_MAX = 256.0
F8_COLS = 4 * HEAD_DIM
NORM_ROWS_A = 8
NORM_ROWS = NORM_ROWS_A + 2 * B_HEADS

VMEM_LIMIT_BYTES = 56 * 1024 * 1024

FFN_ROWS = 512
MIX_ROWS = 512
KEY_TILE = 512
GQA_Q_TILE = 512
DIFF_Q_TILE = 512
ADA_COLS = 1152


def _params(*sem):
    return pltpu.CompilerParams(dimension_semantics=sem, vmem_limit_bytes=VMEM_LIMIT_BYTES)


def _layer_vec(l, j, d):
    return pl.BlockSpec((None, None, 1, d), lambda *_: (l, j, 0, 0))


def _layer_mat(l, a, *, single_buffer=False):
    nd = a.ndim - 1
    mode = dict(pipeline_mode=pl.Buffered(1)) if single_buffer else {}
    return pl.BlockSpec((None,) + a.shape[1:], lambda *_: (l,) + (0,) * nd, **mode)


def _rms_rows(x):
    return x * lax.rsqrt(jnp.mean(x * x, axis=-1, keepdims=True) + EPS)


def _ada_kernel(c_ref, w_ref, b_ref, o_ref):
    c = c_ref[...]
    act = c / (1.0 + jnp.exp(-c))
    o_ref[...] = jnp.sum(w_ref[...] * act, axis=0, keepdims=True) + b_ref[...]


def _ada_mod(c_col, ada_w, ada_b):
    n_layers, d, n = ada_w.shape
    return pl.pallas_call(
        _ada_kernel,
        grid=(n_layers, n // ADA_COLS),
        in_specs=[
            pl.BlockSpec((d, 1), lambda l, j: (0, 0)),
            pl.BlockSpec((None, d, ADA_COLS), lambda l, j: (l, 0, j)),
            pl.BlockSpec((None, 1, ADA_COLS), lambda l, j: (l, 0, j)),
        ],
        out_specs=pl.BlockSpec((None, 1, ADA_COLS), lambda l, j: (l, 0, j)),
        out_shape=jax.ShapeDtypeStruct((n_layers, 1, n), F32),
        compiler_params=_params("parallel", "parallel"),
        name="ada_mod",
    )(c_col, ada_w, ada_b.reshape(n_layers, 1, n))


def _ffn_kernel(h_ref, ng_ref, sh_ref, sc_ref, gt_ref, wg_ref, wu_ref, wd_ref, fg_ref, o_ref, *, final_norm):
    h = h_ref[...]
    n = _rms_rows(h) * ng_ref[...]
    n = (n * (1.0 + sc_ref[...]) + sh_ref[...]).astype(BF16)
    hg = jnp.dot(n, wg_ref[...], preferred_element_type=F32)
    hu = jnp.dot(n, wu_ref[...], preferred_element_type=F32)
    a = (hg / (1.0 + jnp.exp(-hg))) * hu
    out = h + (0.5 * gt_ref[...]) * jnp.dot(a.astype(BF16), wd_ref[...], preferred_element_type=F32)
    if final_norm:
        out = _rms_rows(out) * fg_ref[...]
    o_ref[...] = out


def _ffn(h, l, which, norm_g4, mod4, wg, wu, wd, fg, *, final_norm):
    s, d = h.shape
    m0 = 6 * which

    def resident(a):
        return pl.BlockSpec((None, None) + a.shape[2:], lambda i: (l, which, 0, 0),
                            pipeline_mode=pl.Buffered(1))

    return pl.pallas_call(
        functools.partial(_ffn_kernel, final_norm=final_norm),
        grid=(s // FFN_ROWS,),
        in_specs=[
            pl.BlockSpec((FFN_ROWS, d), lambda i: (i, 0)),
            _layer_vec(l, 2 * which, d),
            _layer_vec(l, m0, d), _layer_vec(l, m0 + 1, d), _layer_vec(l, m0 + 2, d),
            resident(wg), resident(wu), resident(wd),
            pl.BlockSpec((1, d), lambda i: (0, 0)),
        ],
        out_specs=pl.BlockSpec((FFN_ROWS, d), lambda i: (i, 0)),
        out_shape=jax.ShapeDtypeStruct((s, d), F32),
        compiler_params=_params("parallel"),
        name="ffn_final" if final_norm else "ffn",
    )(h, norm_g4, mod4, mod4, mod4, wg, wu, wd, fg)


def _rope_t(x, cr, sr, cc, sc):
    h = ROPE_HALF
    x1r, x2r, x1c, x2c = x[0:h], x[h:2 * h], x[2 * h:3 * h], x[3 * h:4 * h]
    return jnp.concatenate(
        [x1r * cr - x2r * sr, x2r * cr + x1r * sr, x1c * cc - x2c * sc, x2c * cc + x1c * sc], axis=0)


def _head_norm_t(x, g_col):
    ms = jnp.mean(x * x, axis=0, keepdims=True)
    return x * lax.rsqrt(ms + EPS) * g_col


def _max_sq_norm(x_bf16):
    xf = x_bf16.astype(F32)
    n2 = jnp.sum(xf * xf, axis=0, keepdims=True)
    return jnp.broadcast_to(jnp.max(n2, axis=1, keepdims=True), (1, LANES))


def _f8_split(x):
    hi = x.astype(F8).astype(F32)
    return hi, ((x - hi) * 16.0).astype(F8).astype(F32)


def _f8_query_rows(q_bf16, *, weighted):
    hi, lo16 = _f8_split(q_bf16.astype(F32) * F8_Q_SCALE)
    rows = [hi, hi * 0.0625, lo16 * 0.0625, lo16 * 0.00390625] if weighted else [hi, hi, lo16, lo16]
    return jnp.concatenate(rows, axis=0).astype(F8)


def _f8_key_rows(k_bf16, *, weighted):
    hi, lo16 = _f8_split(k_bf16.astype(F32) * F8_K_SCALE)
    rows = [hi, lo16 * 0.0625, hi * 0.0625, lo16 * 0.00390625] if weighted else [hi, lo16, hi, lo16]
    return jnp.concatenate(rows, axis=0)


def _with_ones(v_t):
    return jnp.concatenate([v_t, jnp.ones((ONES_ROWS, v_t.shape[1]), F32)], axis=0).astype(BF16)


def _mix_in_kernel(h_ref, ng_ref, sh_ref, sc_ref, wt_ref, angr_ref, angc_ref, qkg_ref,
                   qta_ref, ka_ref, vta_ref, qtd_ref, kd_ref, vtd_ref, qn_ref, kn_ref,
                   qfa_ref, kfa_ref, qfd_ref, kfd_ref):
    n = _rms_rows(h_ref[...]) * ng_ref[...]
    n = (n * (1.0 + sc_ref[...]) + sh_ref[...]).astype(BF16)
    pt = lax.dot_general(wt_ref[...], n, (((1,), (1,)), ((), ())), preferred_element_type=F32)
    t = pt.shape[1]

    cr, sr = jnp.cos(angr_ref[...]), jnp.sin(angr_ref[...])
    cc, sc = jnp.cos(angc_ref[...]), jnp.sin(angc_ref[...])
    gq = qkg_ref[:, 0:1]
    gk = qkg_ref[:, 1:2]
    zeros = jnp.zeros((HEAD_DIM, t), BF16)
    qscale = QK_SCALE * LOG2E

    qn, kn = [], []
    for hd in range(A_Q_HEADS):
        q = _rope_t(_head_norm_t(pt[hd * HEAD_DIM:(hd + 1) * HEAD_DIM], gq), cr, sr, cc, sc) * qscale
        q = q.astype(BF16)
        qta_ref[hd] = jnp.concatenate([q, zeros] if hd // A_GROUP == 0 else [zeros, q], axis=0)
        qfa_ref[hd] = _f8_query_rows(q, weighted=False)
        qn.append(_max_sq_norm(q))

    c1 = A_Q_COLS
    c2 = c1 + A_KV_COLS
    c3 = c2 + A_KV_COLS
    c4 = c3 + B_QK_COLS
    c5 = c4 + B_QK_COLS
    kt = jnp.concatenate(
        [_rope_t(_head_norm_t(pt[c1 + j * HEAD_DIM:c1 + (j + 1) * HEAD_DIM], gk), cr, sr, cc, sc)
         for j in range(A_KV_HEADS)], axis=0)
    kt = kt.astype(BF16)
    ka_ref[...] = kt.T
    kn += [_max_sq_norm(kt[j * HEAD_DIM:(j + 1) * HEAD_DIM]) for j in range(A_KV_HEADS)]
    for j in range(A_KV_HEADS):
        kfa_ref[j] = _f8_key_rows(kt[j * HEAD_DIM:(j + 1) * HEAD_DIM], weighted=True).astype(BF16).T.astype(F8)
    kn.append(jnp.zeros((NORM_ROWS_A - A_KV_HEADS, LANES), F32))
    for j in range(A_KV_HEADS):
        vta_ref[j] = _with_ones(pt[c2 + j * HEAD_DIM:c2 + (j + 1) * HEAD_DIM])

    for cmb in range(2 * B_HEADS):
        q = (pt[c3 + cmb * HEAD_DIM:c3 + (cmb + 1) * HEAD_DIM] * qscale).astype(BF16)
        qtd_ref[cmb] = jnp.concatenate([q, zeros] if cmb % 2 == 0 else [zeros, q], axis=0)
        qfd_ref[cmb] = _f8_query_rows(q, weighted=True)
        qn.append(_max_sq_norm(q))
        k = pt[c4 + cmb * HEAD_DIM:c4 + (cmb + 1) * HEAD_DIM].astype(BF16)
        kfd_ref[cmb] = _f8_key_rows(k, weighted=False).astype(BF16).T.astype(F8)
        kn.append(_max_sq_norm(k))
    qn_ref[...] = jnp.concatenate(qn, axis=0)
    kn_ref[...] = jnp.concatenate(kn, axis=0)
    kd_ref[...] = pt[c4:c5].T.astype(BF16)
    for hd in range(B_HEADS):
        vtd_ref[hd] = _with_ones(pt[c5 + hd * B_V_DIM:c5 + (hd + 1) * B_V_DIM])


def _mix_in(h, l, norm_g4, mod4, w_in_t, angr_t, angc_t, qkg_t):
    s, d = h.shape
    tm = MIX_ROWS
    nt = s // tm
    n_sets = 2 * B_HEADS
    assert A_Q_HEADS == NORM_ROWS_A
    out_shape = (
        jax.ShapeDtypeStruct((A_Q_HEADS, 2 * HEAD_DIM, s), BF16),
        jax.ShapeDtypeStruct((s, A_KV_COLS), BF16),
        jax.ShapeDtypeStruct((A_KV_HEADS, nt, HEAD_DIM + ONES_ROWS, tm), BF16),
        jax.ShapeDtypeStruct((n_sets, 2 * HEAD_DIM, s), BF16),
        jax.ShapeDtypeStruct((s, B_QK_COLS), BF16),
        jax.ShapeDtypeStruct((B_HEADS, nt, B_V_DIM + ONES_ROWS, tm), BF16),
        jax.ShapeDtypeStruct((nt, NORM_ROWS, LANES), F32),
        jax.ShapeDtypeStruct((nt, NORM_ROWS, LANES), F32),
        jax.ShapeDtypeStruct((A_Q_HEADS, F8_COLS, s), F8),
        jax.ShapeDtypeStruct((A_KV_HEADS, s, F8_COLS), F8),
        jax.ShapeDtypeStruct((n_sets, F8_COLS, s), F8),
        jax.ShapeDtypeStruct((n_sets, s, F8_COLS), F8),
    )
    out_specs = (
        pl.BlockSpec((A_Q_HEADS, 2 * HEAD_DIM, tm), lambda i: (0, 0, i)),
        pl.BlockSpec((tm, A_KV_COLS), lambda i: (i, 0)),
        pl.BlockSpec((A_KV_HEADS, None, HEAD_DIM + ONES_ROWS, tm), lambda i: (0, i, 0, 0)),
        pl.BlockSpec((n_sets, 2 * HEAD_DIM, tm), lambda i: (0, 0, i)),
        pl.BlockSpec((tm, B_QK_COLS), lambda i: (i, 0)),
        pl.BlockSpec((B_HEADS, None, B_V_DIM + ONES_ROWS, tm), lambda i: (0, i, 0, 0)),
        pl.BlockSpec((None, NORM_ROWS, LANES), lambda i: (i, 0, 0)),
        pl.BlockSpec((None, NORM_ROWS, LANES), lambda i: (i, 0, 0)),
        pl.BlockSpec((A_Q_HEADS, F8_COLS, tm), lambda i: (0, 0, i)),
        pl.BlockSpec((A_KV_HEADS, tm, F8_COLS), lambda i: (0, i, 0)),
        pl.BlockSpec((n_sets, F8_COLS, tm), lambda i: (0, 0, i)),
        pl.BlockSpec((n_sets, tm, F8_COLS), lambda i: (0, i, 0)),
    )
    return pl.pallas_call(
        _mix_in_kernel,
        grid=(nt,),
        in_specs=[
            pl.BlockSpec((tm, d), lambda i: (i, 0)),
            _layer_vec(l, 1, d), _layer_vec(l, 3, d), _layer_vec(l, 4, d),
            _layer_mat(l, w_in_t, single_buffer=True),
            pl.BlockSpec((ROPE_HALF, tm), lambda i: (0, i)),
            pl.BlockSpec((ROPE_HALF, tm), lambda i: (0, i)),
            _layer_mat(l, qkg_t),
        ],
        out_specs=out_specs,
        out_shape=out_shape,
        compiler_params=_params("parallel"),
        name="mix_in",
    )(h, norm_g4, mod4, mod4, w_in_t, angr_t, angc_t, qkg_t)


def _softmax_tile(s, m_ref, alpha_ref, p_ref):
    tk, n = s.shape
    part = jnp.max(s.reshape(tk // BF16_SUBLANES, BF16_SUBLANES, n), axis=0)
    m_old = m_ref[...]
    m_new = jnp.maximum(m_old, jnp.max(part.astype(F32), axis=0, keepdims=True))
    alpha_ref[...] = jnp.exp2(m_old - m_new)
    m_ref[...] = m_new
    p_ref[...] = jnp.exp2(s - m_new.astype(BF16))


def _pipelined_sweep(prep, qk, softmax, pv, n_chunks, lo, n_pairs, last):
    ctx = prep(lo)
    for c in range(n_chunks):
        qk(lo, 0, c, ctx)

    def body(j, carry):
        a = lo + 2 * j
        prev = jnp.maximum(a - 1, 0)
        nxt = jnp.minimum(a + 2, last)
        ctx = prep(a + 1)
        for c in range(n_chunks):
            softmax(0, c)
            pv(prev, 1, c)
            qk(a + 1, 1, c, ctx)
        ctx = prep(nxt)
        for c in range(n_chunks):
            softmax(1, c)
            pv(a, 0, c)
            qk(nxt, 0, c, ctx)
        return carry

    lax.fori_loop(0, n_pairs, body, 0)
    for c in range(n_chunks):
        pv(last, 1, c)


def _direct_sweep(prep, qk_exp, pv, n_chunks, lo, n_tiles, last, unroll):
    def phase(kt_new, slot_new, kt_old):
        ctx = prep(kt_new)
        for c in range(n_chunks):
            qk_exp(kt_new, slot_new, c, ctx)
            pv(kt_old, 1 - slot_new, c)

    ctx = prep(lo)
    for c in range(n_chunks):
        qk_exp(lo, 0, c, ctx)

    def pairs(first, n_pairs):
        def body(j, carry):
            a = first + 2 * n_pairs * j
            for i in range(n_pairs):
                phase(a + 2 * i + 1, 1, a + 2 * i)
                phase(a + 2 * i + 2, 0, a + 2 * i + 1)
            return carry
        return body

    assert unroll[-1] == 1
    static = isinstance(n_tiles, int)
    left = (n_tiles - 1) // 2 if static else lax.shift_right_logical(n_tiles - 1, 1)
    first = lo
    for size in unroll:
        trips = left // size if static else lax.div(left, size)
        lax.fori_loop(0, trips, pairs(first, size), 0)
        first = first + 2 * size * trips
        left = left - size * trips

    def tail_odd():
        for c in range(n_chunks):
            pv(last, 0, c)
        return 0

    def tail_even():
        phase(last, 1, last - 1)
        for c in range(n_chunks):
            pv(last, 1, c)
        return 0

    if isinstance(n_tiles, int):
        (tail_odd if n_tiles % 2 else tail_even)()
    else:
        lax.cond(jnp.bitwise_and(n_tiles, 1) == 1, tail_odd, tail_even)


def _score_bound(qn_tile, kn_ref, q_rows, k_row_of_q):
    k_max = jnp.max(kn_ref[...], axis=0)
    row = lax.broadcasted_iota(jnp.int32, k_max.shape, 0)
    sel = q_rows(row)
    q2 = jnp.where(sel, qn_tile, 0.0)
    k2 = jnp.where(sel, k_row_of_q(row, k_max), 0.0)
    fp8_ok = jnp.logical_and(jnp.max(q2) * F8_Q_SCALE ** 2 <= F8_SAFE_MAX ** 2,
                             jnp.max(k2) * F8_K_SCALE ** 2 <= F8_SAFE_MAX ** 2)
    return jnp.max(jnp.sqrt(q2 * k2)), fp8_ok


GQA_CHUNKS = 8
GQA_UNROLL = (15, 1)


def _gqa_kernel(qt_ref, k_ref, vt_ref, qn_ref, kn_ref, qf_ref, kf_ref, o_ref, q_scr, qf_scr, s_buf, p_buf,
                alpha_buf, m_scr, acc_scr):
    g = pl.program_id(0)
    qi = pl.program_id(1)
    nk, tk, _ = k_ref.shape
    tq = qt_ref.shape[2]
    w = A_GROUP * tq // GQA_CHUNKS
    for c in range(GQA_CHUNKS):
        for col in range(c * w, (c + 1) * w, min(w, tq)):
            hh, j, n = col // tq, col % tq, min(w, tq)
            q_scr[c, :, col - c * w:col - c * w + n] = qt_ref[hh, :, j:j + n]
            qf_scr[c, :, col - c * w:col - c * w + n] = qf_ref[hh, :, j:j + n]
    acc_scr[...] = jnp.zeros_like(acc_scr)

    def qk(kt, slot, c, _):
        s_buf[slot, c] = jnp.dot(k_ref[kt], q_scr[c], preferred_element_type=F32).astype(BF16)

    def softmax(slot, c):
        _softmax_tile(s_buf[slot, c], m_scr.at[c], alpha_buf.at[slot, c], p_buf.at[slot, c])

    def pv(kt, slot, c):
        acc_scr[c] = alpha_buf[slot, c] * acc_scr[c] + jnp.dot(
            vt_ref[kt], p_buf[slot, c], preferred_element_type=F32)

    def qk_exp(kt, slot, c, _):
        s = jnp.dot(kf_ref[kt], qf_scr[c], preferred_element_type=F32)
        p_buf[slot, c] = jnp.exp2(s.astype(BF16) * F8_UNSCALE)

    def pv_plain(kt, slot, c):
        acc_scr[c] += jnp.dot(vt_ref[kt], p_buf[slot, c], preferred_element_type=F32)

    def kmax_of_group(row, k_max):
        return jnp.max(jnp.where(row == g, k_max, 0.0), axis=0, keepdims=True)

    bound, fp8_ok = _score_bound(
        qn_ref[lax.div(qi * tq, tk)], kn_ref,
        lambda row: jnp.logical_and(row >= g * A_GROUP, row < (g + 1) * A_GROUP), kmax_of_group)

    def direct():
        _direct_sweep(lambda kt: None, qk_exp, pv_plain, GQA_CHUNKS, 0, nk, nk - 1, GQA_UNROLL)
        return 0

    def online():
        m_scr[...] = jnp.full_like(m_scr, NEG_BIG)
        p_buf[1] = jnp.zeros(p_buf.shape[1:], BF16)
        alpha_buf[1] = jnp.ones(alpha_buf.shape[1:], F32)
        _pipelined_sweep(lambda kt: None, qk, softmax, pv, GQA_CHUNKS, 0, nk // 2, nk - 1)
        return 0

    lax.cond(jnp.logical_and(bound <= FAST_LOG2_LIMIT, fp8_ok), direct, online)
    cols = []
    for c in range(GQA_CHUNKS):
        acc = acc_scr[c]
        cols.append(acc[:HEAD_DIM] * (1.0 / acc[HEAD_DIM:HEAD_DIM + 1]))
    ot = jnp.concatenate(cols, axis=1)
    ot = jnp.concatenate([ot[:, hh * tq:(hh + 1) * tq] for hh in range(A_GROUP)], axis=0)
    o_ref[...] = ot.T.astype(BF16)


def _gqa(qta, ka3, vta, qn, kn, qfa, kfa4):
    s = qta.shape[2]
    nk, tk, _ = ka3.shape
    tq = GQA_Q_TILE
    assert nk % 2 == 0 and tk % tq == 0 and tk == MIX_ROWS
    nc = GQA_CHUNKS
    w = A_GROUP * tq // nc
    v_rows = vta.shape[2]
    return pl.pallas_call(
        _gqa_kernel,
        grid=(A_KV_HEADS, s // tq),
        in_specs=[
            pl.BlockSpec((A_GROUP, 2 * HEAD_DIM, tq), lambda g, i: (g, 0, i)),
            pl.BlockSpec((nk, tk, A_KV_COLS), lambda g, i: (0, 0, 0), pipeline_mode=pl.Buffered(1)),
            pl.BlockSpec((None, nk, v_rows, tk), lambda g, i: (g, 0, 0, 0), pipeline_mode=pl.Buffered(1)),
            pl.BlockSpec(qn.shape, lambda g, i: (0, 0, 0)),
            pl.BlockSpec(kn.shape, lambda g, i: (0, 0, 0)),
            pl.BlockSpec((A_GROUP, F8_COLS, tq), lambda g, i: (g, 0, i)),
            pl.BlockSpec((None, nk, tk, F8_COLS), lambda g, i: (g, 0, 0, 0), pipeline_mode=pl.Buffered(1)),
        ],
        out_specs=pl.BlockSpec((tq, A_GROUP * HEAD_DIM), lambda g, i: (i, g)),
        out_shape=jax.ShapeDtypeStruct((s, A_Q_COLS), BF16),
        scratch_shapes=[
            pltpu.VMEM((nc, 2 * HEAD_DIM, w), BF16),
            pltpu.VMEM((nc, F8_COLS, w), F8),
            pltpu.VMEM((2, nc, tk, w), BF16),
            pltpu.VMEM((2, nc, tk, w), BF16),
            pltpu.VMEM((2, nc, 1, w), F32),
            pltpu.VMEM((nc, 1, w), F32),
            pltpu.VMEM((nc, v_rows, w), F32),
        ],
        compiler_params=_params("parallel", "parallel"),
        name="gqa_attn",
    )(qta, ka3, vta, qn, kn, qfa, kfa4)


DIFF_CHUNKS = 2
DIFF_UNROLL = (5, 3, 1)


def _diff_kernel(coef_ref, qt0_ref, qt1_ref, k0_ref, k1_ref, vt_ref, qn_ref, kn_ref, lamp_ref, sg_ref,
                 qf0_ref, qf1_ref, kf0_ref, kf1_ref,
                 o_ref, rel_scr, s_buf, p_buf, alpha_buf, m_scr, acc_scr, *, lam_init):
    hd = pl.program_id(0)
    qi = pl.program_id(1)
    nk, tk, _ = k0_ref.shape
    tq = qt0_ref.shape[1]
    slope_f8 = coef_ref[0, hd]
    inv_tile_drop = coef_ref[1, hd]
    acc_scr[...] = jnp.zeros_like(acc_scr)

    @pl.when(qi == 0)
    def _():
        rel_scr[...] = (lax.broadcasted_iota(jnp.int32, (tk, tq), 1)
                        - lax.broadcasted_iota(jnp.int32, (tk, tq), 0)).astype(F32) * slope_f8

    bound, fp8_ok = _score_bound(
        qn_ref[qi], kn_ref,
        lambda row: jnp.logical_or(row == NORM_ROWS_A + hd, row == NORM_ROWS_A + B_HEADS + hd),
        lambda row, k_max: k_max)
    use_direct = jnp.logical_and(bound <= FAST_LOG2_LIMIT, fp8_ok)
    drop = 2.0 * bound + SKIP_LOG2
    reach = jnp.minimum(jnp.floor(drop * inv_tile_drop) + 1.0, float(nk)).astype(jnp.int32)
    lo = jnp.maximum(qi - reach, 0)
    hi = jnp.minimum(qi + reach, nk - 1)

    q0 = qi * tq
    w = tq // DIFF_CHUNKS
    n_chunks = 2 * DIFF_CHUNKS
    cols = [slice((c % DIFF_CHUNKS) * w, (c % DIFF_CHUNKS + 1) * w) for c in range(n_chunks)]
    qts = (qt0_ref, qt1_ref)
    ks = (k0_ref, k1_ref)
    qfs = (qf0_ref, qf1_ref)
    kfs = (kf0_ref, kf1_ref)

    def penalty_scaled(kt):
        return jnp.abs(rel_scr[...] + (q0 - kt * tk).astype(F32) * slope_f8)

    def penalty(kt):
        return penalty_scaled(kt) * F8_UNSCALE

    def qk(kt, slot, c, pen):
        mp = c // DIFF_CHUNKS
        s = jnp.dot(ks[mp][kt], qts[mp][:, cols[c]], preferred_element_type=F32) - pen[:, cols[c]]
        s_buf[slot, c] = s.astype(BF16)

    def softmax(slot, c):
        _softmax_tile(s_buf[slot, c], m_scr.at[c], alpha_buf.at[slot, c], p_buf.at[slot, c])

    def pv(kt, slot, c):
        acc_scr[c] = alpha_buf[slot, c] * acc_scr[c] + jnp.dot(
            vt_ref[kt], p_buf[slot, c], preferred_element_type=F32)

    def qk_exp(kt, slot, c, pen_scaled):
        mp = c // DIFF_CHUNKS
        s = jnp.dot(kfs[mp][kt], qfs[mp][:, cols[c]], preferred_element_type=F32) - pen_scaled[:, cols[c]]
        p_buf[slot, c] = jnp.exp2(s.astype(BF16) * F8_UNSCALE)

    def pv_plain(kt, slot, c):
        acc_scr[c] += jnp.dot(vt_ref[kt], p_buf[slot, c], preferred_element_type=F32)

    def direct_all():
        _direct_sweep(penalty_scaled, qk_exp, pv_plain, n_chunks, 0, nk, nk - 1, ((nk - 1) // 2, 1))
        return 0

    def direct_some():
        _direct_sweep(penalty_scaled, qk_exp, pv_plain, n_chunks, lo, hi - lo + 1, hi, DIFF_UNROLL)
        return 0

    def direct():
        return lax.cond(hi - lo + 1 == nk, direct_all, direct_some)

    def online():
        m_scr[...] = jnp.full_like(m_scr, NEG_BIG)
        p_buf[1] = jnp.zeros(p_buf.shape[1:], BF16)
        alpha_buf[1] = jnp.ones(alpha_buf.shape[1:], F32)
        odd = jnp.bitwise_and(hi - lo + 1, 1)
        grow_hi = jnp.where(hi < nk - 1, odd, 0)
        hi2 = hi + grow_hi
        lo2 = lo - (odd - grow_hi)
        _pipelined_sweep(penalty, qk, softmax, pv, n_chunks, lo2, lax.shift_right_logical(hi2 - lo2 + 1, 1), hi2)
        return 0

    lax.cond(use_direct, direct, online)

    lp = lamp_ref[...]
    lam = (jnp.exp(jnp.sum(lp[0:1] * lp[1:2], axis=1, keepdims=True))
           - jnp.exp(jnp.sum(lp[2:3] * lp[3:4], axis=1, keepdims=True)) + lam_init)
    a0, a1 = [jnp.concatenate([acc_scr[mp * DIFF_CHUNKS + j] for j in range(DIFF_CHUNKS)], axis=1)
              for mp in range(2)]
    ot = (a0[:B_V_DIM] * (1.0 / a0[B_V_DIM:B_V_DIM + 1])
          - lam * (a1[:B_V_DIM] * (1.0 / a1[B_V_DIM:B_V_DIM + 1])))
    ot = ot * lax.rsqrt(jnp.mean(ot * ot, axis=0, keepdims=True) + EPS)
    ot = ot * sg_ref[...] * (1.0 - lam_init)
    o_ref[...] = ot.T.astype(BF16)


def _diff(l, coef, qtd, kd3, vtd, qn, kn, qfd, kfd4, lam_p, subln_col, *, lam_init):
    s = qtd.shape[2]
    nk, tk, _ = kd3.shape
    tq = DIFF_Q_TILE
    assert nk % 2 == 0 and tq == tk == MIX_ROWS
    lanes = 2 * HEAD_DIM
    v_rows = vtd.shape[2]
    nc = 2 * DIFF_CHUNKS
    w = tq // DIFF_CHUNKS
    return pl.pallas_call(
        functools.partial(_diff_kernel, lam_init=lam_init),
        grid=(B_HEADS, s // tq),
        in_specs=[
            pl.BlockSpec(memory_space=pltpu.SMEM),
            pl.BlockSpec((None, lanes, tq), lambda h, i: (h, 0, i)),
            pl.BlockSpec((None, lanes, tq), lambda h, i: (B_HEADS + h, 0, i)),
            pl.BlockSpec((nk, tk, lanes), lambda h, i: (0, 0, h // 2), pipeline_mode=pl.Buffered(1)),
            pl.BlockSpec((nk, tk, lanes), lambda h, i: (0, 0, B_HEADS // 2 + h // 2),
                         pipeline_mode=pl.Buffered(1)),
            pl.BlockSpec((None, nk, v_rows, tk), lambda h, i: (h, 0, 0, 0)),
            pl.BlockSpec(qn.shape, lambda h, i: (0, 0, 0)),
            pl.BlockSpec(kn.shape, lambda h, i: (0, 0, 0)),
            _layer_mat(l, lam_p),
            _layer_mat(l, subln_col),
            pl.BlockSpec((None, F8_COLS, tq), lambda h, i: (h, 0, i)),
            pl.BlockSpec((None, F8_COLS, tq), lambda h, i: (B_HEADS + h, 0, i)),
            pl.BlockSpec((None, nk, tk, F8_COLS), lambda h, i: (h, 0, 0, 0)),
            pl.BlockSpec((None, nk, tk, F8_COLS), lambda h, i: (B_HEADS + h, 0, 0, 0)),
        ],
        out_specs=pl.BlockSpec((tq, B_V_DIM), lambda h, i: (i, h)),
        out_shape=jax.ShapeDtypeStruct((s, B_V_COLS), BF16),
        scratch_shapes=[
            pltpu.VMEM((tk, tq), F32),
            pltpu.VMEM((2, nc, tk, w), BF16),
            pltpu.VMEM((2, nc, tk, w), BF16),
            pltpu.VMEM((2, nc, 1, w), F32),
            pltpu.VMEM((nc, 1, w), F32),
            pltpu.VMEM((nc, v_rows, w), F32),
        ],
        compiler_params=_params("arbitrary", "arbitrary"),
        name="diff_attn",
    )(coef, qtd, qtd, kd3, kd3, vtd, qn, kn, lam_p, subln_col, qfd, qfd, kfd4, kfd4)


def _mix_out_kernel(h_ref, ng_ref, sh_ref, sc_ref, gt_ref, oa_ref, od_ref, wba_ref, wbb_ref,
                    wgate_ref, bgate_ref, wo_ref, o_ref):
    h = h_ref[...]
    d = h.shape[1]
    n = _rms_rows(h) * ng_ref[...]
    n = (n * (1.0 + sc_ref[...]) + sh_ref[...]).astype(BF16)
    z = jnp.dot(n, wgate_ref[...], preferred_element_type=F32) + bgate_ref[...]
    g = 1.0 / (1.0 + jnp.exp(-z))
    ya = jnp.dot(oa_ref[...], wba_ref[...], preferred_element_type=F32)
    yb = jnp.dot(od_ref[...], wbb_ref[...], preferred_element_type=F32)
    mix = (g[:, :d] * ya + g[:, d:] * yb).astype(BF16)
    y = jnp.dot(mix, wo_ref[...], preferred_element_type=F32)
    o_ref[...] = h + gt_ref[...] * y


def _mix_out(h, l, norm_g4, mod4, oa, od, w_ba, w_bb, w_gate, b_gate, w_o):
    s, d = h.shape
    tm = MIX_ROWS
    return pl.pallas_call(
        _mix_out_kernel,
        grid=(s // tm,),
        in_specs=[
            pl.BlockSpec((tm, d), lambda i: (i, 0)),
            _layer_vec(l, 1, d), _layer_vec(l, 3, d), _layer_vec(l, 4, d), _layer_vec(l, 5, d),
            pl.BlockSpec((tm, oa.shape[1]), lambda i: (i, 0)),
            pl.BlockSpec((tm, od.shape[1]), lambda i: (i, 0)),
            _layer_mat(l, w_ba, single_buffer=True), _layer_mat(l, w_bb, single_buffer=True),
            _layer_mat(l, w_gate, single_buffer=True), _layer_mat(l, b_gate),
            _layer_mat(l, w_o, single_buffer=True),
        ],
        out_specs=pl.BlockSpec((tm, d), lambda i: (i, 0)),
        out_shape=jax.ShapeDtypeStruct((s, d), F32),
        compiler_params=_params("parallel"),
        name="mix_out",
    )(h, norm_g4, mod4, mod4, mod4, oa, od, w_ba, w_bb, w_gate, b_gate, w_o)


def _axial_angles_t(seq):
    rows = seq // GRID_W
    row = jnp.broadcast_to(jnp.arange(rows)[:, None], (rows, GRID_W)).reshape(seq) - rows // 2
    col = jnp.broadcast_to(jnp.arange(GRID_W)[None, :], (rows, GRID_W)).reshape(seq) - GRID_W // 2
    inv = 1.0 / (ROPE_THETA ** (jnp.arange(0, ROPE_AXIS_DIM, 2, dtype=F32) / ROPE_AXIS_DIM))
    return inv[:, None] * row.astype(F32)[None, :], inv[:, None] * col.astype(F32)[None, :]


def kernel(x, c, ada_w, ada_b, norm_g, ffn_wg, ffn_wu, ffn_wd, w_in, qk_g, lam_p, subln_g, w_ba, w_bb,
           w_gate, b_gate, w_o, final_g):
    batch, s, d = x.shape
    assert batch == 1 and s % KEY_TILE == 0 and MIX_ROWS == KEY_TILE
    depth = ada_w.shape[0]
    h = x.reshape(s, d)

    mod4 = _ada_mod(c.reshape(d, 1), ada_w, ada_b).reshape(depth, N_ADA, 1, d)
    norm_g4 = norm_g.reshape(depth, 3, 1, d)
    angr_t, angc_t = _axial_angles_t(s)
    slopes = 2.0 ** (-8.0 * jnp.arange(1, B_HEADS + 1, dtype=F32) / B_HEADS)
    coef = jnp.stack([slopes * (LOG2E / F8_UNSCALE), 1.0 / (slopes * LOG2E * KEY_TILE)])
    fg = final_g.reshape(1, d)
    wg, wu, wd = ffn_wg.astype(BF16), ffn_wu.astype(BF16), ffn_wd.astype(BF16)
    w_in_t = jnp.swapaxes(w_in, 1, 2).astype(BF16)
    qkg_t = jnp.swapaxes(qk_g, 1, 2)
    w_ba16, w_bb16, w_gate16, w_o16 = (w.astype(BF16) for w in (w_ba, w_bb, w_gate, w_o))
    b_gate3 = b_gate.reshape(depth, 1, -1)
    subln_col = subln_g.reshape(depth, B_V_DIM, 1)
    nk = s // KEY_TILE

    for l in range(depth):
        lam_init = 0.8 - 0.6 * math.exp(-0.3 * l)
        h = _ffn(h, l, 0, norm_g4, mod4, wg, wu, wd, fg, final_norm=False)
        qta, ka, vta, qtd, kd, vtd, qn, kn, qfa, kfa, qfd, kfd = _mix_in(
            h, l, norm_g4, mod4, w_in_t, angr_t, angc_t, qkg_t)
        oa = _gqa(qta, ka.reshape(nk, KEY_TILE, A_KV_COLS), vta, qn, kn,
                  qfa, kfa.reshape(A_KV_HEADS, nk, KEY_TILE, F8_COLS))
        od = _diff(l, coef, qtd, kd.reshape(nk, KEY_TILE, B_QK_COLS), vtd, qn, kn,
                   qfd, kfd.reshape(2 * B_HEADS, nk, KEY_TILE, F8_COLS), lam_p, subln_col, lam_init=lam_init)
        h = _mix_out(h, l, norm_g4, mod4, oa, od, w_ba16, w_bb16, w_gate16, b_gate3, w_o16)
        h = _ffn(h, l, 1, norm_g4, mod4, wg, wu, wd, fg, final_norm=(l == depth - 1))
    return h.reshape(batch, s, d)
```

```python
import functools
import math

import jax
import jax.numpy as jnp
from jax import lax
from jax.experimental import pallas as pl
from jax.experimental.pallas import tpu as pltpu

F32 = jnp.float32
BF16 = jnp.bfloat16
F8 = jnp.float8_e4m3fn

GRID_W = 64
HEAD_DIM = 64
A_Q_HEADS = 8
A_KV_HEADS = 2
A_GROUP = A_Q_HEADS // A_KV_HEADS
B_HEADS = 4
B_V_DIM = 2 * HEAD_DIM
A_Q_COLS = A_Q_HEADS * HEAD_DIM
A_KV_COLS = A_KV_HEADS * HEAD_DIM
B_QK_COLS = 2 * B_HEADS * HEAD_DIM
B_V_COLS = B_HEADS * B_V_DIM
N_ADA = 9
EPS = 1e-6
ROPE_THETA = 10000.0
ROPE_AXIS_DIM = HEAD_DIM // 2
ROPE_HALF = ROPE_AXIS_DIM // 2

LOG2E = math.log2(math.e)
QK_SCALE = HEAD_DIM ** -0.5
NEG_BIG = -1e30

BF16_SUBLANES = 16
LANES = 128
ONES_ROWS = BF16_SUBLANES
SKIP_LOG2 = 64.0
FAST_LOG2_LIMIT = 60.0
F8_Q_SCALE = 2.0 ** 6
F8_K_SCALE = 2.0 ** 2
F8_UNSCALE = 1.0 / (F8_Q_SCALE * F8_K_SCALE)
F8_SAFE_MAX = 256.0
F8_COLS = 4 * HEAD_DIM
NORM_ROWS_A = 8
NORM_ROWS = NORM_ROWS_A + 2 * B_HEADS

VMEM_LIMIT_BYTES = 56 * 1024 * 1024

FFN_ROWS = 512
MIX_ROWS = 512
KEY_TILE = 512
GQA_Q_TILE = 512
DIFF_Q_TILE = 512
ADA_COLS = 1152


def _params(*sem):
    return pltpu.CompilerParams(dimension_semantics=sem, vmem_limit_bytes=VMEM_LIMIT_BYTES)


def _layer_vec(l, j, d):
    return pl.BlockSpec((None, None, 1, d), lambda *_: (l, j, 0, 0))


def _layer_mat(l, a, *, single_buffer=False):
    nd = a.ndim - 1
    mode = dict(pipeline_mode=pl.Buffered(1)) if single_buffer else {}
    return pl.BlockSpec((None,) + a.shape[1:], lambda *_: (l,) + (0,) * nd, **mode)


def _rms_rows(x):
    return x * lax.rsqrt(jnp.mean(x * x, axis=-1, keepdims=True) + EPS)


def _ada_kernel(c_ref, w_ref, b_ref, o_ref):
    c = c_ref[...]
    act = c / (1.0 + jnp.exp(-c))
    o_ref[...] = jnp.sum(w_ref[...] * act, axis=0, keepdims=True) + b_ref[...]


def _ada_mod(c_col, ada_w, ada_b):
    n_layers, d, n = ada_w.shape
    return pl.pallas_call(
        _ada_kernel,
        grid=(n_layers, n // ADA_COLS),
        in_specs=[
            pl.BlockSpec((d, 1), lambda l, j: (0, 0)),
            pl.BlockSpec((None, d, ADA_COLS), lambda l, j: (l, 0, j)),
            pl.BlockSpec((None, 1, ADA_COLS), lambda l, j: (l, 0, j)),
        ],
        out_specs=pl.BlockSpec((None, 1, ADA_COLS), lambda l, j: (l, 0, j)),
        out_shape=jax.ShapeDtypeStruct((n_layers, 1, n), F32),
        compiler_params=_params("parallel", "parallel"),
        name="ada_mod",
    )(c_col, ada_w, ada_b.reshape(n_layers, 1, n))


def _ffn_kernel(h_ref, ng_ref, sh_ref, sc_ref, gt_ref, wg_ref, wu_ref, wd_ref, fg_ref, o_ref, *, final_norm):
    h = h_ref[...]
    n = _rms_rows(h) * ng_ref[...]
    n = (n * (1.0 + sc_ref[...]) + sh_ref[...]).astype(BF16)
    hg = jnp.dot(n, wg_ref[...], preferred_element_type=F32)
    hu = jnp.dot(n, wu_ref[...], preferred_element_type=F32)
    a = (hg / (1.0 + jnp.exp(-hg))) * hu
    out = h + (0.5 * gt_ref[...]) * jnp.dot(a.astype(BF16), wd_ref[...], preferred_element_type=F32)
    if final_norm:
        out = _rms_rows(out) * fg_ref[...]
    o_ref[...] = out


def _ffn(h, l, which, norm_g4, mod4, wg, wu, wd, fg, *, final_norm):
    s, d = h.shape
    m0 = 6 * which

    def resident(a):
        return pl.BlockSpec((None, None) + a.shape[2:], lambda i: (l, which, 0, 0),
                            pipeline_mode=pl.Buffered(1))

    return pl.pallas_call(
        functools.partial(_ffn_kernel, final_norm=final_norm),
        grid=(s // FFN_ROWS,),
        in_specs=[
            pl.BlockSpec((FFN_ROWS, d), lambda i: (i, 0)),
            _layer_vec(l, 2 * which, d),
            _layer_vec(l, m0, d), _layer_vec(l, m0 + 1, d), _layer_vec(l, m0 + 2, d),
            resident(wg), resident(wu), resident(wd),
            pl.BlockSpec((1, d), lambda i: (0, 0)),
        ],
        out_specs=pl.BlockSpec((FFN_ROWS, d), lambda i: (i, 0)),
        out_shape=jax.ShapeDtypeStruct((s, d), F32),
        compiler_params=_params("parallel"),
        name="ffn_final" if final_norm else "ffn",
    )(h, norm_g4, mod4, mod4, mod4, wg, wu, wd, fg)


def _rope_t(x, cr, sr, cc, sc):
    h = ROPE_HALF
    x1r, x2r, x1c, x2c = x[0:h], x[h:2 * h], x[2 * h:3 * h], x[3 * h:4 * h]
    return jnp.concatenate(
        [x1r * cr - x2r * sr, x2r * cr + x1r * sr, x1c * cc - x2c * sc, x2c * cc + x1c * sc], axis=0)


def _head_norm_t(x, g_col):
    ms = jnp.mean(x * x, axis=0, keepdims=True)
    return x * lax.rsqrt(ms + EPS) * g_col


def _max_sq_norm(x_bf16):
    xf = x_bf16.astype(F32)
    n2 = jnp.sum(xf * xf, axis=0, keepdims=True)
    return jnp.broadcast_to(jnp.max(n2, axis=1, keepdims=True), (1, LANES))


def _f8_split(x):
    hi = x.astype(F8).astype(F32)
    return hi, ((x - hi) * 16.0).astype(F8).astype(F32)


def _f8_query_rows(q_bf16, *, weighted):
    hi, lo16 = _f8_split(q_bf16.astype(F32) * F8_Q_SCALE)
    rows = [hi, hi * 0.0625, lo16 * 0.0625, lo16 * 0.00390625] if weighted else [hi, hi, lo16, lo16]
    return jnp.concatenate(rows, axis=0).astype(F8)


def _f8_key_rows(k_bf16, *, weighted):
    hi, lo16 = _f8_split(k_bf16.astype(F32) * F8_K_SCALE)
    rows = [hi, lo16 * 0.0625, hi * 0.0625, lo16 * 0.00390625] if weighted else [hi, lo16, hi, lo16]
    return jnp.concatenate(rows, axis=0)


def _with_ones(v_t):
    return jnp.concatenate([v_t, jnp.ones((ONES_ROWS, v_t.shape[1]), F32)], axis=0).astype(BF16)


def _mix_in_kernel(h_ref, ng_ref, sh_ref, sc_ref, wt_ref, angr_ref, angc_ref, qkg_ref,
                   qta_ref, ka_ref, vta_ref, qtd_ref, kd_ref, vtd_ref, qn_ref, kn_ref,
                   qfa_ref, kfa_ref, qfd_ref, kfd_ref):
    n = _rms_rows(h_ref[...]) * ng_ref[...]
    n = (n * (1.0 + sc_ref[...]) + sh_ref[...]).astype(BF16)
    pt = lax.dot_general(wt_ref[...], n, (((1,), (1,)), ((), ())), preferred_element_type=F32)
    t = pt.shape[1]

    cr, sr = jnp.cos(angr_ref[...]), jnp.sin(angr_ref[...])
    cc, sc = jnp.cos(angc_ref[...]), jnp.sin(angc_ref[...])
    gq = qkg_ref[:, 0:1]
    gk = qkg_ref[:, 1:2]
    zeros = jnp.zeros((HEAD_DIM, t), BF16)
    qscale = QK_SCALE * LOG2E

    qn, kn = [], []
    for hd in range(A_Q_HEADS):
        q = _rope_t(_head_norm_t(pt[hd * HEAD_DIM:(hd + 1) * HEAD_DIM], gq), cr, sr, cc, sc) * qscale
        q = q.astype(BF16)
        qta_ref[hd] = jnp.concatenate([q, zeros] if hd // A_GROUP == 0 else [zeros, q], axis=0)
        qfa_ref[hd] = _f8_query_rows(q, weighted=False)
        qn.append(_max_sq_norm(q))

    c1 = A_Q_COLS
    c2 = c1 + A_KV_COLS
    c3 = c2 + A_KV_COLS
    c4 = c3 + B_QK_COLS
    c5 = c4 + B_QK_COLS
    kt = jnp.concatenate(
        [_rope_t(_head_norm_t(pt[c1 + j * HEAD_DIM:c1 + (j + 1) * HEAD_DIM], gk), cr, sr, cc, sc)
         for j in range(A_KV_HEADS)], axis=0)
    kt = kt.astype(BF16)
    ka_ref[...] = kt.T
    kn += [_max_sq_norm(kt[j * HEAD_DIM:(j + 1) * HEAD_DIM]) for j in range(A_KV_HEADS)]
    for j in range(A_KV_HEADS):
        kfa_ref[j] = _f8_key_rows(kt[j * HEAD_DIM:(j + 1) * HEAD_DIM], weighted=True).astype(BF16).T.astype(F8)
    kn.append(jnp.zeros((NORM_ROWS_A - A_KV_HEADS, LANES), F32))
    for j in range(A_KV_HEADS):
        vta_ref[j] = _with_ones(pt[c2 + j * HEAD_DIM:c2 + (j + 1) * HEAD_DIM])

    for cmb in range(2 * B_HEADS):
        q = (pt[c3 + cmb * HEAD_DIM:c3 + (cmb + 1) * HEAD_DIM] * qscale).astype(BF16)
        qtd_ref[cmb] = jnp.concatenate([q, zeros] if cmb % 2 == 0 else [zeros, q], axis=0)
        qfd_ref[cmb] = _f8_query_rows(q, weighted=True)
        qn.append(_max_sq_norm(q))
        k = pt[c4 + cmb * HEAD_DIM:c4 + (cmb + 1) * HEAD_DIM].astype(BF16)
        kfd_ref[cmb] = _f8_key_rows(k, weighted=False).astype(BF16).T.astype(F8)
        kn.append(_max_sq_norm(k))
    qn_ref[...] = jnp.concatenate(qn, axis=0)
    kn_ref[...] = jnp.concatenate(kn, axis=0)
    kd_ref[...] = pt[c4:c5].T.astype(BF16)
    for hd in range(B_HEADS):
        vtd_ref[hd] = _with_ones(pt[c5 + hd * B_V_DIM:c5 + (hd + 1) * B_V_DIM])


def _mix_in(h, l, norm_g4, mod4, w_in_t, angr_t, angc_t, qkg_t):
    s, d = h.shape
    tm = MIX_ROWS
    nt = s // tm
    n_sets = 2 * B_HEADS
    assert A_Q_HEADS == NORM_ROWS_A
    out_shape = (
        jax.ShapeDtypeStruct((A_Q_HEADS, 2 * HEAD_DIM, s), BF16),
        jax.ShapeDtypeStruct((s, A_KV_COLS), BF16),
        jax.ShapeDtypeStruct((A_KV_HEADS, nt, HEAD_DIM + ONES_ROWS, tm), BF16),
        jax.ShapeDtypeStruct((n_sets, 2 * HEAD_DIM, s), BF16),
        jax.ShapeDtypeStruct((s, B_QK_COLS), BF16),
        jax.ShapeDtypeStruct((B_HEADS, nt, B_V_DIM + ONES_ROWS, tm), BF16),
        jax.ShapeDtypeStruct((nt, NORM_ROWS, LANES), F32),
        jax.ShapeDtypeStruct((nt, NORM_ROWS, LANES), F32),
        jax.ShapeDtypeStruct((A_Q_HEADS, F8_COLS, s), F8),
        jax.ShapeDtypeStruct((A_KV_HEADS, s, F8_COLS), F8),
        jax.ShapeDtypeStruct((n_sets, F8_COLS, s), F8),
        jax.ShapeDtypeStruct((n_sets, s, F8_COLS), F8),
    )
    out_specs = (
        pl.BlockSpec((A_Q_HEADS, 2 * HEAD_DIM, tm), lambda i: (0, 0, i)),
        pl.BlockSpec((tm, A_KV_COLS), lambda i: (i, 0)),
        pl.BlockSpec((A_KV_HEADS, None, HEAD_DIM + ONES_ROWS, tm), lambda i: (0, i, 0, 0)),
        pl.BlockSpec((n_sets, 2 * HEAD_DIM, tm), lambda i: (0, 0, i)),
        pl.BlockSpec((tm, B_QK_COLS), lambda i: (i, 0)),
        pl.BlockSpec((B_HEADS, None, B_V_DIM + ONES_ROWS, tm), lambda i: (0, i, 0, 0)),
        pl.BlockSpec((None, NORM_ROWS, LANES), lambda i: (i, 0, 0)),
        pl.BlockSpec((None, NORM_ROWS, LANES), lambda i: (i, 0, 0)),
        pl.BlockSpec((A_Q_HEADS, F8_COLS, tm), lambda i: (0, 0, i)),
        pl.BlockSpec((A_KV_HEADS, tm, F8_COLS), lambda i: (0, i, 0)),
        pl.BlockSpec((n_sets, F8_COLS, tm), lambda i: (0, 0, i)),
        pl.BlockSpec((n_sets, tm, F8_COLS), lambda i: (0, i, 0)),
    )
    return pl.pallas_call(
        _mix_in_kernel,
        grid=(nt,),
        in_specs=[
            pl.BlockSpec((tm, d), lambda i: (i, 0)),
            _layer_vec(l, 1, d), _layer_vec(l, 3, d), _layer_vec(l, 4, d),
            _layer_mat(l, w_in_t, single_buffer=True),
            pl.BlockSpec((ROPE_HALF, tm), lambda i: (0, i)),
            pl.BlockSpec((ROPE_HALF, tm), lambda i: (0, i)),
            _layer_mat(l, qkg_t),
        ],
        out_specs=out_specs,
        out_shape=out_shape,
        compiler_params=_params("parallel"),
        name="mix_in",
    )(h, norm_g4, mod4, mod4, w_in_t, angr_t, angc_t, qkg_t)


def _softmax_tile(s, m_ref, alpha_ref, p_ref):
    tk, n = s.shape
    part = jnp.max(s.reshape(tk // BF16_SUBLANES, BF16_SUBLANES, n), axis=0)
    m_old = m_ref[...]
    m_new = jnp.maximum(m_old, jnp.max(part.astype(F32), axis=0, keepdims=True))
    alpha_ref[...] = jnp.exp2(m_old - m_new)
    m_ref[...] = m_new
    p_ref[...] = jnp.exp2(s - m_new.astype(BF16))


def _pipelined_sweep(prep, qk, softmax, pv, n_chunks, lo, n_pairs, last):
    ctx = prep(lo)
    for c in range(n_chunks):
        qk(lo, 0, c, ctx)

    def body(j, carry):
        a = lo + 2 * j
        prev = jnp.maximum(a - 1, 0)
        nxt = jnp.minimum(a + 2, last)
        ctx = prep(a + 1)
        for c in range(n_chunks):
            softmax(0, c)
            pv(prev, 1, c)
            qk(a + 1, 1, c, ctx)
        ctx = prep(nxt)
        for c in range(n_chunks):
            softmax(1, c)
            pv(a, 0, c)
            qk(nxt, 0, c, ctx)
        return carry

    lax.fori_loop(0, n_pairs, body, 0)
    for c in range(n_chunks):
        pv(last, 1, c)


def _direct_sweep(prep, qk_exp, pv, n_chunks, lo, n_tiles, last, unroll):
    def phase(kt_new, slot_new, kt_old):
        ctx = prep(kt_new)
        for c in range(n_chunks):
            qk_exp(kt_new, slot_new, c, ctx)
            pv(kt_old, 1 - slot_new, c)

    ctx = prep(lo)
    for c in range(n_chunks):
        qk_exp(lo, 0, c, ctx)

    def pairs(first, n_pairs):
        def body(j, carry):
            a = first + 2 * n_pairs * j
            for i in range(n_pairs):
                phase(a + 2 * i + 1, 1, a + 2 * i)
                phase(a + 2 * i + 2, 0, a + 2 * i + 1)
            return carry
        return body

    assert unroll[-1] == 1
    static = isinstance(n_tiles, int)
    left = (n_tiles - 1) // 2 if static else lax.shift_right_logical(n_tiles - 1, 1)
    first = lo
    for size in unroll:
        trips = left // size if static else lax.div(left, size)
        lax.fori_loop(0, trips, pairs(first, size), 0)
        first = first + 2 * size * trips
        left = left - size * trips

    def tail_odd():
        for c in range(n_chunks):
            pv(last, 0, c)
        return 0

    def tail_even():
        phase(last, 1, last - 1)
        for c in range(n_chunks):
            pv(last, 1, c)
        return 0

    if isinstance(n_tiles, int):
        (tail_odd if n_tiles % 2 else tail_even)()
    else:
        lax.cond(jnp.bitwise_and(n_tiles, 1) == 1, tail_odd, tail_even)


def _score_bound(qn_tile, kn_ref, q_rows, k_row_of_q):
    k_max = jnp.max(kn_ref[...], axis=0)
    row = lax.broadcasted_iota(jnp.int32, k_max.shape, 0)
    sel = q_rows(row)
    q2 = jnp.where(sel, qn_tile, 0.0)
    k2 = jnp.where(sel, k_row_of_q(row, k_max), 0.0)
    fp8_ok = jnp.logical_and(jnp.max(q2) * F8_Q_SCALE ** 2 <= F8_SAFE_MAX ** 2,
                             jnp.max(k2) * F8_K_SCALE ** 2 <= F8_SAFE_MAX ** 2)
    return jnp.max(jnp.sqrt(q2 * k2)), fp8_ok


GQA_CHUNKS = 8
GQA_UNROLL = (15, 1)


def _gqa_kernel(qt_ref, k_ref, vt_ref, qn_ref, kn_ref, qf_ref, kf_ref, o_ref, q_scr, qf_scr, s_buf, p_buf,
                alpha_buf, m_scr, acc_scr):
    g = pl.program_id(0)
    qi = pl.program_id(1)
    nk, tk, _ = k_ref.shape
    tq = qt_ref.shape[2]
    w = A_GROUP * tq // GQA_CHUNKS
    for c in range(GQA_CHUNKS):
        for col in range(c * w, (c + 1) * w, min(w, tq)):
            hh, j, n = col // tq, col % tq, min(w, tq)
            q_scr[c, :, col - c * w:col - c * w + n] = qt_ref[hh, :, j:j + n]
            qf_scr[c, :, col - c * w:col - c * w + n] = qf_ref[hh, :, j:j + n]
    acc_scr[...] = jnp.zeros_like(acc_scr)

    def qk(kt, slot, c, _):
        s_buf[slot, c] = jnp.dot(k_ref[kt], q_scr[c], preferred_element_type=F32).astype(BF16)

    def softmax(slot, c):
        _softmax_tile(s_buf[slot, c], m_scr.at[c], alpha_buf.at[slot, c], p_buf.at[slot, c])

    def pv(kt, slot, c):
        acc_scr[c] = alpha_buf[slot, c] * acc_scr[c] + jnp.dot(
            vt_ref[kt], p_buf[slot, c], preferred_element_type=F32)

    def qk_exp(kt, slot, c, _):
        s = jnp.dot(kf_ref[kt], qf_scr[c], preferred_element_type=F32)
        p_buf[slot, c] = jnp.exp2(s.astype(BF16) * F8_UNSCALE)

    def pv_plain(kt, slot, c):
        acc_scr[c] += jnp.dot(vt_ref[kt], p_buf[slot, c], preferred_element_type=F32)

    def kmax_of_group(row, k_max):
        return jnp.max(jnp.where(row == g, k_max, 0.0), axis=0, keepdims=True)

    bound, fp8_ok = _score_bound(
        qn_ref[lax.div(qi * tq, tk)], kn_ref,
        lambda row: jnp.logical_and(row >= g * A_GROUP, row < (g + 1) * A_GROUP), kmax_of_group)

    def direct():
        _direct_sweep(lambda kt: None, qk_exp, pv_plain, GQA_CHUNKS, 0, nk, nk - 1, GQA_UNROLL)
        return 0

    def online():
        m_scr[...] = jnp.full_like(m_scr, NEG_BIG)
        p_buf[1] = jnp.zeros(p_buf.shape[1:], BF16)
        alpha_buf[1] = jnp.ones(alpha_buf.shape[1:], F32)
        _pipelined_sweep(lambda kt: None, qk, softmax, pv, GQA_CHUNKS, 0, nk // 2, nk - 1)
        return 0

    lax.cond(jnp.logical_and(bound <= FAST_LOG2_LIMIT, fp8_ok), direct, online)
    cols = []
    for c in range(GQA_CHUNKS):
        acc = acc_scr[c]
        cols.append(acc[:HEAD_DIM] * (1.0 / acc[HEAD_DIM:HEAD_DIM + 1]))
    ot = jnp.concatenate(cols, axis=1)
    ot = jnp.concatenate([ot[:, hh * tq:(hh + 1) * tq] for hh in range(A_GROUP)], axis=0)
    o_ref[...] = ot.T.astype(BF16)


def _gqa(qta, ka3, vta, qn, kn, qfa, kfa4):
    s = qta.shape[2]
    nk, tk, _ = ka3.shape
    tq = GQA_Q_TILE
    assert nk % 2 == 0 and tk % tq == 0 and tk == MIX_ROWS
    nc = GQA_CHUNKS
    w = A_GROUP * tq // nc
    v_rows = vta.shape[2]
    return pl.pallas_call(
        _gqa_kernel,
        grid=(A_KV_HEADS, s // tq),
        in_specs=[
            pl.BlockSpec((A_GROUP, 2 * HEAD_DIM, tq), lambda g, i: (g, 0, i)),
            pl.BlockSpec((nk, tk, A_KV_COLS), lambda g, i: (0, 0, 0), pipeline_mode=pl.Buffered(1)),
            pl.BlockSpec((None, nk, v_rows, tk), lambda g, i: (g, 0, 0, 0), pipeline_mode=pl.Buffered(1)),
            pl.BlockSpec(qn.shape, lambda g, i: (0, 0, 0)),
            pl.BlockSpec(kn.shape, lambda g, i: (0, 0, 0)),
            pl.BlockSpec((A_GROUP, F8_COLS, tq), lambda g, i: (g, 0, i)),
            pl.BlockSpec((None, nk, tk, F8_COLS), lambda g, i: (g, 0, 0, 0), pipeline_mode=pl.Buffered(1)),
        ],
        out_specs=pl.BlockSpec((tq, A_GROUP * HEAD_DIM), lambda g, i: (i, g)),
        out_shape=jax.ShapeDtypeStruct((s, A_Q_COLS), BF16),
        scratch_shapes=[
            pltpu.VMEM((nc, 2 * HEAD_DIM, w), BF16),
            pltpu.VMEM((nc, F8_COLS, w), F8),
            pltpu.VMEM((2, nc, tk, w), BF16),
            pltpu.VMEM((2, nc, tk, w), BF16),
            pltpu.VMEM((2, nc, 1, w), F32),
            pltpu.VMEM((nc, 1, w), F32),
            pltpu.VMEM((nc, v_rows, w), F32),
        ],
        compiler_params=_params("parallel", "parallel"),
        name="gqa_attn",
    )(qta, ka3, vta, qn, kn, qfa, kfa4)


DIFF_CHUNKS = 2
DIFF_UNROLL = (5, 3, 1)


def _diff_kernel(coef_ref, qt0_ref, qt1_ref, k0_ref, k1_ref, vt_ref, qn_ref, kn_ref, lamp_ref, sg_ref,
                 qf0_ref, qf1_ref, kf0_ref, kf1_ref,
                 o_ref, rel_scr, s_buf, p_buf, alpha_buf, m_scr, acc_scr, *, lam_init):
    hd = pl.program_id(0)
    qi = pl.program_id(1)
    nk, tk, _ = k0_ref.shape
    tq = qt0_ref.shape[1]
    slope_f8 = coef_ref[0, hd]
    inv_tile_drop = coef_ref[1, hd]
    acc_scr[...] = jnp.zeros_like(acc_scr)

    @pl.when(qi == 0)
    def _():
        rel_scr[...] = (lax.broadcasted_iota(jnp.int32, (tk, tq), 1)
                        - lax.broadcasted_iota(jnp.int32, (tk, tq), 0)).astype(F32) * slope_f8

    bound, fp8_ok = _score_bound(
        qn_ref[qi], kn_ref,
        lambda row: jnp.logical_or(row == NORM_ROWS_A + hd, row == NORM_ROWS_A + B_HEADS + hd),
        lambda row, k_max: k_max)
    use_direct = jnp.logical_and(bound <= FAST_LOG2_LIMIT, fp8_ok)
    drop = 2.0 * bound + SKIP_LOG2
    reach = jnp.minimum(jnp.floor(drop * inv_tile_drop) + 1.0, float(nk)).astype(jnp.int32)
    lo = jnp.maximum(qi - reach, 0)
    hi = jnp.minimum(qi + reach, nk - 1)

    q0 = qi * tq
    w = tq // DIFF_CHUNKS
    n_chunks = 2 * DIFF_CHUNKS
    cols = [slice((c // 2) * w, (c // 2 + 1) * w) for c in range(n_chunks)]
    qts = (qt0_ref, qt1_ref)
    ks = (k0_ref, k1_ref)
    qfs = (qf0_ref, qf1_ref)
    kfs = (kf0_ref, kf1_ref)

    def tile_ctx(kt):
        return {"off": (q0 - kt * tk).astype(F32) * slope_f8}

    def penalty_scaled(ctx, c):
        if c // 2 not in ctx:
            ctx[c // 2] = jnp.abs(rel_scr[:, cols[c]] + ctx["off"])
        return ctx[c // 2]

    def qk(kt, slot, c, ctx):
        mp = c % 2
        s = jnp.dot(ks[mp][kt], qts[mp][:, cols[c]], preferred_element_type=F32)
        s_buf[slot, c] = (s - penalty_scaled(ctx, c) * F8_UNSCALE).astype(BF16)

    def softmax(slot, c):
        _softmax_tile(s_buf[slot, c], m_scr.at[c], alpha_buf.at[slot, c], p_buf.at[slot, c])

    def pv(kt, slot, c):
        acc_scr[c] = alpha_buf[slot, c] * acc_scr[c] + jnp.dot(
            vt_ref[kt], p_buf[slot, c], preferred_element_type=F32)

    def qk_exp(kt, slot, c, ctx):
        mp = c % 2
        s = jnp.dot(kfs[mp][kt], qfs[mp][:, cols[c]], preferred_element_type=F32) - penalty_scaled(ctx, c)
        p_buf[slot, c] = jnp.exp2(s.astype(BF16) * F8_UNSCALE)

    def pv_plain(kt, slot, c):
        acc_scr[c] += jnp.dot(vt_ref[kt], p_buf[slot, c], preferred_element_type=F32)

    def direct_all():
        _direct_sweep(tile_ctx, qk_exp, pv_plain, n_chunks, 0, nk, nk - 1, ((nk - 1) // 2, 1))
        return 0

    def direct_some():
        _direct_sweep(tile_ctx, qk_exp, pv_plain, n_chunks, lo, hi - lo + 1, hi, DIFF_UNROLL)
        return 0

    def direct():
        return lax.cond(hi - lo + 1 == nk, direct_all, direct_some)

    def online():
        m_scr[...] = jnp.full_like(m_scr, NEG_BIG)
        p_buf[1] = jnp.zeros(p_buf.shape[1:], BF16)
        alpha_buf[1] = jnp.ones(alpha_buf.shape[1:], F32)
        odd = jnp.bitwise_and(hi - lo + 1, 1)
        grow_hi = jnp.where(hi < nk - 1, odd, 0)
        hi2 = hi + grow_hi
        lo2 = lo - (odd - grow_hi)
        _pipelined_sweep(tile_ctx, qk, softmax, pv, n_chunks, lo2, lax.shift_right_logical(hi2 - lo2 + 1, 1), hi2)
        return 0

    lax.cond(use_direct, direct, online)

    lp = lamp_ref[...]
    lam = (jnp.exp(jnp.sum(lp[0:1] * lp[1:2], axis=1, keepdims=True))
           - jnp.exp(jnp.sum(lp[2:3] * lp[3:4], axis=1, keepdims=True)) + lam_init)
    a0, a1 = [jnp.concatenate([acc_scr[2 * j + mp] for j in range(DIFF_CHUNKS)], axis=1)
              for mp in range(2)]
    ot = (a0[:B_V_DIM] * (1.0 / a0[B_V_DIM:B_V_DIM + 1])
          - lam * (a1[:B_V_DIM] * (1.0 / a1[B_V_DIM:B_V_DIM + 1])))
    ot = ot * lax.rsqrt(jnp.mean(ot * ot, axis=0, keepdims=True) + EPS)
    ot = ot * sg_ref[...] * (1.0 - lam_init)
    o_ref[...] = ot.T.astype(BF16)


def _diff(l, coef, qtd, kd3, vtd, qn, kn, qfd, kfd4, lam_p, subln_col, *, lam_init):
    s = qtd.shape[2]
    nk, tk, _ = kd3.shape
    tq = DIFF_Q_TILE
    assert nk % 2 == 0 and tq == tk == MIX_ROWS
    lanes = 2 * HEAD_DIM
    v_rows = vtd.shape[2]
    nc = 2 * DIFF_CHUNKS
    w = tq // DIFF_CHUNKS
    return pl.pallas_call(
        functools.partial(_diff_kernel, lam_init=lam_init),
        grid=(B_HEADS, s // tq),
        in_specs=[
            pl.BlockSpec(memory_space=pltpu.SMEM),
            pl.BlockSpec((None, lanes, tq), lambda h, i: (h, 0, i)),
            pl.BlockSpec((None, lanes, tq), lambda h, i: (B_HEADS + h, 0, i)),
            pl.BlockSpec((nk, tk, lanes), lambda h, i: (0, 0, h // 2), pipeline_mode=pl.Buffered(1)),
            pl.BlockSpec((nk, tk, lanes), lambda h, i: (0, 0, B_HEADS // 2 + h // 2),
                         pipeline_mode=pl.Buffered(1)),
            pl.BlockSpec((None, nk, v_rows, tk), lambda h, i: (h, 0, 0, 0)),
            pl.BlockSpec(qn.shape, lambda h, i: (0, 0, 0)),
            pl.BlockSpec(kn.shape, lambda h, i: (0, 0, 0)),
            _layer_mat(l, lam_p),
            _layer_mat(l, subln_col),
            pl.BlockSpec((None, F8_COLS, tq), lambda h, i: (h, 0, i)),
            pl.BlockSpec((None, F8_COLS, tq), lambda h, i: (B_HEADS + h, 0, i)),
            pl.BlockSpec((None, nk, tk, F8_COLS), lambda h, i: (h, 0, 0, 0)),
            pl.BlockSpec((None, nk, tk, F8_COLS), lambda h, i: (B_HEADS + h, 0, 0, 0)),
        ],
        out_specs=pl.BlockSpec((tq, B_V_DIM), lambda h, i: (i, h)),
        out_shape=jax.ShapeDtypeStruct((s, B_V_COLS), BF16),
        scratch_shapes=[
            pltpu.VMEM((tk, tq), F32),
            pltpu.VMEM((2, nc, tk, w), BF16),
            pltpu.VMEM((2, nc, tk, w), BF16),
            pltpu.VMEM((2, nc, 1, w), F32),
            pltpu.VMEM((nc, 1, w), F32),
            pltpu.VMEM((nc, v_rows, w), F32),
        ],
        compiler_params=_params("arbitrary", "arbitrary"),
        name="diff_attn",
    )(coef, qtd, qtd, kd3, kd3, vtd, qn, kn, lam_p, subln_col, qfd, qfd, kfd4, kfd4)


def _mix_out_kernel(h_ref, ng_ref, sh_ref, sc_ref, gt_ref, oa_ref, od_ref, wba_ref, wbb_ref,
                    wgate_ref, bgate_ref, wo_ref, o_ref):
    h = h_ref[...]
    d = h.shape[1]
    n = _rms_rows(h) * ng_ref[...]
    n = (n * (1.0 + sc_ref[...]) + sh_ref[...]).astype(BF16)
    z = jnp.dot(n, wgate_ref[...], preferred_element_type=F32) + bgate_ref[...]
    g = 1.0 / (1.0 + jnp.exp(-z))
    ya = jnp.dot(oa_ref[...], wba_ref[...], preferred_element_type=F32)
    yb = jnp.dot(od_ref[...], wbb_ref[...], preferred_element_type=F32)
    mix = (g[:, :d] * ya + g[:, d:] * yb).astype(BF16)
    y = jnp.dot(mix, wo_ref[...], preferred_element_type=F32)
    o_ref[...] = h + gt_ref[...] * y


def _mix_out(h, l, norm_g4, mod4, oa, od, w_ba, w_bb, w_gate, b_gate, w_o):
    s, d = h.shape
    tm = MIX_ROWS
    return pl.pallas_call(
        _mix_out_kernel,
        grid=(s // tm,),
        in_specs=[
            pl.BlockSpec((tm, d), lambda i: (i, 0)),
            _layer_vec(l, 1, d), _layer_vec(l, 3, d), _layer_vec(l, 4, d), _layer_vec(l, 5, d),
            pl.BlockSpec((tm, oa.shape[1]), lambda i: (i, 0)),
            pl.BlockSpec((tm, od.shape[1]), lambda i: (i, 0)),
            _layer_mat(l, w_ba, single_buffer=True), _layer_mat(l, w_bb, single_buffer=True),
            _layer_mat(l, w_gate, single_buffer=True), _layer_mat(l, b_gate),
            _layer_mat(l, w_o, single_buffer=True),
        ],
        out_specs=pl.BlockSpec((tm, d), lambda i: (i, 0)),
        out_shape=jax.ShapeDtypeStruct((s, d), F32),
        compiler_params=_params("parallel"),
        name="mix_out",
    )(h, norm_g4, mod4, mod4, mod4, oa, od, w_ba, w_bb, w_gate, b_gate, w_o)


def _axial_angles_t(seq):
    rows = seq // GRID_W
    row = jnp.broadcast_to(jnp.arange(rows)[:, None], (rows, GRID_W)).reshape(seq) - rows // 2
    col = jnp.broadcast_to(jnp.arange(GRID_W)[None, :], (rows, GRID_W)).reshape(seq) - GRID_W // 2
    inv = 1.0 / (ROPE_THETA ** (jnp.arange(0, ROPE_AXIS_DIM, 2, dtype=F32) / ROPE_AXIS_DIM))
    return inv[:, None] * row.astype(F32)[None, :], inv[:, None] * col.astype(F32)[None, :]


def kernel(x, c, ada_w, ada_b, norm_g, ffn_wg, ffn_wu, ffn_wd, w_in, qk_g, lam_p, subln_g, w_ba, w_bb,
           w_gate, b_gate, w_o, final_g):
    batch, s, d = x.shape
    assert batch == 1 and s % KEY_TILE == 0 and MIX_ROWS == KEY_TILE
    depth = ada_w.shape[0]
    h = x.reshape(s, d)

    mod4 = _ada_mod(c.reshape(d, 1), ada_w, ada_b).reshape(depth, N_ADA, 1, d)
    norm_g4 = norm_g.reshape(depth, 3, 1, d)
    angr_t, angc_t = _axial_angles_t(s)
    slopes = 2.0 ** (-8.0 * jnp.arange(1, B_HEADS + 1, dtype=F32) / B_HEADS)
    coef = jnp.stack([slopes * (LOG2E / F8_UNSCALE), 1.0 / (slopes * LOG2E * KEY_TILE)])
    fg = final_g.reshape(1, d)
    wg, wu, wd = ffn_wg.astype(BF16), ffn_wu.astype(BF16), ffn_wd.astype(BF16)
    w_in_t = jnp.swapaxes(w_in, 1, 2).astype(BF16)
    qkg_t = jnp.swapaxes(qk_g, 1, 2)
    w_ba16, w_bb16, w_gate16, w_o16 = (w.astype(BF16) for w in (w_ba, w_bb, w_gate, w_o))
    b_gate3 = b_gate.reshape(depth, 1, -1)
    subln_col = subln_g.reshape(depth, B_V_DIM, 1)
    nk = s // KEY_TILE

    for l in range(depth):
        lam_init = 0.8 - 0.6 * math.exp(-0.3 * l)
        h = _ffn(h, l, 0, norm_g4, mod4, wg, wu, wd, fg, final_norm=False)
        qta, ka, vta, qtd, kd, vtd, qn, kn, qfa, kfa, qfd, kfd = _mix_in(
            h, l, norm_g4, mod4, w_in_t, angr_t, angc_t, qkg_t)
        oa = _gqa(qta, ka.reshape(nk, KEY_TILE, A_KV_COLS), vta, qn, kn,
                  qfa, kfa.reshape(A_KV_HEADS, nk, KEY_TILE, F8_COLS))
        od = _diff(l, coef, qtd, kd.reshape(nk, KEY_TILE, B_QK_COLS), vtd, qn, kn,
                   qfd, kfd.reshape(2 * B_HEADS, nk, KEY_TILE, F8_COLS), lam_p, subln_col, lam_init=lam_init)
        h = _mix_out(h, l, norm_g4, mod4, oa, od, w_ba16, w_bb16, w_gate16, b_gate3, w_o16)
        h = _ffn(h, l, 1, norm_g4, mod4, wg, wu, wd, fg, final_norm=(l == depth - 1))
    return h.reshape(batch, s, d)
```

```python
import functools
import math

import jax
import jax.numpy as jnp
from jax import lax
from jax.experimental import pallas as pl
from jax.experimental.pallas import tpu as pltpu

F32 = jnp.float32
BF16 = jnp.bfloat16
F8 = jnp.float8_e4m3fn

GRID_W = 64
HEAD_DIM = 64
A_Q_HEADS = 8
A_KV_HEADS = 2
A_GROUP = A_Q_HEADS // A_KV_HEADS
B_HEADS = 4
B_V_DIM = 2 * HEAD_DIM
A_Q_COLS = A_Q_HEADS * HEAD_DIM
A_KV_COLS = A_KV_HEADS * HEAD_DIM
B_QK_COLS = 2 * B_HEADS * HEAD_DIM
B_V_COLS = B_HEADS * B_V_DIM
N_ADA = 9
EPS = 1e-6
ROPE_THETA = 10000.0
ROPE_AXIS_DIM = HEAD_DIM // 2
ROPE_HALF = ROPE_AXIS_DIM // 2

LOG2E = math.log2(math.e)
QK_SCALE = HEAD_DIM ** -0.5
NEG_BIG = -1e30

BF16_SUBLANES = 16
LANES = 128
ONES_ROWS = BF16_SUBLANES
SKIP_LOG2 = 64.0
FAST_LOG2_LIMIT = 60.0
F8_Q_SCALE = 2.0 ** 6
F8_K_SCALE = 2.0 ** 2
F8_UNSCALE = 1.0 / (F8_Q_SCALE * F8_K_SCALE)
F8_SAFE_MAX = 256.0
F8_COLS = 4 * HEAD_DIM
NORM_ROWS_A = 8
NORM_ROWS = NORM_ROWS_A + 2 * B_HEADS

VMEM_LIMIT_BYTES = 56 * 1024 * 1024

FFN_ROWS = 512
MIX_ROWS = 512
KEY_TILE = 512
GQA_Q_TILE = 512
DIFF_Q_TILE = 512
ADA_COLS = 1152


def _params(*sem):
    return pltpu.CompilerParams(dimension_semantics=sem, vmem_limit_bytes=VMEM_LIMIT_BYTES)


def _layer_vec(l, j, d):
    return pl.BlockSpec((None, None, 1, d), lambda *_: (l, j, 0, 0))


def _layer_mat(l, a, *, single_buffer=False):
    nd = a.ndim - 1
    mode = dict(pipeline_mode=pl.Buffered(1)) if single_buffer else {}
    return pl.BlockSpec((None,) + a.shape[1:], lambda *_: (l,) + (0,) * nd, **mode)


def _rms_rows(x):
    return x * lax.rsqrt(jnp.mean(x * x, axis=-1, keepdims=True) + EPS)


def _ada_kernel(c_ref, w_ref, b_ref, o_ref):
    c = c_ref[...]
    act = c / (1.0 + jnp.exp(-c))
    o_ref[...] = jnp.sum(w_ref[...] * act, axis=0, keepdims=True) + b_ref[...]


def _ada_mod(c_col, ada_w, ada_b):
    n_layers, d, n = ada_w.shape
    return pl.pallas_call(
        _ada_kernel,
        grid=(n_layers, n // ADA_COLS),
        in_specs=[
            pl.BlockSpec((d, 1), lambda l, j: (0, 0)),
            pl.BlockSpec((None, d, ADA_COLS), lambda l, j: (l, 0, j)),
            pl.BlockSpec((None, 1, ADA_COLS), lambda l, j: (l, 0, j)),
        ],
        out_specs=pl.BlockSpec((None, 1, ADA_COLS), lambda l, j: (l, 0, j)),
        out_shape=jax.ShapeDtypeStruct((n_layers, 1, n), F32),
        compiler_params=_params("parallel", "parallel"),
        name="ada_mod",
    )(c_col, ada_w, ada_b.reshape(n_layers, 1, n))


def _ffn_kernel(h_ref, ng_ref, sh_ref, sc_ref, gt_ref, wg_ref, wu_ref, wd_ref, fg_ref, o_ref, *, final_norm):
    h = h_ref[...]
    n = _rms_rows(h) * ng_ref[...]
    n = (n * (1.0 + sc_ref[...]) + sh_ref[...]).astype(BF16)
    hg = jnp.dot(n, wg_ref[...], preferred_element_type=F32)
    hu = jnp.dot(n, wu_ref[...], preferred_element_type=F32)
    a = (hg / (1.0 + jnp.exp(-hg))) * hu
    out = h + (0.5 * gt_ref[...]) * jnp.dot(a.astype(BF16), wd_ref[...], preferred_element_type=F32)
    if final_norm:
        out = _rms_rows(out) * fg_ref[...]
    o_ref[...] = out


def _ffn(h, l, which, norm_g4, mod4, wg, wu, wd, fg, *, final_norm):
    s, d = h.shape
    m0 = 6 * which

    def resident(a):
        return pl.BlockSpec((None, None) + a.shape[2:], lambda i: (l, which, 0, 0),
                            pipeline_mode=pl.Buffered(1))

    return pl.pallas_call(
        functools.partial(_ffn_kernel, final_norm=final_norm),
        grid=(s // FFN_ROWS,),
        in_specs=[
            pl.BlockSpec((FFN_ROWS, d), lambda i: (i, 0)),
            _layer_vec(l, 2 * which, d),
            _layer_vec(l, m0, d), _layer_vec(l, m0 + 1, d), _layer_vec(l, m0 + 2, d),
            resident(wg), resident(wu), resident(wd),
            pl.BlockSpec((1, d), lambda i: (0, 0)),
        ],
        out_specs=pl.BlockSpec((FFN_ROWS, d), lambda i: (i, 0)),
        out_shape=jax.ShapeDtypeStruct((s, d), F32),
        compiler_params=_params("parallel"),
        name="ffn_final" if final_norm else "ffn",
    )(h, norm_g4, mod4, mod4, mod4, wg, wu, wd, fg)


def _rope_t(x, cr, sr, cc, sc):
    h = ROPE_HALF
    x1r, x2r, x1c, x2c = x[0:h], x[h:2 * h], x[2 * h:3 * h], x[3 * h:4 * h]
    return jnp.concatenate(
        [x1r * cr - x2r * sr, x2r * cr + x1r * sr, x1c * cc - x2c * sc, x2c * cc + x1c * sc], axis=0)


def _head_norm_t(x, g_col):
    ms = jnp.mean(x * x, axis=0, keepdims=True)
    return x * lax.rsqrt(ms + EPS) * g_col


def _max_sq_norm(x_bf16):
    xf = x_bf16.astype(F32)
    n2 = jnp.sum(xf * xf, axis=0, keepdims=True)
    return jnp.broadcast_to(jnp.max(n2, axis=1, keepdims=True), (1, LANES))


def _f8_split(x):
    hi = x.astype(F8).astype(F32)
    return hi, ((x - hi) * 16.0).astype(F8).astype(F32)


def _f8_query_rows(q_bf16, *, weighted):
    hi, lo16 = _f8_split(q_bf16.astype(F32) * F8_Q_SCALE)
    rows = [hi, hi * 0.0625, lo16 * 0.0625, lo16 * 0.00390625] if weighted else [hi, hi, lo16, lo16]
    return jnp.concatenate(rows, axis=0).astype(F8)


def _f8_key_rows(k_bf16, *, weighted):
    hi, lo16 = _f8_split(k_bf16.astype(F32) * F8_K_SCALE)
    rows = [hi, lo16 * 0.0625, hi * 0.0625, lo16 * 0.00390625] if weighted else [hi, lo16, hi, lo16]
    return jnp.concatenate(rows, axis=0)


def _with_ones(v_t):
    return jnp.concatenate([v_t, jnp.ones((ONES_ROWS, v_t.shape[1]), F32)], axis=0).astype(BF16)


ROPE_TABLE_COLS = 2048


def _rope_tables_kernel(angr_ref, angc_ref, o_ref):
    o_ref[0] = jnp.cos(angr_ref[...])
    o_ref[1] = jnp.sin(angr_ref[...])
    o_ref[2] = jnp.cos(angc_ref[...])
    o_ref[3] = jnp.sin(angc_ref[...])


def _rope_tables(angr_t, angc_t):
    half, s = angr_t.shape
    t = min(ROPE_TABLE_COLS, s)
    return pl.pallas_call(
        _rope_tables_kernel,
        grid=(s // t,),
        in_specs=[pl.BlockSpec((half, t), lambda i: (0, i)), pl.BlockSpec((half, t), lambda i: (0, i))],
        out_specs=pl.BlockSpec((4, half, t), lambda i: (0, 0, i)),
        out_shape=jax.ShapeDtypeStruct((4, half, s), F32),
        compiler_params=_params("parallel"),
        name="rope_tables",
    )(angr_t, angc_t)


def _mix_in_kernel(h_ref, ng_ref, sh_ref, sc_ref, wt_ref, rope_ref, qkg_ref,
                   qta_ref, ka_ref, vta_ref, qtd_ref, kd_ref, vtd_ref, qn_ref, kn_ref,
                   qfa_ref, kfa_ref, qfd_ref, kfd_ref):
    n = _rms_rows(h_ref[...]) * ng_ref[...]
    n = (n * (1.0 + sc_ref[...]) + sh_ref[...]).astype(BF16)
    pt = lax.dot_general(wt_ref[...], n, (((1,), (1,)), ((), ())), preferred_element_type=F32)
    t = pt.shape[1]

    cr, sr, cc, sc = rope_ref[0], rope_ref[1], rope_ref[2], rope_ref[3]
    gq = qkg_ref[:, 0:1]
    gk = qkg_ref[:, 1:2]
    zeros = jnp.zeros((HEAD_DIM, t), BF16)
    qscale = QK_SCALE * LOG2E

    qn, kn = [], []
    for hd in range(A_Q_HEADS):
        q = _rope_t(_head_norm_t(pt[hd * HEAD_DIM:(hd + 1) * HEAD_DIM], gq), cr, sr, cc, sc) * qscale
        q = q.astype(BF16)
        qta_ref[hd] = jnp.concatenate([q, zeros] if hd // A_GROUP == 0 else [zeros, q], axis=0)
        qfa_ref[hd] = _f8_query_rows(q, weighted=False)
        qn.append(_max_sq_norm(q))

    c1 = A_Q_COLS
    c2 = c1 + A_KV_COLS
    c3 = c2 + A_KV_COLS
    c4 = c3 + B_QK_COLS
    c5 = c4 + B_QK_COLS
    kt = jnp.concatenate(
        [_rope_t(_head_norm_t(pt[c1 + j * HEAD_DIM:c1 + (j + 1) * HEAD_DIM], gk), cr, sr, cc, sc)
         for j in range(A_KV_HEADS)], axis=0)
    kt = kt.astype(BF16)
    ka_ref[...] = kt.T
    kn += [_max_sq_norm(kt[j * HEAD_DIM:(j + 1) * HEAD_DIM]) for j in range(A_KV_HEADS)]
    for j in range(A_KV_HEADS):
        kfa_ref[j] = _f8_key_rows(kt[j * HEAD_DIM:(j + 1) * HEAD_DIM], weighted=True).astype(BF16).T.astype(F8)
    kn.append(jnp.zeros((NORM_ROWS_A - A_KV_HEADS, LANES), F32))
    for j in range(A_KV_HEADS):
        vta_ref[j] = _with_ones(pt[c2 + j * HEAD_DIM:c2 + (j + 1) * HEAD_DIM])

    for cmb in range(2 * B_HEADS):
        q = (pt[c3 + cmb * HEAD_DIM:c3 + (cmb + 1) * HEAD_DIM] * qscale).astype(BF16)
        qtd_ref[cmb] = jnp.concatenate([q, zeros] if cmb % 2 == 0 else [zeros, q], axis=0)
        qfd_ref[cmb] = _f8_query_rows(q, weighted=True)
        qn.append(_max_sq_norm(q))
        k = pt[c4 + cmb * HEAD_DIM:c4 + (cmb + 1) * HEAD_DIM].astype(BF16)
        kfd_ref[cmb] = _f8_key_rows(k, weighted=False).astype(BF16).T.astype(F8)
        kn.append(_max_sq_norm(k))
    qn_ref[...] = jnp.concatenate(qn, axis=0)
    kn_ref[...] = jnp.concatenate(kn, axis=0)
    kd_ref[...] = pt[c4:c5].T.astype(BF16)
    for hd in range(B_HEADS):
        vtd_ref[hd] = _with_ones(pt[c5 + hd * B_V_DIM:c5 + (hd + 1) * B_V_DIM])


def _mix_in(h, l, norm_g4, mod4, w_in_t, rope, qkg_t):
    s, d = h.shape
    tm = MIX_ROWS
    nt = s // tm
    n_sets = 2 * B_HEADS
    assert A_Q_HEADS == NORM_ROWS_A
    out_shape = (
        jax.ShapeDtypeStruct((A_Q_HEADS, 2 * HEAD_DIM, s), BF16),
        jax.ShapeDtypeStruct((s, A_KV_COLS), BF16),
        jax.ShapeDtypeStruct((A_KV_HEADS, nt, HEAD_DIM + ONES_ROWS, tm), BF16),
        jax.ShapeDtypeStruct((n_sets, 2 * HEAD_DIM, s), BF16),
        jax.ShapeDtypeStruct((s, B_QK_COLS), BF16),
        jax.ShapeDtypeStruct((B_HEADS, nt, B_V_DIM + ONES_ROWS, tm), BF16),
        jax.ShapeDtypeStruct((nt, NORM_ROWS, LANES), F32),
        jax.ShapeDtypeStruct((nt, NORM_ROWS, LANES), F32),
        jax.ShapeDtypeStruct((A_Q_HEADS, F8_COLS, s), F8),
        jax.ShapeDtypeStruct((A_KV_HEADS, s, F8_COLS), F8),
        jax.ShapeDtypeStruct((n_sets, F8_COLS, s), F8),
        jax.ShapeDtypeStruct((n_sets, s, F8_COLS), F8),
    )
    out_specs = (
        pl.BlockSpec((A_Q_HEADS, 2 * HEAD_DIM, tm), lambda i: (0, 0, i)),
        pl.BlockSpec((tm, A_KV_COLS), lambda i: (i, 0)),
        pl.BlockSpec((A_KV_HEADS, None, HEAD_DIM + ONES_ROWS, tm), lambda i: (0, i, 0, 0)),
        pl.BlockSpec((n_sets, 2 * HEAD_DIM, tm), lambda i: (0, 0, i)),
        pl.BlockSpec((tm, B_QK_COLS), lambda i: (i, 0)),
        pl.BlockSpec((B_HEADS, None, B_V_DIM + ONES_ROWS, tm), lambda i: (0, i, 0, 0)),
        pl.BlockSpec((None, NORM_ROWS, LANES), lambda i: (i, 0, 0)),
        pl.BlockSpec((None, NORM_ROWS, LANES), lambda i: (i, 0, 0)),
        pl.BlockSpec((A_Q_HEADS, F8_COLS, tm), lambda i: (0, 0, i)),
        pl.BlockSpec((A_KV_HEADS, tm, F8_COLS), lambda i: (0, i, 0)),
        pl.BlockSpec((n_sets, F8_COLS, tm), lambda i: (0, 0, i)),
        pl.BlockSpec((n_sets, tm, F8_COLS), lambda i: (0, i, 0)),
    )
    return pl.pallas_call(
        _mix_in_kernel,
        grid=(nt,),
        in_specs=[
            pl.BlockSpec((tm, d), lambda i: (i, 0)),
            _layer_vec(l, 1, d), _layer_vec(l, 3, d), _layer_vec(l, 4, d),
            _layer_mat(l, w_in_t, single_buffer=True),
            pl.BlockSpec((4, ROPE_HALF, tm), lambda i: (0, 0, i)),
            _layer_mat(l, qkg_t),
        ],
        out_specs=out_specs,
        out_shape=out_shape,
        compiler_params=_params("parallel"),
        name="mix_in",
    )(h, norm_g4, mod4, mod4, w_in_t, rope, qkg_t)


def _softmax_tile(s, m_ref, alpha_ref, p_ref):
    tk, n = s.shape
    part = jnp.max(s.reshape(tk // BF16_SUBLANES, BF16_SUBLANES, n), axis=0)
    m_old = m_ref[...]
    m_new = jnp.maximum(m_old, jnp.max(part.astype(F32), axis=0, keepdims=True))
    alpha_ref[...] = jnp.exp2(m_old - m_new)
    m_ref[...] = m_new
    p_ref[...] = jnp.exp2(s - m_new.astype(BF16))


def _pipelined_sweep(prep, qk, softmax, pv, n_chunks, lo, n_pairs, last):
    ctx = prep(lo)
    for c in range(n_chunks):
        qk(lo, 0, c, ctx)

    def body(j, carry):
        a = lo + 2 * j
        prev = jnp.maximum(a - 1, 0)
        nxt = jnp.minimum(a + 2, last)
        ctx = prep(a + 1)
        for c in range(n_chunks):
            softmax(0, c)
            pv(prev, 1, c)
            qk(a + 1, 1, c, ctx)
        ctx = prep(nxt)
        for c in range(n_chunks):
            softmax(1, c)
            pv(a, 0, c)
            qk(nxt, 0, c, ctx)
        return carry

    lax.fori_loop(0, n_pairs, body, 0)
    for c in range(n_chunks):
        pv(last, 1, c)


def _direct_sweep(prep, qk_exp, pv, n_chunks, lo, n_tiles, last, unroll):
    def phase(kt_new, slot_new, kt_old):
        ctx = prep(kt_new)
        for c in range(n_chunks):
            qk_exp(kt_new, slot_new, c, ctx)
            pv(kt_old, 1 - slot_new, c)

    ctx = prep(lo)
    for c in range(n_chunks):
        qk_exp(lo, 0, c, ctx)

    def pairs(first, n_pairs):
        def body(j, carry):
            a = first + 2 * n_pairs * j
            for i in range(n_pairs):
                phase(a + 2 * i + 1, 1, a + 2 * i)
                phase(a + 2 * i + 2, 0, a + 2 * i + 1)
            return carry
        return body

    assert unroll[-1] == 1
    static = isinstance(n_tiles, int)
    left = (n_tiles - 1) // 2 if static else lax.shift_right_logical(n_tiles - 1, 1)
    first = lo
    for size in unroll:
        trips = left // size if static else lax.div(left, size)
        lax.fori_loop(0, trips, pairs(first, size), 0)
        first = first + 2 * size * trips
        left = left - size * trips

    def tail_odd():
        for c in range(n_chunks):
            pv(last, 0, c)
        return 0

    def tail_even():
        phase(last, 1, last - 1)
        for c in range(n_chunks):
            pv(last, 1, c)
        return 0

    if isinstance(n_tiles, int):
        (tail_odd if n_tiles % 2 else tail_even)()
    else:
        lax.cond(jnp.bitwise_and(n_tiles, 1) == 1, tail_odd, tail_even)


def _score_bound(qn_tile, kn_ref, q_rows, k_row_of_q):
    k_max = jnp.max(kn_ref[...], axis=0)
    row = lax.broadcasted_iota(jnp.int32, k_max.shape, 0)
    sel = q_rows(row)
    q2 = jnp.where(sel, qn_tile, 0.0)
    k2 = jnp.where(sel, k_row_of_q(row, k_max), 0.0)
    fp8_ok = jnp.logical_and(jnp.max(q2) * F8_Q_SCALE ** 2 <= F8_SAFE_MAX ** 2,
                             jnp.max(k2) * F8_K_SCALE ** 2 <= F8_SAFE_MAX ** 2)
    return jnp.max(jnp.sqrt(q2 * k2)), fp8_ok


GQA_CHUNKS = 8
GQA_UNROLL = (15, 1)


def _gqa_kernel(qt_ref, k_ref, vt_ref, qn_ref, kn_ref, qf_ref, kf_ref, o_ref, q_scr, qf_scr, s_buf, p_buf,
                alpha_buf, m_scr, acc_scr):
    g = pl.program_id(0)
    qi = pl.program_id(1)
    nk, tk, _ = k_ref.shape
    tq = qt_ref.shape[2]
    w = A_GROUP * tq // GQA_CHUNKS
    for c in range(GQA_CHUNKS):
        for col in range(c * w, (c + 1) * w, min(w, tq)):
            hh, j, n = col // tq, col % tq, min(w, tq)
            q_scr[c, :, col - c * w:col - c * w + n] = qt_ref[hh, :, j:j + n]
            qf_scr[c, :, col - c * w:col - c * w + n] = qf_ref[hh, :, j:j + n]
    acc_scr[...] = jnp.zeros_like(acc_scr)

    def qk(kt, slot, c, _):
        s_buf[slot, c] = jnp.dot(k_ref[kt], q_scr[c], preferred_element_type=F32).astype(BF16)

    def softmax(slot, c):
        _softmax_tile(s_buf[slot, c], m_scr.at[c], alpha_buf.at[slot, c], p_buf.at[slot, c])

    def pv(kt, slot, c):
        acc_scr[c] = alpha_buf[slot, c] * acc_scr[c] + jnp.dot(
            vt_ref[kt], p_buf[slot, c], preferred_element_type=F32)

    def qk_exp(kt, slot, c, _):
        s = jnp.dot(kf_ref[kt], qf_scr[c], preferred_element_type=F32)
        p_buf[slot, c] = jnp.exp2(s.astype(BF16) * F8_UNSCALE)

    def pv_plain(kt, slot, c):
        acc_scr[c] += jnp.dot(vt_ref[kt], p_buf[slot, c], preferred_element_type=F32)

    def kmax_of_group(row, k_max):
        return jnp.max(jnp.where(row == g, k_max, 0.0), axis=0, keepdims=True)

    bound, fp8_ok = _score_bound(
        qn_ref[lax.div(qi * tq, tk)], kn_ref,
        lambda row: jnp.logical_and(row >= g * A_GROUP, row < (g + 1) * A_GROUP), kmax_of_group)

    def direct():
        _direct_sweep(lambda kt: None, qk_exp, pv_plain, GQA_CHUNKS, 0, nk, nk - 1, GQA_UNROLL)
        return 0

    def online():
        m_scr[...] = jnp.full_like(m_scr, NEG_BIG)
        p_buf[1] = jnp.zeros(p_buf.shape[1:], BF16)
        alpha_buf[1] = jnp.ones(alpha_buf.shape[1:], F32)
        _pipelined_sweep(lambda kt: None, qk, softmax, pv, GQA_CHUNKS, 0, nk // 2, nk - 1)
        return 0

    lax.cond(jnp.logical_and(bound <= FAST_LOG2_LIMIT, fp8_ok), direct, online)
    cols = []
    for c in range(GQA_CHUNKS):
        acc = acc_scr[c]
        cols.append(acc[:HEAD_DIM] * (1.0 / acc[HEAD_DIM:HEAD_DIM + 1]))
    ot = jnp.concatenate(cols, axis=1)
    ot = jnp.concatenate([ot[:, hh * tq:(hh + 1) * tq] for hh in range(A_GROUP)], axis=0)
    o_ref[...] = ot.T.astype(BF16)


def _gqa(qta, ka3, vta, qn, kn, qfa, kfa4):
    s = qta.shape[2]
    nk, tk, _ = ka3.shape
    tq = GQA_Q_TILE
    assert nk % 2 == 0 and tk % tq == 0 and tk == MIX_ROWS
    nc = GQA_CHUNKS
    w = A_GROUP * tq // nc
    v_rows = vta.shape[2]
    return pl.pallas_call(
        _gqa_kernel,
        grid=(A_KV_HEADS, s // tq),
        in_specs=[
            pl.BlockSpec((A_GROUP, 2 * HEAD_DIM, tq), lambda g, i: (g, 0, i)),
            pl.BlockSpec((nk, tk, A_KV_COLS), lambda g, i: (0, 0, 0), pipeline_mode=pl.Buffered(1)),
            pl.BlockSpec((None, nk, v_rows, tk), lambda g, i: (g, 0, 0, 0), pipeline_mode=pl.Buffered(1)),
            pl.BlockSpec(qn.shape, lambda g, i: (0, 0, 0)),
            pl.BlockSpec(kn.shape, lambda g, i: (0, 0, 0)),
            pl.BlockSpec((A_GROUP, F8_COLS, tq), lambda g, i: (g, 0, i)),
            pl.BlockSpec((None, nk, tk, F8_COLS), lambda g, i: (g, 0, 0, 0), pipeline_mode=pl.Buffered(1)),
        ],
        out_specs=pl.BlockSpec((tq, A_GROUP * HEAD_DIM), lambda g, i: (i, g)),
        out_shape=jax.ShapeDtypeStruct((s, A_Q_COLS), BF16),
        scratch_shapes=[
            pltpu.VMEM((nc, 2 * HEAD_DIM, w), BF16),
            pltpu.VMEM((nc, F8_COLS, w), F8),
            pltpu.VMEM((2, nc, tk, w), BF16),
            pltpu.VMEM((2, nc, tk, w), BF16),
            pltpu.VMEM((2, nc, 1, w), F32),
            pltpu.VMEM((nc, 1, w), F32),
            pltpu.VMEM((nc, v_rows, w), F32),
        ],
        compiler_params=_params("parallel", "parallel"),
        name="gqa_attn",
    )(qta, ka3, vta, qn, kn, qfa, kfa4)


DIFF_CHUNKS = 2
DIFF_UNROLL = (5, 3, 1)


def _diff_kernel(coef_ref, qt0_ref, qt1_ref, k0_ref, k1_ref, vt_ref, qn_ref, kn_ref, lamp_ref, sg_ref,
                 qf0_ref, qf1_ref, kf0_ref, kf1_ref,
                 o_ref, rel_scr, s_buf, p_buf, alpha_buf, m_scr, acc_scr, *, lam_init):
    hd = pl.program_id(0)
    qi = pl.program_id(1)
    nk, tk, _ = k0_ref.shape
    tq = qt0_ref.shape[1]
    slope_f8 = coef_ref[0, hd]
    inv_tile_drop = coef_ref[1, hd]
    acc_scr[...] = jnp.zeros_like(acc_scr)

    @pl.when(qi == 0)
    def _():
        rel_scr[...] = (lax.broadcasted_iota(jnp.int32, (tk, tq), 1)
                        - lax.broadcasted_iota(jnp.int32, (tk, tq), 0)).astype(F32) * slope_f8

    bound, fp8_ok = _score_bound(
        qn_ref[qi], kn_ref,
        lambda row: jnp.logical_or(row == NORM_ROWS_A + hd, row == NORM_ROWS_A + B_HEADS + hd),
        lambda row, k_max: k_max)
    use_direct = jnp.logical_and(bound <= FAST_LOG2_LIMIT, fp8_ok)
    drop = 2.0 * bound + SKIP_LOG2
    reach = jnp.minimum(jnp.floor(drop * inv_tile_drop) + 1.0, float(nk)).astype(jnp.int32)
    lo = jnp.maximum(qi - reach, 0)
    hi = jnp.minimum(qi + reach, nk - 1)

    q0 = qi * tq
    w = tq // DIFF_CHUNKS
    n_chunks = 2 * DIFF_CHUNKS
    cols = [slice((c // 2) * w, (c // 2 + 1) * w) for c in range(n_chunks)]
    qts = (qt0_ref, qt1_ref)
    ks = (k0_ref, k1_ref)
    qfs = (qf0_ref, qf1_ref)
    kfs = (kf0_ref, kf1_ref)

    def tile_ctx(kt):
        return {"off": (q0 - kt * tk).astype(F32) * slope_f8}

    def penalty_scaled(ctx, c):
        if c // 2 not in ctx:
            ctx[c // 2] = jnp.abs(rel_scr[:, cols[c]] + ctx["off"])
        return ctx[c // 2]

    def qk(kt, slot, c, ctx):
        mp = c % 2
        s = jnp.dot(ks[mp][kt], qts[mp][:, cols[c]], preferred_element_type=F32)
        s_buf[slot, c] = (s - penalty_scaled(ctx, c) * F8_UNSCALE).astype(BF16)

    def softmax(slot, c):
        _softmax_tile(s_buf[slot, c], m_scr.at[c], alpha_buf.at[slot, c], p_buf.at[slot, c])

    def pv(kt, slot, c):
        acc_scr[c] = alpha_buf[slot, c] * acc_scr[c] + jnp.dot(
            vt_ref[kt], p_buf[slot, c], preferred_element_type=F32)

    def qk_exp(kt, slot, c, ctx):
        mp = c % 2
        s = jnp.dot(kfs[mp][kt], qfs[mp][:, cols[c]], preferred_element_type=F32) - penalty_scaled(ctx, c)
        p_buf[slot, c] = jnp.exp2(s.astype(BF16) * F8_UNSCALE)

    def pv_plain(kt, slot, c):
        acc_scr[c] += jnp.dot(vt_ref[kt], p_buf[slot, c], preferred_element_type=F32)

    def direct_all():
        _direct_sweep(tile_ctx, qk_exp, pv_plain, n_chunks, 0, nk, nk - 1, ((nk - 1) // 2, 1))
        return 0

    def direct_some():
        _direct_sweep(tile_ctx, qk_exp, pv_plain, n_chunks, lo, hi - lo + 1, hi, DIFF_UNROLL)
        return 0

    def direct():
        return lax.cond(hi - lo + 1 == nk, direct_all, direct_some)

    def online():
        m_scr[...] = jnp.full_like(m_scr, NEG_BIG)
        p_buf[1] = jnp.zeros(p_buf.shape[1:], BF16)
        alpha_buf[1] = jnp.ones(alpha_buf.shape[1:], F32)
        odd = jnp.bitwise_and(hi - lo + 1, 1)
        grow_hi = jnp.where(hi < nk - 1, odd, 0)
        hi2 = hi + grow_hi
        lo2 = lo - (odd - grow_hi)
        _pipelined_sweep(tile_ctx, qk, softmax, pv, n_chunks, lo2, lax.shift_right_logical(hi2 - lo2 + 1, 1), hi2)
        return 0

    lax.cond(use_direct, direct, online)

    lp = lamp_ref[...]
    lam = (jnp.exp(jnp.sum(lp[0:1] * lp[1:2], axis=1, keepdims=True))
           - jnp.exp(jnp.sum(lp[2:3] * lp[3:4], axis=1, keepdims=True)) + lam_init)
    a0, a1 = [jnp.concatenate([acc_scr[2 * j + mp] for j in range(DIFF_CHUNKS)], axis=1)
              for mp in range(2)]
    ot = (a0[:B_V_DIM] * (1.0 / a0[B_V_DIM:B_V_DIM + 1])
          - lam * (a1[:B_V_DIM] * (1.0 / a1[B_V_DIM:B_V_DIM + 1])))
    ot = ot * lax.rsqrt(jnp.mean(ot * ot, axis=0, keepdims=True) + EPS)
    ot = ot * sg_ref[...] * (1.0 - lam_init)
    o_ref[...] = ot.T.astype(BF16)


def _diff(l, coef, qtd, kd3, vtd, qn, kn, qfd, kfd4, lam_p, subln_col, *, lam_init):
    s = qtd.shape[2]
    nk, tk, _ = kd3.shape
    tq = DIFF_Q_TILE
    assert nk % 2 == 0 and tq == tk == MIX_ROWS
    lanes = 2 * HEAD_DIM
    v_rows = vtd.shape[2]
    nc = 2 * DIFF_CHUNKS
    w = tq // DIFF_CHUNKS
    return pl.pallas_call(
        functools.partial(_diff_kernel, lam_init=lam_init),
        grid=(B_HEADS, s // tq),
        in_specs=[
            pl.BlockSpec(memory_space=pltpu.SMEM),
            pl.BlockSpec((None, lanes, tq), lambda h, i: (h, 0, i)),
            pl.BlockSpec((None, lanes, tq), lambda h, i: (B_HEADS + h, 0, i)),
            pl.BlockSpec((nk, tk, lanes), lambda h, i: (0, 0, h // 2), pipeline_mode=pl.Buffered(1)),
            pl.BlockSpec((nk, tk, lanes), lambda h, i: (0, 0, B_HEADS // 2 + h // 2),
                         pipeline_mode=pl.Buffered(1)),
            pl.BlockSpec((None, nk, v_rows, tk), lambda h, i: (h, 0, 0, 0)),
            pl.BlockSpec(qn.shape, lambda h, i: (0, 0, 0)),
            pl.BlockSpec(kn.shape, lambda h, i: (0, 0, 0)),
            _layer_mat(l, lam_p),
            _layer_mat(l, subln_col),
            pl.BlockSpec((None, F8_COLS, tq), lambda h, i: (h, 0, i)),
            pl.BlockSpec((None, F8_COLS, tq), lambda h, i: (B_HEADS + h, 0, i)),
            pl.BlockSpec((None, nk, tk, F8_COLS), lambda h, i: (h, 0, 0, 0)),
            pl.BlockSpec((None, nk, tk, F8_COLS), lambda h, i: (B_HEADS + h, 0, 0, 0)),
        ],
        out_specs=pl.BlockSpec((tq, B_V_DIM), lambda h, i: (i, h)),
        out_shape=jax.ShapeDtypeStruct((s, B_V_COLS), BF16),
        scratch_shapes=[
            pltpu.VMEM((tk, tq), F32),
            pltpu.VMEM((2, nc, tk, w), BF16),
            pltpu.VMEM((2, nc, tk, w), BF16),
            pltpu.VMEM((2, nc, 1, w), F32),
            pltpu.VMEM((nc, 1, w), F32),
            pltpu.VMEM((nc, v_rows, w), F32),
        ],
        compiler_params=_params("arbitrary", "arbitrary"),
        name="diff_attn",
    )(coef, qtd, qtd, kd3, kd3, vtd, qn, kn, lam_p, subln_col, qfd, qfd, kfd4, kfd4)


def _mix_out_kernel(h_ref, ng_ref, sh_ref, sc_ref, gt_ref, oa_ref, od_ref, wba_ref, wbb_ref,
                    wgate_ref, bgate_ref, wo_ref, o_ref):
    h = h_ref[...]
    d = h.shape[1]
    n = _rms_rows(h) * ng_ref[...]
    n = (n * (1.0 + sc_ref[...]) + sh_ref[...]).astype(BF16)
    z = jnp.dot(n, wgate_ref[...], preferred_element_type=F32) + bgate_ref[...]
    g = 1.0 / (1.0 + jnp.exp(-z))
    ya = jnp.dot(oa_ref[...], wba_ref[...], preferred_element_type=F32)
    yb = jnp.dot(od_ref[...], wbb_ref[...], preferred_element_type=F32)
    mix = (g[:, :d] * ya + g[:, d:] * yb).astype(BF16)
    y = jnp.dot(mix, wo_ref[...], preferred_element_type=F32)
    o_ref[...] = h + gt_ref[...] * y


def _mix_out(h, l, norm_g4, mod4, oa, od, w_ba, w_bb, w_gate, b_gate, w_o):
    s, d = h.shape
    tm = MIX_ROWS
    return pl.pallas_call(
        _mix_out_kernel,
        grid=(s // tm,),
        in_specs=[
            pl.BlockSpec((tm, d), lambda i: (i, 0)),
            _layer_vec(l, 1, d), _layer_vec(l, 3, d), _layer_vec(l, 4, d), _layer_vec(l, 5, d),
            pl.BlockSpec((tm, oa.shape[1]), lambda i: (i, 0)),
            pl.BlockSpec((tm, od.shape[1]), lambda i: (i, 0)),
            _layer_mat(l, w_ba, single_buffer=True), _layer_mat(l, w_bb, single_buffer=True),
            _layer_mat(l, w_gate, single_buffer=True), _layer_mat(l, b_gate),
            _layer_mat(l, w_o, single_buffer=True),
        ],
        out_specs=pl.BlockSpec((tm, d), lambda i: (i, 0)),
        out_shape=jax.ShapeDtypeStruct((s, d), F32),
        compiler_params=_params("parallel"),
        name="mix_out",
    )(h, norm_g4, mod4, mod4, mod4, oa, od, w_ba, w_bb, w_gate, b_gate, w_o)


def _axial_angles_t(seq):
    rows = seq // GRID_W
    row = jnp.broadcast_to(jnp.arange(rows)[:, None], (rows, GRID_W)).reshape(seq) - rows // 2
    col = jnp.broadcast_to(jnp.arange(GRID_W)[None, :], (rows, GRID_W)).reshape(seq) - GRID_W // 2
    inv = 1.0 / (ROPE_THETA ** (jnp.arange(0, ROPE_AXIS_DIM, 2, dtype=F32) / ROPE_AXIS_DIM))
    return inv[:, None] * row.astype(F32)[None, :], inv[:, None] * col.astype(F32)[None, :]


def kernel(x, c, ada_w, ada_b, norm_g, ffn_wg, ffn_wu, ffn_wd, w_in, qk_g, lam_p, subln_g, w_ba, w_bb,
           w_gate, b_gate, w_o, final_g):
    batch, s, d = x.shape
    assert batch == 1 and s % KEY_TILE == 0 and MIX_ROWS == KEY_TILE
    depth = ada_w.shape[0]
    h = x.reshape(s, d)

    mod4 = _ada_mod(c.reshape(d, 1), ada_w, ada_b).reshape(depth, N_ADA, 1, d)
    norm_g4 = norm_g.reshape(depth, 3, 1, d)
    rope = _rope_tables(*_axial_angles_t(s))
    slopes = 2.0 ** (-8.0 * jnp.arange(1, B_HEADS + 1, dtype=F32) / B_HEADS)
    coef = jnp.stack([slopes * (LOG2E / F8_UNSCALE), 1.0 / (slopes * LOG2E * KEY_TILE)])
    fg = final_g.reshape(1, d)
    wg, wu, wd = ffn_wg.astype(BF16), ffn_wu.astype(BF16), ffn_wd.astype(BF16)
    w_in_t = jnp.swapaxes(w_in, 1, 2).astype(BF16)
    qkg_t = jnp.swapaxes(qk_g, 1, 2)
    w_ba16, w_bb16, w_gate16, w_o16 = (w.astype(BF16) for w in (w_ba, w_bb, w_gate, w_o))
    b_gate3 = b_gate.reshape(depth, 1, -1)
    subln_col = subln_g.reshape(depth, B_V_DIM, 1)
    nk = s // KEY_TILE

    for l in range(depth):
        lam_init = 0.8 - 0.6 * math.exp(-0.3 * l)
        h = _ffn(h, l, 0, norm_g4, mod4, wg, wu, wd, fg, final_norm=False)
        qta, ka, vta, qtd, kd, vtd, qn, kn, qfa, kfa, qfd, kfd = _mix_in(
            h, l, norm_g4, mod4, w_in_t, rope, qkg_t)
        oa = _gqa(qta, ka.reshape(nk, KEY_TILE, A_KV_COLS), vta, qn, kn,
                  qfa, kfa.reshape(A_KV_HEADS, nk, KEY_TILE, F8_COLS))
        od = _diff(l, coef, qtd, kd.reshape(nk, KEY_TILE, B_QK_COLS), vtd, qn, kn,
                   qfd, kfd.reshape(2 * B_HEADS, nk, KEY_TILE, F8_COLS), lam_p, subln_col, lam_init=lam_init)
        h = _mix_out(h, l, norm_g4, mod4, oa, od, w_ba16, w_bb16, w_gate16, b_gate3, w_o16)
        h = _ffn(h, l, 1, norm_g4, mod4, wg, wu, wd, fg, final_norm=(l == depth - 1))
    return h.reshape(batch, s, d)
```
